```python
import math
import jax, jax.numpy as jnp
from jax import lax
import numpy as np

D_MODEL = 1024
BATCH = 2
SEQ = 8192
DEPTH = 4
DEC_BATCH = 128
DEC_SEQ = 4
PAST_LEN = 8192
PAGE_SIZE = 128

N_A_LAYERS = DEPTH // 2
N_B_LAYERS = DEPTH - N_A_LAYERS
GROUP_SIZE = 16
N_GROUPS = D_MODEL // GROUP_SIZE
STATE_DIM = 64
DT_MIN = 0.001
DT_MAX = 0.1
HEAD_DIM = 64
N_HEADS = D_MODEL // HEAD_DIM
N_KV_HEADS = max(1, N_HEADS // 8)
Q_PER_KV = N_HEADS // N_KV_HEADS
WINDOW = 128
BLOCK = WINDOW
ATTN_SCALE = 1.0 / math.sqrt(HEAD_DIM)
NUM_BUCKETS = 32
MAX_DISTANCE = WINDOW
D_FF = ((8 * D_MODEL // 3 + 127) // 128) * 128
N_NORMS = 6
RMS_EPS = 1e-6

kernel_name = 'yoco_s5_swa_sink_macaron'


def _rms(x, g):
    xf = x.astype(jnp.float32)
    y = xf * lax.rsqrt(jnp.mean(xf * xf, axis=-1, keepdims=True) + RMS_EPS) * g.astype(jnp.float32)
    return y.astype(x.dtype)


def _swiglu(x, w_gu, w_down):
    gate, up = jnp.split(x @ w_gu, 2, axis=-1)
    return (jax.nn.silu(gate) * up) @ w_down


def _ssm_combine(left, right):
    a_l, b_l = left
    a_r, b_r = right
    return a_r * a_l, a_r * b_l + b_r


def _ssm_mixer(u, lam_re, lam_im, log_dt, b_re, b_im, c_re, c_im, d, w_glu, b_glu, x0_re, x0_im):
    f32 = jnp.float32
    bsz, seq, _ = u.shape
    uf = u.astype(f32)
    ug = uf.reshape(bsz, seq, N_GROUPS, GROUP_SIZE)
    lam = lax.complex(lam_re.astype(f32), lam_im.astype(f32))
    dt = jnp.exp(log_dt.astype(f32))
    lam_dt = lam * dt
    lam_bar = jnp.exp(lam_dt)
    b_bar = ((lam_bar - 1.0) / lam)[..., None] * lax.complex(b_re.astype(f32), b_im.astype(f32))
    bu = lax.complex(jnp.einsum('bsgh,gph->bsgp', ug, jnp.real(b_bar)),
                     jnp.einsum('bsgh,gph->bsgp', ug, jnp.imag(b_bar)))
    a = jnp.broadcast_to(lam_bar, bu.shape)
    xs = lax.associative_scan(_ssm_combine, (a, bu), axis=1)[1]
    if x0_re is not None:
        t = jnp.arange(1, seq + 1, dtype=f32)[:, None, None]
        x0 = lax.complex(x0_re.astype(f32), x0_im.astype(f32))
        xs = xs + jnp.exp(lam_dt * t) * x0[:, None]
    y = (jnp.einsum('bsgp,ghp->bsgh', jnp.real(xs), c_re.astype(f32))
         - jnp.einsum('bsgp,ghp->bsgh', jnp.imag(xs), c_im.astype(f32)))
    y = y.reshape(bsz, seq, D_MODEL) + d.astype(f32) * uf
    h = jax.nn.gelu(y) @ w_glu.astype(f32) + b_glu.astype(f32)
    out = h[..., :D_MODEL] * jax.nn.sigmoid(h[..., D_MODEL:])
    last = xs[:, -1]
    return out.astype(u.dtype), jnp.real(last), jnp.imag(last)


def _t5_bucket(dist):
    n = jnp.maximum(dist, 0)
    max_exact = NUM_BUCKETS // 2
    nf = jnp.maximum(n, 1).astype(jnp.float32)
    large = max_exact + (jnp.log(nf / max_exact) / math.log(MAX_DISTANCE / max_exact)
                         * (NUM_BUCKETS - max_exact)).astype(jnp.int32)
    large = jnp.minimum(large, NUM_BUCKETS - 1)
    return jnp.where(n < max_exact, n, large)


def _band_bias_mask(rel_bias, n_q, n_k, q_offset):
    dist = (jnp.arange(n_q)[:, None] + q_offset) - jnp.arange(n_k)[None, :]
    valid = (dist >= 0) & (dist < WINDOW)
    bias = rel_bias.astype(jnp.float32)[_t5_bucket(dist)]
    bias = bias.transpose(2, 0, 1).reshape(N_KV_HEADS, Q_PER_KV, n_q, n_k)
    return bias, valid


def _prompt_bands(k, v, rel_bias):
    bsz, seq = k.shape[:2]
    nb = seq // BLOCK

    def band(t):
        tb = t.reshape(bsz, nb, BLOCK, N_KV_HEADS, HEAD_DIM)
        prev = jnp.concatenate([jnp.zeros_like(tb[:, :1]), tb[:, :-1]], axis=1)
        return jnp.concatenate([prev, tb], axis=2)

    bias, valid = _band_bias_mask(rel_bias, BLOCK, 2 * BLOCK, BLOCK)
    key_ok = (jnp.arange(nb)[:, None] > 0) | (jnp.arange(2 * BLOCK)[None, :] >= BLOCK)
    return band(k), band(v), bias, valid[None] & key_ok[:, None, :]


def _window_attention(u, k_cat, v_cat, bias, mask, w_q, b_q, sinks, w_o, b_o):
    f32 = jnp.float32
    bsz, seq, _ = u.shape
    n_blk = k_cat.shape[1]
    q = (u @ w_q + b_q).reshape(bsz, n_blk, seq // n_blk, N_KV_HEADS, Q_PER_KV, HEAD_DIM)
    s = jnp.einsum('bnqkgd,bnjkd->bnkgqj', q, k_cat, preferred_element_type=f32) * ATTN_SCALE + bias
    s = jnp.where(mask[None, :, None, None], s, -jnp.inf)
    sink = sinks.astype(f32).reshape(N_KV_HEADS, Q_PER_KV)[None, None, :, :, None, None]
    mx = jnp.maximum(s.max(axis=-1, keepdims=True), sink)
    p = jnp.exp(s - mx)
    p = p / (p.sum(axis=-1, keepdims=True) + jnp.exp(sink - mx))
    o = jnp.einsum('bnkgqj,bnjkd->bnqkgd', p.astype(v_cat.dtype), v_cat, preferred_element_type=f32)
    return o.reshape(bsz, seq, N_HEADS * HEAD_DIM).astype(u.dtype) @ w_o + b_o


def _trunk(x, ssm_re0, ssm_im0, win_k0, win_v0, w):
    bsz, seq, _ = x.shape
    new_re, new_im = [], []
    for l in range(DEPTH):
        if l == N_A_LAYERS:
            kv = (_rms(x, w['kv_norm_g']) @ w['w_kv'] + w['b_kv']).reshape(bsz, seq, 2, N_KV_HEADS, HEAD_DIM)
            k, v = kv[:, :, 0], kv[:, :, 1]
            if win_k0 is None:
                k_cat, v_cat, bias, mask = _prompt_bands(k, v, w['rel_bias'])
                new_k, new_v = k[:, -WINDOW:], v[:, -WINDOW:]
            else:
                n_past = win_k0.shape[1]
                k_full = jnp.concatenate([win_k0.astype(k.dtype), k], axis=1)
                v_full = jnp.concatenate([win_v0.astype(v.dtype), v], axis=1)
                bias, valid = _band_bias_mask(w['rel_bias'], seq, n_past + seq, n_past)
                k_cat, v_cat, mask = k_full[:, None], v_full[:, None], valid[None]
                new_k, new_v = k_full[:, -WINDOW:], v_full[:, -WINDOW:]
        g = w['norm_g'][l]
        x = x + 0.5 * _rms(_swiglu(_rms(x, g[0]), w['ffn1_w_gu'][l], w['ffn1_w_down'][l]), g[1])
        u = _rms(x, g[2])
        if l < N_A_LAYERS:
            m, fr, fi = _ssm_mixer(
                u, w['ssm_lambda_re'][l], w['ssm_lambda_im'][l], w['ssm_log_dt'][l],
                w['ssm_b_re'][l], w['ssm_b_im'][l], w['ssm_c_re'][l], w['ssm_c_im'][l],
                w['ssm_d'][l], w['ssm_w_glu'][l], w['ssm_b_glu'][l],
                None if ssm_re0 is None else ssm_re0[l],
                None if ssm_im0 is None else ssm_im0[l])
            new_re.append(fr)
            new_im.append(fi)
        else:
            bl = l - N_A_LAYERS
            m = _window_attention(u, k_cat, v_cat, bias, mask, w['attn_w_q'][bl], w['attn_b_q'][bl],
                                  w['attn_sinks'][bl], w['attn_w_o'][bl], w['attn_b_o'][bl])
        x = x + _rms(m, g[3])
        x = x + 0.5 * _rms(_swiglu(_rms(x, g[4]), w['ffn2_w_gu'][l], w['ffn2_w_down'][l]), g[5])
    return x, jnp.stack(new_re), jnp.stack(new_im), new_k, new_v


def setup_inputs(seed: int = 0) -> dict:
    key = jax.random.key(seed)
    kit = iter(list(jax.random.split(key, 64)))
    f32 = jnp.float32

    def nrm(shape, scale):
        return jax.random.normal(next(kit), shape, f32) * scale

    win_rows = min(WINDOW, PAST_LEN)
    kvw = 2 * N_KV_HEADS * HEAD_DIM
    qw = N_HEADS * HEAD_DIM
    return {
        'x_prompt': nrm((BATCH, SEQ, D_MODEL), 1.0),
        'x_sample': nrm((DEC_BATCH, DEC_SEQ, D_MODEL), 1.0),
        'state_ssm_re': nrm((N_A_LAYERS, DEC_BATCH, N_GROUPS, STATE_DIM), 0.5),
        'state_ssm_im': nrm((N_A_LAYERS, DEC_BATCH, N_GROUPS, STATE_DIM), 0.5),
        'cache_win_k': nrm((DEC_BATCH, win_rows, N_KV_HEADS, HEAD_DIM), 1.0),
        'cache_win_v': nrm((DEC_BATCH, win_rows, N_KV_HEADS, HEAD_DIM), 1.0),
        'norm_g': 1.0 + nrm((DEPTH, N_NORMS, D_MODEL), 0.05),
        'ffn1_w_gu': nrm((DEPTH, D_MODEL, 2 * D_FF), D_MODEL ** -0.5),
        'ffn1_w_down': nrm((DEPTH, D_FF, D_MODEL), D_FF ** -0.5),
        'ffn2_w_gu': nrm((DEPTH, D_MODEL, 2 * D_FF), D_MODEL ** -0.5),
        'ffn2_w_down': nrm((DEPTH, D_FF, D_MODEL), D_FF ** -0.5),
        'ssm_lambda_re': -0.5 + nrm((N_A_LAYERS, N_GROUPS, STATE_DIM), 0.01),
        'ssm_lambda_im': jnp.pi * jnp.arange(STATE_DIM, dtype=f32) + nrm((N_A_LAYERS, N_GROUPS, STATE_DIM), 0.01),
        'ssm_log_dt': jax.random.uniform(next(kit), (N_A_LAYERS, N_GROUPS, STATE_DIM), f32,
                                         math.log(DT_MIN), math.log(DT_MAX)),
        'ssm_b_re': nrm((N_A_LAYERS, N_GROUPS, STATE_DIM, GROUP_SIZE), (2 * GROUP_SIZE) ** -0.5),
        'ssm_b_im': nrm((N_A_LAYERS, N_GROUPS, STATE_DIM, GROUP_SIZE), (2 * GROUP_SIZE) ** -0.5),
        'ssm_c_re': nrm((N_A_LAYERS, N_GROUPS, GROUP_SIZE, STATE_DIM), STATE_DIM ** -0.5),
        'ssm_c_im': nrm((N_A_LAYERS, N_GROUPS, GROUP_SIZE, STATE_DIM), STATE_DIM ** -0.5),
        'ssm_d': nrm((N_A_LAYERS, D_MODEL), 1.0),
        'ssm_w_glu': nrm((N_A_LAYERS, D_MODEL, 2 * D_MODEL), D_MODEL ** -0.5),
        'ssm_b_glu': nrm((N_A_LAYERS, 2 * D_MODEL), 0.01),
        'kv_norm_g': 1.0 + nrm((D_MODEL,), 0.05),
        'w_kv': nrm((D_MODEL, kvw), D_MODEL ** -0.5),
        'b_kv': nrm((kvw,), 0.01),
        'attn_w_q': nrm((N_B_LAYERS, D_MODEL, qw), D_MODEL ** -0.5),
        'attn_b_q': nrm((N_B_LAYERS, qw), 0.01),
        'attn_sinks': nrm((N_B_LAYERS, N_HEADS), 0.5),
        'attn_w_o': nrm((N_B_LAYERS, qw, D_MODEL), qw ** -0.5),
        'attn_b_o': nrm((N_B_LAYERS, D_MODEL), 0.01),
        'rel_bias': nrm((NUM_BUCKETS, N_HEADS), 0.5),
    }


def reference(x_prompt, x_sample, state_ssm_re, state_ssm_im, cache_win_k, cache_win_v,
              norm_g, ffn1_w_gu, ffn1_w_down, ffn2_w_gu, ffn2_w_down,
              ssm_lambda_re, ssm_lambda_im, ssm_log_dt, ssm_b_re, ssm_b_im, ssm_c_re, ssm_c_im,
              ssm_d, ssm_w_glu, ssm_b_glu,
              kv_norm_g, w_kv, b_kv,
              attn_w_q, attn_b_q, attn_sinks, attn_w_o, attn_b_o, rel_bias):
    w = dict(norm_g=norm_g, ffn1_w_gu=ffn1_w_gu, ffn1_w_down=ffn1_w_down,
             ffn2_w_gu=ffn2_w_gu, ffn2_w_down=ffn2_w_down,
             ssm_lambda_re=ssm_lambda_re, ssm_lambda_im=ssm_lambda_im, ssm_log_dt=ssm_log_dt,
             ssm_b_re=ssm_b_re, ssm_b_im=ssm_b_im, ssm_c_re=ssm_c_re, ssm_c_im=ssm_c_im,
             ssm_d=ssm_d, ssm_w_glu=ssm_w_glu, ssm_b_glu=ssm_b_glu,
             kv_norm_g=kv_norm_g, w_kv=w_kv, b_kv=b_kv,
             attn_w_q=attn_w_q, attn_b_q=attn_b_q, attn_sinks=attn_sinks,
             attn_w_o=attn_w_o, attn_b_o=attn_b_o, rel_bias=rel_bias)
    y_prompt, re_p, im_p, k_p, v_p = _trunk(x_prompt, None, None, None, None, w)
    y_sample, re_s, im_s, k_s, v_s = _trunk(x_sample, state_ssm_re, state_ssm_im, cache_win_k, cache_win_v, w)
    return (y_prompt, y_sample, re_p, im_p, k_p, v_p, re_s, im_s, k_s, v_s)
```

```python
import functools
import math

import numpy as np
import jax
import jax.numpy as jnp
from jax import lax
from jax.experimental import pallas as pl
from jax.experimental.pallas import tpu as pltpu

F32 = jnp.float32
BF16 = jnp.bfloat16

LANE = 128
SUBLANE = 8
VMEM_LIMIT_BYTES = 56 * 1024 * 1024

RMS_EPS = 1e-6
GROUP_SIZE = 16
STATE_DIM = 64
HEAD_DIM = 64
WINDOW = 128
NUM_BUCKETS = 32
MAX_DISTANCE = WINDOW
GROUPS_PER_BLOCK = LANE // GROUP_SIZE
HALF = GROUPS_PER_BLOCK * STATE_DIM
N_SEG = SUBLANE
ROW_TILE = 512
NEG_INF = float("-inf")


def _params(n_axes=1):
    return pltpu.CompilerParams(dimension_semantics=("arbitrary",) * n_axes,
                                vmem_limit_bytes=VMEM_LIMIT_BYTES)


def _const_spec(shape):
    nd = len(shape)
    return pl.BlockSpec(shape, lambda *_: (0,) * nd, pipeline_mode=pl.Buffered(1))


def _rms(x, g):
    ms = jnp.mean(x * x, axis=-1, keepdims=True)
    return x * lax.rsqrt(ms + RMS_EPS) * g


def _aligned(row):
    return row if isinstance(row, int) else pl.multiple_of(row, SUBLANE)


def _sigmoid(x):
    return 1.0 / (1.0 + jnp.exp(-x))


def _ffn_kernel(x_ref, ga_ref, gb_ref, wgu_ref, wd_ref, o_ref, *, d_ff):
    x = x_ref[...]
    xn = _rms(x, ga_ref[...]).astype(BF16)
    gu = jnp.dot(xn, wgu_ref[...], preferred_element_type=F32)
    gate = gu[:, :d_ff]
    up = gu[:, d_ff:]
    h = (gate * _sigmoid(gate) * up).astype(BF16)
    y = jnp.dot(h, wd_ref[...], preferred_element_type=F32)
    o_ref[...] = x + 0.5 * _rms(y, gb_ref[...])


def _ffn(x, ga, gb, wgu, wd):
    t, d = x.shape
    d_ff = wd.shape[0]
    tm = min(ROW_TILE, t)
    row = pl.BlockSpec((tm, d), lambda i: (i, 0))
    return pl.pallas_call(
        functools.partial(_ffn_kernel, d_ff=d_ff),
        grid=(t // tm,),
        in_specs=[row, _const_spec((1, d)), _const_spec((1, d)),
                  _const_spec(wgu.shape), _const_spec(wd.shape)],
        out_specs=row,
        out_shape=jax.ShapeDtypeStruct((t, d), F32),
        compiler_params=_params(),
        name="ffn",
    )(x, ga, gb, wgu, wd)


def _proj_kernel(x_ref, g_ref, w_ref, b_ref, o_ref):
    xn = _rms(x_ref[...], g_ref[...]).astype(BF16)
    o_ref[...] = jnp.dot(xn, w_ref[...], preferred_element_type=F32) + b_ref[...]


def _norm_proj(x, g, w, b):
    t, d = x.shape
    n = w.shape[1]
    tm = min(ROW_TILE, t)
    return pl.pallas_call(
        _proj_kernel,
        grid=(t // tm,),
        in_specs=[pl.BlockSpec((tm, d), lambda i: (i, 0)), _const_spec((1, d)),
                  _const_spec(w.shape), _const_spec((1, n))],
        out_specs=pl.BlockSpec((tm, n), lambda i: (i, 0)),
        out_shape=jax.ShapeDtypeStruct((t, n), F32),
        compiler_params=_params(),
        name="norm_proj",
    )(x, g, w, b)


def _out_proj_kernel(x_ref, o_ref_in, g_ref, w_ref, b_ref, out_ref):
    a = jnp.dot(o_ref_in[...].astype(BF16), w_ref[...], preferred_element_type=F32) + b_ref[...]
    out_ref[...] = x_ref[...] + _rms(a, g_ref[...])


def _out_proj_residual(x, o, g, w, b):
    t, d = x.shape
    tm = min(ROW_TILE, t)
    row = pl.BlockSpec((tm, d), lambda i: (i, 0))
    return pl.pallas_call(
        _out_proj_kernel,
        grid=(t // tm,),
        in_specs=[row, pl.BlockSpec((tm, o.shape[1]), lambda i: (i, 0)), _const_spec((1, d)),
                  _const_spec(w.shape), _const_spec((1, d))],
        out_specs=row,
        out_shape=jax.ShapeDtypeStruct((t, d), F32),
        compiler_params=_params(),
        name="out_proj_residual",
    )(x, o, g, w, b)


def _complex_pow(re, im, n):
    out_re = out_im = None
    while n:
        if n & 1:
            if out_re is None:
                out_re, out_im = re, im
            else:
                out_re, out_im = out_re * re - out_im * im, out_re * im + out_im * re
        n >>= 1
        if n:
            re, im = re * re - im * im, 2.0 * re * im
    return out_re, out_im


def _discretize_kernel(lre_ref, lim_ref, ldt_ref, bre_ref, bim_ref, cre_ref, cim_ref,
                       wb_ref, wc_ref, lam_ref, lampow_ref, *, seg_len):
    a = lre_ref[0, 0]
    b = lim_ref[0, 0]
    dt = jnp.exp(ldt_ref[0, 0])
    mag = jnp.exp(a * dt)
    lbr = mag * jnp.cos(b * dt)
    lbi = mag * jnp.sin(b * dt)
    den = a * a + b * b
    cr = ((lbr - 1.0) * a + lbi * b) / den
    ci = (lbi * a - (lbr - 1.0) * b) / den
    bre = bre_ref[0, 0]
    bim = bim_ref[0, 0]
    wb_ref[0, 0, :, :HALF] = (cr * bre - ci * bim).astype(BF16)
    wb_ref[0, 0, :, HALF:] = (cr * bim + ci * bre).astype(BF16)
    wc_ref[0, 0, :HALF, :] = cre_ref[0, 0].astype(BF16)
    wc_ref[0, 0, HALF:, :] = (-cim_ref[0, 0]).astype(BF16)
    lam_ref[0, 0, :, :HALF] = lbr
    lam_ref[0, 0, :, HALF:] = lbi
    pr, pi = _complex_pow(lbr, lbi, seg_len)
    lampow_ref[0, 0, :, :HALF] = pr
    lampow_ref[0, 0, :, HALF:] = pi


def _block_diag_in(b):
    nl, g, p, h = b.shape
    nb = g // GROUPS_PER_BLOCK
    bt = b.transpose(0, 1, 3, 2).reshape(nl, nb, GROUPS_PER_BLOCK, h, p)
    eye = jnp.eye(GROUPS_PER_BLOCK, dtype=b.dtype)
    out = bt[:, :, :, :, None, :] * eye[None, None, :, None, :, None]
    return out.reshape(nl, nb, GROUPS_PER_BLOCK * h, GROUPS_PER_BLOCK * p)


def _block_diag_out(c):
    nl, g, h, p = c.shape
    nb = g // GROUPS_PER_BLOCK
    ct = c.transpose(0, 1, 3, 2).reshape(nl, nb, GROUPS_PER_BLOCK, p, h)
    eye = jnp.eye(GROUPS_PER_BLOCK, dtype=c.dtype)
    out = ct[:, :, :, :, None, :] * eye[None, None, :, None, :, None]
    return out.reshape(nl, nb, GROUPS_PER_BLOCK * p, GROUPS_PER_BLOCK * h)


def _discretize(lam_re, lam_im, log_dt, b_re, b_im, c_re, c_im, seg_len):
    nl, g, p = lam_re.shape
    nb = g // GROUPS_PER_BLOCK

    def rows(v):
        return v.reshape(nl, nb, 1, HALF)

    row_spec = pl.BlockSpec((1, 1, 1, HALF), lambda l, k: (l, k, 0, 0))
    in_spec = pl.BlockSpec((1, 1, LANE, HALF), lambda l, k: (l, k, 0, 0))
    out_spec = pl.BlockSpec((1, 1, HALF, LANE), lambda l, k: (l, k, 0, 0))
    row2_spec = pl.BlockSpec((1, 1, 1, 2 * HALF), lambda l, k: (l, k, 0, 0))
    return pl.pallas_call(
        functools.partial(_discretize_kernel, seg_len=seg_len),
        grid=(nl, nb),
        in_specs=[row_spec, row_spec, row_spec, in_spec, in_spec, out_spec, out_spec],
        out_specs=[pl.BlockSpec((1, 1, LANE, 2 * HALF), lambda l, k: (l, k, 0, 0)),
                   pl.BlockSpec((1, 1, 2 * HALF, LANE), lambda l, k: (l, k, 0, 0)),
                   row2_spec, row2_spec],
        out_shape=[jax.ShapeDtypeStruct((nl, nb, LANE, 2 * HALF), BF16),
                   jax.ShapeDtypeStruct((nl, nb, 2 * HALF, LANE), BF16),
                   jax.ShapeDtypeStruct((nl, nb, 1, 2 * HALF), F32),
                   jax.ShapeDtypeStruct((nl, nb, 1, 2 * HALF), F32)],
        compiler_params=_params(2),
        name="s5_discretize",
    )(rows(lam_re), rows(lam_im), rows(log_dt), _block_diag_in(b_re), _block_diag_in(b_im),
      _block_diag_out(c_re), _block_diag_out(c_im))


def _scan_block(bu_ref, state_ref, lam_row, k, n_grp, n_t, store):
    lre = jnp.broadcast_to(lam_row[:, :HALF], (SUBLANE, HALF))
    lim = jnp.broadcast_to(lam_row[:, HALF:], (SUBLANE, HALF))
    lane0 = k * 2 * HALF

    def group_body(grp, carry):
        srow = _aligned(grp * SUBLANE)
        sre = state_ref[pl.ds(srow, SUBLANE), lane0:lane0 + HALF]
        sim = state_ref[pl.ds(srow, SUBLANE), lane0 + HALF:lane0 + 2 * HALF]

        def step(t, st):
            xre, xim = st
            r0 = _aligned((grp * n_t + t) * SUBLANE)
            bre = bu_ref[pl.ds(r0, SUBLANE), :HALF]
            bim = bu_ref[pl.ds(r0, SUBLANE), HALF:]
            nre = lre * xre - lim * xim + bre
            nim = lre * xim + lim * xre + bim
            if store:
                bu_ref[pl.ds(r0, SUBLANE), :HALF] = nre
                bu_ref[pl.ds(r0, SUBLANE), HALF:] = nim
            return nre, nim

        sre, sim = lax.fori_loop(0, n_t, step, (sre, sim), unroll=min(n_t, 8))
        state_ref[pl.ds(srow, SUBLANE), lane0:lane0 + HALF] = sre
        state_ref[pl.ds(srow, SUBLANE), lane0 + HALF:lane0 + 2 * HALF] = sim
        return carry

    if n_grp == 1:
        group_body(0, 0)
    else:
        lax.fori_loop(0, n_grp, group_body, 0)


def _segment_starts(end_ref, lampow_ref, state_ref, n_blocks, seg_per_seq):
    row = lax.broadcasted_iota(jnp.int32, (N_SEG, HALF), 0)
    first = (row % seg_per_seq) == 0
    for k in range(n_blocks):
        lane0 = k * 2 * HALF
        ere = end_ref[:, lane0:lane0 + HALF]
        eim = end_ref[:, lane0 + HALF:lane0 + 2 * HALF]
        pre = lampow_ref[k][:, :HALF]
        pim = lampow_ref[k][:, HALF:]
        tre, tim = ere, eim
        sre = sim = None
        for _ in range(seg_per_seq - 1):
            sre = jnp.where(first, 0.0, pltpu.roll(tre, 1, axis=0))
            sim = jnp.where(first, 0.0, pltpu.roll(tim, 1, axis=0))
            tre = ere + pre * sre - pim * sim
            tim = eim + pre * sim + pim * sre
        if sre is None:
            sre = jnp.zeros_like(ere)
            sim = jnp.zeros_like(eim)
        state_ref[:, lane0:lane0 + HALF] = sre
        state_ref[:, lane0 + HALF:lane0 + 2 * HALF] = sim


def _s5_state_kernel(x_ref, g2_ref, wb_ref, lam_ref, end_ref, bu_ref, state_ref, *, n_blocks, n_t):
    step = pl.program_id(0)

    @pl.when(step == 0)
    def _():
        state_ref[...] = jnp.zeros_like(state_ref)

    u = _rms(x_ref[...], g2_ref[...]).astype(BF16)
    for k in range(n_blocks):
        bu_ref[...] = jnp.dot(u[:, k * LANE:(k + 1) * LANE], wb_ref[k], preferred_element_type=F32)
        _scan_block(bu_ref, state_ref, lam_ref[k], k, 1, n_t, store=False)

    @pl.when(step == pl.num_programs(0) - 1)
    def _():
        end_ref[...] = state_ref[...]


def _s5_mixer_kernel(x_ref, g2_ref, g3_ref, wb_ref, wc_ref, lam_ref, lampow_ref, d_ref, wglu_ref,
                     bglu_ref, init_ref, o_ref, final_ref, bu_ref, y_ref, state_ref,
                     *, n_blocks, n_grp, n_t, seg_per_seq, d_model):
    step = pl.program_id(0)

    @pl.when(step == 0)
    def _():
        if seg_per_seq:
            _segment_starts(init_ref, lampow_ref, state_ref, n_blocks, seg_per_seq)
        else:
            state_ref[...] = init_ref[...]

    x = x_ref[...]
    u = _rms(x, g2_ref[...])
    ub = u.astype(BF16)
    for k in range(n_blocks):
        bu_ref[...] = jnp.dot(ub[:, k * LANE:(k + 1) * LANE], wb_ref[k], preferred_element_type=F32)
        _scan_block(bu_ref, state_ref, lam_ref[k], k, n_grp, n_t, store=True)
        y_ref[:, k * LANE:(k + 1) * LANE] = jnp.dot(bu_ref[...].astype(BF16), wc_ref[k],
                                                    preferred_element_type=F32)
    y = y_ref[...] + d_ref[...] * u
    h = jnp.dot(jax.nn.gelu(y).astype(BF16), wglu_ref[...], preferred_element_type=F32) + bglu_ref[...]
    m = h[:, :d_model] * _sigmoid(h[:, d_model:])
    o_ref[...] = x + _rms(m, g3_ref[...])

    @pl.when(step == pl.num_programs(0) - 1)
    def _():
        final_ref[...] = state_ref[...]


def _s5_layer(x, g2, g3, wb, wc, lam, lampow, d, wglu, bglu, init, *, n_grp, n_t, seg_per_seq):
    t, dm = x.shape
    n_blocks = wb.shape[0]
    rows = n_grp * n_t * SUBLANE
    n_steps = t // rows
    n_state = n_blocks * 2 * HALF
    row = pl.BlockSpec((rows, dm), lambda i: (i, 0))
    vec = _const_spec((1, dm))
    state_spec = _const_spec((n_grp * SUBLANE, n_state))
    state_out = pl.BlockSpec((n_grp * SUBLANE, n_state), lambda i: (0, 0))
    if seg_per_seq:
        init = pl.pallas_call(
            functools.partial(_s5_state_kernel, n_blocks=n_blocks, n_t=n_t),
            grid=(n_steps,),
            in_specs=[row, vec, _const_spec(wb.shape), _const_spec(lam.shape)],
            out_specs=state_out,
            out_shape=jax.ShapeDtypeStruct((SUBLANE, n_state), F32),
            scratch_shapes=[pltpu.VMEM((rows, 2 * HALF), F32), pltpu.VMEM((SUBLANE, n_state), F32)],
            compiler_params=_params(),
            name="s5_segment_states",
        )(x, g2, wb, lam)
    return pl.pallas_call(
        functools.partial(_s5_mixer_kernel, n_blocks=n_blocks, n_grp=n_grp, n_t=n_t,
                          seg_per_seq=seg_per_seq, d_model=dm),
        grid=(n_steps,),
        in_specs=[row, vec, vec, _const_spec(wb.shape), _const_spec(wc.shape), _const_spec(lam.shape),
                  _const_spec(lampow.shape), vec, _const_spec(wglu.shape), _const_spec((1, 2 * dm)),
                  state_spec],
        out_specs=[row, state_out],
        out_shape=[jax.ShapeDtypeStruct((t, dm), F32),
                   jax.ShapeDtypeStruct((n_grp * SUBLANE, n_state), F32)],
        scratch_shapes=[pltpu.VMEM((rows, 2 * HALF), F32), pltpu.VMEM((rows, dm), F32),
                        pltpu.VMEM((n_grp * SUBLANE, n_state), F32)],
        compiler_params=_params(),
        name="s5_mixer",
    )(x, g2, g3, wb, wc, lam, lampow, d, wglu, bglu, init)


def _state_to_lanes(re, im):
    b = re.shape[0]
    re = re.reshape(b, -1, HALF)
    im = im.reshape(b, -1, HALF)
    return jnp.concatenate([re, im], axis=-1).reshape(b, -1)


def _lanes_to_state(s, n_groups):
    b = s.shape[0]
    s = s.reshape(b, -1, 2, HALF)
    return (s[:, :, 0].reshape(b, n_groups, STATE_DIM), s[:, :, 1].reshape(b, n_groups, STATE_DIM))


def _t5_bucket_np(dist):
    n = np.maximum(dist, 0)
    max_exact = NUM_BUCKETS // 2
    nf = np.maximum(n, 1).astype(np.float32)
    large = max_exact + (np.log(nf / np.float32(max_exact)) / np.float32(math.log(MAX_DISTANCE / max_exact))
                         * np.float32(NUM_BUCKETS - max_exact)).astype(np.int32)
    large = np.minimum(large, NUM_BUCKETS - 1)
    return np.where(n < max_exact, n, large).astype(np.int32)


def _bucket_table(n_q, n_k, q_offset, n_q_pad, n_k_pad, first_key=0):
    dist = (np.arange(n_q)[:, None] + q_offset) - np.arange(n_k)[None, :]
    valid = (dist >= 0) & (dist < WINDOW) & (np.arange(n_k)[None, :] >= first_key)
    table = np.full((n_q_pad, n_k_pad), -1, np.int32)
    table[:n_q, :n_k] = np.where(valid, _t5_bucket_np(dist), -1)
    return table


def _bias_kernel(rel_ref, idx_ref, o_ref):
    h = pl.program_id(0)
    idx = idx_ref[...]
    acc = jnp.full(idx.shape, NEG_INF, F32)
    for b in range(NUM_BUCKETS):
        acc = jnp.where(idx == b, rel_ref[b, h], acc)
    o_ref[0] = acc


def _bias_planes(rel_bias, table):
    n_heads = rel_bias.shape[1]
    r, c = table.shape
    return pl.pallas_call(
        _bias_kernel,
        grid=(n_heads,),
        in_specs=[pl.BlockSpec(memory_space=pltpu.SMEM), pl.BlockSpec((r, c), lambda h: (0, 0))],
        out_specs=pl.BlockSpec((1, r, c), lambda h: (h, 0, 0)),
        out_shape=jax.ShapeDtypeStruct((n_heads, r, c), F32),
        compiler_params=_params(),
        name="attn_bias",
    )(rel_bias, jnp.asarray(table))


def _pair_layout(planes, n_kv, n_q):
    h, _, n_k = planes.shape
    pairs = h // n_kv // 2
    p = planes[:, :n_q].reshape(n_kv, pairs, 2, n_q, n_k).transpose(0, 1, 3, 2, 4)
    return p.reshape(n_kv, pairs * n_q, 2 * n_k)


def _sink_rows(sinks, n_kv, n_q):
    h = sinks.shape[0]
    pairs = h // n_kv // 2
    s = sinks.reshape(n_kv, pairs, 1, 2)
    return jnp.broadcast_to(s, (n_kv, pairs, n_q, 2)).reshape(n_kv, pairs * n_q, 2)


def _attend(q_pairs, k_rows, v_rows, bias, sink, scale):
    n_k = k_rows.shape[0]
    k_odd = pltpu.roll(k_rows, HEAD_DIM, axis=1)
    v_odd = pltpu.roll(v_rows, HEAD_DIM, axis=1)
    ke = jnp.concatenate([k_rows, k_odd], axis=0).astype(BF16)
    ve = jnp.concatenate([v_rows, v_odd], axis=0).astype(BF16)
    s = lax.dot_general(q_pairs.astype(BF16), ke, (((1,), (1,)), ((), ())),
                        preferred_element_type=F32) * scale + bias
    probs, inv = [], []
    for par in range(2):
        sp = s[:, par * n_k:(par + 1) * n_k]
        sk = sink[:, par:par + 1]
        mx = jnp.maximum(jnp.max(sp, axis=-1, keepdims=True), sk)
        p = jnp.exp(sp - mx)
        den = jnp.sum(p, axis=-1, keepdims=True) + jnp.exp(sk - mx)
        probs.append(p.astype(BF16))
        inv.append(1.0 / den)
    o = jnp.dot(jnp.concatenate(probs, axis=1), ve, preferred_element_type=F32)
    lane = lax.broadcasted_iota(jnp.int32, o.shape, 1)
    return o * jnp.where(lane < HEAD_DIM, inv[0], inv[1])


def _kv_head_rows(kv, g):
    lane = lax.broadcasted_iota(jnp.int32, kv.shape, 1)
    if g == 1:
        kv = pltpu.roll(kv, HEAD_DIM, axis=1)
    return jnp.where(lane < HEAD_DIM, kv, 0.0)


def _prompt_attn_kernel(x_ref, kvc_ref, kvp_ref, bias_ref, sink_ref, g2_ref, g3_ref, wq_ref, bq_ref,
                        wo_ref, bo_ref, o_ref, *, n_kv, scale):
    x = x_ref[...]
    blk = x.shape[0]
    u = _rms(x, g2_ref[...]).astype(BF16)
    q = jnp.dot(u, wq_ref[...], preferred_element_type=F32) + bq_ref[...]
    kv = jnp.concatenate([kvp_ref[...], kvc_ref[...]], axis=0)
    pairs = q.shape[1] // LANE // n_kv
    outs = []
    for g in range(n_kv):
        k_rows = _kv_head_rows(kv[:, :n_kv * HEAD_DIM], g)
        v_rows = _kv_head_rows(kv[:, n_kv * HEAD_DIM:], g)
        qp = jnp.concatenate([q[:, (g * pairs + i) * LANE:(g * pairs + i + 1) * LANE]
                              for i in range(pairs)], axis=0)
        og = _attend(qp, k_rows, v_rows, bias_ref[0, g], sink_ref[g], scale)
        outs += [og[i * blk:(i + 1) * blk] for i in range(pairs)]
    o = jnp.concatenate(outs, axis=1).astype(BF16)
    a = jnp.dot(o, wo_ref[...], preferred_element_type=F32) + bo_ref[...]
    o_ref[...] = x + _rms(a, g3_ref[...])


def _prompt_attention(x, kv, bias, sink, g2, g3, wq, bq, wo, bo, *, blocks_per_seq, n_kv):
    t, d = x.shape
    blk = WINDOW
    assert n_kv * HEAD_DIM == LANE
    row = pl.BlockSpec((blk, d), lambda i: (i, 0))
    kvw = kv.shape[1]
    return pl.pallas_call(
        functools.partial(_prompt_attn_kernel, n_kv=n_kv, scale=1.0 / math.sqrt(HEAD_DIM)),
        grid=(t // blk,),
        in_specs=[row,
                  pl.BlockSpec((blk, kvw), lambda i: (i, 0)),
                  pl.BlockSpec((blk, kvw), lambda i: (jnp.maximum(i - 1, 0), 0)),
                  pl.BlockSpec((1,) + bias.shape[1:],
                               lambda i: (jnp.where(i % blocks_per_seq == 0, 1, 0), 0, 0, 0)),
                  _const_spec(sink.shape), _const_spec((1, d)), _const_spec((1, d)),
                  _const_spec(wq.shape), _const_spec((1, wq.shape[1])),
                  _const_spec(wo.shape), _const_spec((1, d))],
        out_specs=row,
        out_shape=jax.ShapeDtypeStruct((t, d), F32),
        compiler_params=_params(),
        name="prompt_attention",
    )(x, kv, kv, bias, sink, g2, g3, wq, bq, wo, bo)


def _sample_attn_kernel(q_ref, k_ref, v_ref, bias_ref, sink_ref, o_ref, *, n_kv, scale):
    for b in range(q_ref.shape[0]):
        k = k_ref[b]
        v = v_ref[b]
        for g in range(n_kv):
            o_ref[b, g] = _attend(q_ref[b, g], _kv_head_rows(k, g), _kv_head_rows(v, g),
                                  bias_ref[g], sink_ref[g], scale)


def _sample_attention(q_pairs, k_full, v_full, bias, sink, *, n_kv):
    nb, _, r, _ = q_pairs.shape
    n_k = k_full.shape[1]
    bb = SUBLANE
    return pl.pallas_call(
        functools.partial(_sample_attn_kernel, n_kv=n_kv, scale=1.0 / math.sqrt(HEAD_DIM)),
        grid=(nb // bb,),
        in_specs=[pl.BlockSpec((bb, n_kv, r, LANE), lambda i: (i, 0, 0, 0)),
                  pl.BlockSpec((bb, n_k, LANE), lambda i: (i, 0, 0)),
                  pl.BlockSpec((bb, n_k, LANE), lambda i: (i, 0, 0)),
                  _const_spec(bias.shape), _const_spec(sink.shape)],
        out_specs=pl.BlockSpec((bb, n_kv, r, LANE), lambda i: (i, 0, 0, 0)),
        out_shape=jax.ShapeDtypeStruct(q_pairs.shape, F32),
        compiler_params=_params(),
        name="sample_attention",
    )(q_pairs, k_full, v_full, bias, sink)


def kernel(x_prompt, x_sample, state_ssm_re, state_ssm_im, cache_win_k, cache_win_v, norm_g, ffn1_w_gu, ffn1_w_down, ffn2_w_gu, ffn2_w_down, ssm_lambda_re, ssm_lambda_im, ssm_log_dt, ssm_b_re, ssm_b_im, ssm_c_re, ssm_c_im, ssm_d, ssm_w_glu, ssm_b_glu, kv_norm_g, w_kv, b_kv, attn_w_q, attn_b_q, attn_sinks, attn_w_o, attn_b_o, rel_bias):
    bsz, seq, dm = x_prompt.shape
    dec_b, dec_s, _ = x_sample.shape
    depth = norm_g.shape[0]
    n_a = ssm_lambda_re.shape[0]
    n_groups = ssm_lambda_re.shape[1]
    n_heads = attn_sinks.shape[1]
    n_kv = cache_win_k.shape[2]
    n_past = cache_win_k.shape[1]
    pairs = n_heads // n_kv // 2
    seg_per_seq = N_SEG // bsz
    seg_len = seq // seg_per_seq
    n_t = ROW_TILE // N_SEG
    n_k_pad = 2 * WINDOW
    assert N_SEG % bsz == 0 and seq % (seg_per_seq * n_t) == 0 and seq % WINDOW == 0
    assert dec_b % SUBLANE == 0 and dec_b * dec_s == ROW_TILE and n_past + dec_s <= n_k_pad
    assert n_kv == 2 and dm == n_heads * HEAD_DIM and n_groups % GROUPS_PER_BLOCK == 0

    bf = lambda w: w.astype(BF16)
    row = lambda v: v.reshape(1, -1)
    wb, wc, lam, lampow = _discretize(ssm_lambda_re, ssm_lambda_im, ssm_log_dt, ssm_b_re, ssm_b_im,
                                      ssm_c_re, ssm_c_im, seg_len)
    f1gu, f1d, f2gu, f2d = bf(ffn1_w_gu), bf(ffn1_w_down), bf(ffn2_w_gu), bf(ffn2_w_down)
    wglu, wkv, wq, wo = bf(ssm_w_glu), bf(w_kv), bf(attn_w_q), bf(attn_w_o)

    tab_p = _bucket_table(WINDOW, 2 * WINDOW, WINDOW, WINDOW, 2 * WINDOW)
    tab_p0 = _bucket_table(WINDOW, 2 * WINDOW, WINDOW, WINDOW, 2 * WINDOW, first_key=WINDOW)
    tab_s = _bucket_table(dec_s, n_past + dec_s, n_past, SUBLANE, n_k_pad)
    bias_p = jnp.stack([_pair_layout(_bias_planes(rel_bias, tab_p), n_kv, WINDOW),
                        _pair_layout(_bias_planes(rel_bias, tab_p0), n_kv, WINDOW)])
    bias_s = _pair_layout(_bias_planes(rel_bias, tab_s), n_kv, dec_s)

    xp = x_prompt.reshape(bsz, seg_per_seq, seg_len, dm).transpose(2, 0, 1, 3).reshape(bsz * seq, dm)
    xs = x_sample.reshape(dec_b // SUBLANE, SUBLANE, dec_s, dm).transpose(0, 2, 1, 3).reshape(dec_b * dec_s, dm)

    ends_p, ends_s = [], []
    kv_p = kv_s = k_full = v_full = None
    for l in range(depth):
        g = [row(norm_g[l, i]) for i in range(norm_g.shape[1])]
        if l == n_a:
            xp = xp.reshape(seg_len, bsz, seg_per_seq, dm).transpose(1, 2, 0, 3).reshape(bsz * seq, dm)
            xs = xs.reshape(dec_b // SUBLANE, dec_s, SUBLANE, dm).transpose(0, 2, 1, 3).reshape(dec_b * dec_s, dm)
            kv_p = _norm_proj(xp, row(kv_norm_g), wkv, row(b_kv))
            kv_s = _norm_proj(xs, row(kv_norm_g), wkv, row(b_kv))
            half = n_kv * HEAD_DIM
            kv_s3 = kv_s.reshape(dec_b, dec_s, 2 * half)
            k_full = jnp.concatenate([cache_win_k.reshape(dec_b, n_past, half), kv_s3[:, :, :half]], axis=1)
            v_full = jnp.concatenate([cache_win_v.reshape(dec_b, n_past, half), kv_s3[:, :, half:]], axis=1)
        xp = _ffn(xp, g[0], g[1], f1gu[l], f1d[l])
        xs = _ffn(xs, g[0], g[1], f1gu[l], f1d[l])
        if l < n_a:
            common = (g[2], g[3], wb[l], wc[l], lam[l], lampow[l], row(ssm_d[l]), wglu[l], row(ssm_b_glu[l]))
            xp, fin_p = _s5_layer(xp, *common, jnp.zeros((SUBLANE, lam.shape[1] * 2 * HALF), F32),
                                  n_grp=1, n_t=n_t, seg_per_seq=seg_per_seq)
            xs, fin_s = _s5_layer(xs, *common, _state_to_lanes(state_ssm_re[l], state_ssm_im[l]),
                                  n_grp=dec_b // SUBLANE, n_t=dec_s, seg_per_seq=0)
            ends_p.append(_lanes_to_state(fin_p[seg_per_seq - 1::seg_per_seq], n_groups))
            ends_s.append(_lanes_to_state(fin_s, n_groups))
        else:
            bl = l - n_a
            wq_l, bq_l, wo_l, bo_l = wq[bl], row(attn_b_q[bl]), wo[bl], row(attn_b_o[bl])
            xp = _prompt_attention(xp, kv_p, bias_p, _sink_rows(attn_sinks[bl], n_kv, WINDOW), g[2], g[3],
                                   wq_l, bq_l, wo_l, bo_l, blocks_per_seq=seq // WINDOW, n_kv=n_kv)
            q = _norm_proj(xs, g[2], wq_l, bq_l)
            q = q.reshape(dec_b, dec_s, n_kv, pairs, LANE).transpose(0, 2, 3, 1, 4)
            q = q.reshape(dec_b, n_kv, pairs * dec_s, LANE)
            pad = n_k_pad - (n_past + dec_s)
            o = _sample_attention(q, jnp.pad(k_full, ((0, 0), (0, pad), (0, 0))),
                                  jnp.pad(v_full, ((0, 0), (0, pad), (0, 0))),
                                  bias_s, _sink_rows(attn_sinks[bl], n_kv, dec_s), n_kv=n_kv)
            o = o.reshape(dec_b, n_kv, pairs, dec_s, LANE).transpose(0, 3, 1, 2, 4).reshape(dec_b * dec_s, dm)
            xs = _out_proj_residual(xs, o, g[3], wo_l, bo_l)
        xp = _ffn(xp, g[4], g[5], f2gu[l], f2d[l])
        xs = _ffn(xs, g[4], g[5], f2gu[l], f2d[l])

    half = n_kv * HEAD_DIM
    kv_p3 = kv_p.reshape(bsz, seq, 2 * half)
    new_k_p = kv_p3[:, -WINDOW:, :half].reshape(bsz, WINDOW, n_kv, HEAD_DIM)
    new_v_p = kv_p3[:, -WINDOW:, half:].reshape(bsz, WINDOW, n_kv, HEAD_DIM)
    new_k_s = k_full[:, -WINDOW:].reshape(dec_b, WINDOW, n_kv, HEAD_DIM)
    new_v_s = v_full[:, -WINDOW:].reshape(dec_b, WINDOW, n_kv, HEAD_DIM)
    return (xp.reshape(bsz, seq, dm), xs.reshape(dec_b, dec_s, dm),
            jnp.stack([e[0] for e in ends_p]), jnp.stack([e[1] for e in ends_p]),
            new_k_p, new_v_p,
            jnp.stack([e[0] for e in ends_s]), jnp.stack([e[1] for e in ends_s]),
            new_k_s, new_v_s)
```

```python
import functools
import math

import numpy as np
import jax
import jax.numpy as jnp
from jax import lax
from jax.experimental import pallas as pl
from jax.experimental.pallas import tpu as pltpu

F32 = jnp.float32
BF16 = jnp.bfloat16

LANE = 128
SUBLANE = 8
VMEM_LIMIT_BYTES = 56 * 1024 * 1024

RMS_EPS = 1e-6
GROUP_SIZE = 16
STATE_DIM = 64
HEAD_DIM = 64
WINDOW = 128
NUM_BUCKETS = 32
MAX_DISTANCE = WINDOW
GROUPS_PER_BLOCK = LANE // GROUP_SIZE
HALF = GROUPS_PER_BLOCK * STATE_DIM
N_SLABS = 2 * HALF // LANE
N_PAIRS = N_SLABS // 2
ROW_TILE = 512
NEG_INF = float("-inf")


def _params(n_axes=1):
    return pltpu.CompilerParams(dimension_semantics=("arbitrary",) * n_axes,
                                vmem_limit_bytes=VMEM_LIMIT_BYTES)


def _const_spec(shape):
    nd = len(shape)
    return pl.BlockSpec(shape, lambda *_: (0,) * nd, pipeline_mode=pl.Buffered(1))


def _layer_spec(shape, layer):
    nd = len(shape)
    return pl.BlockSpec((None,) + tuple(shape[1:]), lambda *_: (layer,) + (0,) * (nd - 1),
                        pipeline_mode=pl.Buffered(1))


def _rms(x, g):
    ms = jnp.mean(x * x, axis=-1, keepdims=True)
    return x * lax.rsqrt(ms + RMS_EPS) * g


def _aligned(row):
    return row if isinstance(row, int) else pl.multiple_of(row, SUBLANE)


def _sigmoid(x):
    return 1.0 / (1.0 + jnp.exp(-x))


def _ffn_kernel(x_ref, ga_ref, gb_ref, wgu_ref, wd_ref, o_ref, *, d_ff):
    x = x_ref[...]
    xn = _rms(x, ga_ref[...]).astype(BF16)
    gu = jnp.dot(xn, wgu_ref[...], preferred_element_type=F32)
    gate = gu[:, :d_ff]
    up = gu[:, d_ff:]
    h = (gate * _sigmoid(gate) * up).astype(BF16)
    y = jnp.dot(h, wd_ref[...], preferred_element_type=F32)
    o_ref[...] = x + 0.5 * _rms(y, gb_ref[...])


def _ffn(x, ga, gb, wgu, wd, layer):
    t, d = x.shape
    d_ff = wd.shape[1]
    tm = min(ROW_TILE, t)
    row = pl.BlockSpec((tm, d), lambda i: (i, 0))
    return pl.pallas_call(
        functools.partial(_ffn_kernel, d_ff=d_ff),
        grid=(t // tm,),
        in_specs=[row, _const_spec((1, d)), _const_spec((1, d)),
                  _layer_spec(wgu.shape, layer), _layer_spec(wd.shape, layer)],
        out_specs=row,
        out_shape=jax.ShapeDtypeStruct((t, d), F32),
        compiler_params=_params(),
        name="ffn",
    )(x, ga, gb, wgu, wd)


def _proj_kernel(x_ref, g_ref, w_ref, b_ref, o_ref):
    xn = _rms(x_ref[...], g_ref[...]).astype(BF16)
    o_ref[...] = jnp.dot(xn, w_ref[...], preferred_element_type=F32) + b_ref[...]


def _norm_proj(x, g, w, b, layer):
    t, d = x.shape
    n = w.shape[-1]
    tm = min(ROW_TILE, t)
    return pl.pallas_call(
        _proj_kernel,
        grid=(t // tm,),
        in_specs=[pl.BlockSpec((tm, d), lambda i: (i, 0)), _const_spec((1, d)),
                  _layer_spec(w.shape, layer), _const_spec((1, n))],
        out_specs=pl.BlockSpec((tm, n), lambda i: (i, 0)),
        out_shape=jax.ShapeDtypeStruct((t, n), F32),
        compiler_params=_params(),
        name="norm_proj",
    )(x, g, w, b)


def _out_proj_kernel(x_ref, o_ref_in, g_ref, w_ref, b_ref, out_ref):
    a = jnp.dot(o_ref_in[...].astype(BF16), w_ref[...], preferred_element_type=F32) + b_ref[...]
    out_ref[...] = x_ref[...] + _rms(a, g_ref[...])


def _out_proj_residual(x, o, g, w, b, layer):
    t, d = x.shape
    tm = min(ROW_TILE, t)
    row = pl.BlockSpec((tm, d), lambda i: (i, 0))
    return pl.pallas_call(
        _out_proj_kernel,
        grid=(t // tm,),
        in_specs=[row, pl.BlockSpec((tm, o.shape[1]), lambda i: (i, 0)), _const_spec((1, d)),
                  _layer_spec(w.shape, layer), _const_spec((1, d))],
        out_specs=row,
        out_shape=jax.ShapeDtypeStruct((t, d), F32),
        compiler_params=_params(),
        name="out_proj_residual",
    )(x, o, g, w, b)


def _discretize_kernel(lre_ref, lim_ref, ldt_ref, bre_ref, bim_ref, cre_ref, cim_ref,
                       wb_ref, wc_ref, lam_ref):
    a = lre_ref[0, 0]
    b = lim_ref[0, 0]
    dt = jnp.exp(ldt_ref[0, 0])
    mag = jnp.exp(a * dt)
    lbr = mag * jnp.cos(b * dt)
    lbi = mag * jnp.sin(b * dt)
    den = a * a + b * b
    cr = ((lbr - 1.0) * a + lbi * b) / den
    ci = (lbi * a - (lbr - 1.0) * b) / den
    bre = bre_ref[0, 0]
    bim = bim_ref[0, 0]
    wb_ref[0, 0, :, :HALF] = (cr * bre - ci * bim).astype(BF16)
    wb_ref[0, 0, :, HALF:] = (cr * bim + ci * bre).astype(BF16)
    wc_ref[0, 0, :HALF, :] = cre_ref[0, 0].astype(BF16)
    wc_ref[0, 0, HALF:, :] = (-cim_ref[0, 0]).astype(BF16)
    lam_ref[0, 0, :, :HALF] = lbr
    lam_ref[0, 0, :, HALF:] = lbi


def _block_diag_in(b):
    nl, g, p, h = b.shape
    nb = g // GROUPS_PER_BLOCK
    bt = b.transpose(0, 1, 3, 2).reshape(nl, nb, GROUPS_PER_BLOCK, h, p)
    eye = jnp.eye(GROUPS_PER_BLOCK, dtype=b.dtype)
    out = bt[:, :, :, :, None, :] * eye[None, None, :, None, :, None]
    return out.reshape(nl, nb, GROUPS_PER_BLOCK * h, GROUPS_PER_BLOCK * p)


def _block_diag_out(c):
    nl, g, h, p = c.shape
    nb = g // GROUPS_PER_BLOCK
    ct = c.transpose(0, 1, 3, 2).reshape(nl, nb, GROUPS_PER_BLOCK, p, h)
    eye = jnp.eye(GROUPS_PER_BLOCK, dtype=c.dtype)
    out = ct[:, :, :, :, None, :] * eye[None, None, :, None, :, None]
    return out.reshape(nl, nb, GROUPS_PER_BLOCK * p, GROUPS_PER_BLOCK * h)


def _discretize(lam_re, lam_im, log_dt, b_re, b_im, c_re, c_im):
    nl, g, p = lam_re.shape
    nb = g // GROUPS_PER_BLOCK

    def rows(v):
        return v.reshape(nl, nb, 1, HALF)

    row_spec = pl.BlockSpec((1, 1, 1, HALF), lambda l, k: (l, k, 0, 0))
    in_spec = pl.BlockSpec((1, 1, LANE, HALF), lambda l, k: (l, k, 0, 0))
    out_spec = pl.BlockSpec((1, 1, HALF, LANE), lambda l, k: (l, k, 0, 0))
    wb, wc, lam = pl.pallas_call(
        _discretize_kernel,
        grid=(nl, nb),
        in_specs=[row_spec, row_spec, row_spec, in_spec, in_spec, out_spec, out_spec],
        out_specs=[pl.BlockSpec((1, 1, LANE, 2 * HALF), lambda l, k: (l, k, 0, 0)),
                   pl.BlockSpec((1, 1, 2 * HALF, LANE), lambda l, k: (l, k, 0, 0)),
                   pl.BlockSpec((1, 1, 1, 2 * HALF), lambda l, k: (l, k, 0, 0))],
        out_shape=[jax.ShapeDtypeStruct((nl, nb, LANE, 2 * HALF), BF16),
                   jax.ShapeDtypeStruct((nl, nb, 2 * HALF, LANE), BF16),
                   jax.ShapeDtypeStruct((nl, nb, 1, 2 * HALF), F32)],
        compiler_params=_params(2),
        name="s5_discretize",
    )(rows(lam_re), rows(lam_im), rows(log_dt), _block_diag_in(b_re), _block_diag_in(b_im),
      _block_diag_out(c_re), _block_diag_out(c_im))
    return wb, wc, lam.reshape(nl, nb, 2 * HALF)


def _s5_scan(s_ref, state_ref, lam_ref, n_seq, n_t):
    lre = [lam_ref[:, p * LANE:(p + 1) * LANE] for p in range(N_PAIRS)]
    lim = [lam_ref[:, HALF + p * LANE:HALF + (p + 1) * LANE] for p in range(N_PAIRS)]

    def seq_body(seq, carry):
        st0 = state_ref[seq]
        xre = tuple(st0[:, p * LANE:(p + 1) * LANE] for p in range(N_PAIRS))
        xim = tuple(st0[:, HALF + p * LANE:HALF + (p + 1) * LANE] for p in range(N_PAIRS))

        def step(t, st):
            xre, xim = st
            r0 = _aligned((seq * n_t + t) * SUBLANE)
            nre, nim = [], []
            for p in range(N_PAIRS):
                bre = s_ref[p, pl.ds(r0, SUBLANE), :]
                bim = s_ref[N_PAIRS + p, pl.ds(r0, SUBLANE), :]
                re = lre[p] * xre[p] - lim[p] * xim[p] + bre
                im = lre[p] * xim[p] + lim[p] * xre[p] + bim
                s_ref[p, pl.ds(r0, SUBLANE), :] = re
                s_ref[N_PAIRS + p, pl.ds(r0, SUBLANE), :] = im
                nre.append(re)
                nim.append(im)
            return tuple(nre), tuple(nim)

        xre, xim = lax.fori_loop(0, n_t, step, (xre, xim), unroll=min(n_t, 8))
        state_ref[seq] = jnp.concatenate(list(xre) + list(xim), axis=1)
        return carry

    if n_seq == 1:
        seq_body(0, 0)
    else:
        lax.fori_loop(0, n_seq, seq_body, 0)


def _s5_mixer_kernel(x_ref, g2_ref, g3_ref, wb_ref, wc_ref, lam_ref, d_ref, wglu_ref, bglu_ref,
                     init_ref, o_ref, final_ref, s_ref, y_ref, state_ref,
                     *, n_blocks, n_seq, n_t, d_model):
    tile = pl.program_id(1)
    rows = n_seq * n_t

    @pl.when(tile == 0)
    def _():
        state_ref[...] = init_ref[...]

    x = x_ref[0]
    u = _rms(x, g2_ref[...])
    ub = u.astype(BF16)
    for k in range(n_blocks):
        bu = jnp.dot(ub[:, k * LANE:(k + 1) * LANE], wb_ref[k], preferred_element_type=F32)
        for j in range(N_SLABS):
            s_ref[j, pl.ds(k, rows, stride=n_blocks), :] = bu[:, j * LANE:(j + 1) * LANE]
    _s5_scan(s_ref, state_ref, lam_ref, n_seq, n_t)
    for k in range(n_blocks):
        xs = jnp.concatenate([s_ref[j, pl.ds(k, rows, stride=n_blocks), :] for j in range(N_SLABS)], axis=1)
        y_ref[:, k * LANE:(k + 1) * LANE] = jnp.dot(xs.astype(BF16), wc_ref[k], preferred_element_type=F32)
    y = y_ref[...] + d_ref[...] * u
    h = jnp.dot(jax.nn.gelu(y).astype(BF16), wglu_ref[...], preferred_element_type=F32) + bglu_ref[...]
    m = h[:, :d_model] * _sigmoid(h[:, d_model:])
    o_ref[0] = x + _rms(m, g3_ref[...])

    @pl.when(tile == pl.num_programs(1) - 1)
    def _():
        final_ref[...] = state_ref[...]


def _s5_layer(x, g2, g3, wb, wc, lam, d, wglu, bglu, init, layer, *, n_seq, n_t):
    nb_rows, s, dm = x.shape
    n_blocks = wb.shape[1]
    assert n_blocks == SUBLANE
    rows = n_seq * n_t
    n_tiles = s // rows
    assert n_seq == 1 or n_tiles == 1
    row = pl.BlockSpec((1, rows, dm), lambda b, i: (b, i, 0))
    vec = _const_spec((1, dm))
    state = pl.BlockSpec((n_seq, SUBLANE, 2 * HALF), lambda b, i: (b, 0, 0))
    return pl.pallas_call(
        functools.partial(_s5_mixer_kernel, n_blocks=n_blocks, n_seq=n_seq, n_t=n_t, d_model=dm),
        grid=(nb_rows, n_tiles),
        in_specs=[row, vec, vec, _layer_spec(wb.shape, layer), _layer_spec(wc.shape, layer),
                  _layer_spec(lam.shape, layer), vec, _layer_spec(wglu.shape, layer),
                  _const_spec((1, 2 * dm)), state],
        out_specs=[row, state],
        out_shape=[jax.ShapeDtypeStruct(x.shape, F32),
                   jax.ShapeDtypeStruct(init.shape, F32)],
        scratch_shapes=[pltpu.VMEM((N_SLABS, rows * SUBLANE, LANE), F32), pltpu.VMEM((rows, dm), F32),
                        pltpu.VMEM((n_seq, SUBLANE, 2 * HALF), F32)],
        compiler_params=_params(2),
        name="s5_mixer",
    )(x, g2, g3, wb, wc, lam, d, wglu, bglu, init)


def _state_to_rows(re, im):
    b = re.shape[0]
    return jnp.concatenate([re.reshape(b, -1, HALF), im.reshape(b, -1, HALF)], axis=-1)


def _rows_to_state(s, n_groups):
    b = s.shape[0]
    return (s[:, :, :HALF].reshape(b, n_groups, STATE_DIM), s[:, :, HALF:].reshape(b, n_groups, STATE_DIM))


def _t5_bucket_np(dist):
    n = np.maximum(dist, 0)
    max_exact = NUM_BUCKETS // 2
    nf = np.maximum(n, 1).astype(np.float32)
    large = max_exact + (np.log(nf / np.float32(max_exact)) / np.float32(math.log(MAX_DISTANCE / max_exact))
                         * np.float32(NUM_BUCKETS - max_exact)).astype(np.int32)
    large = np.minimum(large, NUM_BUCKETS - 1)
    return np.where(n < max_exact, n, large).astype(np.int32)


def _bucket_table(n_q, n_k, q_offset, n_q_pad, n_k_pad, first_key=0):
    dist = (np.arange(n_q)[:, None] + q_offset) - np.arange(n_k)[None, :]
    valid = (dist >= 0) & (dist < WINDOW) & (np.arange(n_k)[None, :] >= first_key)
    table = np.full((n_q_pad, n_k_pad), -1, np.int32)
    table[:n_q, :n_k] = np.where(valid, _t5_bucket_np(dist), -1)
    return table


def _bias_kernel(rel_ref, idx_ref, o_ref):
    h = pl.program_id(0)
    idx = idx_ref[...]
    acc = jnp.full(idx.shape, NEG_INF, F32)
    for b in range(NUM_BUCKETS):
        acc = jnp.where(idx == b, rel_ref[b, h], acc)
    o_ref[0] = acc


def _bias_planes(rel_bias, table):
    n_heads = rel_bias.shape[1]
    r, c = table.shape
    return pl.pallas_call(
        _bias_kernel,
        grid=(n_heads,),
        in_specs=[pl.BlockSpec(memory_space=pltpu.SMEM), pl.BlockSpec((r, c), lambda h: (0, 0))],
        out_specs=pl.BlockSpec((1, r, c), lambda h: (h, 0, 0)),
        out_shape=jax.ShapeDtypeStruct((n_heads, r, c), F32),
        compiler_params=_params(),
        name="attn_bias",
    )(rel_bias, jnp.asarray(table))


def _pair_layout(planes, n_kv, n_q):
    h, _, n_k = planes.shape
    pairs = h // n_kv // 2
    p = planes[:, :n_q].reshape(n_kv, pairs, 2, n_q, n_k).transpose(0, 1, 3, 2, 4)
    return p.reshape(n_kv, pairs * n_q, 2 * n_k)


def _sink_rows(sinks, n_kv, n_q):
    h = sinks.shape[0]
    pairs = h // n_kv // 2
    s = sinks.reshape(n_kv, pairs, 1, 2)
    return jnp.broadcast_to(s, (n_kv, pairs, n_q, 2)).reshape(n_kv, pairs * n_q, 2)


def _attend(q_pairs, k_rows, v_rows, bias, sink, scale):
    n_k = k_rows.shape[0]
    k_odd = pltpu.roll(k_rows, HEAD_DIM, axis=1)
    v_odd = pltpu.roll(v_rows, HEAD_DIM, axis=1)
    ke = jnp.concatenate([k_rows, k_odd], axis=0).astype(BF16)
    ve = jnp.concatenate([v_rows, v_odd], axis=0).astype(BF16)
    s = lax.dot_general(q_pairs.astype(BF16), ke, (((1,), (1,)), ((), ())),
                        preferred_element_type=F32) * scale + bias
    probs, inv = [], []
    for par in range(2):
        sp = s[:, par * n_k:(par + 1) * n_k]
        sk = sink[:, par:par + 1]
        mx = jnp.maximum(jnp.max(sp, axis=-1, keepdims=True), sk)
        p = jnp.exp(sp - mx)
        den = jnp.sum(p, axis=-1, keepdims=True) + jnp.exp(sk - mx)
        probs.append(p.astype(BF16))
        inv.append(1.0 / den)
    o = jnp.dot(jnp.concatenate(probs, axis=1), ve, preferred_element_type=F32)
    lane = lax.broadcasted_iota(jnp.int32, o.shape, 1)
    return o * jnp.where(lane < HEAD_DIM, inv[0], inv[1])


def _kv_head_rows(kv, g):
    lane = lax.broadcasted_iota(jnp.int32, kv.shape, 1)
    if g == 1:
        kv = pltpu.roll(kv, HEAD_DIM, axis=1)
    return jnp.where(lane < HEAD_DIM, kv, 0.0)


def _prompt_attn_kernel(x_ref, kvc_ref, kvp_ref, bias_ref, sink_ref, g2_ref, g3_ref, wq_ref, bq_ref,
                        wo_ref, bo_ref, o_ref, *, n_kv, scale):
    x = x_ref[...]
    blk = x.shape[0]
    u = _rms(x, g2_ref[...]).astype(BF16)
    q = jnp.dot(u, wq_ref[...], preferred_element_type=F32) + bq_ref[...]
    kv = jnp.concatenate([kvp_ref[...], kvc_ref[...]], axis=0)
    pairs = q.shape[1] // LANE // n_kv
    outs = []
    for g in range(n_kv):
        k_rows = _kv_head_rows(kv[:, :n_kv * HEAD_DIM], g)
        v_rows = _kv_head_rows(kv[:, n_kv * HEAD_DIM:], g)
        qp = jnp.concatenate([q[:, (g * pairs + i) * LANE:(g * pairs + i + 1) * LANE]
                              for i in range(pairs)], axis=0)
        og = _attend(qp, k_rows, v_rows, bias_ref[0, g], sink_ref[g], scale)
        outs += [og[i * blk:(i + 1) * blk] for i in range(pairs)]
    o = jnp.concatenate(outs, axis=1).astype(BF16)
    a = jnp.dot(o, wo_ref[...], preferred_element_type=F32) + bo_ref[...]
    o_ref[...] = x + _rms(a, g3_ref[...])


def _prompt_attention(x, kv, bias, sink, g2, g3, wq, bq, wo, bo, layer, *, blocks_per_seq, n_kv):
    t, d = x.shape
    blk = WINDOW
    assert n_kv * HEAD_DIM == LANE
    row = pl.BlockSpec((blk, d), lambda i: (i, 0))
    kvw = kv.shape[1]
    return pl.pallas_call(
        functools.partial(_prompt_attn_kernel, n_kv=n_kv, scale=1.0 / math.sqrt(HEAD_DIM)),
        grid=(t // blk,),
        in_specs=[row,
                  pl.BlockSpec((blk, kvw), lambda i: (i, 0)),
                  pl.BlockSpec((blk, kvw), lambda i: (jnp.maximum(i - 1, 0), 0)),
                  pl.BlockSpec((1,) + bias.shape[1:],
                               lambda i: (jnp.where(i % blocks_per_seq == 0, 1, 0), 0, 0, 0)),
                  _const_spec(sink.shape), _const_spec((1, d)), _const_spec((1, d)),
                  _layer_spec(wq.shape, layer), _const_spec((1, wq.shape[-1])),
                  _layer_spec(wo.shape, layer), _const_spec((1, d))],
        out_specs=row,
        out_shape=jax.ShapeDtypeStruct((t, d), F32),
        compiler_params=_params(),
        name="prompt_attention",
    )(x, kv, kv, bias, sink, g2, g3, wq, bq, wo, bo)


def _sample_attn_kernel(q_ref, k_ref, v_ref, bias_ref, sink_ref, o_ref, *, n_kv, scale):
    for b in range(q_ref.shape[0]):
        k = k_ref[b]
        v = v_ref[b]
        for g in range(n_kv):
            o_ref[b, g] = _attend(q_ref[b, g], _kv_head_rows(k, g), _kv_head_rows(v, g),
                                  bias_ref[g], sink_ref[g], scale)


def _sample_attention(q_pairs, k_full, v_full, bias, sink, *, n_kv):
    nb, _, r, _ = q_pairs.shape
    n_k = k_full.shape[1]
    bb = SUBLANE
    return pl.pallas_call(
        functools.partial(_sample_attn_kernel, n_kv=n_kv, scale=1.0 / math.sqrt(HEAD_DIM)),
        grid=(nb // bb,),
        in_specs=[pl.BlockSpec((bb, n_kv, r, LANE), lambda i: (i, 0, 0, 0)),
                  pl.BlockSpec((bb, n_k, LANE), lambda i: (i, 0, 0)),
                  pl.BlockSpec((bb, n_k, LANE), lambda i: (i, 0, 0)),
                  _const_spec(bias.shape), _const_spec(sink.shape)],
        out_specs=pl.BlockSpec((bb, n_kv, r, LANE), lambda i: (i, 0, 0, 0)),
        out_shape=jax.ShapeDtypeStruct(q_pairs.shape, F32),
        compiler_params=_params(),
        name="sample_attention",
    )(q_pairs, k_full, v_full, bias, sink)


def kernel(x_prompt, x_sample, state_ssm_re, state_ssm_im, cache_win_k, cache_win_v, norm_g, ffn1_w_gu, ffn1_w_down, ffn2_w_gu, ffn2_w_down, ssm_lambda_re, ssm_lambda_im, ssm_log_dt, ssm_b_re, ssm_b_im, ssm_c_re, ssm_c_im, ssm_d, ssm_w_glu, ssm_b_glu, kv_norm_g, w_kv, b_kv, attn_w_q, attn_b_q, attn_sinks, attn_w_o, attn_b_o, rel_bias):
    bsz, seq, dm = x_prompt.shape
    dec_b, dec_s, _ = x_sample.shape
    depth = norm_g.shape[0]
    n_a = ssm_lambda_re.shape[0]
    n_groups = ssm_lambda_re.shape[1]
    n_heads = attn_sinks.shape[1]
    n_kv = cache_win_k.shape[2]
    n_past = cache_win_k.shape[1]
    pairs = n_heads // n_kv // 2
    n_k_pad = 2 * WINDOW
    assert seq % ROW_TILE == 0 and seq % WINDOW == 0
    assert dec_b * dec_s == ROW_TILE and n_past + dec_s <= n_k_pad and dec_b % SUBLANE == 0
    assert n_kv == 2 and dm == n_heads * HEAD_DIM and n_groups == SUBLANE * GROUPS_PER_BLOCK

    bf = lambda w: w.astype(BF16)
    row = lambda v: v.reshape(1, -1)
    wb, wc, lam = _discretize(ssm_lambda_re, ssm_lambda_im, ssm_log_dt, ssm_b_re, ssm_b_im,
                              ssm_c_re, ssm_c_im)
    f1gu, f1d, f2gu, f2d = bf(ffn1_w_gu), bf(ffn1_w_down), bf(ffn2_w_gu), bf(ffn2_w_down)
    wglu, wkv, wq, wo = bf(ssm_w_glu), bf(w_kv)[None], bf(attn_w_q), bf(attn_w_o)

    tab_p = _bucket_table(WINDOW, 2 * WINDOW, WINDOW, WINDOW, 2 * WINDOW)
    tab_p0 = _bucket_table(WINDOW, 2 * WINDOW, WINDOW, WINDOW, 2 * WINDOW, first_key=WINDOW)
    tab_s = _bucket_table(dec_s, n_past + dec_s, n_past, SUBLANE, n_k_pad)
    bias_p = jnp.stack([_pair_layout(_bias_planes(rel_bias, tab_p), n_kv, WINDOW),
                        _pair_layout(_bias_planes(rel_bias, tab_p0), n_kv, WINDOW)])
    bias_s = _pair_layout(_bias_planes(rel_bias, tab_s), n_kv, dec_s)

    xp = x_prompt.reshape(bsz * seq, dm)
    xs = x_sample.reshape(dec_b * dec_s, dm)
    zero_state = jnp.zeros((bsz, SUBLANE, 2 * HALF), F32)

    ends_p, ends_s = [], []
    kv_p = k_full = v_full = None
    half = n_kv * HEAD_DIM
    for l in range(depth):
        g = [row(norm_g[l, i]) for i in range(norm_g.shape[1])]
        if l == n_a:
            kv_p = _norm_proj(xp, row(kv_norm_g), wkv, row(b_kv), 0)
            kv_s = _norm_proj(xs, row(kv_norm_g), wkv, row(b_kv), 0).reshape(dec_b, dec_s, 2 * half)
            k_full = jnp.concatenate([cache_win_k.reshape(dec_b, n_past, half), kv_s[:, :, :half]], axis=1)
            v_full = jnp.concatenate([cache_win_v.reshape(dec_b, n_past, half), kv_s[:, :, half:]], axis=1)
        xp = _ffn(xp, g[0], g[1], f1gu, f1d, l)
        xs = _ffn(xs, g[0], g[1], f1gu, f1d, l)
        if l < n_a:
            common = (g[2], g[3], wb, wc, lam, row(ssm_d[l]), wglu, row(ssm_b_glu[l]))
            xp3, fin_p = _s5_layer(xp.reshape(bsz, seq, dm), *common, zero_state, l, n_seq=1, n_t=ROW_TILE)
            xs3, fin_s = _s5_layer(xs.reshape(1, dec_b * dec_s, dm), *common,
                                   _state_to_rows(state_ssm_re[l], state_ssm_im[l]), l,
                                   n_seq=dec_b, n_t=dec_s)
            xp, xs = xp3.reshape(bsz * seq, dm), xs3.reshape(dec_b * dec_s, dm)
            ends_p.append(_rows_to_state(fin_p, n_groups))
            ends_s.append(_rows_to_state(fin_s, n_groups))
        else:
            bl = l - n_a
            bq_l, bo_l = row(attn_b_q[bl]), row(attn_b_o[bl])
            xp = _prompt_attention(xp, kv_p, bias_p, _sink_rows(attn_sinks[bl], n_kv, WINDOW), g[2], g[3],
                                   wq, bq_l, wo, bo_l, bl, blocks_per_seq=seq // WINDOW, n_kv=n_kv)
            q = _norm_proj(xs, g[2], wq, bq_l, bl)
            q = q.reshape(dec_b, dec_s, n_kv, pairs, LANE).transpose(0, 2, 3, 1, 4)
            q = q.reshape(dec_b, n_kv, pairs * dec_s, LANE)
            pad = n_k_pad - (n_past + dec_s)
            o = _sample_attention(q, jnp.pad(k_full, ((0, 0), (0, pad), (0, 0))),
                                  jnp.pad(v_full, ((0, 0), (0, pad), (0, 0))),
                                  bias_s, _sink_rows(attn_sinks[bl], n_kv, dec_s), n_kv=n_kv)
            o = o.reshape(dec_b, n_kv, pairs, dec_s, LANE).transpose(0, 3, 1, 2, 4).reshape(dec_b * dec_s, dm)
            xs = _out_proj_residual(xs, o, g[3], wo, bo_l, bl)
        xp = _ffn(xp, g[4], g[5], f2gu, f2d, l)
        xs = _ffn(xs, g[4], g[5], f2gu, f2d, l)

    kv_p3 = kv_p.reshape(bsz, seq, 2 * half)
    new_k_p = kv_p3[:, -WINDOW:, :half].reshape(bsz, WINDOW, n_kv, HEAD_DIM)
    new_v_p = kv_p3[:, -WINDOW:, half:].reshape(bsz, WINDOW, n_kv, HEAD_DIM)
    new_k_s = k_full[:, -WINDOW:].reshape(dec_b, WINDOW, n_kv, HEAD_DIM)
    new_v_s = v_full[:, -WINDOW:].reshape(dec_b, WINDOW, n_kv, HEAD_DIM)
    return (xp.reshape(bsz, seq, dm), xs.reshape(dec_b, dec_s, dm),
            jnp.stack([e[0] for e in ends_p]), jnp.stack([e[1] for e in ends_p]),
            new_k_p, new_v_p,
            jnp.stack([e[0] for e in ends_s]), jnp.stack([e[1] for e in ends_s]),
            new_k_s, new_v_s)
```

```python
import functools
import math

import numpy as np
import jax
import jax.numpy as jnp
from jax import lax
from jax.experimental import pallas as pl
from jax.experimental.pallas import tpu as pltpu

F32 = jnp.float32
BF16 = jnp.bfloat16

LANE = 128
SUBLANE = 8
VMEM_LIMIT_BYTES = 56 * 1024 * 1024

RMS_EPS = 1e-6
GROUP_SIZE = 16
STATE_DIM = 64
HEAD_DIM = 64
WINDOW = 128
NUM_BUCKETS = 32
MAX_DISTANCE = WINDOW
GROUPS_PER_BLOCK = LANE // GROUP_SIZE
HALF = GROUPS_PER_BLOCK * STATE_DIM
N_SLABS = 2 * HALF // LANE
N_PAIRS = N_SLABS // 2
ROW_TILE = 512
ATTN_BLOCKS_PER_STEP = 2
NEG_INF = float("-inf")


def _params(n_axes=1):
    return pltpu.CompilerParams(dimension_semantics=("arbitrary",) * n_axes,
                                vmem_limit_bytes=VMEM_LIMIT_BYTES)


def _const_spec(shape):
    nd = len(shape)
    return pl.BlockSpec(shape, lambda *_: (0,) * nd, pipeline_mode=pl.Buffered(1))


def _layer_spec(shape, layer):
    nd = len(shape)
    return pl.BlockSpec((None,) + tuple(shape[1:]), lambda *_: (layer,) + (0,) * (nd - 1),
                        pipeline_mode=pl.Buffered(1))


def _rms(x, g):
    ms = jnp.mean(x * x, axis=-1, keepdims=True)
    return x * lax.rsqrt(ms + RMS_EPS) * g


def _aligned(row):
    return row if isinstance(row, int) else pl.multiple_of(row, SUBLANE)


def _sigmoid(x):
    return 1.0 / (1.0 + jnp.exp(-x))


def _ffn_kernel(x_ref, ga_ref, gb_ref, wgu_ref, wd_ref, o_ref, *, d_ff):
    x = x_ref[...]
    xn = _rms(x, ga_ref[...]).astype(BF16)
    gu = jnp.dot(xn, wgu_ref[...], preferred_element_type=F32)
    gate = gu[:, :d_ff]
    up = gu[:, d_ff:]
    h = (gate * _sigmoid(gate) * up).astype(BF16)
    y = jnp.dot(h, wd_ref[...], preferred_element_type=F32)
    o_ref[...] = x + 0.5 * _rms(y, gb_ref[...])


def _ffn(x, ga, gb, wgu, wd, layer):
    t, d = x.shape
    d_ff = wd.shape[1]
    tm = min(ROW_TILE, t)
    row = pl.BlockSpec((tm, d), lambda i: (i, 0))
    return pl.pallas_call(
        functools.partial(_ffn_kernel, d_ff=d_ff),
        grid=(t // tm,),
        in_specs=[row, _const_spec((1, d)), _const_spec((1, d)),
                  _layer_spec(wgu.shape, layer), _layer_spec(wd.shape, layer)],
        out_specs=row,
        out_shape=jax.ShapeDtypeStruct((t, d), F32),
        compiler_params=_params(),
        name="ffn",
    )(x, ga, gb, wgu, wd)


def _proj_kernel(x_ref, g_ref, w_ref, b_ref, o_ref):
    xn = _rms(x_ref[...], g_ref[...]).astype(BF16)
    o_ref[...] = jnp.dot(xn, w_ref[...], preferred_element_type=F32) + b_ref[...]


def _norm_proj(x, g, w, b, layer):
    t, d = x.shape
    n = w.shape[-1]
    tm = min(ROW_TILE, t)
    return pl.pallas_call(
        _proj_kernel,
        grid=(t // tm,),
        in_specs=[pl.BlockSpec((tm, d), lambda i: (i, 0)), _const_spec((1, d)),
                  _layer_spec(w.shape, layer), _const_spec((1, n))],
        out_specs=pl.BlockSpec((tm, n), lambda i: (i, 0)),
        out_shape=jax.ShapeDtypeStruct((t, n), F32),
        compiler_params=_params(),
        name="norm_proj",
    )(x, g, w, b)


def _out_proj_kernel(x_ref, o_ref_in, g_ref, w_ref, b_ref, out_ref):
    a = jnp.dot(o_ref_in[...].astype(BF16), w_ref[...], preferred_element_type=F32) + b_ref[...]
    out_ref[...] = x_ref[...] + _rms(a, g_ref[...])


def _out_proj_residual(x, o, g, w, b, layer):
    t, d = x.shape
    tm = min(ROW_TILE, t)
    row = pl.BlockSpec((tm, d), lambda i: (i, 0))
    return pl.pallas_call(
        _out_proj_kernel,
        grid=(t // tm,),
        in_specs=[row, pl.BlockSpec((tm, o.shape[1]), lambda i: (i, 0)), _const_spec((1, d)),
                  _layer_spec(w.shape, layer), _const_spec((1, d))],
        out_specs=row,
        out_shape=jax.ShapeDtypeStruct((t, d), F32),
        compiler_params=_params(),
        name="out_proj_residual",
    )(x, o, g, w, b)


def _discretize_kernel(lre_ref, lim_ref, ldt_ref, bre_ref, bim_ref, cre_ref, cim_ref,
                       wb_ref, wc_ref, lam_ref):
    a = lre_ref[0, 0]
    b = lim_ref[0, 0]
    dt = jnp.exp(ldt_ref[0, 0])
    mag = jnp.exp(a * dt)
    lbr = mag * jnp.cos(b * dt)
    lbi = mag * jnp.sin(b * dt)
    den = a * a + b * b
    cr = ((lbr - 1.0) * a + lbi * b) / den
    ci = (lbi * a - (lbr - 1.0) * b) / den
    bre = bre_ref[0, 0]
    bim = bim_ref[0, 0]
    wb_ref[0, 0, :, :HALF] = (cr * bre - ci * bim).astype(BF16)
    wb_ref[0, 0, :, HALF:] = (cr * bim + ci * bre).astype(BF16)
    wc_ref[0, 0, :HALF, :] = cre_ref[0, 0].astype(BF16)
    wc_ref[0, 0, HALF:, :] = (-cim_ref[0, 0]).astype(BF16)
    lam_ref[0, 0, :, :HALF] = lbr
    lam_ref[0, 0, :, HALF:] = lbi


def _block_diag_in(b):
    nl, g, p, h = b.shape
    nb = g // GROUPS_PER_BLOCK
    bt = b.transpose(0, 1, 3, 2).reshape(nl, nb, GROUPS_PER_BLOCK, h, p)
    eye = jnp.eye(GROUPS_PER_BLOCK, dtype=b.dtype)
    out = bt[:, :, :, :, None, :] * eye[None, None, :, None, :, None]
    return out.reshape(nl, nb, GROUPS_PER_BLOCK * h, GROUPS_PER_BLOCK * p)


def _block_diag_out(c):
    nl, g, h, p = c.shape
    nb = g // GROUPS_PER_BLOCK
    ct = c.transpose(0, 1, 3, 2).reshape(nl, nb, GROUPS_PER_BLOCK, p, h)
    eye = jnp.eye(GROUPS_PER_BLOCK, dtype=c.dtype)
    out = ct[:, :, :, :, None, :] * eye[None, None, :, None, :, None]
    return out.reshape(nl, nb, GROUPS_PER_BLOCK * p, GROUPS_PER_BLOCK * h)


def _discretize(lam_re, lam_im, log_dt, b_re, b_im, c_re, c_im):
    nl, g, p = lam_re.shape
    nb = g // GROUPS_PER_BLOCK

    def rows(v):
        return v.reshape(nl, nb, 1, HALF)

    row_spec = pl.BlockSpec((1, 1, 1, HALF), lambda l, k: (l, k, 0, 0))
    in_spec = pl.BlockSpec((1, 1, LANE, HALF), lambda l, k: (l, k, 0, 0))
    out_spec = pl.BlockSpec((1, 1, HALF, LANE), lambda l, k: (l, k, 0, 0))
    wb, wc, lam = pl.pallas_call(
        _discretize_kernel,
        grid=(nl, nb),
        in_specs=[row_spec, row_spec, row_spec, in_spec, in_spec, out_spec, out_spec],
        out_specs=[pl.BlockSpec((1, 1, LANE, 2 * HALF), lambda l, k: (l, k, 0, 0)),
                   pl.BlockSpec((1, 1, 2 * HALF, LANE), lambda l, k: (l, k, 0, 0)),
                   pl.BlockSpec((1, 1, 1, 2 * HALF), lambda l, k: (l, k, 0, 0))],
        out_shape=[jax.ShapeDtypeStruct((nl, nb, LANE, 2 * HALF), BF16),
                   jax.ShapeDtypeStruct((nl, nb, 2 * HALF, LANE), BF16),
                   jax.ShapeDtypeStruct((nl, nb, 1, 2 * HALF), F32)],
        compiler_params=_params(2),
        name="s5_discretize",
    )(rows(lam_re), rows(lam_im), rows(log_dt), _block_diag_in(b_re), _block_diag_in(b_im),
      _block_diag_out(c_re), _block_diag_out(c_im))
    return wb, wc, lam.reshape(nl, nb, 2 * HALF)


def _s5_scan(s_ref, state_ref, lam_ref, n_seq, n_t):
    lre = [lam_ref[:, p * LANE:(p + 1) * LANE] for p in range(N_PAIRS)]
    lim = [lam_ref[:, HALF + p * LANE:HALF + (p + 1) * LANE] for p in range(N_PAIRS)]

    def seq_body(seq, carry):
        st0 = state_ref[seq]
        xre = tuple(st0[:, p * LANE:(p + 1) * LANE] for p in range(N_PAIRS))
        xim = tuple(st0[:, HALF + p * LANE:HALF + (p + 1) * LANE] for p in range(N_PAIRS))

        def step(t, st):
            xre, xim = st
            r0 = _aligned((seq * n_t + t) * SUBLANE)
            nre, nim = [], []
            for p in range(N_PAIRS):
                bre = s_ref[p, pl.ds(r0, SUBLANE), :]
                bim = s_ref[N_PAIRS + p, pl.ds(r0, SUBLANE), :]
                re = lre[p] * xre[p] - lim[p] * xim[p] + bre
                im = lre[p] * xim[p] + lim[p] * xre[p] + bim
                s_ref[p, pl.ds(r0, SUBLANE), :] = re
                s_ref[N_PAIRS + p, pl.ds(r0, SUBLANE), :] = im
                nre.append(re)
                nim.append(im)
            return tuple(nre), tuple(nim)

        xre, xim = lax.fori_loop(0, n_t, step, (xre, xim), unroll=min(n_t, 8))
        state_ref[seq] = jnp.concatenate(list(xre) + list(xim), axis=1)
        return carry

    if n_seq == 1:
        seq_body(0, 0)
    else:
        lax.fori_loop(0, n_seq, seq_body, 0)


def _s5_mixer_kernel(x_ref, g2_ref, g3_ref, wb_ref, wc_ref, lam_ref, d_ref, wglu_ref, bglu_ref,
                     init_ref, o_ref, final_ref, s_ref, y_ref, state_ref,
                     *, n_blocks, n_seq, n_t, d_model):
    tile = pl.program_id(1)
    rows = n_seq * n_t

    @pl.when(tile == 0)
    def _():
        state_ref[...] = init_ref[...]

    x = x_ref[0]
    u = _rms(x, g2_ref[...])
    ub = u.astype(BF16)
    for k in range(n_blocks):
        bu = jnp.dot(ub[:, k * LANE:(k + 1) * LANE], wb_ref[k], preferred_element_type=F32)
        for j in range(N_SLABS):
            s_ref[j, pl.ds(k, rows, stride=n_blocks), :] = bu[:, j * LANE:(j + 1) * LANE]
    _s5_scan(s_ref, state_ref, lam_ref, n_seq, n_t)
    for k in range(n_blocks):
        xs = jnp.concatenate([s_ref[j, pl.ds(k, rows, stride=n_blocks), :] for j in range(N_SLABS)], axis=1)
        y_ref[:, k * LANE:(k + 1) * LANE] = jnp.dot(xs.astype(BF16), wc_ref[k], preferred_element_type=F32)
    y = y_ref[...] + d_ref[...] * u
    h = jnp.dot(jax.nn.gelu(y).astype(BF16), wglu_ref[...], preferred_element_type=F32) + bglu_ref[...]
    m = h[:, :d_model] * _sigmoid(h[:, d_model:])
    o_ref[0] = x + _rms(m, g3_ref[...])

    @pl.when(tile == pl.num_programs(1) - 1)
    def _():
        final_ref[...] = state_ref[...]


def _s5_layer(x, g2, g3, wb, wc, lam, d, wglu, bglu, init, layer, *, n_seq, n_t):
    nb_rows, s, dm = x.shape
    n_blocks = wb.shape[1]
    assert n_blocks == SUBLANE
    rows = n_seq * n_t
    n_tiles = s // rows
    assert n_seq == 1 or n_tiles == 1
    row = pl.BlockSpec((1, rows, dm), lambda b, i: (b, i, 0))
    vec = _const_spec((1, dm))
    state = pl.BlockSpec((n_seq, SUBLANE, 2 * HALF), lambda b, i: (b, 0, 0))
    return pl.pallas_call(
        functools.partial(_s5_mixer_kernel, n_blocks=n_blocks, n_seq=n_seq, n_t=n_t, d_model=dm),
        grid=(nb_rows, n_tiles),
        in_specs=[row, vec, vec, _layer_spec(wb.shape, layer), _layer_spec(wc.shape, layer),
                  _layer_spec(lam.shape, layer), vec, _layer_spec(wglu.shape, layer),
                  _const_spec((1, 2 * dm)), state],
        out_specs=[row, state],
        out_shape=[jax.ShapeDtypeStruct(x.shape, F32),
                   jax.ShapeDtypeStruct(init.shape, F32)],
        scratch_shapes=[pltpu.VMEM((N_SLABS, rows * SUBLANE, LANE), F32), pltpu.VMEM((rows, dm), F32),
                        pltpu.VMEM((n_seq, SUBLANE, 2 * HALF), F32)],
        compiler_params=_params(2),
        name="s5_mixer",
    )(x, g2, g3, wb, wc, lam, d, wglu, bglu, init)


def _state_to_rows(re, im):
    b = re.shape[0]
    return jnp.concatenate([re.reshape(b, -1, HALF), im.reshape(b, -1, HALF)], axis=-1)


def _rows_to_state(s, n_groups):
    b = s.shape[0]
    return (s[:, :, :HALF].reshape(b, n_groups, STATE_DIM), s[:, :, HALF:].reshape(b, n_groups, STATE_DIM))


def _t5_bucket_np(dist):
    n = np.maximum(dist, 0)
    max_exact = NUM_BUCKETS // 2
    nf = np.maximum(n, 1).astype(np.float32)
    large = max_exact + (np.log(nf / np.float32(max_exact)) / np.float32(math.log(MAX_DISTANCE / max_exact))
                         * np.float32(NUM_BUCKETS - max_exact)).astype(np.int32)
    large = np.minimum(large, NUM_BUCKETS - 1)
    return np.where(n < max_exact, n, large).astype(np.int32)


def _bucket_table(n_q, n_k, q_offset, n_q_pad, n_k_pad, first_key=0):
    dist = (np.arange(n_q)[:, None] + q_offset) - np.arange(n_k)[None, :]
    valid = (dist >= 0) & (dist < WINDOW) & (np.arange(n_k)[None, :] >= first_key)
    table = np.full((n_q_pad, n_k_pad), -1, np.int32)
    table[:n_q, :n_k] = np.where(valid, _t5_bucket_np(dist), -1)
    return table


def _bias_kernel(rel_ref, idx_ref, o_ref):
    h = pl.program_id(0)
    idx = idx_ref[...]
    acc = jnp.full(idx.shape, NEG_INF, F32)
    for b in range(NUM_BUCKETS):
        acc = jnp.where(idx == b, rel_ref[b, h], acc)
    o_ref[0] = acc


def _bias_planes(rel_bias, table):
    n_heads = rel_bias.shape[1]
    r, c = table.shape
    return pl.pallas_call(
        _bias_kernel,
        grid=(n_heads,),
        in_specs=[pl.BlockSpec(memory_space=pltpu.SMEM), pl.BlockSpec((r, c), lambda h: (0, 0))],
        out_specs=pl.BlockSpec((1, r, c), lambda h: (h, 0, 0)),
        out_shape=jax.ShapeDtypeStruct((n_heads, r, c), F32),
        compiler_params=_params(),
        name="attn_bias",
    )(rel_bias, jnp.asarray(table))


def _kv_head_rows(kv, g):
    lane = lax.broadcasted_iota(jnp.int32, kv.shape, 1)
    swapped = pltpu.roll(kv, HEAD_DIM, axis=1)
    even, odd = (kv, swapped) if g == 0 else (swapped, kv)
    return jnp.where(lane < HEAD_DIM, even, 0.0), jnp.where(lane >= HEAD_DIM, odd, 0.0)


def _pair_layout_t(planes, n_kv, n_q):
    h, _, n_k = planes.shape
    pairs = h // n_kv // 2
    p = planes[:, :n_q].reshape(n_kv, pairs, 2, n_q, n_k).transpose(0, 2, 4, 1, 3)
    return p.reshape(n_kv, 2 * n_k, pairs * n_q)


def _sink_cols(sinks, n_kv, n_q):
    h = sinks.shape[0]
    pairs = h // n_kv // 2
    s = sinks.reshape(n_kv, pairs, 2).transpose(0, 2, 1)[..., None]
    return jnp.broadcast_to(s, (n_kv, 2, pairs, n_q)).reshape(n_kv, 2, pairs * n_q)


def _attend_t(chains):
    scores = []
    for q_t, k_even, k_odd, _, _, bias, _ in chains:
        ke = jnp.concatenate([k_even, k_odd], axis=0).astype(BF16)
        scores.append(jnp.dot(ke, q_t, preferred_element_type=F32) + bias)
    weights = []
    for s, chain in zip(scores, chains):
        sink = chain[6]
        n_k = s.shape[0] // 2
        probs, inv = [], []
        for par in range(2):
            sp = s[par * n_k:(par + 1) * n_k]
            sk = sink[par:par + 1]
            mx = jnp.maximum(jnp.max(sp, axis=0, keepdims=True), sk)
            p = jnp.exp(sp - mx)
            den = jnp.sum(p, axis=0, keepdims=True) + jnp.exp(sk - mx)
            probs.append(p.astype(BF16))
            inv.append(1.0 / den)
        weights.append((jnp.concatenate(probs, axis=0), inv))
    outs = []
    for (p_t, inv), chain in zip(weights, chains):
        ve = jnp.concatenate([chain[3], chain[4]], axis=0).astype(BF16)
        o = lax.dot_general(ve, p_t, (((0,), (0,)), ((), ())), preferred_element_type=F32)
        row = lax.broadcasted_iota(jnp.int32, o.shape, 0)
        outs.append(o * jnp.where(row < HEAD_DIM, inv[0], inv[1]))
    return outs


def _prompt_attn_kernel(x_ref, kvc_ref, kvp_ref, bias_ref, sink_ref, g2_ref, g3_ref, wq_ref, bq_ref,
                        wo_ref, bo_ref, o_ref, *, n_kv, n_sub, steps_per_seq, scale):
    blk = WINDOW
    x = x_ref[...]
    u = _rms(x, g2_ref[...]).astype(BF16)
    q_t = lax.dot_general(wq_ref[...], u, (((1,), (1,)), ((), ())), preferred_element_type=F32)
    q_t = ((q_t + jnp.concatenate([bq_ref[...]] * n_sub, axis=1)) * scale).astype(BF16)
    kv = jnp.concatenate([kvp_ref[...], kvc_ref[...]], axis=0)
    pairs = q_t.shape[0] // LANE // n_kv
    first = jnp.where(pl.program_id(0) % steps_per_seq == 0, 1, 0)
    k_rows = [_kv_head_rows(kv[:, :n_kv * HEAD_DIM], g) for g in range(n_kv)]
    v_rows = [_kv_head_rows(kv[:, n_kv * HEAD_DIM:], g) for g in range(n_kv)]
    chains = []
    for s in range(n_sub):
        keys = slice(s * blk, (s + 2) * blk)
        for g in range(n_kv):
            qp = jnp.concatenate([q_t[(g * pairs + i) * LANE:(g * pairs + i + 1) * LANE, s * blk:(s + 1) * blk]
                                  for i in range(pairs)], axis=1)
            bias = bias_ref[first, g] if s == 0 else bias_ref[0, g]
            chains.append((qp, k_rows[g][0][keys], k_rows[g][1][keys], v_rows[g][0][keys],
                           v_rows[g][1][keys], bias, sink_ref[g]))
    outs = _attend_t(chains)
    o_t = jnp.concatenate(
        [jnp.concatenate([outs[s * n_kv + g][:, i * blk:(i + 1) * blk] for s in range(n_sub)], axis=1)
         for g in range(n_kv) for i in range(pairs)], axis=0).astype(BF16)
    a = lax.dot_general(o_t, wo_ref[...], (((0,), (0,)), ((), ())), preferred_element_type=F32) + bo_ref[...]
    o_ref[...] = x + _rms(a, g3_ref[...])


def _prompt_attention(x, kv, bias, sink, g2, g3, wq_t, bq_col, wo, bo, layer, *, blocks_per_seq, n_kv):
    t, d = x.shape
    blk = WINDOW
    n_sub = ATTN_BLOCKS_PER_STEP
    assert n_kv * HEAD_DIM == LANE and blocks_per_seq % n_sub == 0
    row = pl.BlockSpec((n_sub * blk, d), lambda i: (i, 0))
    kvw = kv.shape[1]
    return pl.pallas_call(
        functools.partial(_prompt_attn_kernel, n_kv=n_kv, n_sub=n_sub,
                          steps_per_seq=blocks_per_seq // n_sub, scale=1.0 / math.sqrt(HEAD_DIM)),
        grid=(t // (n_sub * blk),),
        in_specs=[row,
                  pl.BlockSpec((n_sub * blk, kvw), lambda i: (i, 0)),
                  pl.BlockSpec((blk, kvw), lambda i: (jnp.maximum(i * n_sub - 1, 0), 0)),
                  _const_spec(bias.shape),
                  _const_spec(sink.shape), _const_spec((1, d)), _const_spec((1, d)),
                  _layer_spec(wq_t.shape, layer), _const_spec(bq_col.shape),
                  _layer_spec(wo.shape, layer), _const_spec((1, d))],
        out_specs=row,
        out_shape=jax.ShapeDtypeStruct((t, d), F32),
        compiler_params=_params(),
        name="prompt_attention",
    )(x, kv, kv, bias, sink, g2, g3, wq_t, bq_col, wo, bo)


def _sample_attn_kernel(q_ref, k_ref, v_ref, bias_ref, sink_ref, o_ref, *, n_kv, scale):
    chains = []
    for b in range(q_ref.shape[0]):
        k = k_ref[b]
        v = v_ref[b]
        for g in range(n_kv):
            chains.append(((q_ref[b, g] * scale).astype(BF16), *_kv_head_rows(k, g), *_kv_head_rows(v, g),
                           bias_ref[g], sink_ref[g]))
    outs = _attend_t(chains)
    for b in range(q_ref.shape[0]):
        for g in range(n_kv):
            o_ref[b, g] = outs[b * n_kv + g]


def _sample_attention(q_t, k_full, v_full, bias, sink, *, n_kv):
    nb, _, _, n_l = q_t.shape
    n_k = k_full.shape[1]
    bb = SUBLANE
    return pl.pallas_call(
        functools.partial(_sample_attn_kernel, n_kv=n_kv, scale=1.0 / math.sqrt(HEAD_DIM)),
        grid=(nb // bb,),
        in_specs=[pl.BlockSpec((bb, n_kv, LANE, n_l), lambda i: (i, 0, 0, 0)),
                  pl.BlockSpec((bb, n_k, LANE), lambda i: (i, 0, 0)),
                  pl.BlockSpec((bb, n_k, LANE), lambda i: (i, 0, 0)),
                  _const_spec(bias.shape), _const_spec(sink.shape)],
        out_specs=pl.BlockSpec((bb, n_kv, LANE, n_l), lambda i: (i, 0, 0, 0)),
        out_shape=jax.ShapeDtypeStruct(q_t.shape, F32),
        compiler_params=_params(),
        name="sample_attention",
    )(q_t, k_full, v_full, bias, sink)


def kernel(x_prompt, x_sample, state_ssm_re, state_ssm_im, cache_win_k, cache_win_v, norm_g, ffn1_w_gu, ffn1_w_down, ffn2_w_gu, ffn2_w_down, ssm_lambda_re, ssm_lambda_im, ssm_log_dt, ssm_b_re, ssm_b_im, ssm_c_re, ssm_c_im, ssm_d, ssm_w_glu, ssm_b_glu, kv_norm_g, w_kv, b_kv, attn_w_q, attn_b_q, attn_sinks, attn_w_o, attn_b_o, rel_bias):
    bsz, seq, dm = x_prompt.shape
    dec_b, dec_s, _ = x_sample.shape
    depth = norm_g.shape[0]
    n_a = ssm_lambda_re.shape[0]
    n_groups = ssm_lambda_re.shape[1]
    n_heads = attn_sinks.shape[1]
    n_kv = cache_win_k.shape[2]
    n_past = cache_win_k.shape[1]
    pairs = n_heads // n_kv // 2
    n_k_pad = 2 * WINDOW
    assert seq % ROW_TILE == 0 and seq % WINDOW == 0
    assert dec_b * dec_s == ROW_TILE and n_past + dec_s <= n_k_pad and dec_b % SUBLANE == 0
    assert n_kv == 2 and dm == n_heads * HEAD_DIM and n_groups == SUBLANE * GROUPS_PER_BLOCK

    bf = lambda w: w.astype(BF16)
    row = lambda v: v.reshape(1, -1)
    wb, wc, lam = _discretize(ssm_lambda_re, ssm_lambda_im, ssm_log_dt, ssm_b_re, ssm_b_im,
                              ssm_c_re, ssm_c_im)
    f1gu, f1d, f2gu, f2d = bf(ffn1_w_gu), bf(ffn1_w_down), bf(ffn2_w_gu), bf(ffn2_w_down)
    wglu, wkv, wq, wo = bf(ssm_w_glu), bf(w_kv)[None], bf(attn_w_q), bf(attn_w_o)

    tab_p = _bucket_table(WINDOW, 2 * WINDOW, WINDOW, WINDOW, 2 * WINDOW)
    tab_p0 = _bucket_table(WINDOW, 2 * WINDOW, WINDOW, WINDOW, 2 * WINDOW, first_key=WINDOW)
    tab_s = _bucket_table(dec_s, n_past + dec_s, n_past, SUBLANE, n_k_pad)
    wq_t = wq.transpose(0, 2, 1)
    bias_p = jnp.stack([_pair_layout_t(_bias_planes(rel_bias, tab_p), n_kv, WINDOW),
                        _pair_layout_t(_bias_planes(rel_bias, tab_p0), n_kv, WINDOW)])
    bias_s = _pair_layout_t(_bias_planes(rel_bias, tab_s), n_kv, dec_s)

    xp = x_prompt.reshape(bsz * seq, dm)
    xs = x_sample.reshape(dec_b * dec_s, dm)
    zero_state = jnp.zeros((bsz, SUBLANE, 2 * HALF), F32)

    ends_p, ends_s = [], []
    kv_p = k_full = v_full = None
    half = n_kv * HEAD_DIM
    for l in range(depth):
        g = [row(norm_g[l, i]) for i in range(norm_g.shape[1])]
        if l == n_a:
            kv_p = _norm_proj(xp, row(kv_norm_g), wkv, row(b_kv), 0)
            kv_s = _norm_proj(xs, row(kv_norm_g), wkv, row(b_kv), 0).reshape(dec_b, dec_s, 2 * half)
            k_full = jnp.concatenate([cache_win_k.reshape(dec_b, n_past, half), kv_s[:, :, :half]], axis=1)
            v_full = jnp.concatenate([cache_win_v.reshape(dec_b, n_past, half), kv_s[:, :, half:]], axis=1)
        xp = _ffn(xp, g[0], g[1], f1gu, f1d, l)
        xs = _ffn(xs, g[0], g[1], f1gu, f1d, l)
        if l < n_a:
            common = (g[2], g[3], wb, wc, lam, row(ssm_d[l]), wglu, row(ssm_b_glu[l]))
            xp3, fin_p = _s5_layer(xp.reshape(bsz, seq, dm), *common, zero_state, l, n_seq=1, n_t=ROW_TILE)
            xs3, fin_s = _s5_layer(xs.reshape(1, dec_b * dec_s, dm), *common,
                                   _state_to_rows(state_ssm_re[l], state_ssm_im[l]), l,
                                   n_seq=dec_b, n_t=dec_s)
            xp, xs = xp3.reshape(bsz * seq, dm), xs3.reshape(dec_b * dec_s, dm)
            ends_p.append(_rows_to_state(fin_p, n_groups))
            ends_s.append(_rows_to_state(fin_s, n_groups))
        else:
            bl = l - n_a
            bq_l, bo_l = row(attn_b_q[bl]), row(attn_b_o[bl])
            bq_col = jnp.broadcast_to(attn_b_q[bl][:, None], (dm, WINDOW))
            xp = _prompt_attention(xp, kv_p, bias_p, _sink_cols(attn_sinks[bl], n_kv, WINDOW), g[2], g[3],
                                   wq_t, bq_col, wo, bo_l, bl, blocks_per_seq=seq // WINDOW, n_kv=n_kv)
            q = _norm_proj(xs, g[2], wq, bq_l, bl)
            q = q.reshape(dec_b, dec_s, n_kv, pairs, LANE).transpose(0, 2, 4, 3, 1)
            q = q.reshape(dec_b, n_kv, LANE, pairs * dec_s)
            pad = n_k_pad - (n_past + dec_s)
            o = _sample_attention(q, jnp.pad(k_full, ((0, 0), (0, pad), (0, 0))),
                                  jnp.pad(v_full, ((0, 0), (0, pad), (0, 0))),
                                  bias_s, _sink_cols(attn_sinks[bl], n_kv, dec_s), n_kv=n_kv)
            o = o.reshape(dec_b, n_kv, LANE, pairs, dec_s).transpose(0, 4, 1, 3, 2).reshape(dec_b * dec_s, dm)
            xs = _out_proj_residual(xs, o, g[3], wo, bo_l, bl)
        xp = _ffn(xp, g[4], g[5], f2gu, f2d, l)
        xs = _ffn(xs, g[4], g[5], f2gu, f2d, l)

    kv_p3 = kv_p.reshape(bsz, seq, 2 * half)
    new_k_p = kv_p3[:, -WINDOW:, :half].reshape(bsz, WINDOW, n_kv, HEAD_DIM)
    new_v_p = kv_p3[:, -WINDOW:, half:].reshape(bsz, WINDOW, n_kv, HEAD_DIM)
    new_k_s = k_full[:, -WINDOW:].reshape(dec_b, WINDOW, n_kv, HEAD_DIM)
    new_v_s = v_full[:, -WINDOW:].reshape(dec_b, WINDOW, n_kv, HEAD_DIM)
    return (xp.reshape(bsz, seq, dm), xs.reshape(dec_b, dec_s, dm),
            jnp.stack([e[0] for e in ends_p]), jnp.stack([e[1] for e in ends_p]),
            new_k_p, new_v_p,
            jnp.stack([e[0] for e in ends_s]), jnp.stack([e[1] for e in ends_s]),
            new_k_s, new_v_s)
```

```python
import functools
import math

import numpy as np
import jax
import jax.numpy as jnp
from jax import lax
from jax.experimental import pallas as pl
from jax.experimental.pallas import tpu as pltpu

F32 = jnp.float32
BF16 = jnp.bfloat16

LANE = 128
SUBLANE = 8
VMEM_LIMIT_BYTES = 56 * 1024 * 1024

RMS_EPS = 1e-6
GROUP_SIZE = 16
STATE_DIM = 64
HEAD_DIM = 64
WINDOW = 128
NUM_BUCKETS = 32
MAX_DISTANCE = WINDOW
GROUPS_PER_BLOCK = LANE // GROUP_SIZE
HALF = GROUPS_PER_BLOCK * STATE_DIM
N_SLABS = 2 * HALF // LANE
N_PAIRS = N_SLABS // 2
ROW_TILE = 512
ATTN_BLOCKS_PER_STEP = 2
NEG_INF = float("-inf")


def _params(n_axes=1):
    return pltpu.CompilerParams(dimension_semantics=("arbitrary",) * n_axes,
                                vmem_limit_bytes=VMEM_LIMIT_BYTES)


def _const_spec(shape):
    nd = len(shape)
    return pl.BlockSpec(shape, lambda *_: (0,) * nd, pipeline_mode=pl.Buffered(1))


def _layer_spec(shape, layer):
    nd = len(shape)
    return pl.BlockSpec((None,) + tuple(shape[1:]), lambda *_: (layer,) + (0,) * (nd - 1),
                        pipeline_mode=pl.Buffered(1))


def _rms(x, g):
    ms = jnp.mean(x * x, axis=-1, keepdims=True)
    return x * lax.rsqrt(ms + RMS_EPS) * g


def _aligned(row):
    return row if isinstance(row, int) else pl.multiple_of(row, SUBLANE)


def _sigmoid(x):
    return 1.0 / (1.0 + jnp.exp(-x))


def _ffn_kernel(x_ref, ga_ref, gb_ref, wgu_ref, wd_ref, o_ref, *, d_ff):
    x = x_ref[...]
    xn = _rms(x, ga_ref[...]).astype(BF16)
    gu = jnp.dot(xn, wgu_ref[...], preferred_element_type=F32)
    gate = gu[:, :d_ff]
    up = gu[:, d_ff:]
    h = (gate * _sigmoid(gate) * up).astype(BF16)
    y = jnp.dot(h, wd_ref[...], preferred_element_type=F32)
    o_ref[...] = x + 0.5 * _rms(y, gb_ref[...])


def _ffn(x, ga, gb, wgu, wd, layer):
    t, d = x.shape
    d_ff = wd.shape[1]
    tm = min(ROW_TILE, t)
    row = pl.BlockSpec((tm, d), lambda i: (i, 0))
    return pl.pallas_call(
        functools.partial(_ffn_kernel, d_ff=d_ff),
        grid=(t // tm,),
        in_specs=[row, _const_spec((1, d)), _const_spec((1, d)),
                  _layer_spec(wgu.shape, layer), _layer_spec(wd.shape, layer)],
        out_specs=row,
        out_shape=jax.ShapeDtypeStruct((t, d), F32),
        compiler_params=_params(),
        name="ffn",
    )(x, ga, gb, wgu, wd)


def _proj_kernel(x_ref, g_ref, w_ref, b_ref, o_ref):
    xn = _rms(x_ref[...], g_ref[...]).astype(BF16)
    o_ref[...] = jnp.dot(xn, w_ref[...], preferred_element_type=F32) + b_ref[...]


def _norm_proj(x, g, w, b, layer):
    t, d = x.shape
    n = w.shape[-1]
    tm = min(ROW_TILE, t)
    return pl.pallas_call(
        _proj_kernel,
        grid=(t // tm,),
        in_specs=[pl.BlockSpec((tm, d), lambda i: (i, 0)), _const_spec((1, d)),
                  _layer_spec(w.shape, layer), _const_spec((1, n))],
        out_specs=pl.BlockSpec((tm, n), lambda i: (i, 0)),
        out_shape=jax.ShapeDtypeStruct((t, n), F32),
        compiler_params=_params(),
        name="norm_proj",
    )(x, g, w, b)


def _out_proj_kernel(x_ref, o_ref_in, g_ref, w_ref, b_ref, out_ref):
    a = jnp.dot(o_ref_in[...].astype(BF16), w_ref[...], preferred_element_type=F32) + b_ref[...]
    out_ref[...] = x_ref[...] + _rms(a, g_ref[...])


def _out_proj_residual(x, o, g, w, b, layer):
    t, d = x.shape
    tm = min(ROW_TILE, t)
    row = pl.BlockSpec((tm, d), lambda i: (i, 0))
    return pl.pallas_call(
        _out_proj_kernel,
        grid=(t // tm,),
        in_specs=[row, pl.BlockSpec((tm, o.shape[1]), lambda i: (i, 0)), _const_spec((1, d)),
                  _layer_spec(w.shape, layer), _const_spec((1, d))],
        out_specs=row,
        out_shape=jax.ShapeDtypeStruct((t, d), F32),
        compiler_params=_params(),
        name="out_proj_residual",
    )(x, o, g, w, b)


def _discretize_kernel(lre_ref, lim_ref, ldt_ref, bre_ref, bim_ref, cre_ref, cim_ref,
                       wb_ref, wc_ref, lam_ref):
    a = lre_ref[0, 0]
    b = lim_ref[0, 0]
    dt = jnp.exp(ldt_ref[0, 0])
    mag = jnp.exp(a * dt)
    lbr = mag * jnp.cos(b * dt)
    lbi = mag * jnp.sin(b * dt)
    den = a * a + b * b
    cr = ((lbr - 1.0) * a + lbi * b) / den
    ci = (lbi * a - (lbr - 1.0) * b) / den
    bre = bre_ref[0, 0]
    bim = bim_ref[0, 0]
    wb_ref[0, 0, :, :HALF] = (cr * bre - ci * bim).astype(BF16)
    wb_ref[0, 0, :, HALF:] = (cr * bim + ci * bre).astype(BF16)
    wc_ref[0, 0, :HALF, :] = cre_ref[0, 0].astype(BF16)
    wc_ref[0, 0, HALF:, :] = (-cim_ref[0, 0]).astype(BF16)
    lam_ref[0, 0, :, :HALF] = lbr
    lam_ref[0, 0, :, HALF:] = lbi


def _block_diag_in(b):
    nl, g, p, h = b.shape
    nb = g // GROUPS_PER_BLOCK
    bt = b.transpose(0, 1, 3, 2).reshape(nl, nb, GROUPS_PER_BLOCK, h, p)
    eye = jnp.eye(GROUPS_PER_BLOCK, dtype=b.dtype)
    out = bt[:, :, :, :, None, :] * eye[None, None, :, None, :, None]
    return out.reshape(nl, nb, GROUPS_PER_BLOCK * h, GROUPS_PER_BLOCK * p)


def _block_diag_out(c):
    nl, g, h, p = c.shape
    nb = g // GROUPS_PER_BLOCK
    ct = c.transpose(0, 1, 3, 2).reshape(nl, nb, GROUPS_PER_BLOCK, p, h)
    eye = jnp.eye(GROUPS_PER_BLOCK, dtype=c.dtype)
    out = ct[:, :, :, :, None, :] * eye[None, None, :, None, :, None]
    return out.reshape(nl, nb, GROUPS_PER_BLOCK * p, GROUPS_PER_BLOCK * h)


def _discretize(lam_re, lam_im, log_dt, b_re, b_im, c_re, c_im):
    nl, g, p = lam_re.shape
    nb = g // GROUPS_PER_BLOCK

    def rows(v):
        return v.reshape(nl, nb, 1, HALF)

    row_spec = pl.BlockSpec((1, 1, 1, HALF), lambda l, k: (l, k, 0, 0))
    in_spec = pl.BlockSpec((1, 1, LANE, HALF), lambda l, k: (l, k, 0, 0))
    out_spec = pl.BlockSpec((1, 1, HALF, LANE), lambda l, k: (l, k, 0, 0))
    wb, wc, lam = pl.pallas_call(
        _discretize_kernel,
        grid=(nl, nb),
        in_specs=[row_spec, row_spec, row_spec, in_spec, in_spec, out_spec, out_spec],
        out_specs=[pl.BlockSpec((1, 1, LANE, 2 * HALF), lambda l, k: (l, k, 0, 0)),
                   pl.BlockSpec((1, 1, 2 * HALF, LANE), lambda l, k: (l, k, 0, 0)),
                   pl.BlockSpec((1, 1, 1, 2 * HALF), lambda l, k: (l, k, 0, 0))],
        out_shape=[jax.ShapeDtypeStruct((nl, nb, LANE, 2 * HALF), BF16),
                   jax.ShapeDtypeStruct((nl, nb, 2 * HALF, LANE), BF16),
                   jax.ShapeDtypeStruct((nl, nb, 1, 2 * HALF), F32)],
        compiler_params=_params(2),
        name="s5_discretize",
    )(rows(lam_re), rows(lam_im), rows(log_dt), _block_diag_in(b_re), _block_diag_in(b_im),
      _block_diag_out(c_re), _block_diag_out(c_im))
    return wb, wc, lam.reshape(nl, nb, 2 * HALF)


def _slab_pitch(rows):
    assert rows % SUBLANE == 0
    return rows + SUBLANE // 2


def _s5_scan(s_ref, state_ref, lam_ref, n_seq, n_t):
    pitch = _slab_pitch(n_seq * n_t)
    lre = [lam_ref[:, p * LANE:(p + 1) * LANE] for p in range(N_PAIRS)]
    lim = [lam_ref[:, HALF + p * LANE:HALF + (p + 1) * LANE] for p in range(N_PAIRS)]

    def seq_body(seq, carry):
        st0 = state_ref[seq]
        xre = tuple(st0[:, p * LANE:(p + 1) * LANE] for p in range(N_PAIRS))
        xim = tuple(st0[:, HALF + p * LANE:HALF + (p + 1) * LANE] for p in range(N_PAIRS))

        def step(t, st):
            xre, xim = st
            token = pl.ds(seq * n_t + t, SUBLANE, stride=pitch)
            nre, nim = [], []
            for p in range(N_PAIRS):
                bre = s_ref[p, token, :]
                bim = s_ref[N_PAIRS + p, token, :]
                re = lre[p] * xre[p] - lim[p] * xim[p] + bre
                im = lre[p] * xim[p] + lim[p] * xre[p] + bim
                s_ref[p, token, :] = re
                s_ref[N_PAIRS + p, token, :] = im
                nre.append(re)
                nim.append(im)
            return tuple(nre), tuple(nim)

        xre, xim = lax.fori_loop(0, n_t, step, (xre, xim), unroll=min(n_t, 8))
        state_ref[seq] = jnp.concatenate(list(xre) + list(xim), axis=1)
        return carry

    if n_seq == 1:
        seq_body(0, 0)
    else:
        lax.fori_loop(0, n_seq, seq_body, 0)


def _s5_mixer_kernel(x_ref, g2_ref, g3_ref, wb_ref, wc_ref, lam_ref, d_ref, wglu_ref, bglu_ref,
                     init_ref, o_ref, final_ref, s_ref, y_ref, state_ref,
                     *, n_blocks, n_seq, n_t, d_model):
    tile = pl.program_id(1)
    rows = n_seq * n_t

    @pl.when(tile == 0)
    def _():
        state_ref[...] = init_ref[...]

    pitch = _slab_pitch(rows)
    x = x_ref[0]
    u = _rms(x, g2_ref[...])
    ub = u.astype(BF16)
    for k in range(n_blocks):
        bu = jnp.dot(ub[:, k * LANE:(k + 1) * LANE], wb_ref[k], preferred_element_type=F32)
        for j in range(N_SLABS):
            s_ref[j, pl.ds(k * pitch, rows), :] = bu[:, j * LANE:(j + 1) * LANE]
    _s5_scan(s_ref, state_ref, lam_ref, n_seq, n_t)
    for k in range(n_blocks):
        xs = jnp.concatenate([s_ref[j, pl.ds(k * pitch, rows), :] for j in range(N_SLABS)], axis=1)
        y_ref[:, k * LANE:(k + 1) * LANE] = jnp.dot(xs.astype(BF16), wc_ref[k], preferred_element_type=F32)
    y = y_ref[...] + d_ref[...] * u
    h = jnp.dot(jax.nn.gelu(y).astype(BF16), wglu_ref[...], preferred_element_type=F32) + bglu_ref[...]
    m = h[:, :d_model] * _sigmoid(h[:, d_model:])
    o_ref[0] = x + _rms(m, g3_ref[...])

    @pl.when(tile == pl.num_programs(1) - 1)
    def _():
        final_ref[...] = state_ref[...]


def _s5_layer(x, g2, g3, wb, wc, lam, d, wglu, bglu, init, layer, *, n_seq, n_t):
    nb_rows, s, dm = x.shape
    n_blocks = wb.shape[1]
    assert n_blocks == SUBLANE
    rows = n_seq * n_t
    n_tiles = s // rows
    assert n_seq == 1 or n_tiles == 1
    row = pl.BlockSpec((1, rows, dm), lambda b, i: (b, i, 0))
    vec = _const_spec((1, dm))
    state = pl.BlockSpec((n_seq, SUBLANE, 2 * HALF), lambda b, i: (b, 0, 0))
    return pl.pallas_call(
        functools.partial(_s5_mixer_kernel, n_blocks=n_blocks, n_seq=n_seq, n_t=n_t, d_model=dm),
        grid=(nb_rows, n_tiles),
        in_specs=[row, vec, vec, _layer_spec(wb.shape, layer), _layer_spec(wc.shape, layer),
                  _layer_spec(lam.shape, layer), vec, _layer_spec(wglu.shape, layer),
                  _const_spec((1, 2 * dm)), state],
        out_specs=[row, state],
        out_shape=[jax.ShapeDtypeStruct(x.shape, F32),
                   jax.ShapeDtypeStruct(init.shape, F32)],
        scratch_shapes=[pltpu.VMEM((N_SLABS, n_blocks * _slab_pitch(rows), LANE), F32),
                        pltpu.VMEM((rows, dm), F32),
                        pltpu.VMEM((n_seq, SUBLANE, 2 * HALF), F32)],
        compiler_params=_params(2),
        name="s5_mixer",
    )(x, g2, g3, wb, wc, lam, d, wglu, bglu, init)


def _state_to_rows(re, im):
    b = re.shape[0]
    return jnp.concatenate([re.reshape(b, -1, HALF), im.reshape(b, -1, HALF)], axis=-1)


def _rows_to_state(s, n_groups):
    b = s.shape[0]
    return (s[:, :, :HALF].reshape(b, n_groups, STATE_DIM), s[:, :, HALF:].reshape(b, n_groups, STATE_DIM))


def _t5_bucket_np(dist):
    n = np.maximum(dist, 0)
    max_exact = NUM_BUCKETS // 2
    nf = np.maximum(n, 1).astype(np.float32)
    large = max_exact + (np.log(nf / np.float32(max_exact)) / np.float32(math.log(MAX_DISTANCE / max_exact))
                         * np.float32(NUM_BUCKETS - max_exact)).astype(np.int32)
    large = np.minimum(large, NUM_BUCKETS - 1)
    return np.where(n < max_exact, n, large).astype(np.int32)


def _bucket_table(n_q, n_k, q_offset, n_q_pad, n_k_pad, first_key=0):
    dist = (np.arange(n_q)[:, None] + q_offset) - np.arange(n_k)[None, :]
    valid = (dist >= 0) & (dist < WINDOW) & (np.arange(n_k)[None, :] >= first_key)
    table = np.full((n_q_pad, n_k_pad), -1, np.int32)
    table[:n_q, :n_k] = np.where(valid, _t5_bucket_np(dist), -1)
    return table


def _bias_kernel(rel_ref, idx_ref, o_ref):
    h = pl.program_id(0)
    idx = idx_ref[...]
    acc = jnp.full(idx.shape, NEG_INF, F32)
    for b in range(NUM_BUCKETS):
        acc = jnp.where(idx == b, rel_ref[b, h], acc)
    o_ref[0] = acc


def _bias_planes(rel_bias, table):
    n_heads = rel_bias.shape[1]
    r, c = table.shape
    return pl.pallas_call(
        _bias_kernel,
        grid=(n_heads,),
        in_specs=[pl.BlockSpec(memory_space=pltpu.SMEM), pl.BlockSpec((r, c), lambda h: (0, 0))],
        out_specs=pl.BlockSpec((1, r, c), lambda h: (h, 0, 0)),
        out_shape=jax.ShapeDtypeStruct((n_heads, r, c), F32),
        compiler_params=_params(),
        name="attn_bias",
    )(rel_bias, jnp.asarray(table))


def _kv_head_rows(kv, g):
    lane = lax.broadcasted_iota(jnp.int32, kv.shape, 1)
    swapped = pltpu.roll(kv, HEAD_DIM, axis=1)
    even, odd = (kv, swapped) if g == 0 else (swapped, kv)
    return jnp.where(lane < HEAD_DIM, even, 0.0), jnp.where(lane >= HEAD_DIM, odd, 0.0)


def _pair_layout_t(planes, n_kv, n_q):
    h, _, n_k = planes.shape
    pairs = h // n_kv // 2
    p = planes[:, :n_q].reshape(n_kv, pairs, 2, n_q, n_k).transpose(0, 2, 4, 1, 3)
    return p.reshape(n_kv, 2 * n_k, pairs * n_q)


def _sink_cols(sinks, n_kv, n_q):
    h = sinks.shape[0]
    pairs = h // n_kv // 2
    s = sinks.reshape(n_kv, pairs, 2).transpose(0, 2, 1)[..., None]
    return jnp.broadcast_to(s, (n_kv, 2, pairs, n_q)).reshape(n_kv, 2, pairs * n_q)


def _attend_t(chains):
    scores = []
    for q_t, k_even, k_odd, _, _, bias, _ in chains:
        ke = jnp.concatenate([k_even, k_odd], axis=0).astype(BF16)
        scores.append(jnp.dot(ke, q_t, preferred_element_type=F32) + bias)
    weights = []
    for s, chain in zip(scores, chains):
        sink = chain[6]
        n_k = s.shape[0] // 2
        probs, inv = [], []
        for par in range(2):
            sp = s[par * n_k:(par + 1) * n_k]
            sk = sink[par:par + 1]
            mx = jnp.maximum(jnp.max(sp, axis=0, keepdims=True), sk)
            p = jnp.exp(sp - mx)
            den = jnp.sum(p, axis=0, keepdims=True) + jnp.exp(sk - mx)
            probs.append(p.astype(BF16))
            inv.append(1.0 / den)
        weights.append((jnp.concatenate(probs, axis=0), inv))
    outs = []
    for (p_t, inv), chain in zip(weights, chains):
        ve = jnp.concatenate([chain[3], chain[4]], axis=0).astype(BF16)
        o = lax.dot_general(ve, p_t, (((0,), (0,)), ((), ())), preferred_element_type=F32)
        row = lax.broadcasted_iota(jnp.int32, o.shape, 0)
        outs.append(o * jnp.where(row < HEAD_DIM, inv[0], inv[1]))
    return outs


def _prompt_attn_kernel(x_ref, kvc_ref, kvp_ref, bias_ref, sink_ref, g2_ref, g3_ref, wq_ref, bq_ref,
                        wo_ref, bo_ref, o_ref, *, n_kv, n_sub, steps_per_seq, scale):
    blk = WINDOW
    x = x_ref[...]
    u = _rms(x, g2_ref[...]).astype(BF16)
    q_t = lax.dot_general(wq_ref[...], u, (((1,), (1,)), ((), ())), preferred_element_type=F32)
    q_t = ((q_t + jnp.concatenate([bq_ref[...]] * n_sub, axis=1)) * scale).astype(BF16)
    kv = jnp.concatenate([kvp_ref[...], kvc_ref[...]], axis=0)
    pairs = q_t.shape[0] // LANE // n_kv
    first = jnp.where(pl.program_id(0) % steps_per_seq == 0, 1, 0)
    k_rows = [_kv_head_rows(kv[:, :n_kv * HEAD_DIM], g) for g in range(n_kv)]
    v_rows = [_kv_head_rows(kv[:, n_kv * HEAD_DIM:], g) for g in range(n_kv)]
    chains = []
    for s in range(n_sub):
        keys = slice(s * blk, (s + 2) * blk)
        for g in range(n_kv):
            qp = jnp.concatenate([q_t[(g * pairs + i) * LANE:(g * pairs + i + 1) * LANE, s * blk:(s + 1) * blk]
                                  for i in range(pairs)], axis=1)
            bias = bias_ref[first, g] if s == 0 else bias_ref[0, g]
            chains.append((qp, k_rows[g][0][keys], k_rows[g][1][keys], v_rows[g][0][keys],
                           v_rows[g][1][keys], bias, sink_ref[g]))
    outs = _attend_t(chains)
    o_t = jnp.concatenate(
        [jnp.concatenate([outs[s * n_kv + g][:, i * blk:(i + 1) * blk] for s in range(n_sub)], axis=1)
         for g in range(n_kv) for i in range(pairs)], axis=0).astype(BF16)
    a = lax.dot_general(o_t, wo_ref[...], (((0,), (0,)), ((), ())), preferred_element_type=F32) + bo_ref[...]
    o_ref[...] = x + _rms(a, g3_ref[...])


def _prompt_attention(x, kv, bias, sink, g2, g3, wq_t, bq_col, wo, bo, layer, *, blocks_per_seq, n_kv):
    t, d = x.shape
    blk = WINDOW
    n_sub = ATTN_BLOCKS_PER_STEP
    assert n_kv * HEAD_DIM == LANE and blocks_per_seq % n_sub == 0
    row = pl.BlockSpec((n_sub * blk, d), lambda i: (i, 0))
    kvw = kv.shape[1]
    return pl.pallas_call(
        functools.partial(_prompt_attn_kernel, n_kv=n_kv, n_sub=n_sub,
                          steps_per_seq=blocks_per_seq // n_sub, scale=1.0 / math.sqrt(HEAD_DIM)),
        grid=(t // (n_sub * blk),),
        in_specs=[row,
                  pl.BlockSpec((n_sub * blk, kvw), lambda i: (i, 0)),
                  pl.BlockSpec((blk, kvw), lambda i: (jnp.maximum(i * n_sub - 1, 0), 0)),
                  _const_spec(bias.shape),
                  _const_spec(sink.shape), _const_spec((1, d)), _const_spec((1, d)),
                  _layer_spec(wq_t.shape, layer), _const_spec(bq_col.shape),
                  _layer_spec(wo.shape, layer), _const_spec((1, d))],
        out_specs=row,
        out_shape=jax.ShapeDtypeStruct((t, d), F32),
        compiler_params=_params(),
        name="prompt_attention",
    )(x, kv, kv, bias, sink, g2, g3, wq_t, bq_col, wo, bo)


def _sample_attn_kernel(q_ref, k_ref, v_ref, bias_ref, sink_ref, o_ref, *, n_kv, scale):
    chains = []
    for b in range(q_ref.shape[0]):
        k = k_ref[b]
        v = v_ref[b]
        for g in range(n_kv):
            chains.append(((q_ref[b, g] * scale).astype(BF16), *_kv_head_rows(k, g), *_kv_head_rows(v, g),
                           bias_ref[g], sink_ref[g]))
    outs = _attend_t(chains)
    for b in range(q_ref.shape[0]):
        for g in range(n_kv):
            o_ref[b, g] = outs[b * n_kv + g]


def _sample_attention(q_t, k_full, v_full, bias, sink, *, n_kv):
    nb, _, _, n_l = q_t.shape
    n_k = k_full.shape[1]
    bb = SUBLANE
    return pl.pallas_call(
        functools.partial(_sample_attn_kernel, n_kv=n_kv, scale=1.0 / math.sqrt(HEAD_DIM)),
        grid=(nb // bb,),
        in_specs=[pl.BlockSpec((bb, n_kv, LANE, n_l), lambda i: (i, 0, 0, 0)),
                  pl.BlockSpec((bb, n_k, LANE), lambda i: (i, 0, 0)),
                  pl.BlockSpec((bb, n_k, LANE), lambda i: (i, 0, 0)),
                  _const_spec(bias.shape), _const_spec(sink.shape)],
        out_specs=pl.BlockSpec((bb, n_kv, LANE, n_l), lambda i: (i, 0, 0, 0)),
        out_shape=jax.ShapeDtypeStruct(q_t.shape, F32),
        compiler_params=_params(),
        name="sample_attention",
    )(q_t, k_full, v_full, bias, sink)


def kernel(x_prompt, x_sample, state_ssm_re, state_ssm_im, cache_win_k, cache_win_v, norm_g, ffn1_w_gu, ffn1_w_down, ffn2_w_gu, ffn2_w_down, ssm_lambda_re, ssm_lambda_im, ssm_log_dt, ssm_b_re, ssm_b_im, ssm_c_re, ssm_c_im, ssm_d, ssm_w_glu, ssm_b_glu, kv_norm_g, w_kv, b_kv, attn_w_q, attn_b_q, attn_sinks, attn_w_o, attn_b_o, rel_bias):
    bsz, seq, dm = x_prompt.shape
    dec_b, dec_s, _ = x_sample.shape
    depth = norm_g.shape[0]
    n_a = ssm_lambda_re.shape[0]
    n_groups = ssm_lambda_re.shape[1]
    n_heads = attn_sinks.shape[1]
    n_kv = cache_win_k.shape[2]
    n_past = cache_win_k.shape[1]
    pairs = n_heads // n_kv // 2
    n_k_pad = 2 * WINDOW
    assert seq % ROW_TILE == 0 and seq % WINDOW == 0
    assert dec_b * dec_s == ROW_TILE and n_past + dec_s <= n_k_pad and dec_b % SUBLANE == 0
    assert n_kv == 2 and dm == n_heads * HEAD_DIM and n_groups == SUBLANE * GROUPS_PER_BLOCK

    bf = lambda w: w.astype(BF16)
    row = lambda v: v.reshape(1, -1)
    wb, wc, lam = _discretize(ssm_lambda_re, ssm_lambda_im, ssm_log_dt, ssm_b_re, ssm_b_im,
                              ssm_c_re, ssm_c_im)
    f1gu, f1d, f2gu, f2d = bf(ffn1_w_gu), bf(ffn1_w_down), bf(ffn2_w_gu), bf(ffn2_w_down)
    wglu, wkv, wq, wo = bf(ssm_w_glu), bf(w_kv)[None], bf(attn_w_q), bf(attn_w_o)

    tab_p = _bucket_table(WINDOW, 2 * WINDOW, WINDOW, WINDOW, 2 * WINDOW)
    tab_p0 = _bucket_table(WINDOW, 2 * WINDOW, WINDOW, WINDOW, 2 * WINDOW, first_key=WINDOW)
    tab_s = _bucket_table(dec_s, n_past + dec_s, n_past, SUBLANE, n_k_pad)
    wq_t = wq.transpose(0, 2, 1)
    bias_p = jnp.stack([_pair_layout_t(_bias_planes(rel_bias, tab_p), n_kv, WINDOW),
                        _pair_layout_t(_bias_planes(rel_bias, tab_p0), n_kv, WINDOW)])
    bias_s = _pair_layout_t(_bias_planes(rel_bias, tab_s), n_kv, dec_s)

    xp = x_prompt.reshape(bsz * seq, dm)
    xs = x_sample.reshape(dec_b * dec_s, dm)
    zero_state = jnp.zeros((bsz, SUBLANE, 2 * HALF), F32)

    ends_p, ends_s = [], []
    kv_p = k_full = v_full = None
    half = n_kv * HEAD_DIM
    for l in range(depth):
        g = [row(norm_g[l, i]) for i in range(norm_g.shape[1])]
        if l == n_a:
            kv_p = _norm_proj(xp, row(kv_norm_g), wkv, row(b_kv), 0)
            kv_s = _norm_proj(xs, row(kv_norm_g), wkv, row(b_kv), 0).reshape(dec_b, dec_s, 2 * half)
            k_full = jnp.concatenate([cache_win_k.reshape(dec_b, n_past, half), kv_s[:, :, :half]], axis=1)
            v_full = jnp.concatenate([cache_win_v.reshape(dec_b, n_past, half), kv_s[:, :, half:]], axis=1)
        xp = _ffn(xp, g[0], g[1], f1gu, f1d, l)
        xs = _ffn(xs, g[0], g[1], f1gu, f1d, l)
        if l < n_a:
            common = (g[2], g[3], wb, wc, lam, row(ssm_d[l]), wglu, row(ssm_b_glu[l]))
            xp3, fin_p = _s5_layer(xp.reshape(bsz, seq, dm), *common, zero_state, l, n_seq=1, n_t=ROW_TILE)
            xs3, fin_s = _s5_layer(xs.reshape(1, dec_b * dec_s, dm), *common,
                                   _state_to_rows(state_ssm_re[l], state_ssm_im[l]), l,
                                   n_seq=dec_b, n_t=dec_s)
            xp, xs = xp3.reshape(bsz * seq, dm), xs3.reshape(dec_b * dec_s, dm)
            ends_p.append(_rows_to_state(fin_p, n_groups))
            ends_s.append(_rows_to_state(fin_s, n_groups))
        else:
            bl = l - n_a
            bq_l, bo_l = row(attn_b_q[bl]), row(attn_b_o[bl])
            bq_col = jnp.broadcast_to(attn_b_q[bl][:, None], (dm, WINDOW))
            xp = _prompt_attention(xp, kv_p, bias_p, _sink_cols(attn_sinks[bl], n_kv, WINDOW), g[2], g[3],
                                   wq_t, bq_col, wo, bo_l, bl, blocks_per_seq=seq // WINDOW, n_kv=n_kv)
            q = _norm_proj(xs, g[2], wq, bq_l, bl)
            q = q.reshape(dec_b, dec_s, n_kv, pairs, LANE).transpose(0, 2, 4, 3, 1)
            q = q.reshape(dec_b, n_kv, LANE, pairs * dec_s)
            pad = n_k_pad - (n_past + dec_s)
            o = _sample_attention(q, jnp.pad(k_full, ((0, 0), (0, pad), (0, 0))),
                                  jnp.pad(v_full, ((0, 0), (0, pad), (0, 0))),
                                  bias_s, _sink_cols(attn_sinks[bl], n_kv, dec_s), n_kv=n_kv)
            o = o.reshape(dec_b, n_kv, LANE, pairs, dec_s).transpose(0, 4, 1, 3, 2).reshape(dec_b * dec_s, dm)
            xs = _out_proj_residual(xs, o, g[3], wo, bo_l, bl)
        xp = _ffn(xp, g[4], g[5], f2gu, f2d, l)
        xs = _ffn(xs, g[4], g[5], f2gu, f2d, l)

    kv_p3 = kv_p.reshape(bsz, seq, 2 * half)
    new_k_p = kv_p3[:, -WINDOW:, :half].reshape(bsz, WINDOW, n_kv, HEAD_DIM)
    new_v_p = kv_p3[:, -WINDOW:, half:].reshape(bsz, WINDOW, n_kv, HEAD_DIM)
    new_k_s = k_full[:, -WINDOW:].reshape(dec_b, WINDOW, n_kv, HEAD_DIM)
    new_v_s = v_full[:, -WINDOW:].reshape(dec_b, WINDOW, n_kv, HEAD_DIM)
    return (xp.reshape(bsz, seq, dm), xs.reshape(dec_b, dec_s, dm),
            jnp.stack([e[0] for e in ends_p]), jnp.stack([e[1] for e in ends_p]),
            new_k_p, new_v_p,
            jnp.stack([e[0] for e in ends_s]), jnp.stack([e[1] for e in ends_s]),
            new_k_s, new_v_s)
```

```python
import functools
import math

import numpy as np
import jax
import jax.numpy as jnp
from jax import lax
from jax.experimental import pallas as pl
from jax.experimental.pallas import tpu as pltpu

F32 = jnp.float32
BF16 = jnp.bfloat16

LANE = 128
SUBLANE = 8
VMEM_LIMIT_BYTES = 56 * 1024 * 1024

RMS_EPS = 1e-6
GROUP_SIZE = 16
STATE_DIM = 64
HEAD_DIM = 64
WINDOW = 128
NUM_BUCKETS = 32
MAX_DISTANCE = WINDOW
GROUPS_PER_BLOCK = LANE // GROUP_SIZE
HALF = GROUPS_PER_BLOCK * STATE_DIM
N_SLABS = 2 * HALF // LANE
N_PAIRS = N_SLABS // 2
ROW_TILE = 512
ATTN_BLOCKS_PER_STEP = 2
NEG_INF = float("-inf")


def _params(n_axes=1):
    return pltpu.CompilerParams(dimension_semantics=("arbitrary",) * n_axes,
                                vmem_limit_bytes=VMEM_LIMIT_BYTES)


def _const_spec(shape):
    nd = len(shape)
    return pl.BlockSpec(shape, lambda *_: (0,) * nd, pipeline_mode=pl.Buffered(1))


def _layer_spec(shape, layer):
    nd = len(shape)
    return pl.BlockSpec((None,) + tuple(shape[1:]), lambda *_: (layer,) + (0,) * (nd - 1),
                        pipeline_mode=pl.Buffered(1))


def _rms(x, g):
    ms = jnp.mean(x * x, axis=-1, keepdims=True)
    return x * lax.rsqrt(ms + RMS_EPS) * g


def _aligned(row):
    return row if isinstance(row, int) else pl.multiple_of(row, SUBLANE)


def _sigmoid(x):
    return 1.0 / (1.0 + jnp.exp(-x))


def _ffn_kernel(x_ref, ga_ref, gb_ref, wgu_ref, wd_ref, o_ref, *, d_ff):
    x = x_ref[...]
    xn = _rms(x, ga_ref[...]).astype(BF16)
    gu = jnp.dot(xn, wgu_ref[...], preferred_element_type=F32)
    gate = gu[:, :d_ff]
    up = gu[:, d_ff:]
    h = (gate * _sigmoid(gate) * up).astype(BF16)
    y = jnp.dot(h, wd_ref[...], preferred_element_type=F32)
    o_ref[...] = x + 0.5 * _rms(y, gb_ref[...])


def _ffn(x, ga, gb, wgu, wd, layer):
    t, d = x.shape
    d_ff = wd.shape[1]
    tm = min(ROW_TILE, t)
    row = pl.BlockSpec((tm, d), lambda i: (i, 0))
    return pl.pallas_call(
        functools.partial(_ffn_kernel, d_ff=d_ff),
        grid=(t // tm,),
        in_specs=[row, _const_spec((1, d)), _const_spec((1, d)),
                  _layer_spec(wgu.shape, layer), _layer_spec(wd.shape, layer)],
        out_specs=row,
        out_shape=jax.ShapeDtypeStruct((t, d), F32),
        compiler_params=_params(),
        name="ffn",
    )(x, ga, gb, wgu, wd)


def _proj_kernel(x_ref, g_ref, w_ref, b_ref, o_ref):
    xn = _rms(x_ref[...], g_ref[...]).astype(BF16)
    o_ref[...] = jnp.dot(xn, w_ref[...], preferred_element_type=F32) + b_ref[...]


def _norm_proj(x, g, w, b, layer):
    t, d = x.shape
    n = w.shape[-1]
    tm = min(ROW_TILE, t)
    return pl.pallas_call(
        _proj_kernel,
        grid=(t // tm,),
        in_specs=[pl.BlockSpec((tm, d), lambda i: (i, 0)), _const_spec((1, d)),
                  _layer_spec(w.shape, layer), _const_spec((1, n))],
        out_specs=pl.BlockSpec((tm, n), lambda i: (i, 0)),
        out_shape=jax.ShapeDtypeStruct((t, n), F32),
        compiler_params=_params(),
        name="norm_proj",
    )(x, g, w, b)


def _out_proj_kernel(x_ref, o_ref_in, g_ref, w_ref, b_ref, out_ref):
    a = jnp.dot(o_ref_in[...].astype(BF16), w_ref[...], preferred_element_type=F32) + b_ref[...]
    out_ref[...] = x_ref[...] + _rms(a, g_ref[...])


def _out_proj_residual(x, o, g, w, b, layer):
    t, d = x.shape
    tm = min(ROW_TILE, t)
    row = pl.BlockSpec((tm, d), lambda i: (i, 0))
    return pl.pallas_call(
        _out_proj_kernel,
        grid=(t // tm,),
        in_specs=[row, pl.BlockSpec((tm, o.shape[1]), lambda i: (i, 0)), _const_spec((1, d)),
                  _layer_spec(w.shape, layer), _const_spec((1, d))],
        out_specs=row,
        out_shape=jax.ShapeDtypeStruct((t, d), F32),
        compiler_params=_params(),
        name="out_proj_residual",
    )(x, o, g, w, b)


def _lam_bar(lam_re, lam_im, log_dt):
    dt = jnp.exp(log_dt)
    mag = jnp.exp(lam_re * dt)
    return mag * jnp.cos(lam_im * dt), mag * jnp.sin(lam_im * dt)


def _discretize_kernel(lre_ref, lim_ref, ldt_ref, lre_col_ref, lim_col_ref, ldt_col_ref,
                       bre_ref, bim_ref, cre_ref, cim_ref, win_ref, wout_ref, wskip_ref, lam2_ref):
    a = lre_ref[0, 0]
    b = lim_ref[0, 0]
    lbr, lbi = _lam_bar(a, b, ldt_ref[0, 0])
    den = a * a + b * b
    cr = ((lbr - 1.0) * a + lbi * b) / den
    ci = (lbi * a - (lbr - 1.0) * b) / den
    bre = bre_ref[0, 0]
    bim = bim_ref[0, 0]
    wbr = cr * bre - ci * bim
    wbi = cr * bim + ci * bre
    win_ref[0, 0, :LANE, :HALF] = (lbr * wbr - lbi * wbi).astype(BF16)
    win_ref[0, 0, :LANE, HALF:] = (lbr * wbi + lbi * wbr).astype(BF16)
    win_ref[0, 0, LANE:, :HALF] = wbr.astype(BF16)
    win_ref[0, 0, LANE:, HALF:] = wbi.astype(BF16)
    lam2_ref[0, 0, :, :HALF] = lbr * lbr - lbi * lbi
    lam2_ref[0, 0, :, HALF:] = 2.0 * lbr * lbi
    lcr, lci = _lam_bar(lre_col_ref[0, 0], lim_col_ref[0, 0], ldt_col_ref[0, 0])
    cre = cre_ref[0, 0]
    cim = cim_ref[0, 0]
    zeros = jnp.zeros((HALF, LANE), BF16)
    wout_ref[0, 0, 0 * HALF:1 * HALF, :LANE] = (cre * lcr - cim * lci).astype(BF16)
    wout_ref[0, 0, 1 * HALF:2 * HALF, :LANE] = (-(cre * lci + cim * lcr)).astype(BF16)
    wout_ref[0, 0, 2 * HALF:3 * HALF, :LANE] = zeros
    wout_ref[0, 0, 3 * HALF:4 * HALF, :LANE] = zeros
    wout_ref[0, 0, 0 * HALF:1 * HALF, LANE:] = zeros
    wout_ref[0, 0, 1 * HALF:2 * HALF, LANE:] = zeros
    wout_ref[0, 0, 2 * HALF:3 * HALF, LANE:] = cre.astype(BF16)
    wout_ref[0, 0, 3 * HALF:4 * HALF, LANE:] = (-cim).astype(BF16)
    skip = (jnp.dot(wbr, cre, precision=lax.Precision.HIGHEST, preferred_element_type=F32)
            - jnp.dot(wbi, cim, precision=lax.Precision.HIGHEST, preferred_element_type=F32))
    wskip_ref[0, 0, :, :LANE] = skip.astype(BF16)
    wskip_ref[0, 0, :, LANE:] = jnp.zeros((LANE, LANE), BF16)


def _block_diag_in(b):
    nl, g, p, h = b.shape
    nb = g // GROUPS_PER_BLOCK
    bt = b.transpose(0, 1, 3, 2).reshape(nl, nb, GROUPS_PER_BLOCK, h, p)
    eye = jnp.eye(GROUPS_PER_BLOCK, dtype=b.dtype)
    out = bt[:, :, :, :, None, :] * eye[None, None, :, None, :, None]
    return out.reshape(nl, nb, GROUPS_PER_BLOCK * h, GROUPS_PER_BLOCK * p)


def _block_diag_out(c):
    nl, g, h, p = c.shape
    nb = g // GROUPS_PER_BLOCK
    ct = c.transpose(0, 1, 3, 2).reshape(nl, nb, GROUPS_PER_BLOCK, p, h)
    eye = jnp.eye(GROUPS_PER_BLOCK, dtype=c.dtype)
    out = ct[:, :, :, :, None, :] * eye[None, None, :, None, :, None]
    return out.reshape(nl, nb, GROUPS_PER_BLOCK * p, GROUPS_PER_BLOCK * h)


def _discretize(lam_re, lam_im, log_dt, b_re, b_im, c_re, c_im):
    nl, g, p = lam_re.shape
    nb = g // GROUPS_PER_BLOCK

    def rows(v):
        return v.reshape(nl, nb, 1, HALF)

    def cols(v):
        return v.reshape(nl, nb, HALF, 1)

    def spec(r, c):
        return pl.BlockSpec((1, 1, r, c), lambda l, k: (l, k, 0, 0))

    win, wout, wskip, lam2 = pl.pallas_call(
        _discretize_kernel,
        grid=(nl, nb),
        in_specs=[spec(1, HALF)] * 3 + [spec(HALF, 1)] * 3 + [spec(LANE, HALF)] * 2 + [spec(HALF, LANE)] * 2,
        out_specs=[spec(2 * LANE, 2 * HALF), spec(4 * HALF, 2 * LANE), spec(LANE, 2 * LANE), spec(1, 2 * HALF)],
        out_shape=[jax.ShapeDtypeStruct((nl, nb, 2 * LANE, 2 * HALF), BF16),
                   jax.ShapeDtypeStruct((nl, nb, 4 * HALF, 2 * LANE), BF16),
                   jax.ShapeDtypeStruct((nl, nb, LANE, 2 * LANE), BF16),
                   jax.ShapeDtypeStruct((nl, nb, 1, 2 * HALF), F32)],
        compiler_params=_params(2),
        name="s5_discretize",
    )(rows(lam_re), rows(lam_im), rows(log_dt), cols(lam_re), cols(lam_im), cols(log_dt),
      _block_diag_in(b_re), _block_diag_in(b_im), _block_diag_out(c_re), _block_diag_out(c_im))
    return win, wout, wskip, lam2.reshape(nl, nb, 2 * HALF)


def _slab_pitch(rows):
    assert rows % SUBLANE == 0
    return rows + SUBLANE // 2


def _split_state(st):
    return (tuple(st[:, p * LANE:(p + 1) * LANE] for p in range(N_PAIRS)),
            tuple(st[:, HALF + p * LANE:HALF + (p + 1) * LANE] for p in range(N_PAIRS)))


def _s5_mixer_kernel(x_ref, g2_ref, g3_ref, win_ref, wout_ref, wskip_ref, lam_ref, d_ref, wglu_ref, bglu_ref,
                     init_ref, o_ref, final_ref, state_ref, y_ref, s_ref, prev_ref=None,
                     *, n_blocks, n_seq, n_t, d_model):
    tile = pl.program_id(1)
    rows = n_seq * n_t
    pitch = _slab_pitch(rows)

    @pl.when(tile == 0)
    def _():
        state_ref[...] = init_ref[...]

    lre, lim = _split_state(lam_ref[...])
    xe = x_ref[0, :, :d_model]
    xo = x_ref[0, :, d_model:]
    ue = _rms(xe, g2_ref[...])
    uo = _rms(xo, g2_ref[...])
    ueb = ue.astype(BF16)
    uob = uo.astype(BF16)


    for k in range(n_blocks):
        lhs = jnp.concatenate([ueb[:, k * LANE:(k + 1) * LANE], uob[:, k * LANE:(k + 1) * LANE]], axis=1)
        v = jnp.dot(lhs, win_ref[k], preferred_element_type=F32)
        for j in range(N_SLABS):
            s_ref[j, pl.ds(k * pitch + 1, rows), :] = v[:, j * LANE:(j + 1) * LANE]

    def seq_body(seq, carry):
        xre, xim = _split_state(state_ref[seq])
        if n_seq == 1:
            before = pl.ds(0, SUBLANE, stride=pitch)
            for p in range(N_PAIRS):
                s_ref[p, before, :] = xre[p]
                s_ref[N_PAIRS + p, before, :] = xim[p]

        def step(t, st):
            xre, xim = st
            m = seq * n_t + t
            pair = pl.ds(m + 1, SUBLANE, stride=pitch)
            nre, nim = [], []
            for p in range(N_PAIRS):
                if n_seq > 1:
                    prev_ref[p, pl.ds(m, SUBLANE, stride=pitch), :] = xre[p]
                    prev_ref[N_PAIRS + p, pl.ds(m, SUBLANE, stride=pitch), :] = xim[p]
                re = lre[p] * xre[p] - lim[p] * xim[p] + s_ref[p, pair, :]
                im = lre[p] * xim[p] + lim[p] * xre[p] + s_ref[N_PAIRS + p, pair, :]
                s_ref[p, pair, :] = re
                s_ref[N_PAIRS + p, pair, :] = im
                nre.append(re)
                nim.append(im)
            return tuple(nre), tuple(nim)

        xre, xim = lax.fori_loop(0, n_t, step, (xre, xim), unroll=min(n_t, 8))
        state_ref[seq] = jnp.concatenate(list(xre) + list(xim), axis=1)
        return carry

    if n_seq == 1:
        seq_body(0, 0)
    else:
        lax.fori_loop(0, n_seq, seq_body, 0)

    before_ref = s_ref if n_seq == 1 else prev_ref
    for k in range(n_blocks):
        xb = jnp.concatenate([before_ref[j, pl.ds(k * pitch, rows), :] for j in range(N_SLABS)], axis=1)
        xa = jnp.concatenate([s_ref[j, pl.ds(k * pitch + 1, rows), :] for j in range(N_SLABS)], axis=1)
        yk = jnp.dot(jnp.concatenate([xb, xa], axis=1).astype(BF16), wout_ref[k], preferred_element_type=F32)
        yk = yk + jnp.dot(ueb[:, k * LANE:(k + 1) * LANE], wskip_ref[k], preferred_element_type=F32)
        y_ref[:, k * LANE:(k + 1) * LANE] = yk[:, :LANE]
        y_ref[:, d_model + k * LANE:d_model + (k + 1) * LANE] = yk[:, LANE:]
    y = jnp.concatenate([y_ref[:, :d_model] + d_ref[...] * ue, y_ref[:, d_model:] + d_ref[...] * uo], axis=0)
    h = jnp.dot(jax.nn.gelu(y).astype(BF16), wglu_ref[...], preferred_element_type=F32) + bglu_ref[...]
    r = _rms(h[:, :d_model] * _sigmoid(h[:, d_model:]), g3_ref[...])
    o_ref[0] = jnp.concatenate([xe + r[:rows], xo + r[rows:]], axis=1)

    @pl.when(tile == pl.num_programs(1) - 1)
    def _():
        final_ref[...] = state_ref[...]


def _s5_layer(x, g2, g3, win, wout, wskip, lam2, d, wglu, bglu, init, layer, *, n_seq, n_t):
    nb_rows, s2, dm2 = x.shape
    dm = dm2 // 2
    n_blocks = win.shape[1]
    assert n_blocks == SUBLANE
    rows = n_seq * n_t
    n_tiles = s2 // rows
    assert n_seq == 1 or n_tiles == 1
    row = pl.BlockSpec((1, rows, dm2), lambda b, i: (b, i, 0))
    vec = _const_spec((1, dm))
    state = pl.BlockSpec((n_seq, SUBLANE, 2 * HALF), lambda b, i: (b, 0, 0))
    slab = pltpu.VMEM((N_SLABS, n_blocks * _slab_pitch(rows), LANE), F32)
    return pl.pallas_call(
        functools.partial(_s5_mixer_kernel, n_blocks=n_blocks, n_seq=n_seq, n_t=n_t, d_model=dm),
        grid=(nb_rows, n_tiles),
        in_specs=[row, vec, vec, _layer_spec(win.shape, layer), _layer_spec(wout.shape, layer),
                  _layer_spec(wskip.shape, layer), _layer_spec(lam2.shape, layer), vec,
                  _layer_spec(wglu.shape, layer), _const_spec((1, 2 * dm)), state],
        out_specs=[row, state],
        out_shape=[jax.ShapeDtypeStruct(x.shape, F32),
                   jax.ShapeDtypeStruct(init.shape, F32)],
        scratch_shapes=[pltpu.VMEM((n_seq, SUBLANE, 2 * HALF), F32), pltpu.VMEM((rows, dm2), F32), slab]
        + ([slab] if n_seq > 1 else []),
        compiler_params=_params(2),
        name="s5_mixer",
    )(x, g2, g3, win, wout, wskip, lam2, d, wglu, bglu, init)


def _state_to_rows(re, im):
    b = re.shape[0]
    return jnp.concatenate([re.reshape(b, -1, HALF), im.reshape(b, -1, HALF)], axis=-1)


def _rows_to_state(s, n_groups):
    b = s.shape[0]
    return (s[:, :, :HALF].reshape(b, n_groups, STATE_DIM), s[:, :, HALF:].reshape(b, n_groups, STATE_DIM))


def _t5_bucket_np(dist):
    n = np.maximum(dist, 0)
    max_exact = NUM_BUCKETS // 2
    nf = np.maximum(n, 1).astype(np.float32)
    large = max_exact + (np.log(nf / np.float32(max_exact)) / np.float32(math.log(MAX_DISTANCE / max_exact))
                         * np.float32(NUM_BUCKETS - max_exact)).astype(np.int32)
    large = np.minimum(large, NUM_BUCKETS - 1)
    return np.where(n < max_exact, n, large).astype(np.int32)


def _bucket_table(n_q, n_k, q_offset, n_q_pad, n_k_pad, first_key=0):
    dist = (np.arange(n_q)[:, None] + q_offset) - np.arange(n_k)[None, :]
    valid = (dist >= 0) & (dist < WINDOW) & (np.arange(n_k)[None, :] >= first_key)
    table = np.full((n_q_pad, n_k_pad), -1, np.int32)
    table[:n_q, :n_k] = np.where(valid, _t5_bucket_np(dist), -1)
    return table


def _bias_kernel(rel_ref, idx_ref, o_ref):
    h = pl.program_id(0)
    idx = idx_ref[...]
    acc = jnp.full(idx.shape, NEG_INF, F32)
    for b in range(NUM_BUCKETS):
        acc = jnp.where(idx == b, rel_ref[b, h], acc)
    o_ref[0] = acc


def _bias_planes(rel_bias, table):
    n_heads = rel_bias.shape[1]
    r, c = table.shape
    return pl.pallas_call(
        _bias_kernel,
        grid=(n_heads,),
        in_specs=[pl.BlockSpec(memory_space=pltpu.SMEM), pl.BlockSpec((r, c), lambda h: (0, 0))],
        out_specs=pl.BlockSpec((1, r, c), lambda h: (h, 0, 0)),
        out_shape=jax.ShapeDtypeStruct((n_heads, r, c), F32),
        compiler_params=_params(),
        name="attn_bias",
    )(rel_bias, jnp.asarray(table))


def _kv_head_rows(kv, g):
    lane = lax.broadcasted_iota(jnp.int32, kv.shape, 1)
    swapped = pltpu.roll(kv, HEAD_DIM, axis=1)
    even, odd = (kv, swapped) if g == 0 else (swapped, kv)
    return jnp.where(lane < HEAD_DIM, even, 0.0), jnp.where(lane >= HEAD_DIM, odd, 0.0)


def _pair_layout_t(planes, n_kv, n_q):
    h, _, n_k = planes.shape
    pairs = h // n_kv // 2
    p = planes[:, :n_q].reshape(n_kv, pairs, 2, n_q, n_k).transpose(0, 2, 4, 1, 3)
    return p.reshape(n_kv, 2 * n_k, pairs * n_q)


def _sink_cols(sinks, n_kv, n_q):
    h = sinks.shape[0]
    pairs = h // n_kv // 2
    s = sinks.reshape(n_kv, pairs, 2).transpose(0, 2, 1)[..., None]
    return jnp.broadcast_to(s, (n_kv, 2, pairs, n_q)).reshape(n_kv, 2, pairs * n_q)


def _attend_t(chains):
    scores = []
    for q_t, k_even, k_odd, _, _, bias, _ in chains:
        ke = jnp.concatenate([k_even, k_odd], axis=0).astype(BF16)
        scores.append(jnp.dot(ke, q_t, preferred_element_type=F32) + bias)
    weights = []
    for s, chain in zip(scores, chains):
        sink = chain[6]
        n_k = s.shape[0] // 2
        probs, inv = [], []
        for par in range(2):
            sp = s[par * n_k:(par + 1) * n_k]
            sk = sink[par:par + 1]
            mx = jnp.maximum(jnp.max(sp, axis=0, keepdims=True), sk)
            p = jnp.exp(sp - mx)
            den = jnp.sum(p, axis=0, keepdims=True) + jnp.exp(sk - mx)
            probs.append(p.astype(BF16))
            inv.append(1.0 / den)
        weights.append((jnp.concatenate(probs, axis=0), inv))
    outs = []
    for (p_t, inv), chain in zip(weights, chains):
        ve = jnp.concatenate([chain[3], chain[4]], axis=0).astype(BF16)
        o = lax.dot_general(ve, p_t, (((0,), (0,)), ((), ())), preferred_element_type=F32)
        row = lax.broadcasted_iota(jnp.int32, o.shape, 0)
        outs.append(o * jnp.where(row < HEAD_DIM, inv[0], inv[1]))
    return outs


def _prompt_attn_kernel(x_ref, kvc_ref, kvp_ref, bias_ref, sink_ref, g2_ref, g3_ref, wq_ref, bq_ref,
                        wo_ref, bo_ref, o_ref, *, n_kv, n_sub, steps_per_seq, scale):
    blk = WINDOW
    x = x_ref[...]
    u = _rms(x, g2_ref[...]).astype(BF16)
    q_t = lax.dot_general(wq_ref[...], u, (((1,), (1,)), ((), ())), preferred_element_type=F32)
    q_t = ((q_t + jnp.concatenate([bq_ref[...]] * n_sub, axis=1)) * scale).astype(BF16)
    kv = jnp.concatenate([kvp_ref[...], kvc_ref[...]], axis=0)
    pairs = q_t.shape[0] // LANE // n_kv
    first = jnp.where(pl.program_id(0) % steps_per_seq == 0, 1, 0)
    k_rows = [_kv_head_rows(kv[:, :n_kv * HEAD_DIM], g) for g in range(n_kv)]
    v_rows = [_kv_head_rows(kv[:, n_kv * HEAD_DIM:], g) for g in range(n_kv)]
    chains = []
    for s in range(n_sub):
        keys = slice(s * blk, (s + 2) * blk)
        for g in range(n_kv):
            qp = jnp.concatenate([q_t[(g * pairs + i) * LANE:(g * pairs + i + 1) * LANE, s * blk:(s + 1) * blk]
                                  for i in range(pairs)], axis=1)
            bias = bias_ref[first, g] if s == 0 else bias_ref[0, g]
            chains.append((qp, k_rows[g][0][keys], k_rows[g][1][keys], v_rows[g][0][keys],
                           v_rows[g][1][keys], bias, sink_ref[g]))
    outs = _attend_t(chains)
    o_t = jnp.concatenate(
        [jnp.concatenate([outs[s * n_kv + g][:, i * blk:(i + 1) * blk] for s in range(n_sub)], axis=1)
         for g in range(n_kv) for i in range(pairs)], axis=0).astype(BF16)
    a = lax.dot_general(o_t, wo_ref[...], (((0,), (0,)), ((), ())), preferred_element_type=F32) + bo_ref[...]
    o_ref[...] = x + _rms(a, g3_ref[...])


def _prompt_attention(x, kv, bias, sink, g2, g3, wq_t, bq_col, wo, bo, layer, *, blocks_per_seq, n_kv):
    t, d = x.shape
    blk = WINDOW
    n_sub = ATTN_BLOCKS_PER_STEP
    assert n_kv * HEAD_DIM == LANE and blocks_per_seq % n_sub == 0
    row = pl.BlockSpec((n_sub * blk, d), lambda i: (i, 0))
    kvw = kv.shape[1]
    return pl.pallas_call(
        functools.partial(_prompt_attn_kernel, n_kv=n_kv, n_sub=n_sub,
                          steps_per_seq=blocks_per_seq // n_sub, scale=1.0 / math.sqrt(HEAD_DIM)),
        grid=(t // (n_sub * blk),),
        in_specs=[row,
                  pl.BlockSpec((n_sub * blk, kvw), lambda i: (i, 0)),
                  pl.BlockSpec((blk, kvw), lambda i: (jnp.maximum(i * n_sub - 1, 0), 0)),
                  _const_spec(bias.shape),
                  _const_spec(sink.shape), _const_spec((1, d)), _const_spec((1, d)),
                  _layer_spec(wq_t.shape, layer), _const_spec(bq_col.shape),
                  _layer_spec(wo.shape, layer), _const_spec((1, d))],
        out_specs=row,
        out_shape=jax.ShapeDtypeStruct((t, d), F32),
        compiler_params=_params(),
        name="prompt_attention",
    )(x, kv, kv, bias, sink, g2, g3, wq_t, bq_col, wo, bo)


def _sample_attn_kernel(q_ref, k_ref, v_ref, bias_ref, sink_ref, o_ref, *, n_kv, scale):
    chains = []
    for b in range(q_ref.shape[0]):
        k = k_ref[b]
        v = v_ref[b]
        for g in range(n_kv):
            chains.append(((q_ref[b, g] * scale).astype(BF16), *_kv_head_rows(k, g), *_kv_head_rows(v, g),
                           bias_ref[g], sink_ref[g]))
    outs = _attend_t(chains)
    for b in range(q_ref.shape[0]):
        for g in range(n_kv):
            o_ref[b, g] = outs[b * n_kv + g]


def _sample_attention(q_t, k_full, v_full, bias, sink, *, n_kv):
    nb, _, _, n_l = q_t.shape
    n_k = k_full.shape[1]
    bb = SUBLANE
    return pl.pallas_call(
        functools.partial(_sample_attn_kernel, n_kv=n_kv, scale=1.0 / math.sqrt(HEAD_DIM)),
        grid=(nb // bb,),
        in_specs=[pl.BlockSpec((bb, n_kv, LANE, n_l), lambda i: (i, 0, 0, 0)),
                  pl.BlockSpec((bb, n_k, LANE), lambda i: (i, 0, 0)),
                  pl.BlockSpec((bb, n_k, LANE), lambda i: (i, 0, 0)),
                  _const_spec(bias.shape), _const_spec(sink.shape)],
        out_specs=pl.BlockSpec((bb, n_kv, LANE, n_l), lambda i: (i, 0, 0, 0)),
        out_shape=jax.ShapeDtypeStruct(q_t.shape, F32),
        compiler_params=_params(),
        name="sample_attention",
    )(q_t, k_full, v_full, bias, sink)


def kernel(x_prompt, x_sample, state_ssm_re, state_ssm_im, cache_win_k, cache_win_v, norm_g, ffn1_w_gu, ffn1_w_down, ffn2_w_gu, ffn2_w_down, ssm_lambda_re, ssm_lambda_im, ssm_log_dt, ssm_b_re, ssm_b_im, ssm_c_re, ssm_c_im, ssm_d, ssm_w_glu, ssm_b_glu, kv_norm_g, w_kv, b_kv, attn_w_q, attn_b_q, attn_sinks, attn_w_o, attn_b_o, rel_bias):
    bsz, seq, dm = x_prompt.shape
    dec_b, dec_s, _ = x_sample.shape
    depth = norm_g.shape[0]
    n_a = ssm_lambda_re.shape[0]
    n_groups = ssm_lambda_re.shape[1]
    n_heads = attn_sinks.shape[1]
    n_kv = cache_win_k.shape[2]
    n_past = cache_win_k.shape[1]
    pairs = n_heads // n_kv // 2
    n_k_pad = 2 * WINDOW
    assert seq % ROW_TILE == 0 and seq % WINDOW == 0 and dec_s % 2 == 0
    assert dec_b * dec_s == ROW_TILE and n_past + dec_s <= n_k_pad and dec_b % SUBLANE == 0
    assert n_kv == 2 and dm == n_heads * HEAD_DIM and n_groups == SUBLANE * GROUPS_PER_BLOCK

    bf = lambda w: w.astype(BF16)
    row = lambda v: v.reshape(1, -1)
    s5_w = _discretize(ssm_lambda_re, ssm_lambda_im, ssm_log_dt, ssm_b_re, ssm_b_im, ssm_c_re, ssm_c_im)
    f1gu, f1d, f2gu, f2d = bf(ffn1_w_gu), bf(ffn1_w_down), bf(ffn2_w_gu), bf(ffn2_w_down)
    wglu, wkv, wq, wo = bf(ssm_w_glu), bf(w_kv)[None], bf(attn_w_q), bf(attn_w_o)

    tab_p = _bucket_table(WINDOW, 2 * WINDOW, WINDOW, WINDOW, 2 * WINDOW)
    tab_p0 = _bucket_table(WINDOW, 2 * WINDOW, WINDOW, WINDOW, 2 * WINDOW, first_key=WINDOW)
    tab_s = _bucket_table(dec_s, n_past + dec_s, n_past, SUBLANE, n_k_pad)
    wq_t = wq.transpose(0, 2, 1)
    bias_p = jnp.stack([_pair_layout_t(_bias_planes(rel_bias, tab_p), n_kv, WINDOW),
                        _pair_layout_t(_bias_planes(rel_bias, tab_p0), n_kv, WINDOW)])
    bias_s = _pair_layout_t(_bias_planes(rel_bias, tab_s), n_kv, dec_s)

    xp = x_prompt.reshape(bsz * seq, dm)
    xs = x_sample.reshape(dec_b * dec_s, dm)
    zero_state = jnp.zeros((bsz, SUBLANE, 2 * HALF), F32)

    ends_p, ends_s = [], []
    kv_p = k_full = v_full = None
    half = n_kv * HEAD_DIM
    for l in range(depth):
        g = [row(norm_g[l, i]) for i in range(norm_g.shape[1])]
        if l == n_a:
            kv_p = _norm_proj(xp, row(kv_norm_g), wkv, row(b_kv), 0)
            kv_s = _norm_proj(xs, row(kv_norm_g), wkv, row(b_kv), 0).reshape(dec_b, dec_s, 2 * half)
            k_full = jnp.concatenate([cache_win_k.reshape(dec_b, n_past, half), kv_s[:, :, :half]], axis=1)
            v_full = jnp.concatenate([cache_win_v.reshape(dec_b, n_past, half), kv_s[:, :, half:]], axis=1)
        xp = _ffn(xp, g[0], g[1], f1gu, f1d, l)
        xs = _ffn(xs, g[0], g[1], f1gu, f1d, l)
        if l < n_a:
            common = (g[2], g[3], *s5_w, row(ssm_d[l]), wglu, row(ssm_b_glu[l]))
            xp3, fin_p = _s5_layer(xp.reshape(bsz, seq // 2, 2 * dm), *common, zero_state, l,
                                   n_seq=1, n_t=ROW_TILE // 2)
            xs3, fin_s = _s5_layer(xs.reshape(1, dec_b * dec_s // 2, 2 * dm), *common,
                                   _state_to_rows(state_ssm_re[l], state_ssm_im[l]), l,
                                   n_seq=dec_b, n_t=dec_s // 2)
            xp, xs = xp3.reshape(bsz * seq, dm), xs3.reshape(dec_b * dec_s, dm)
            ends_p.append(_rows_to_state(fin_p, n_groups))
            ends_s.append(_rows_to_state(fin_s, n_groups))
        else:
            bl = l - n_a
            bq_l, bo_l = row(attn_b_q[bl]), row(attn_b_o[bl])
            bq_col = jnp.broadcast_to(attn_b_q[bl][:, None], (dm, WINDOW))
            xp = _prompt_attention(xp, kv_p, bias_p, _sink_cols(attn_sinks[bl], n_kv, WINDOW), g[2], g[3],
                                   wq_t, bq_col, wo, bo_l, bl, blocks_per_seq=seq // WINDOW, n_kv=n_kv)
            q = _norm_proj(xs, g[2], wq, bq_l, bl)
            q = q.reshape(dec_b, dec_s, n_kv, pairs, LANE).transpose(0, 2, 4, 3, 1)
            q = q.reshape(dec_b, n_kv, LANE, pairs * dec_s)
            pad = n_k_pad - (n_past + dec_s)
            o = _sample_attention(q, jnp.pad(k_full, ((0, 0), (0, pad), (0, 0))),
                                  jnp.pad(v_full, ((0, 0), (0, pad), (0, 0))),
                                  bias_s, _sink_cols(attn_sinks[bl], n_kv, dec_s), n_kv=n_kv)
            o = o.reshape(dec_b, n_kv, LANE, pairs, dec_s).transpose(0, 4, 1, 3, 2).reshape(dec_b * dec_s, dm)
            xs = _out_proj_residual(xs, o, g[3], wo, bo_l, bl)
        xp = _ffn(xp, g[4], g[5], f2gu, f2d, l)
        xs = _ffn(xs, g[4], g[5], f2gu, f2d, l)

    kv_p3 = kv_p.reshape(bsz, seq, 2 * half)
    new_k_p = kv_p3[:, -WINDOW:, :half].reshape(bsz, WINDOW, n_kv, HEAD_DIM)
    new_v_p = kv_p3[:, -WINDOW:, half:].reshape(bsz, WINDOW, n_kv, HEAD_DIM)
    new_k_s = k_full[:, -WINDOW:].reshape(dec_b, WINDOW, n_kv, HEAD_DIM)
    new_v_s = v_full[:, -WINDOW:].reshape(dec_b, WINDOW, n_kv, HEAD_DIM)
    return (xp.reshape(bsz, seq, dm), xs.reshape(dec_b, dec_s, dm),
            jnp.stack([e[0] for e in ends_p]), jnp.stack([e[1] for e in ends_p]),
            new_k_p, new_v_p,
            jnp.stack([e[0] for e in ends_s]), jnp.stack([e[1] for e in ends_s]),
            new_k_s, new_v_s)
```

```python
import functools
import math

import numpy as np
import jax
import jax.numpy as jnp
from jax import lax
from jax.experimental import pallas as pl
from jax.experimental.pallas import tpu as pltpu

F32 = jnp.float32
BF16 = jnp.bfloat16

LANE = 128
SUBLANE = 8
VMEM_LIMIT_BYTES = 56 * 1024 * 1024

RMS_EPS = 1e-6
GROUP_SIZE = 16
STATE_DIM = 64
HEAD_DIM = 64
WINDOW = 128
NUM_BUCKETS = 32
MAX_DISTANCE = WINDOW
GROUPS_PER_BLOCK = LANE // GROUP_SIZE
HALF = GROUPS_PER_BLOCK * STATE_DIM
N_SLABS = 2 * HALF // LANE
N_PAIRS = N_SLABS // 2
ROW_TILE = 512
ATTN_BLOCKS_PER_STEP = 2
NEG_INF = float("-inf")


def _params(n_axes=1):
    return pltpu.CompilerParams(dimension_semantics=("arbitrary",) * n_axes,
                                vmem_limit_bytes=VMEM_LIMIT_BYTES)


def _const_spec(shape):
    nd = len(shape)
    return pl.BlockSpec(shape, lambda *_: (0,) * nd, pipeline_mode=pl.Buffered(1))


def _layer_spec(shape, layer):
    nd = len(shape)
    return pl.BlockSpec((None,) + tuple(shape[1:]), lambda *_: (layer,) + (0,) * (nd - 1),
                        pipeline_mode=pl.Buffered(1))


def _rms(x, g):
    ms = jnp.mean(x * x, axis=-1, keepdims=True)
    return x * lax.rsqrt(ms + RMS_EPS) * g


def _aligned(row):
    return row if isinstance(row, int) else pl.multiple_of(row, SUBLANE)


def _sigmoid(x):
    return 1.0 / (1.0 + jnp.exp(-x))


def _ffn_kernel(x_ref, ga_ref, gb_ref, wgu_ref, wd_ref, o_ref, *slab_ref, d_ff, pair_in, pair_out):
    d = ga_ref.shape[1]
    if pair_in:
        x = jnp.concatenate([x_ref[:, :d], x_ref[:, d:]], axis=0)
    else:
        x = x_ref[...]
    half = x.shape[0] // 2
    xn = _rms(x, ga_ref[...]).astype(BF16)
    gu = jnp.dot(xn, wgu_ref[...], preferred_element_type=F32)
    gate = gu[:, :d_ff]
    up = gu[:, d_ff:]
    h = (gate * _sigmoid(gate) * up).astype(BF16)
    y = jnp.dot(h, wd_ref[...], preferred_element_type=F32)
    out = x + 0.5 * _rms(y, gb_ref[...])
    if pair_in == pair_out:
        o_ref[...] = jnp.concatenate([out[:half], out[half:]], axis=1) if pair_in else out
        return
    (slab,) = slab_ref
    even = pl.ds(0, half, stride=2)
    odd = pl.ds(1, half, stride=2)
    for j in range(d // LANE):
        lanes = slice(j * LANE, (j + 1) * LANE)
        if pair_out:
            slab[j] = out[:, lanes]
            o_ref[:, lanes] = slab[j, even, :]
            o_ref[:, d + j * LANE:d + (j + 1) * LANE] = slab[j, odd, :]
        else:
            slab[j, even, :] = out[:half, lanes]
            slab[j, odd, :] = out[half:, lanes]
            o_ref[:, lanes] = slab[j]


def _ffn(x, ga, gb, wgu, wd, layer, pair_in=False, pair_out=False):
    d = ga.shape[1]
    t = x.shape[0] * (2 if pair_in else 1)
    d_ff = wd.shape[1]
    tm = min(ROW_TILE, t)

    def rows(paired):
        return pl.BlockSpec((tm // 2, 2 * d) if paired else (tm, d), lambda i: (i, 0))

    return pl.pallas_call(
        functools.partial(_ffn_kernel, d_ff=d_ff, pair_in=pair_in, pair_out=pair_out),
        grid=(t // tm,),
        in_specs=[rows(pair_in), _const_spec((1, d)), _const_spec((1, d)),
                  _layer_spec(wgu.shape, layer), _layer_spec(wd.shape, layer)],
        out_specs=rows(pair_out),
        out_shape=jax.ShapeDtypeStruct((t // 2, 2 * d) if pair_out else (t, d), F32),
        scratch_shapes=[pltpu.VMEM((d // LANE, tm, LANE), F32)] if pair_in != pair_out else [],
        compiler_params=_params(),
        name="ffn",
    )(x, ga, gb, wgu, wd)


def _proj_kernel(x_ref, g_ref, w_ref, b_ref, o_ref):
    xn = _rms(x_ref[...], g_ref[...]).astype(BF16)
    o_ref[...] = jnp.dot(xn, w_ref[...], preferred_element_type=F32) + b_ref[...]


def _norm_proj(x, g, w, b, layer):
    t, d = x.shape
    n = w.shape[-1]
    tm = min(ROW_TILE, t)
    return pl.pallas_call(
        _proj_kernel,
        grid=(t // tm,),
        in_specs=[pl.BlockSpec((tm, d), lambda i: (i, 0)), _const_spec((1, d)),
                  _layer_spec(w.shape, layer), _const_spec((1, n))],
        out_specs=pl.BlockSpec((tm, n), lambda i: (i, 0)),
        out_shape=jax.ShapeDtypeStruct((t, n), F32),
        compiler_params=_params(),
        name="norm_proj",
    )(x, g, w, b)


def _out_proj_kernel(x_ref, o_ref_in, g_ref, w_ref, b_ref, out_ref):
    a = jnp.dot(o_ref_in[...].astype(BF16), w_ref[...], preferred_element_type=F32) + b_ref[...]
    out_ref[...] = x_ref[...] + _rms(a, g_ref[...])


def _out_proj_residual(x, o, g, w, b, layer):
    t, d = x.shape
    tm = min(ROW_TILE, t)
    row = pl.BlockSpec((tm, d), lambda i: (i, 0))
    return pl.pallas_call(
        _out_proj_kernel,
        grid=(t // tm,),
        in_specs=[row, pl.BlockSpec((tm, o.shape[1]), lambda i: (i, 0)), _const_spec((1, d)),
                  _layer_spec(w.shape, layer), _const_spec((1, d))],
        out_specs=row,
        out_shape=jax.ShapeDtypeStruct((t, d), F32),
        compiler_params=_params(),
        name="out_proj_residual",
    )(x, o, g, w, b)


def _lam_bar(lam_re, lam_im, log_dt):
    dt = jnp.exp(log_dt)
    mag = jnp.exp(lam_re * dt)
    return mag * jnp.cos(lam_im * dt), mag * jnp.sin(lam_im * dt)


def _discretize_kernel(lre_ref, lim_ref, ldt_ref, lre_col_ref, lim_col_ref, ldt_col_ref,
                       bre_ref, bim_ref, cre_ref, cim_ref, win_ref, wout_ref, wskip_ref, lam2_ref):
    a = lre_ref[0, 0]
    b = lim_ref[0, 0]
    lbr, lbi = _lam_bar(a, b, ldt_ref[0, 0])
    den = a * a + b * b
    cr = ((lbr - 1.0) * a + lbi * b) / den
    ci = (lbi * a - (lbr - 1.0) * b) / den
    bre = bre_ref[0, 0]
    bim = bim_ref[0, 0]
    wbr = cr * bre - ci * bim
    wbi = cr * bim + ci * bre
    win_ref[0, 0, :LANE, :HALF] = (lbr * wbr - lbi * wbi).astype(BF16)
    win_ref[0, 0, :LANE, HALF:] = (lbr * wbi + lbi * wbr).astype(BF16)
    win_ref[0, 0, LANE:, :HALF] = wbr.astype(BF16)
    win_ref[0, 0, LANE:, HALF:] = wbi.astype(BF16)
    lam2_ref[0, 0, :, :HALF] = lbr * lbr - lbi * lbi
    lam2_ref[0, 0, :, HALF:] = 2.0 * lbr * lbi
    lcr, lci = _lam_bar(lre_col_ref[0, 0], lim_col_ref[0, 0], ldt_col_ref[0, 0])
    cre = cre_ref[0, 0]
    cim = cim_ref[0, 0]
    zeros = jnp.zeros((HALF, LANE), BF16)
    wout_ref[0, 0, 0 * HALF:1 * HALF, :LANE] = (cre * lcr - cim * lci).astype(BF16)
    wout_ref[0, 0, 1 * HALF:2 * HALF, :LANE] = (-(cre * lci + cim * lcr)).astype(BF16)
    wout_ref[0, 0, 2 * HALF:3 * HALF, :LANE] = zeros
    wout_ref[0, 0, 3 * HALF:4 * HALF, :LANE] = zeros
    wout_ref[0, 0, 0 * HALF:1 * HALF, LANE:] = zeros
    wout_ref[0, 0, 1 * HALF:2 * HALF, LANE:] = zeros
    wout_ref[0, 0, 2 * HALF:3 * HALF, LANE:] = cre.astype(BF16)
    wout_ref[0, 0, 3 * HALF:4 * HALF, LANE:] = (-cim).astype(BF16)
    skip = (jnp.dot(wbr, cre, precision=lax.Precision.HIGHEST, preferred_element_type=F32)
            - jnp.dot(wbi, cim, precision=lax.Precision.HIGHEST, preferred_element_type=F32))
    wskip_ref[0, 0, :, :LANE] = skip.astype(BF16)
    wskip_ref[0, 0, :, LANE:] = jnp.zeros((LANE, LANE), BF16)


def _block_diag_in(b):
    nl, g, p, h = b.shape
    nb = g // GROUPS_PER_BLOCK
    bt = b.transpose(0, 1, 3, 2).reshape(nl, nb, GROUPS_PER_BLOCK, h, p)
    eye = jnp.eye(GROUPS_PER_BLOCK, dtype=b.dtype)
    out = bt[:, :, :, :, None, :] * eye[None, None, :, None, :, None]
    return out.reshape(nl, nb, GROUPS_PER_BLOCK * h, GROUPS_PER_BLOCK * p)


def _block_diag_out(c):
    nl, g, h, p = c.shape
    nb = g // GROUPS_PER_BLOCK
    ct = c.transpose(0, 1, 3, 2).reshape(nl, nb, GROUPS_PER_BLOCK, p, h)
    eye = jnp.eye(GROUPS_PER_BLOCK, dtype=c.dtype)
    out = ct[:, :, :, :, None, :] * eye[None, None, :, None, :, None]
    return out.reshape(nl, nb, GROUPS_PER_BLOCK * p, GROUPS_PER_BLOCK * h)


def _discretize(lam_re, lam_im, log_dt, b_re, b_im, c_re, c_im):
    nl, g, p = lam_re.shape
    nb = g // GROUPS_PER_BLOCK

    def rows(v):
        return v.reshape(nl, nb, 1, HALF)

    def cols(v):
        return v.reshape(nl, nb, HALF, 1)

    def spec(r, c):
        return pl.BlockSpec((1, 1, r, c), lambda l, k: (l, k, 0, 0))

    win, wout, wskip, lam2 = pl.pallas_call(
        _discretize_kernel,
        grid=(nl, nb),
        in_specs=[spec(1, HALF)] * 3 + [spec(HALF, 1)] * 3 + [spec(LANE, HALF)] * 2 + [spec(HALF, LANE)] * 2,
        out_specs=[spec(2 * LANE, 2 * HALF), spec(4 * HALF, 2 * LANE), spec(LANE, 2 * LANE), spec(1, 2 * HALF)],
        out_shape=[jax.ShapeDtypeStruct((nl, nb, 2 * LANE, 2 * HALF), BF16),
                   jax.ShapeDtypeStruct((nl, nb, 4 * HALF, 2 * LANE), BF16),
                   jax.ShapeDtypeStruct((nl, nb, LANE, 2 * LANE), BF16),
                   jax.ShapeDtypeStruct((nl, nb, 1, 2 * HALF), F32)],
        compiler_params=_params(2),
        name="s5_discretize",
    )(rows(lam_re), rows(lam_im), rows(log_dt), cols(lam_re), cols(lam_im), cols(log_dt),
      _block_diag_in(b_re), _block_diag_in(b_im), _block_diag_out(c_re), _block_diag_out(c_im))
    return win, wout, wskip, lam2.reshape(nl, nb, 2 * HALF)


def _slab_pitch(rows):
    assert rows % SUBLANE == 0
    return rows + SUBLANE // 2


def _split_state(st):
    return (tuple(st[:, p * LANE:(p + 1) * LANE] for p in range(N_PAIRS)),
            tuple(st[:, HALF + p * LANE:HALF + (p + 1) * LANE] for p in range(N_PAIRS)))


def _s5_mixer_kernel(x_ref, g2_ref, g3_ref, win_ref, wout_ref, wskip_ref, lam_ref, d_ref, wglu_ref, bglu_ref,
                     init_ref, o_ref, final_ref, state_ref, y_ref, s_ref, prev_ref=None,
                     *, n_blocks, n_seq, n_t, d_model):
    tile = pl.program_id(1)
    rows = n_seq * n_t
    pitch = _slab_pitch(rows)

    @pl.when(tile == 0)
    def _():
        state_ref[...] = init_ref[...]

    lre, lim = _split_state(lam_ref[...])
    xe = x_ref[0, :, :d_model]
    xo = x_ref[0, :, d_model:]
    ue = _rms(xe, g2_ref[...])
    uo = _rms(xo, g2_ref[...])
    ueb = ue.astype(BF16)
    uob = uo.astype(BF16)


    for k in range(n_blocks):
        lhs = jnp.concatenate([ueb[:, k * LANE:(k + 1) * LANE], uob[:, k * LANE:(k + 1) * LANE]], axis=1)
        v = jnp.dot(lhs, win_ref[k], preferred_element_type=F32)
        for j in range(N_SLABS):
            s_ref[j, pl.ds(k * pitch + 1, rows), :] = v[:, j * LANE:(j + 1) * LANE]

    def seq_body(seq, carry):
        xre, xim = _split_state(state_ref[seq])
        if n_seq == 1:
            before = pl.ds(0, SUBLANE, stride=pitch)
            for p in range(N_PAIRS):
                s_ref[p, before, :] = xre[p]
                s_ref[N_PAIRS + p, before, :] = xim[p]

        def step(t, st):
            xre, xim = st
            m = seq * n_t + t
            pair = pl.ds(m + 1, SUBLANE, stride=pitch)
            nre, nim = [], []
            for p in range(N_PAIRS):
                if n_seq > 1:
                    prev_ref[p, pl.ds(m, SUBLANE, stride=pitch), :] = xre[p]
                    prev_ref[N_PAIRS + p, pl.ds(m, SUBLANE, stride=pitch), :] = xim[p]
                re = lre[p] * xre[p] - lim[p] * xim[p] + s_ref[p, pair, :]
                im = lre[p] * xim[p] + lim[p] * xre[p] + s_ref[N_PAIRS + p, pair, :]
                s_ref[p, pair, :] = re
                s_ref[N_PAIRS + p, pair, :] = im
                nre.append(re)
                nim.append(im)
            return tuple(nre), tuple(nim)

        xre, xim = lax.fori_loop(0, n_t, step, (xre, xim), unroll=min(n_t, 8))
        state_ref[seq] = jnp.concatenate(list(xre) + list(xim), axis=1)
        return carry

    if n_seq == 1:
        seq_body(0, 0)
    else:
        lax.fori_loop(0, n_seq, seq_body, 0)

    before_ref = s_ref if n_seq == 1 else prev_ref
    for k in range(n_blocks):
        xb = jnp.concatenate([before_ref[j, pl.ds(k * pitch, rows), :] for j in range(N_SLABS)], axis=1)
        xa = jnp.concatenate([s_ref[j, pl.ds(k * pitch + 1, rows), :] for j in range(N_SLABS)], axis=1)
        yk = jnp.dot(jnp.concatenate([xb, xa], axis=1).astype(BF16), wout_ref[k], preferred_element_type=F32)
        yk = yk + jnp.dot(ueb[:, k * LANE:(k + 1) * LANE], wskip_ref[k], preferred_element_type=F32)
        y_ref[:, k * LANE:(k + 1) * LANE] = yk[:, :LANE]
        y_ref[:, d_model + k * LANE:d_model + (k + 1) * LANE] = yk[:, LANE:]
    y = jnp.concatenate([y_ref[:, :d_model] + d_ref[...] * ue, y_ref[:, d_model:] + d_ref[...] * uo], axis=0)
    h = jnp.dot(jax.nn.gelu(y).astype(BF16), wglu_ref[...], preferred_element_type=F32) + bglu_ref[...]
    r = _rms(h[:, :d_model] * _sigmoid(h[:, d_model:]), g3_ref[...])
    o_ref[0] = jnp.concatenate([xe + r[:rows], xo + r[rows:]], axis=1)

    @pl.when(tile == pl.num_programs(1) - 1)
    def _():
        final_ref[...] = state_ref[...]


def _s5_layer(x, g2, g3, win, wout, wskip, lam2, d, wglu, bglu, init, layer, *, n_seq, n_t):
    nb_rows, s2, dm2 = x.shape
    dm = dm2 // 2
    n_blocks = win.shape[1]
    assert n_blocks == SUBLANE
    rows = n_seq * n_t
    n_tiles = s2 // rows
    assert n_seq == 1 or n_tiles == 1
    row = pl.BlockSpec((1, rows, dm2), lambda b, i: (b, i, 0))
    vec = _const_spec((1, dm))
    state = pl.BlockSpec((n_seq, SUBLANE, 2 * HALF), lambda b, i: (b, 0, 0))
    slab = pltpu.VMEM((N_SLABS, n_blocks * _slab_pitch(rows), LANE), F32)
    return pl.pallas_call(
        functools.partial(_s5_mixer_kernel, n_blocks=n_blocks, n_seq=n_seq, n_t=n_t, d_model=dm),
        grid=(nb_rows, n_tiles),
        in_specs=[row, vec, vec, _layer_spec(win.shape, layer), _layer_spec(wout.shape, layer),
                  _layer_spec(wskip.shape, layer), _layer_spec(lam2.shape, layer), vec,
                  _layer_spec(wglu.shape, layer), _const_spec((1, 2 * dm)), state],
        out_specs=[row, state],
        out_shape=[jax.ShapeDtypeStruct(x.shape, F32),
                   jax.ShapeDtypeStruct(init.shape, F32)],
        scratch_shapes=[pltpu.VMEM((n_seq, SUBLANE, 2 * HALF), F32), pltpu.VMEM((rows, dm2), F32), slab]
        + ([slab] if n_seq > 1 else []),
        compiler_params=_params(2),
        name="s5_mixer",
    )(x, g2, g3, win, wout, wskip, lam2, d, wglu, bglu, init)


def _state_to_rows(re, im):
    b = re.shape[0]
    return jnp.concatenate([re.reshape(b, -1, HALF), im.reshape(b, -1, HALF)], axis=-1)


def _rows_to_state(s, n_groups):
    b = s.shape[0]
    return (s[:, :, :HALF].reshape(b, n_groups, STATE_DIM), s[:, :, HALF:].reshape(b, n_groups, STATE_DIM))


def _t5_bucket_np(dist):
    n = np.maximum(dist, 0)
    max_exact = NUM_BUCKETS // 2
    nf = np.maximum(n, 1).astype(np.float32)
    large = max_exact + (np.log(nf / np.float32(max_exact)) / np.float32(math.log(MAX_DISTANCE / max_exact))
                         * np.float32(NUM_BUCKETS - max_exact)).astype(np.int32)
    large = np.minimum(large, NUM_BUCKETS - 1)
    return np.where(n < max_exact, n, large).astype(np.int32)


def _bucket_table(n_q, n_k, q_offset, n_q_pad, n_k_pad, first_key=0):
    dist = (np.arange(n_q)[:, None] + q_offset) - np.arange(n_k)[None, :]
    valid = (dist >= 0) & (dist < WINDOW) & (np.arange(n_k)[None, :] >= first_key)
    table = np.full((n_q_pad, n_k_pad), -1, np.int32)
    table[:n_q, :n_k] = np.where(valid, _t5_bucket_np(dist), -1)
    return table


def _bias_kernel(rel_ref, idx_ref, o_ref):
    h = pl.program_id(0)
    idx = idx_ref[...]
    acc = jnp.full(idx.shape, NEG_INF, F32)
    for b in range(NUM_BUCKETS):
        acc = jnp.where(idx == b, rel_ref[b, h], acc)
    o_ref[0] = acc


def _bias_planes(rel_bias, table):
    n_heads = rel_bias.shape[1]
    r, c = table.shape
    return pl.pallas_call(
        _bias_kernel,
        grid=(n_heads,),
        in_specs=[pl.BlockSpec(memory_space=pltpu.SMEM), pl.BlockSpec((r, c), lambda h: (0, 0))],
        out_specs=pl.BlockSpec((1, r, c), lambda h: (h, 0, 0)),
        out_shape=jax.ShapeDtypeStruct((n_heads, r, c), F32),
        compiler_params=_params(),
        name="attn_bias",
    )(rel_bias, jnp.asarray(table))


def _kv_head_rows(kv, g):
    lane = lax.broadcasted_iota(jnp.int32, kv.shape, 1)
    swapped = pltpu.roll(kv, HEAD_DIM, axis=1)
    even, odd = (kv, swapped) if g == 0 else (swapped, kv)
    return jnp.where(lane < HEAD_DIM, even, 0.0), jnp.where(lane >= HEAD_DIM, odd, 0.0)


def _pair_layout_t(planes, n_kv, n_q):
    h, _, n_k = planes.shape
    pairs = h // n_kv // 2
    p = planes[:, :n_q].reshape(n_kv, pairs, 2, n_q, n_k).transpose(0, 2, 4, 1, 3)
    return p.reshape(n_kv, 2 * n_k, pairs * n_q)


def _sink_cols(sinks, n_kv, n_q):
    h = sinks.shape[0]
    pairs = h // n_kv // 2
    s = sinks.reshape(n_kv, pairs, 2).transpose(0, 2, 1)[..., None]
    return jnp.broadcast_to(s, (n_kv, 2, pairs, n_q)).reshape(n_kv, 2, pairs * n_q)


def _attend_t(chains):
    scores = []
    for q_t, k_even, k_odd, _, _, bias, _ in chains:
        ke = jnp.concatenate([k_even, k_odd], axis=0).astype(BF16)
        scores.append(jnp.dot(ke, q_t, preferred_element_type=F32) + bias)
    weights = []
    for s, chain in zip(scores, chains):
        sink = chain[6]
        n_k = s.shape[0] // 2
        probs, inv = [], []
        for par in range(2):
            sp = s[par * n_k:(par + 1) * n_k]
            sk = sink[par:par + 1]
            mx = jnp.maximum(jnp.max(sp, axis=0, keepdims=True), sk)
            p = jnp.exp(sp - mx)
            den = jnp.sum(p, axis=0, keepdims=True) + jnp.exp(sk - mx)
            probs.append(p.astype(BF16))
            inv.append(1.0 / den)
        weights.append((jnp.concatenate(probs, axis=0), inv))
    outs = []
    for (p_t, inv), chain in zip(weights, chains):
        ve = jnp.concatenate([chain[3], chain[4]], axis=0).astype(BF16)
        o = lax.dot_general(ve, p_t, (((0,), (0,)), ((), ())), preferred_element_type=F32)
        row = lax.broadcasted_iota(jnp.int32, o.shape, 0)
        outs.append(o * jnp.where(row < HEAD_DIM, inv[0], inv[1]))
    return outs


def _prompt_attn_kernel(x_ref, kvc_ref, kvp_ref, bias_ref, sink_ref, g2_ref, g3_ref, wq_ref, bq_ref,
                        wo_ref, bo_ref, o_ref, *, n_kv, n_sub, steps_per_seq, scale):
    blk = WINDOW
    x = x_ref[...]
    u = _rms(x, g2_ref[...]).astype(BF16)
    q_t = lax.dot_general(wq_ref[...], u, (((1,), (1,)), ((), ())), preferred_element_type=F32)
    q_t = ((q_t + jnp.concatenate([bq_ref[...]] * n_sub, axis=1)) * scale).astype(BF16)
    kv = jnp.concatenate([kvp_ref[...], kvc_ref[...]], axis=0)
    pairs = q_t.shape[0] // LANE // n_kv
    first = jnp.where(pl.program_id(0) % steps_per_seq == 0, 1, 0)
    k_rows = [_kv_head_rows(kv[:, :n_kv * HEAD_DIM], g) for g in range(n_kv)]
    v_rows = [_kv_head_rows(kv[:, n_kv * HEAD_DIM:], g) for g in range(n_kv)]
    chains = []
    for s in range(n_sub):
        keys = slice(s * blk, (s + 2) * blk)
        for g in range(n_kv):
            qp = jnp.concatenate([q_t[(g * pairs + i) * LANE:(g * pairs + i + 1) * LANE, s * blk:(s + 1) * blk]
                                  for i in range(pairs)], axis=1)
            bias = bias_ref[first, g] if s == 0 else bias_ref[0, g]
            chains.append((qp, k_rows[g][0][keys], k_rows[g][1][keys], v_rows[g][0][keys],
                           v_rows[g][1][keys], bias, sink_ref[g]))
    outs = _attend_t(chains)
    o_t = jnp.concatenate(
        [jnp.concatenate([outs[s * n_kv + g][:, i * blk:(i + 1) * blk] for s in range(n_sub)], axis=1)
         for g in range(n_kv) for i in range(pairs)], axis=0).astype(BF16)
    a = lax.dot_general(o_t, wo_ref[...], (((0,), (0,)), ((), ())), preferred_element_type=F32) + bo_ref[...]
    o_ref[...] = x + _rms(a, g3_ref[...])


def _prompt_attention(x, kv, bias, sink, g2, g3, wq_t, bq_col, wo, bo, layer, *, blocks_per_seq, n_kv):
    t, d = x.shape
    blk = WINDOW
    n_sub = ATTN_BLOCKS_PER_STEP
    assert n_kv * HEAD_DIM == LANE and blocks_per_seq % n_sub == 0
    row = pl.BlockSpec((n_sub * blk, d), lambda i: (i, 0))
    kvw = kv.shape[1]
    return pl.pallas_call(
        functools.partial(_prompt_attn_kernel, n_kv=n_kv, n_sub=n_sub,
                          steps_per_seq=blocks_per_seq // n_sub, scale=1.0 / math.sqrt(HEAD_DIM)),
        grid=(t // (n_sub * blk),),
        in_specs=[row,
                  pl.BlockSpec((n_sub * blk, kvw), lambda i: (i, 0)),
                  pl.BlockSpec((blk, kvw), lambda i: (jnp.maximum(i * n_sub - 1, 0), 0)),
                  _const_spec(bias.shape),
                  _const_spec(sink.shape), _const_spec((1, d)), _const_spec((1, d)),
                  _layer_spec(wq_t.shape, layer), _const_spec(bq_col.shape),
                  _layer_spec(wo.shape, layer), _const_spec((1, d))],
        out_specs=row,
        out_shape=jax.ShapeDtypeStruct((t, d), F32),
        compiler_params=_params(),
        name="prompt_attention",
    )(x, kv, kv, bias, sink, g2, g3, wq_t, bq_col, wo, bo)


def _sample_attn_kernel(q_ref, k_ref, v_ref, bias_ref, sink_ref, o_ref, *, n_kv, scale):
    chains = []
    for b in range(q_ref.shape[0]):
        k = k_ref[b]
        v = v_ref[b]
        for g in range(n_kv):
            chains.append(((q_ref[b, g] * scale).astype(BF16), *_kv_head_rows(k, g), *_kv_head_rows(v, g),
                           bias_ref[g], sink_ref[g]))
    outs = _attend_t(chains)
    for b in range(q_ref.shape[0]):
        for g in range(n_kv):
            o_ref[b, g] = outs[b * n_kv + g]


def _sample_attention(q_t, k_full, v_full, bias, sink, *, n_kv):
    nb, _, _, n_l = q_t.shape
    n_k = k_full.shape[1]
    bb = SUBLANE
    return pl.pallas_call(
        functools.partial(_sample_attn_kernel, n_kv=n_kv, scale=1.0 / math.sqrt(HEAD_DIM)),
        grid=(nb // bb,),
        in_specs=[pl.BlockSpec((bb, n_kv, LANE, n_l), lambda i: (i, 0, 0, 0)),
                  pl.BlockSpec((bb, n_k, LANE), lambda i: (i, 0, 0)),
                  pl.BlockSpec((bb, n_k, LANE), lambda i: (i, 0, 0)),
                  _const_spec(bias.shape), _const_spec(sink.shape)],
        out_specs=pl.BlockSpec((bb, n_kv, LANE, n_l), lambda i: (i, 0, 0, 0)),
        out_shape=jax.ShapeDtypeStruct(q_t.shape, F32),
        compiler_params=_params(),
        name="sample_attention",
    )(q_t, k_full, v_full, bias, sink)


def kernel(x_prompt, x_sample, state_ssm_re, state_ssm_im, cache_win_k, cache_win_v, norm_g, ffn1_w_gu, ffn1_w_down, ffn2_w_gu, ffn2_w_down, ssm_lambda_re, ssm_lambda_im, ssm_log_dt, ssm_b_re, ssm_b_im, ssm_c_re, ssm_c_im, ssm_d, ssm_w_glu, ssm_b_glu, kv_norm_g, w_kv, b_kv, attn_w_q, attn_b_q, attn_sinks, attn_w_o, attn_b_o, rel_bias):
    bsz, seq, dm = x_prompt.shape
    dec_b, dec_s, _ = x_sample.shape
    depth = norm_g.shape[0]
    n_a = ssm_lambda_re.shape[0]
    n_groups = ssm_lambda_re.shape[1]
    n_heads = attn_sinks.shape[1]
    n_kv = cache_win_k.shape[2]
    n_past = cache_win_k.shape[1]
    pairs = n_heads // n_kv // 2
    n_k_pad = 2 * WINDOW
    assert seq % ROW_TILE == 0 and seq % WINDOW == 0 and dec_s % 2 == 0
    assert dec_b * dec_s == ROW_TILE and n_past + dec_s <= n_k_pad and dec_b % SUBLANE == 0
    assert n_kv == 2 and dm == n_heads * HEAD_DIM and n_groups == SUBLANE * GROUPS_PER_BLOCK

    bf = lambda w: w.astype(BF16)
    row = lambda v: v.reshape(1, -1)
    s5_w = _discretize(ssm_lambda_re, ssm_lambda_im, ssm_log_dt, ssm_b_re, ssm_b_im, ssm_c_re, ssm_c_im)
    f1gu, f1d, f2gu, f2d = bf(ffn1_w_gu), bf(ffn1_w_down), bf(ffn2_w_gu), bf(ffn2_w_down)
    wglu, wkv, wq, wo = bf(ssm_w_glu), bf(w_kv)[None], bf(attn_w_q), bf(attn_w_o)

    tab_p = _bucket_table(WINDOW, 2 * WINDOW, WINDOW, WINDOW, 2 * WINDOW)
    tab_p0 = _bucket_table(WINDOW, 2 * WINDOW, WINDOW, WINDOW, 2 * WINDOW, first_key=WINDOW)
    tab_s = _bucket_table(dec_s, n_past + dec_s, n_past, SUBLANE, n_k_pad)
    wq_t = wq.transpose(0, 2, 1)
    bias_p = jnp.stack([_pair_layout_t(_bias_planes(rel_bias, tab_p), n_kv, WINDOW),
                        _pair_layout_t(_bias_planes(rel_bias, tab_p0), n_kv, WINDOW)])
    bias_s = _pair_layout_t(_bias_planes(rel_bias, tab_s), n_kv, dec_s)

    xp = x_prompt.reshape(bsz * seq, dm)
    xs = x_sample.reshape(dec_b * dec_s, dm)
    zero_state = jnp.zeros((bsz, SUBLANE, 2 * HALF), F32)

    ends_p, ends_s = [], []
    kv_p = k_full = v_full = None
    half = n_kv * HEAD_DIM
    for l in range(depth):
        g = [row(norm_g[l, i]) for i in range(norm_g.shape[1])]
        if l == n_a:
            kv_p = _norm_proj(xp, row(kv_norm_g), wkv, row(b_kv), 0)
            kv_s = _norm_proj(xs, row(kv_norm_g), wkv, row(b_kv), 0).reshape(dec_b, dec_s, 2 * half)
            k_full = jnp.concatenate([cache_win_k.reshape(dec_b, n_past, half), kv_s[:, :, :half]], axis=1)
            v_full = jnp.concatenate([cache_win_v.reshape(dec_b, n_past, half), kv_s[:, :, half:]], axis=1)
        pair_io = dict(pair_in=0 < l < n_a, pair_out=l < n_a)
        xp = _ffn(xp, g[0], g[1], f1gu, f1d, l, **pair_io)
        xs = _ffn(xs, g[0], g[1], f1gu, f1d, l, **pair_io)
        if l < n_a:
            common = (g[2], g[3], *s5_w, row(ssm_d[l]), wglu, row(ssm_b_glu[l]))
            xp3, fin_p = _s5_layer(xp.reshape(bsz, seq // 2, 2 * dm), *common, zero_state, l,
                                   n_seq=1, n_t=ROW_TILE // 2)
            xs3, fin_s = _s5_layer(xs.reshape(1, dec_b * dec_s // 2, 2 * dm), *common,
                                   _state_to_rows(state_ssm_re[l], state_ssm_im[l]), l,
                                   n_seq=dec_b, n_t=dec_s // 2)
            xp, xs = xp3.reshape(bsz * seq // 2, 2 * dm), xs3.reshape(dec_b * dec_s // 2, 2 * dm)
            ends_p.append(_rows_to_state(fin_p, n_groups))
            ends_s.append(_rows_to_state(fin_s, n_groups))
        else:
            bl = l - n_a
            bq_l, bo_l = row(attn_b_q[bl]), row(attn_b_o[bl])
            bq_col = jnp.broadcast_to(attn_b_q[bl][:, None], (dm, WINDOW))
            xp = _prompt_attention(xp, kv_p, bias_p, _sink_cols(attn_sinks[bl], n_kv, WINDOW), g[2], g[3],
                                   wq_t, bq_col, wo, bo_l, bl, blocks_per_seq=seq // WINDOW, n_kv=n_kv)
            q = _norm_proj(xs, g[2], wq, bq_l, bl)
            q = q.reshape(dec_b, dec_s, n_kv, pairs, LANE).transpose(0, 2, 4, 3, 1)
            q = q.reshape(dec_b, n_kv, LANE, pairs * dec_s)
            pad = n_k_pad - (n_past + dec_s)
            o = _sample_attention(q, jnp.pad(k_full, ((0, 0), (0, pad), (0, 0))),
                                  jnp.pad(v_full, ((0, 0), (0, pad), (0, 0))),
                                  bias_s, _sink_cols(attn_sinks[bl], n_kv, dec_s), n_kv=n_kv)
            o = o.reshape(dec_b, n_kv, LANE, pairs, dec_s).transpose(0, 4, 1, 3, 2).reshape(dec_b * dec_s, dm)
            xs = _out_proj_residual(xs, o, g[3], wo, bo_l, bl)
        pair_io = dict(pair_in=l < n_a, pair_out=l < n_a - 1)
        xp = _ffn(xp, g[4], g[5], f2gu, f2d, l, **pair_io)
        xs = _ffn(xs, g[4], g[5], f2gu, f2d, l, **pair_io)

    kv_p3 = kv_p.reshape(bsz, seq, 2 * half)
    new_k_p = kv_p3[:, -WINDOW:, :half].reshape(bsz, WINDOW, n_kv, HEAD_DIM)
    new_v_p = kv_p3[:, -WINDOW:, half:].reshape(bsz, WINDOW, n_kv, HEAD_DIM)
    new_k_s = k_full[:, -WINDOW:].reshape(dec_b, WINDOW, n_kv, HEAD_DIM)
    new_v_s = v_full[:, -WINDOW:].reshape(dec_b, WINDOW, n_kv, HEAD_DIM)
    return (xp.reshape(bsz, seq, dm), xs.reshape(dec_b, dec_s, dm),
            jnp.stack([e[0] for e in ends_p]), jnp.stack([e[1] for e in ends_p]),
            new_k_p, new_v_p,
            jnp.stack([e[0] for e in ends_s]), jnp.stack([e[1] for e in ends_s]),
            new_k_s, new_v_s)
```

```python
import functools
import math

import numpy as np
import jax
import jax.numpy as jnp
from jax import lax
from jax.experimental import pallas as pl
from jax.experimental.pallas import tpu as pltpu

F32 = jnp.float32
BF16 = jnp.bfloat16

LANE = 128
SUBLANE = 8
VMEM_LIMIT_BYTES = 56 * 1024 * 1024

RMS_EPS = 1e-6
GROUP_SIZE = 16
STATE_DIM = 64
HEAD_DIM = 64
WINDOW = 128
NUM_BUCKETS = 32
MAX_DISTANCE = WINDOW
GROUPS_PER_BLOCK = LANE // GROUP_SIZE
HALF = GROUPS_PER_BLOCK * STATE_DIM
N_SLABS = 2 * HALF // LANE
N_PAIRS = N_SLABS // 2
ROW_TILE = 512
FFN_CHAINS = 4
ATTN_BLOCKS_PER_STEP = 2
NEG_INF = float("-inf")


def _params(n_axes=1):
    return pltpu.CompilerParams(dimension_semantics=("arbitrary",) * n_axes,
                                vmem_limit_bytes=VMEM_LIMIT_BYTES)


def _const_spec(shape):
    nd = len(shape)
    return pl.BlockSpec(shape, lambda *_: (0,) * nd, pipeline_mode=pl.Buffered(1))


def _layer_spec(shape, layer):
    nd = len(shape)
    return pl.BlockSpec((None,) + tuple(shape[1:]), lambda *_: (layer,) + (0,) * (nd - 1),
                        pipeline_mode=pl.Buffered(1))


def _rms(x, g):
    ms = jnp.mean(x * x, axis=-1, keepdims=True)
    return x * lax.rsqrt(ms + RMS_EPS) * g


def _aligned(row):
    return row if isinstance(row, int) else pl.multiple_of(row, SUBLANE)


def _sigmoid(x):
    return 1.0 / (1.0 + jnp.exp(-x))


def _ffn_kernel(x_ref, ga_ref, gb_ref, wgu_ref, wd_ref, o_ref, *slab_ref, d_ff, pair_in, pair_out):
    d = ga_ref.shape[1]
    if pair_in:
        x = jnp.concatenate([x_ref[:, :d], x_ref[:, d:]], axis=0)
    else:
        x = x_ref[...]
    half = x.shape[0] // 2
    rc = x.shape[0] // FFN_CHAINS
    xc = [x[c * rc:(c + 1) * rc] for c in range(FFN_CHAINS)]
    xn = [_rms(v, ga_ref[...]).astype(BF16) for v in xc]
    gu = [jnp.dot(v, wgu_ref[...], preferred_element_type=F32) for v in xn]
    h = [(v[:, :d_ff] * _sigmoid(v[:, :d_ff]) * v[:, d_ff:]).astype(BF16) for v in gu]
    y = [jnp.dot(v, wd_ref[...], preferred_element_type=F32) for v in h]
    out = jnp.concatenate([v + 0.5 * _rms(w, gb_ref[...]) for v, w in zip(xc, y)], axis=0)
    if pair_in == pair_out:
        o_ref[...] = jnp.concatenate([out[:half], out[half:]], axis=1) if pair_in else out
        return
    (slab,) = slab_ref
    even = pl.ds(0, half, stride=2)
    odd = pl.ds(1, half, stride=2)
    for j in range(d // LANE):
        lanes = slice(j * LANE, (j + 1) * LANE)
        if pair_out:
            slab[j] = out[:, lanes]
            o_ref[:, lanes] = slab[j, even, :]
            o_ref[:, d + j * LANE:d + (j + 1) * LANE] = slab[j, odd, :]
        else:
            slab[j, even, :] = out[:half, lanes]
            slab[j, odd, :] = out[half:, lanes]
            o_ref[:, lanes] = slab[j]


def _ffn(x, ga, gb, wgu, wd, layer, pair_in=False, pair_out=False):
    d = ga.shape[1]
    t = x.shape[0] * (2 if pair_in else 1)
    d_ff = wd.shape[1]
    tm = min(ROW_TILE, t)

    def rows(paired):
        return pl.BlockSpec((tm // 2, 2 * d) if paired else (tm, d), lambda i: (i, 0))

    return pl.pallas_call(
        functools.partial(_ffn_kernel, d_ff=d_ff, pair_in=pair_in, pair_out=pair_out),
        grid=(t // tm,),
        in_specs=[rows(pair_in), _const_spec((1, d)), _const_spec((1, d)),
                  _layer_spec(wgu.shape, layer), _layer_spec(wd.shape, layer)],
        out_specs=rows(pair_out),
        out_shape=jax.ShapeDtypeStruct((t // 2, 2 * d) if pair_out else (t, d), F32),
        scratch_shapes=[pltpu.VMEM((d // LANE, tm, LANE), F32)] if pair_in != pair_out else [],
        compiler_params=_params(),
        name="ffn",
    )(x, ga, gb, wgu, wd)


def _proj_kernel(x_ref, g_ref, w_ref, b_ref, o_ref):
    xn = _rms(x_ref[...], g_ref[...]).astype(BF16)
    o_ref[...] = jnp.dot(xn, w_ref[...], preferred_element_type=F32) + b_ref[...]


def _norm_proj(x, g, w, b, layer):
    t, d = x.shape
    n = w.shape[-1]
    tm = min(ROW_TILE, t)
    return pl.pallas_call(
        _proj_kernel,
        grid=(t // tm,),
        in_specs=[pl.BlockSpec((tm, d), lambda i: (i, 0)), _const_spec((1, d)),
                  _layer_spec(w.shape, layer), _const_spec((1, n))],
        out_specs=pl.BlockSpec((tm, n), lambda i: (i, 0)),
        out_shape=jax.ShapeDtypeStruct((t, n), F32),
        compiler_params=_params(),
        name="norm_proj",
    )(x, g, w, b)


def _out_proj_kernel(x_ref, o_ref_in, g_ref, w_ref, b_ref, out_ref):
    a = jnp.dot(o_ref_in[...].astype(BF16), w_ref[...], preferred_element_type=F32) + b_ref[...]
    out_ref[...] = x_ref[...] + _rms(a, g_ref[...])


def _out_proj_residual(x, o, g, w, b, layer):
    t, d = x.shape
    tm = min(ROW_TILE, t)
    row = pl.BlockSpec((tm, d), lambda i: (i, 0))
    return pl.pallas_call(
        _out_proj_kernel,
        grid=(t // tm,),
        in_specs=[row, pl.BlockSpec((tm, o.shape[1]), lambda i: (i, 0)), _const_spec((1, d)),
                  _layer_spec(w.shape, layer), _const_spec((1, d))],
        out_specs=row,
        out_shape=jax.ShapeDtypeStruct((t, d), F32),
        compiler_params=_params(),
        name="out_proj_residual",
    )(x, o, g, w, b)


def _lam_bar(lam_re, lam_im, log_dt):
    dt = jnp.exp(log_dt)
    mag = jnp.exp(lam_re * dt)
    return mag * jnp.cos(lam_im * dt), mag * jnp.sin(lam_im * dt)


def _discretize_kernel(lre_ref, lim_ref, ldt_ref, lre_col_ref, lim_col_ref, ldt_col_ref,
                       bre_ref, bim_ref, cre_ref, cim_ref, win_ref, wout_ref, wskip_ref, lam2_ref):
    a = lre_ref[0, 0]
    b = lim_ref[0, 0]
    lbr, lbi = _lam_bar(a, b, ldt_ref[0, 0])
    den = a * a + b * b
    cr = ((lbr - 1.0) * a + lbi * b) / den
    ci = (lbi * a - (lbr - 1.0) * b) / den
    bre = bre_ref[0, 0]
    bim = bim_ref[0, 0]
    wbr = cr * bre - ci * bim
    wbi = cr * bim + ci * bre
    win_ref[0, 0, :LANE, :HALF] = (lbr * wbr - lbi * wbi).astype(BF16)
    win_ref[0, 0, :LANE, HALF:] = (lbr * wbi + lbi * wbr).astype(BF16)
    win_ref[0, 0, LANE:, :HALF] = wbr.astype(BF16)
    win_ref[0, 0, LANE:, HALF:] = wbi.astype(BF16)
    lam2_ref[0, 0, :, :HALF] = lbr * lbr - lbi * lbi
    lam2_ref[0, 0, :, HALF:] = 2.0 * lbr * lbi
    lcr, lci = _lam_bar(lre_col_ref[0, 0], lim_col_ref[0, 0], ldt_col_ref[0, 0])
    cre = cre_ref[0, 0]
    cim = cim_ref[0, 0]
    zeros = jnp.zeros((HALF, LANE), BF16)
    wout_ref[0, 0, 0 * HALF:1 * HALF, :LANE] = (cre * lcr - cim * lci).astype(BF16)
    wout_ref[0, 0, 1 * HALF:2 * HALF, :LANE] = (-(cre * lci + cim * lcr)).astype(BF16)
    wout_ref[0, 0, 2 * HALF:3 * HALF, :LANE] = zeros
    wout_ref[0, 0, 3 * HALF:4 * HALF, :LANE] = zeros
    wout_ref[0, 0, 0 * HALF:1 * HALF, LANE:] = zeros
    wout_ref[0, 0, 1 * HALF:2 * HALF, LANE:] = zeros
    wout_ref[0, 0, 2 * HALF:3 * HALF, LANE:] = cre.astype(BF16)
    wout_ref[0, 0, 3 * HALF:4 * HALF, LANE:] = (-cim).astype(BF16)
    skip = (jnp.dot(wbr, cre, precision=lax.Precision.HIGHEST, preferred_element_type=F32)
            - jnp.dot(wbi, cim, precision=lax.Precision.HIGHEST, preferred_element_type=F32))
    wskip_ref[0, 0, :, :LANE] = skip.astype(BF16)
    wskip_ref[0, 0, :, LANE:] = jnp.zeros((LANE, LANE), BF16)


def _block_diag_in(b):
    nl, g, p, h = b.shape
    nb = g // GROUPS_PER_BLOCK
    bt = b.transpose(0, 1, 3, 2).reshape(nl, nb, GROUPS_PER_BLOCK, h, p)
    eye = jnp.eye(GROUPS_PER_BLOCK, dtype=b.dtype)
    out = bt[:, :, :, :, None, :] * eye[None, None, :, None, :, None]
    return out.reshape(nl, nb, GROUPS_PER_BLOCK * h, GROUPS_PER_BLOCK * p)


def _block_diag_out(c):
    nl, g, h, p = c.shape
    nb = g // GROUPS_PER_BLOCK
    ct = c.transpose(0, 1, 3, 2).reshape(nl, nb, GROUPS_PER_BLOCK, p, h)
    eye = jnp.eye(GROUPS_PER_BLOCK, dtype=c.dtype)
    out = ct[:, :, :, :, None, :] * eye[None, None, :, None, :, None]
    return out.reshape(nl, nb, GROUPS_PER_BLOCK * p, GROUPS_PER_BLOCK * h)


def _discretize(lam_re, lam_im, log_dt, b_re, b_im, c_re, c_im):
    nl, g, p = lam_re.shape
    nb = g // GROUPS_PER_BLOCK

    def rows(v):
        return v.reshape(nl, nb, 1, HALF)

    def cols(v):
        return v.reshape(nl, nb, HALF, 1)

    def spec(r, c):
        return pl.BlockSpec((1, 1, r, c), lambda l, k: (l, k, 0, 0))

    win, wout, wskip, lam2 = pl.pallas_call(
        _discretize_kernel,
        grid=(nl, nb),
        in_specs=[spec(1, HALF)] * 3 + [spec(HALF, 1)] * 3 + [spec(LANE, HALF)] * 2 + [spec(HALF, LANE)] * 2,
        out_specs=[spec(2 * LANE, 2 * HALF), spec(4 * HALF, 2 * LANE), spec(LANE, 2 * LANE), spec(1, 2 * HALF)],
        out_shape=[jax.ShapeDtypeStruct((nl, nb, 2 * LANE, 2 * HALF), BF16),
                   jax.ShapeDtypeStruct((nl, nb, 4 * HALF, 2 * LANE), BF16),
                   jax.ShapeDtypeStruct((nl, nb, LANE, 2 * LANE), BF16),
                   jax.ShapeDtypeStruct((nl, nb, 1, 2 * HALF), F32)],
        compiler_params=_params(2),
        name="s5_discretize",
    )(rows(lam_re), rows(lam_im), rows(log_dt), cols(lam_re), cols(lam_im), cols(log_dt),
      _block_diag_in(b_re), _block_diag_in(b_im), _block_diag_out(c_re), _block_diag_out(c_im))
    return win, wout, wskip, lam2.reshape(nl, nb, 2 * HALF)


def _slab_pitch(rows):
    assert rows % SUBLANE == 0
    return rows + SUBLANE // 2


def _split_state(st):
    return (tuple(st[:, p * LANE:(p + 1) * LANE] for p in range(N_PAIRS)),
            tuple(st[:, HALF + p * LANE:HALF + (p + 1) * LANE] for p in range(N_PAIRS)))


def _s5_mixer_kernel(x_ref, g2_ref, g3_ref, win_ref, wout_ref, wskip_ref, lam_ref, d_ref, wglu_ref, bglu_ref,
                     init_ref, o_ref, final_ref, state_ref, y_ref, s_ref, prev_ref=None,
                     *, n_blocks, n_seq, n_t, d_model):
    tile = pl.program_id(1)
    rows = n_seq * n_t
    pitch = _slab_pitch(rows)

    @pl.when(tile == 0)
    def _():
        state_ref[...] = init_ref[...]

    lre, lim = _split_state(lam_ref[...])
    xe = x_ref[0, :, :d_model]
    xo = x_ref[0, :, d_model:]
    ue = _rms(xe, g2_ref[...])
    uo = _rms(xo, g2_ref[...])
    ueb = ue.astype(BF16)
    uob = uo.astype(BF16)


    for k in range(n_blocks):
        lhs = jnp.concatenate([ueb[:, k * LANE:(k + 1) * LANE], uob[:, k * LANE:(k + 1) * LANE]], axis=1)
        v = jnp.dot(lhs, win_ref[k], preferred_element_type=F32)
        for j in range(N_SLABS):
            s_ref[j, pl.ds(k * pitch + 1, rows), :] = v[:, j * LANE:(j + 1) * LANE]

    def seq_body(seq, carry):
        xre, xim = _split_state(state_ref[seq])
        if n_seq == 1:
            before = pl.ds(0, SUBLANE, stride=pitch)
            for p in range(N_PAIRS):
                s_ref[p, before, :] = xre[p]
                s_ref[N_PAIRS + p, before, :] = xim[p]

        def step(t, st):
            xre, xim = st
            m = seq * n_t + t
            pair = pl.ds(m + 1, SUBLANE, stride=pitch)
            nre, nim = [], []
            for p in range(N_PAIRS):
                if n_seq > 1:
                    prev_ref[p, pl.ds(m, SUBLANE, stride=pitch), :] = xre[p]
                    prev_ref[N_PAIRS + p, pl.ds(m, SUBLANE, stride=pitch), :] = xim[p]
                re = lre[p] * xre[p] - lim[p] * xim[p] + s_ref[p, pair, :]
                im = lre[p] * xim[p] + lim[p] * xre[p] + s_ref[N_PAIRS + p, pair, :]
                s_ref[p, pair, :] = re
                s_ref[N_PAIRS + p, pair, :] = im
                nre.append(re)
                nim.append(im)
            return tuple(nre), tuple(nim)

        xre, xim = lax.fori_loop(0, n_t, step, (xre, xim), unroll=min(n_t, 8))
        state_ref[seq] = jnp.concatenate(list(xre) + list(xim), axis=1)
        return carry

    if n_seq == 1:
        seq_body(0, 0)
    else:
        lax.fori_loop(0, n_seq, seq_body, 0)

    before_ref = s_ref if n_seq == 1 else prev_ref
    for k in range(n_blocks):
        xb = jnp.concatenate([before_ref[j, pl.ds(k * pitch, rows), :] for j in range(N_SLABS)], axis=1)
        xa = jnp.concatenate([s_ref[j, pl.ds(k * pitch + 1, rows), :] for j in range(N_SLABS)], axis=1)
        yk = jnp.dot(jnp.concatenate([xb, xa], axis=1).astype(BF16), wout_ref[k], preferred_element_type=F32)
        yk = yk + jnp.dot(ueb[:, k * LANE:(k + 1) * LANE], wskip_ref[k], preferred_element_type=F32)
        y_ref[:, k * LANE:(k + 1) * LANE] = yk[:, :LANE]
        y_ref[:, d_model + k * LANE:d_model + (k + 1) * LANE] = yk[:, LANE:]
    y = jnp.concatenate([y_ref[:, :d_model] + d_ref[...] * ue, y_ref[:, d_model:] + d_ref[...] * uo], axis=0)
    h = jnp.dot(jax.nn.gelu(y).astype(BF16), wglu_ref[...], preferred_element_type=F32) + bglu_ref[...]
    r = _rms(h[:, :d_model] * _sigmoid(h[:, d_model:]), g3_ref[...])
    o_ref[0] = jnp.concatenate([xe + r[:rows], xo + r[rows:]], axis=1)

    @pl.when(tile == pl.num_programs(1) - 1)
    def _():
        final_ref[...] = state_ref[...]


def _s5_layer(x, g2, g3, win, wout, wskip, lam2, d, wglu, bglu, init, layer, *, n_seq, n_t):
    nb_rows, s2, dm2 = x.shape
    dm = dm2 // 2
    n_blocks = win.shape[1]
    assert n_blocks == SUBLANE
    rows = n_seq * n_t
    n_tiles = s2 // rows
    assert n_seq == 1 or n_tiles == 1
    row = pl.BlockSpec((1, rows, dm2), lambda b, i: (b, i, 0))
    vec = _const_spec((1, dm))
    state = pl.BlockSpec((n_seq, SUBLANE, 2 * HALF), lambda b, i: (b, 0, 0))
    slab = pltpu.VMEM((N_SLABS, n_blocks * _slab_pitch(rows), LANE), F32)
    return pl.pallas_call(
        functools.partial(_s5_mixer_kernel, n_blocks=n_blocks, n_seq=n_seq, n_t=n_t, d_model=dm),
        grid=(nb_rows, n_tiles),
        in_specs=[row, vec, vec, _layer_spec(win.shape, layer), _layer_spec(wout.shape, layer),
                  _layer_spec(wskip.shape, layer), _layer_spec(lam2.shape, layer), vec,
                  _layer_spec(wglu.shape, layer), _const_spec((1, 2 * dm)), state],
        out_specs=[row, state],
        out_shape=[jax.ShapeDtypeStruct(x.shape, F32),
                   jax.ShapeDtypeStruct(init.shape, F32)],
        scratch_shapes=[pltpu.VMEM((n_seq, SUBLANE, 2 * HALF), F32), pltpu.VMEM((rows, dm2), F32), slab]
        + ([slab] if n_seq > 1 else []),
        compiler_params=_params(2),
        name="s5_mixer",
    )(x, g2, g3, win, wout, wskip, lam2, d, wglu, bglu, init)


def _state_to_rows(re, im):
    b = re.shape[0]
    return jnp.concatenate([re.reshape(b, -1, HALF), im.reshape(b, -1, HALF)], axis=-1)


def _rows_to_state(s, n_groups):
    b = s.shape[0]
    return (s[:, :, :HALF].reshape(b, n_groups, STATE_DIM), s[:, :, HALF:].reshape(b, n_groups, STATE_DIM))


def _t5_bucket_np(dist):
    n = np.maximum(dist, 0)
    max_exact = NUM_BUCKETS // 2
    nf = np.maximum(n, 1).astype(np.float32)
    large = max_exact + (np.log(nf / np.float32(max_exact)) / np.float32(math.log(MAX_DISTANCE / max_exact))
                         * np.float32(NUM_BUCKETS - max_exact)).astype(np.int32)
    large = np.minimum(large, NUM_BUCKETS - 1)
    return np.where(n < max_exact, n, large).astype(np.int32)


def _bucket_table(n_q, n_k, q_offset, n_q_pad, n_k_pad, first_key=0):
    dist = (np.arange(n_q)[:, None] + q_offset) - np.arange(n_k)[None, :]
    valid = (dist >= 0) & (dist < WINDOW) & (np.arange(n_k)[None, :] >= first_key)
    table = np.full((n_q_pad, n_k_pad), -1, np.int32)
    table[:n_q, :n_k] = np.where(valid, _t5_bucket_np(dist), -1)
    return table


def _bias_kernel(rel_ref, idx_ref, o_ref):
    h = pl.program_id(0)
    idx = idx_ref[...]
    acc = jnp.full(idx.shape, NEG_INF, F32)
    for b in range(NUM_BUCKETS):
        acc = jnp.where(idx == b, rel_ref[b, h], acc)
    o_ref[0] = acc


def _bias_planes(rel_bias, table):
    n_heads = rel_bias.shape[1]
    r, c = table.shape
    return pl.pallas_call(
        _bias_kernel,
        grid=(n_heads,),
        in_specs=[pl.BlockSpec(memory_space=pltpu.SMEM), pl.BlockSpec((r, c), lambda h: (0, 0))],
        out_specs=pl.BlockSpec((1, r, c), lambda h: (h, 0, 0)),
        out_shape=jax.ShapeDtypeStruct((n_heads, r, c), F32),
        compiler_params=_params(),
        name="attn_bias",
    )(rel_bias, jnp.asarray(table))


def _kv_head_rows(kv, g):
    lane = lax.broadcasted_iota(jnp.int32, kv.shape, 1)
    swapped = pltpu.roll(kv, HEAD_DIM, axis=1)
    even, odd = (kv, swapped) if g == 0 else (swapped, kv)
    return jnp.where(lane < HEAD_DIM, even, 0.0), jnp.where(lane >= HEAD_DIM, odd, 0.0)


def _pair_layout_t(planes, n_kv, n_q):
    h, _, n_k = planes.shape
    pairs = h // n_kv // 2
    p = planes[:, :n_q].reshape(n_kv, pairs, 2, n_q, n_k).transpose(0, 2, 4, 1, 3)
    return p.reshape(n_kv, 2 * n_k, pairs * n_q)


def _sink_cols(sinks, n_kv, n_q):
    h = sinks.shape[0]
    pairs = h // n_kv // 2
    s = sinks.reshape(n_kv, pairs, 2).transpose(0, 2, 1)[..., None]
    return jnp.broadcast_to(s, (n_kv, 2, pairs, n_q)).reshape(n_kv, 2, pairs * n_q)


def _attend_t(chains):
    scores = []
    for q_t, k_even, k_odd, _, _, bias, _ in chains:
        ke = jnp.concatenate([k_even, k_odd], axis=0).astype(BF16)
        scores.append(jnp.dot(ke, q_t, preferred_element_type=F32) + bias)
    weights = []
    for s, chain in zip(scores, chains):
        sink = chain[6]
        n_k = s.shape[0] // 2
        probs, inv = [], []
        for par in range(2):
            sp = s[par * n_k:(par + 1) * n_k]
            sk = sink[par:par + 1]
            mx = jnp.maximum(jnp.max(sp, axis=0, keepdims=True), sk)
            p = jnp.exp(sp - mx)
            den = jnp.sum(p, axis=0, keepdims=True) + jnp.exp(sk - mx)
            probs.append(p.astype(BF16))
            inv.append(1.0 / den)
        weights.append((jnp.concatenate(probs, axis=0), inv))
    outs = []
    for (p_t, inv), chain in zip(weights, chains):
        ve = jnp.concatenate([chain[3], chain[4]], axis=0).astype(BF16)
        o = lax.dot_general(ve, p_t, (((0,), (0,)), ((), ())), preferred_element_type=F32)
        row = lax.broadcasted_iota(jnp.int32, o.shape, 0)
        outs.append(o * jnp.where(row < HEAD_DIM, inv[0], inv[1]))
    return outs


def _prompt_attn_kernel(x_ref, kvc_ref, kvp_ref, bias_ref, sink_ref, g2_ref, g3_ref, wq_ref, bq_ref,
                        wo_ref, bo_ref, o_ref, *, n_kv, n_sub, steps_per_seq, scale):
    blk = WINDOW
    x = x_ref[...]
    u = _rms(x, g2_ref[...]).astype(BF16)
    q_t = lax.dot_general(wq_ref[...], u, (((1,), (1,)), ((), ())), preferred_element_type=F32)
    q_t = ((q_t + jnp.concatenate([bq_ref[...]] * n_sub, axis=1)) * scale).astype(BF16)
    kv = jnp.concatenate([kvp_ref[...], kvc_ref[...]], axis=0)
    pairs = q_t.shape[0] // LANE // n_kv
    first = jnp.where(pl.program_id(0) % steps_per_seq == 0, 1, 0)
    k_rows = [_kv_head_rows(kv[:, :n_kv * HEAD_DIM], g) for g in range(n_kv)]
    v_rows = [_kv_head_rows(kv[:, n_kv * HEAD_DIM:], g) for g in range(n_kv)]
    chains = []
    for s in range(n_sub):
        keys = slice(s * blk, (s + 2) * blk)
        for g in range(n_kv):
            qp = jnp.concatenate([q_t[(g * pairs + i) * LANE:(g * pairs + i + 1) * LANE, s * blk:(s + 1) * blk]
                                  for i in range(pairs)], axis=1)
            bias = bias_ref[first, g] if s == 0 else bias_ref[0, g]
            chains.append((qp, k_rows[g][0][keys], k_rows[g][1][keys], v_rows[g][0][keys],
                           v_rows[g][1][keys], bias, sink_ref[g]))
    outs = _attend_t(chains)
    o_t = jnp.concatenate(
        [jnp.concatenate([outs[s * n_kv + g][:, i * blk:(i + 1) * blk] for s in range(n_sub)], axis=1)
         for g in range(n_kv) for i in range(pairs)], axis=0).astype(BF16)
    a = lax.dot_general(o_t, wo_ref[...], (((0,), (0,)), ((), ())), preferred_element_type=F32) + bo_ref[...]
    o_ref[...] = x + _rms(a, g3_ref[...])


def _prompt_attention(x, kv, bias, sink, g2, g3, wq_t, bq_col, wo, bo, layer, *, blocks_per_seq, n_kv):
    t, d = x.shape
    blk = WINDOW
    n_sub = ATTN_BLOCKS_PER_STEP
    assert n_kv * HEAD_DIM == LANE and blocks_per_seq % n_sub == 0
    row = pl.BlockSpec((n_sub * blk, d), lambda i: (i, 0))
    kvw = kv.shape[1]
    return pl.pallas_call(
        functools.partial(_prompt_attn_kernel, n_kv=n_kv, n_sub=n_sub,
                          steps_per_seq=blocks_per_seq // n_sub, scale=1.0 / math.sqrt(HEAD_DIM)),
        grid=(t // (n_sub * blk),),
        in_specs=[row,
                  pl.BlockSpec((n_sub * blk, kvw), lambda i: (i, 0)),
                  pl.BlockSpec((blk, kvw), lambda i: (jnp.maximum(i * n_sub - 1, 0), 0)),
                  _const_spec(bias.shape),
                  _const_spec(sink.shape), _const_spec((1, d)), _const_spec((1, d)),
                  _layer_spec(wq_t.shape, layer), _const_spec(bq_col.shape),
                  _layer_spec(wo.shape, layer), _const_spec((1, d))],
        out_specs=row,
        out_shape=jax.ShapeDtypeStruct((t, d), F32),
        compiler_params=_params(),
        name="prompt_attention",
    )(x, kv, kv, bias, sink, g2, g3, wq_t, bq_col, wo, bo)


def _sample_attn_kernel(q_ref, k_ref, v_ref, bias_ref, sink_ref, o_ref, *, n_kv, scale):
    chains = []
    for b in range(q_ref.shape[0]):
        k = k_ref[b]
        v = v_ref[b]
        for g in range(n_kv):
            chains.append(((q_ref[b, g] * scale).astype(BF16), *_kv_head_rows(k, g), *_kv_head_rows(v, g),
                           bias_ref[g], sink_ref[g]))
    outs = _attend_t(chains)
    for b in range(q_ref.shape[0]):
        for g in range(n_kv):
            o_ref[b, g] = outs[b * n_kv + g]


def _sample_attention(q_t, k_full, v_full, bias, sink, *, n_kv):
    nb, _, _, n_l = q_t.shape
    n_k = k_full.shape[1]
    bb = SUBLANE
    return pl.pallas_call(
        functools.partial(_sample_attn_kernel, n_kv=n_kv, scale=1.0 / math.sqrt(HEAD_DIM)),
        grid=(nb // bb,),
        in_specs=[pl.BlockSpec((bb, n_kv, LANE, n_l), lambda i: (i, 0, 0, 0)),
                  pl.BlockSpec((bb, n_k, LANE), lambda i: (i, 0, 0)),
                  pl.BlockSpec((bb, n_k, LANE), lambda i: (i, 0, 0)),
                  _const_spec(bias.shape), _const_spec(sink.shape)],
        out_specs=pl.BlockSpec((bb, n_kv, LANE, n_l), lambda i: (i, 0, 0, 0)),
        out_shape=jax.ShapeDtypeStruct(q_t.shape, F32),
        compiler_params=_params(),
        name="sample_attention",
    )(q_t, k_full, v_full, bias, sink)


def kernel(x_prompt, x_sample, state_ssm_re, state_ssm_im, cache_win_k, cache_win_v, norm_g, ffn1_w_gu, ffn1_w_down, ffn2_w_gu, ffn2_w_down, ssm_lambda_re, ssm_lambda_im, ssm_log_dt, ssm_b_re, ssm_b_im, ssm_c_re, ssm_c_im, ssm_d, ssm_w_glu, ssm_b_glu, kv_norm_g, w_kv, b_kv, attn_w_q, attn_b_q, attn_sinks, attn_w_o, attn_b_o, rel_bias):
    bsz, seq, dm = x_prompt.shape
    dec_b, dec_s, _ = x_sample.shape
    depth = norm_g.shape[0]
    n_a = ssm_lambda_re.shape[0]
    n_groups = ssm_lambda_re.shape[1]
    n_heads = attn_sinks.shape[1]
    n_kv = cache_win_k.shape[2]
    n_past = cache_win_k.shape[1]
    pairs = n_heads // n_kv // 2
    n_k_pad = 2 * WINDOW
    assert seq % ROW_TILE == 0 and seq % WINDOW == 0 and dec_s % 2 == 0
    assert dec_b * dec_s == ROW_TILE and n_past + dec_s <= n_k_pad and dec_b % SUBLANE == 0
    assert n_kv == 2 and dm == n_heads * HEAD_DIM and n_groups == SUBLANE * GROUPS_PER_BLOCK

    bf = lambda w: w.astype(BF16)
    row = lambda v: v.reshape(1, -1)
    s5_w = _discretize(ssm_lambda_re, ssm_lambda_im, ssm_log_dt, ssm_b_re, ssm_b_im, ssm_c_re, ssm_c_im)
    f1gu, f1d, f2gu, f2d = bf(ffn1_w_gu), bf(ffn1_w_down), bf(ffn2_w_gu), bf(ffn2_w_down)
    wglu, wkv, wq, wo = bf(ssm_w_glu), bf(w_kv)[None], bf(attn_w_q), bf(attn_w_o)

    tab_p = _bucket_table(WINDOW, 2 * WINDOW, WINDOW, WINDOW, 2 * WINDOW)
    tab_p0 = _bucket_table(WINDOW, 2 * WINDOW, WINDOW, WINDOW, 2 * WINDOW, first_key=WINDOW)
    tab_s = _bucket_table(dec_s, n_past + dec_s, n_past, SUBLANE, n_k_pad)
    wq_t = wq.transpose(0, 2, 1)
    bias_p = jnp.stack([_pair_layout_t(_bias_planes(rel_bias, tab_p), n_kv, WINDOW),
                        _pair_layout_t(_bias_planes(rel_bias, tab_p0), n_kv, WINDOW)])
    bias_s = _pair_layout_t(_bias_planes(rel_bias, tab_s), n_kv, dec_s)

    xp = x_prompt.reshape(bsz * seq, dm)
    xs = x_sample.reshape(dec_b * dec_s, dm)
    zero_state = jnp.zeros((bsz, SUBLANE, 2 * HALF), F32)

    ends_p, ends_s = [], []
    kv_p = k_full = v_full = None
    half = n_kv * HEAD_DIM
    for l in range(depth):
        g = [row(norm_g[l, i]) for i in range(norm_g.shape[1])]
        if l == n_a:
            kv_p = _norm_proj(xp, row(kv_norm_g), wkv, row(b_kv), 0)
            kv_s = _norm_proj(xs, row(kv_norm_g), wkv, row(b_kv), 0).reshape(dec_b, dec_s, 2 * half)
            k_full = jnp.concatenate([cache_win_k.reshape(dec_b, n_past, half), kv_s[:, :, :half]], axis=1)
            v_full = jnp.concatenate([cache_win_v.reshape(dec_b, n_past, half), kv_s[:, :, half:]], axis=1)
        pair_io = dict(pair_in=0 < l < n_a, pair_out=l < n_a)
        xp = _ffn(xp, g[0], g[1], f1gu, f1d, l, **pair_io)
        xs = _ffn(xs, g[0], g[1], f1gu, f1d, l, **pair_io)
        if l < n_a:
            common = (g[2], g[3], *s5_w, row(ssm_d[l]), wglu, row(ssm_b_glu[l]))
            xp3, fin_p = _s5_layer(xp.reshape(bsz, seq // 2, 2 * dm), *common, zero_state, l,
                                   n_seq=1, n_t=ROW_TILE // 2)
            xs3, fin_s = _s5_layer(xs.reshape(1, dec_b * dec_s // 2, 2 * dm), *common,
                                   _state_to_rows(state_ssm_re[l], state_ssm_im[l]), l,
                                   n_seq=dec_b, n_t=dec_s // 2)
            xp, xs = xp3.reshape(bsz * seq // 2, 2 * dm), xs3.reshape(dec_b * dec_s // 2, 2 * dm)
            ends_p.append(_rows_to_state(fin_p, n_groups))
            ends_s.append(_rows_to_state(fin_s, n_groups))
        else:
            bl = l - n_a
            bq_l, bo_l = row(attn_b_q[bl]), row(attn_b_o[bl])
            bq_col = jnp.broadcast_to(attn_b_q[bl][:, None], (dm, WINDOW))
            xp = _prompt_attention(xp, kv_p, bias_p, _sink_cols(attn_sinks[bl], n_kv, WINDOW), g[2], g[3],
                                   wq_t, bq_col, wo, bo_l, bl, blocks_per_seq=seq // WINDOW, n_kv=n_kv)
            q = _norm_proj(xs, g[2], wq, bq_l, bl)
            q = q.reshape(dec_b, dec_s, n_kv, pairs, LANE).transpose(0, 2, 4, 3, 1)
            q = q.reshape(dec_b, n_kv, LANE, pairs * dec_s)
            pad = n_k_pad - (n_past + dec_s)
            o = _sample_attention(q, jnp.pad(k_full, ((0, 0), (0, pad), (0, 0))),
                                  jnp.pad(v_full, ((0, 0), (0, pad), (0, 0))),
                                  bias_s, _sink_cols(attn_sinks[bl], n_kv, dec_s), n_kv=n_kv)
            o = o.reshape(dec_b, n_kv, LANE, pairs, dec_s).transpose(0, 4, 1, 3, 2).reshape(dec_b * dec_s, dm)
            xs = _out_proj_residual(xs, o, g[3], wo, bo_l, bl)
        pair_io = dict(pair_in=l < n_a, pair_out=l < n_a - 1)
        xp = _ffn(xp, g[4], g[5], f2gu, f2d, l, **pair_io)
        xs = _ffn(xs, g[4], g[5], f2gu, f2d, l, **pair_io)

    kv_p3 = kv_p.reshape(bsz, seq, 2 * half)
    new_k_p = kv_p3[:, -WINDOW:, :half].reshape(bsz, WINDOW, n_kv, HEAD_DIM)
    new_v_p = kv_p3[:, -WINDOW:, half:].reshape(bsz, WINDOW, n_kv, HEAD_DIM)
    new_k_s = k_full[:, -WINDOW:].reshape(dec_b, WINDOW, n_kv, HEAD_DIM)
    new_v_s = v_full[:, -WINDOW:].reshape(dec_b, WINDOW, n_kv, HEAD_DIM)
    return (xp.reshape(bsz, seq, dm), xs.reshape(dec_b, dec_s, dm),
            jnp.stack([e[0] for e in ends_p]), jnp.stack([e[1] for e in ends_p]),
            new_k_p, new_v_p,
            jnp.stack([e[0] for e in ends_s]), jnp.stack([e[1] for e in ends_s]),
            new_k_s, new_v_s)
```

```python
import functools
import math

import numpy as np
import jax
import jax.numpy as jnp
from jax import lax
from jax.experimental import pallas as pl
from jax.experimental.pallas import tpu as pltpu

F32 = jnp.float32
BF16 = jnp.bfloat16

LANE = 128
SUBLANE = 8
VMEM_LIMIT_BYTES = 56 * 1024 * 1024

RMS_EPS = 1e-6
GROUP_SIZE = 16
STATE_DIM = 64
HEAD_DIM = 64
WINDOW = 128
NUM_BUCKETS = 32
MAX_DISTANCE = WINDOW
GROUPS_PER_BLOCK = LANE // GROUP_SIZE
HALF = GROUPS_PER_BLOCK * STATE_DIM
N_SLABS = 2 * HALF // LANE
N_PAIRS = N_SLABS // 2
ROW_TILE = 512
FFN_CHAINS = 4
ATTN_BLOCKS_PER_STEP = 2
NEG_INF = float("-inf")


def _params(n_axes=1):
    return pltpu.CompilerParams(dimension_semantics=("arbitrary",) * n_axes,
                                vmem_limit_bytes=VMEM_LIMIT_BYTES)


def _const_spec(shape):
    nd = len(shape)
    return pl.BlockSpec(shape, lambda *_: (0,) * nd, pipeline_mode=pl.Buffered(1))


def _layer_spec(shape, layer):
    nd = len(shape)
    return pl.BlockSpec((None,) + tuple(shape[1:]), lambda *_: (layer,) + (0,) * (nd - 1),
                        pipeline_mode=pl.Buffered(1))


def _rms(x, g):
    ms = jnp.mean(x * x, axis=-1, keepdims=True)
    return x * lax.rsqrt(ms + RMS_EPS) * g


def _aligned(row):
    return row if isinstance(row, int) else pl.multiple_of(row, SUBLANE)


def _sigmoid(x):
    return 1.0 / (1.0 + jnp.exp(-x))


def _cast_chunk(rows, n_steps):
    tile = 2 * SUBLANE
    for chunk in range(tile, rows + 1, tile):
        if rows % chunk == 0 and rows // chunk <= n_steps:
            return chunk
    raise ValueError((rows, n_steps))


def _ffn_kernel(*refs, d_ff, pair_in, pair_out, cast_steps):
    if cast_steps:
        x_ref, ga_ref, gb_ref, wgu_ref, wd_ref, nwgu_ref, nwd_ref, o_ref, cwgu_ref, cwd_ref, *slab_ref = refs
        for src, dst, steps in ((nwgu_ref, cwgu_ref, cast_steps[0]), (nwd_ref, cwd_ref, cast_steps[1])):
            @pl.when(pl.program_id(0) < steps)
            def _(src=src, dst=dst):
                dst[...] = src[...].astype(BF16)
    else:
        x_ref, ga_ref, gb_ref, wgu_ref, wd_ref, o_ref, *slab_ref = refs
    d = ga_ref.shape[1]
    if pair_in:
        x = jnp.concatenate([x_ref[:, :d], x_ref[:, d:]], axis=0)
    else:
        x = x_ref[...]
    half = x.shape[0] // 2
    rc = x.shape[0] // FFN_CHAINS
    xc = [x[c * rc:(c + 1) * rc] for c in range(FFN_CHAINS)]
    xn = [_rms(v, ga_ref[...]).astype(BF16) for v in xc]
    gu = [jnp.dot(v, wgu_ref[...], preferred_element_type=F32) for v in xn]
    h = [(v[:, :d_ff] * _sigmoid(v[:, :d_ff]) * v[:, d_ff:]).astype(BF16) for v in gu]
    y = [jnp.dot(v, wd_ref[...], preferred_element_type=F32) for v in h]
    out = jnp.concatenate([v + 0.5 * _rms(w, gb_ref[...]) for v, w in zip(xc, y)], axis=0)
    if pair_in == pair_out:
        o_ref[...] = jnp.concatenate([out[:half], out[half:]], axis=1) if pair_in else out
        return
    (slab,) = slab_ref
    even = pl.ds(0, half, stride=2)
    odd = pl.ds(1, half, stride=2)
    for j in range(d // LANE):
        lanes = slice(j * LANE, (j + 1) * LANE)
        if pair_out:
            slab[j] = out[:, lanes]
            o_ref[:, lanes] = slab[j, even, :]
            o_ref[:, d + j * LANE:d + (j + 1) * LANE] = slab[j, odd, :]
        else:
            slab[j, even, :] = out[:half, lanes]
            slab[j, odd, :] = out[half:, lanes]
            o_ref[:, lanes] = slab[j]


def _ffn(x, ga, gb, wgu, wd, pair_in=False, pair_out=False, cast_next=None):
    d = ga.shape[1]
    t = x.shape[0] * (2 if pair_in else 1)
    d_ff = wd.shape[1]
    tm = min(ROW_TILE, t)
    n_steps = t // tm

    def rows(paired):
        return pl.BlockSpec((tm // 2, 2 * d) if paired else (tm, d), lambda i: (i, 0))

    in_specs = [rows(pair_in), _const_spec((1, d)), _const_spec((1, d)),
                _layer_spec(wgu.shape, 0), _layer_spec(wd.shape, 0)]
    out_specs = [rows(pair_out)]
    out_shape = [jax.ShapeDtypeStruct((t // 2, 2 * d) if pair_out else (t, d), F32)]
    operands = [x, ga, gb, wgu, wd]
    cast_steps = None
    if cast_next is not None:
        *next_w, layer = cast_next
        cast_steps = []
        for w in next_w:
            chunk = _cast_chunk(w.shape[1], n_steps)
            last = w.shape[1] // chunk - 1
            cast_steps.append(last + 1)
            in_specs.append(pl.BlockSpec((None, chunk, w.shape[2]),
                                         lambda i, last=last: (layer, jnp.minimum(i, last), 0)))
            out_specs.append(pl.BlockSpec((None, chunk, w.shape[2]),
                                          lambda i, last=last: (0, jnp.minimum(i, last), 0)))
            out_shape.append(jax.ShapeDtypeStruct((1,) + w.shape[1:], BF16))
            operands.append(w)
    res = pl.pallas_call(
        functools.partial(_ffn_kernel, d_ff=d_ff, pair_in=pair_in, pair_out=pair_out, cast_steps=cast_steps),
        grid=(n_steps,),
        in_specs=in_specs,
        out_specs=out_specs,
        out_shape=out_shape,
        scratch_shapes=[pltpu.VMEM((d // LANE, tm, LANE), F32)] if pair_in != pair_out else [],
        compiler_params=_params(),
        name="ffn",
    )(*operands)
    return res if cast_next is not None else res[0]


def _proj_kernel(x_ref, g_ref, w_ref, b_ref, o_ref):
    xn = _rms(x_ref[...], g_ref[...]).astype(BF16)
    o_ref[...] = jnp.dot(xn, w_ref[...], preferred_element_type=F32) + b_ref[...]


def _norm_proj(x, g, w, b, layer):
    t, d = x.shape
    n = w.shape[-1]
    tm = min(ROW_TILE, t)
    return pl.pallas_call(
        _proj_kernel,
        grid=(t // tm,),
        in_specs=[pl.BlockSpec((tm, d), lambda i: (i, 0)), _const_spec((1, d)),
                  _layer_spec(w.shape, layer), _const_spec((1, n))],
        out_specs=pl.BlockSpec((tm, n), lambda i: (i, 0)),
        out_shape=jax.ShapeDtypeStruct((t, n), F32),
        compiler_params=_params(),
        name="norm_proj",
    )(x, g, w, b)


def _out_proj_kernel(x_ref, o_ref_in, g_ref, w_ref, b_ref, out_ref):
    a = jnp.dot(o_ref_in[...].astype(BF16), w_ref[...], preferred_element_type=F32) + b_ref[...]
    out_ref[...] = x_ref[...] + _rms(a, g_ref[...])


def _out_proj_residual(x, o, g, w, b, layer):
    t, d = x.shape
    tm = min(ROW_TILE, t)
    row = pl.BlockSpec((tm, d), lambda i: (i, 0))
    return pl.pallas_call(
        _out_proj_kernel,
        grid=(t // tm,),
        in_specs=[row, pl.BlockSpec((tm, o.shape[1]), lambda i: (i, 0)), _const_spec((1, d)),
                  _layer_spec(w.shape, layer), _const_spec((1, d))],
        out_specs=row,
        out_shape=jax.ShapeDtypeStruct((t, d), F32),
        compiler_params=_params(),
        name="out_proj_residual",
    )(x, o, g, w, b)


def _lam_bar(lam_re, lam_im, log_dt):
    dt = jnp.exp(log_dt)
    mag = jnp.exp(lam_re * dt)
    return mag * jnp.cos(lam_im * dt), mag * jnp.sin(lam_im * dt)


def _discretize_kernel(lre_ref, lim_ref, ldt_ref, lre_col_ref, lim_col_ref, ldt_col_ref,
                       bre_ref, bim_ref, cre_ref, cim_ref, win_ref, wout_ref, wskip_ref, lam2_ref):
    a = lre_ref[0, 0]
    b = lim_ref[0, 0]
    lbr, lbi = _lam_bar(a, b, ldt_ref[0, 0])
    den = a * a + b * b
    cr = ((lbr - 1.0) * a + lbi * b) / den
    ci = (lbi * a - (lbr - 1.0) * b) / den
    bre = bre_ref[0, 0]
    bim = bim_ref[0, 0]
    wbr = cr * bre - ci * bim
    wbi = cr * bim + ci * bre
    win_ref[0, 0, :LANE, :HALF] = (lbr * wbr - lbi * wbi).astype(BF16)
    win_ref[0, 0, :LANE, HALF:] = (lbr * wbi + lbi * wbr).astype(BF16)
    win_ref[0, 0, LANE:, :HALF] = wbr.astype(BF16)
    win_ref[0, 0, LANE:, HALF:] = wbi.astype(BF16)
    lam2_ref[0, 0, :, :HALF] = lbr * lbr - lbi * lbi
    lam2_ref[0, 0, :, HALF:] = 2.0 * lbr * lbi
    lcr, lci = _lam_bar(lre_col_ref[0, 0], lim_col_ref[0, 0], ldt_col_ref[0, 0])
    cre = cre_ref[0, 0]
    cim = cim_ref[0, 0]
    zeros = jnp.zeros((HALF, LANE), BF16)
    wout_ref[0, 0, 0 * HALF:1 * HALF, :LANE] = (cre * lcr - cim * lci).astype(BF16)
    wout_ref[0, 0, 1 * HALF:2 * HALF, :LANE] = (-(cre * lci + cim * lcr)).astype(BF16)
    wout_ref[0, 0, 2 * HALF:3 * HALF, :LANE] = zeros
    wout_ref[0, 0, 3 * HALF:4 * HALF, :LANE] = zeros
    wout_ref[0, 0, 0 * HALF:1 * HALF, LANE:] = zeros
    wout_ref[0, 0, 1 * HALF:2 * HALF, LANE:] = zeros
    wout_ref[0, 0, 2 * HALF:3 * HALF, LANE:] = cre.astype(BF16)
    wout_ref[0, 0, 3 * HALF:4 * HALF, LANE:] = (-cim).astype(BF16)
    skip = (jnp.dot(wbr, cre, precision=lax.Precision.HIGHEST, preferred_element_type=F32)
            - jnp.dot(wbi, cim, precision=lax.Precision.HIGHEST, preferred_element_type=F32))
    wskip_ref[0, 0, :, :LANE] = skip.astype(BF16)
    wskip_ref[0, 0, :, LANE:] = jnp.zeros((LANE, LANE), BF16)


def _block_diag_in(b):
    nl, g, p, h = b.shape
    nb = g // GROUPS_PER_BLOCK
    bt = b.transpose(0, 1, 3, 2).reshape(nl, nb, GROUPS_PER_BLOCK, h, p)
    eye = jnp.eye(GROUPS_PER_BLOCK, dtype=b.dtype)
    out = bt[:, :, :, :, None, :] * eye[None, None, :, None, :, None]
    return out.reshape(nl, nb, GROUPS_PER_BLOCK * h, GROUPS_PER_BLOCK * p)


def _block_diag_out(c):
    nl, g, h, p = c.shape
    nb = g // GROUPS_PER_BLOCK
    ct = c.transpose(0, 1, 3, 2).reshape(nl, nb, GROUPS_PER_BLOCK, p, h)
    eye = jnp.eye(GROUPS_PER_BLOCK, dtype=c.dtype)
    out = ct[:, :, :, :, None, :] * eye[None, None, :, None, :, None]
    return out.reshape(nl, nb, GROUPS_PER_BLOCK * p, GROUPS_PER_BLOCK * h)


def _discretize(lam_re, lam_im, log_dt, b_re, b_im, c_re, c_im):
    nl, g, p = lam_re.shape
    nb = g // GROUPS_PER_BLOCK

    def rows(v):
        return v.reshape(nl, nb, 1, HALF)

    def cols(v):
        return v.reshape(nl, nb, HALF, 1)

    def spec(r, c):
        return pl.BlockSpec((1, 1, r, c), lambda l, k: (l, k, 0, 0))

    win, wout, wskip, lam2 = pl.pallas_call(
        _discretize_kernel,
        grid=(nl, nb),
        in_specs=[spec(1, HALF)] * 3 + [spec(HALF, 1)] * 3 + [spec(LANE, HALF)] * 2 + [spec(HALF, LANE)] * 2,
        out_specs=[spec(2 * LANE, 2 * HALF), spec(4 * HALF, 2 * LANE), spec(LANE, 2 * LANE), spec(1, 2 * HALF)],
        out_shape=[jax.ShapeDtypeStruct((nl, nb, 2 * LANE, 2 * HALF), BF16),
                   jax.ShapeDtypeStruct((nl, nb, 4 * HALF, 2 * LANE), BF16),
                   jax.ShapeDtypeStruct((nl, nb, LANE, 2 * LANE), BF16),
                   jax.ShapeDtypeStruct((nl, nb, 1, 2 * HALF), F32)],
        compiler_params=_params(2),
        name="s5_discretize",
    )(rows(lam_re), rows(lam_im), rows(log_dt), cols(lam_re), cols(lam_im), cols(log_dt),
      _block_diag_in(b_re), _block_diag_in(b_im), _block_diag_out(c_re), _block_diag_out(c_im))
    return win, wout, wskip, lam2.reshape(nl, nb, 2 * HALF)


def _slab_pitch(rows):
    assert rows % SUBLANE == 0
    return rows + SUBLANE // 2


def _split_state(st):
    return (tuple(st[:, p * LANE:(p + 1) * LANE] for p in range(N_PAIRS)),
            tuple(st[:, HALF + p * LANE:HALF + (p + 1) * LANE] for p in range(N_PAIRS)))


def _s5_mixer_kernel(x_ref, g2_ref, g3_ref, win_ref, wout_ref, wskip_ref, lam_ref, d_ref, wglu_ref, bglu_ref,
                     init_ref, o_ref, final_ref, state_ref, y_ref, s_ref, prev_ref=None,
                     *, n_blocks, n_seq, n_t, d_model):
    tile = pl.program_id(1)
    rows = n_seq * n_t
    pitch = _slab_pitch(rows)

    @pl.when(tile == 0)
    def _():
        state_ref[...] = init_ref[...]

    lre, lim = _split_state(lam_ref[...])
    xe = x_ref[0, :, :d_model]
    xo = x_ref[0, :, d_model:]
    ue = _rms(xe, g2_ref[...])
    uo = _rms(xo, g2_ref[...])
    ueb = ue.astype(BF16)
    uob = uo.astype(BF16)


    for k in range(n_blocks):
        lhs = jnp.concatenate([ueb[:, k * LANE:(k + 1) * LANE], uob[:, k * LANE:(k + 1) * LANE]], axis=1)
        v = jnp.dot(lhs, win_ref[k], preferred_element_type=F32)
        for j in range(N_SLABS):
            s_ref[j, pl.ds(k * pitch + 1, rows), :] = v[:, j * LANE:(j + 1) * LANE]

    def seq_body(seq, carry):
        xre, xim = _split_state(state_ref[seq])
        if n_seq == 1:
            before = pl.ds(0, SUBLANE, stride=pitch)
            for p in range(N_PAIRS):
                s_ref[p, before, :] = xre[p]
                s_ref[N_PAIRS + p, before, :] = xim[p]

        def step(t, st):
            xre, xim = st
            m = seq * n_t + t
            pair = pl.ds(m + 1, SUBLANE, stride=pitch)
            nre, nim = [], []
            for p in range(N_PAIRS):
                if n_seq > 1:
                    prev_ref[p, pl.ds(m, SUBLANE, stride=pitch), :] = xre[p]
                    prev_ref[N_PAIRS + p, pl.ds(m, SUBLANE, stride=pitch), :] = xim[p]
                re = lre[p] * xre[p] - lim[p] * xim[p] + s_ref[p, pair, :]
                im = lre[p] * xim[p] + lim[p] * xre[p] + s_ref[N_PAIRS + p, pair, :]
                s_ref[p, pair, :] = re
                s_ref[N_PAIRS + p, pair, :] = im
                nre.append(re)
                nim.append(im)
            return tuple(nre), tuple(nim)

        xre, xim = lax.fori_loop(0, n_t, step, (xre, xim), unroll=min(n_t, 8))
        state_ref[seq] = jnp.concatenate(list(xre) + list(xim), axis=1)
        return carry

    if n_seq == 1:
        seq_body(0, 0)
    else:
        lax.fori_loop(0, n_seq, seq_body, 0)

    before_ref = s_ref if n_seq == 1 else prev_ref
    for k in range(n_blocks):
        xb = jnp.concatenate([before_ref[j, pl.ds(k * pitch, rows), :] for j in range(N_SLABS)], axis=1)
        xa = jnp.concatenate([s_ref[j, pl.ds(k * pitch + 1, rows), :] for j in range(N_SLABS)], axis=1)
        yk = jnp.dot(jnp.concatenate([xb, xa], axis=1).astype(BF16), wout_ref[k], preferred_element_type=F32)
        yk = yk + jnp.dot(ueb[:, k * LANE:(k + 1) * LANE], wskip_ref[k], preferred_element_type=F32)
        y_ref[:, k * LANE:(k + 1) * LANE] = yk[:, :LANE]
        y_ref[:, d_model + k * LANE:d_model + (k + 1) * LANE] = yk[:, LANE:]
    y = jnp.concatenate([y_ref[:, :d_model] + d_ref[...] * ue, y_ref[:, d_model:] + d_ref[...] * uo], axis=0)
    h = jnp.dot(jax.nn.gelu(y).astype(BF16), wglu_ref[...], preferred_element_type=F32) + bglu_ref[...]
    r = _rms(h[:, :d_model] * _sigmoid(h[:, d_model:]), g3_ref[...])
    o_ref[0] = jnp.concatenate([xe + r[:rows], xo + r[rows:]], axis=1)

    @pl.when(tile == pl.num_programs(1) - 1)
    def _():
        final_ref[...] = state_ref[...]


def _s5_layer(x, g2, g3, win, wout, wskip, lam2, d, wglu, bglu, init, layer, *, n_seq, n_t):
    nb_rows, s2, dm2 = x.shape
    dm = dm2 // 2
    n_blocks = win.shape[1]
    assert n_blocks == SUBLANE
    rows = n_seq * n_t
    n_tiles = s2 // rows
    assert n_seq == 1 or n_tiles == 1
    row = pl.BlockSpec((1, rows, dm2), lambda b, i: (b, i, 0))
    vec = _const_spec((1, dm))
    state = pl.BlockSpec((n_seq, SUBLANE, 2 * HALF), lambda b, i: (b, 0, 0))
    slab = pltpu.VMEM((N_SLABS, n_blocks * _slab_pitch(rows), LANE), F32)
    return pl.pallas_call(
        functools.partial(_s5_mixer_kernel, n_blocks=n_blocks, n_seq=n_seq, n_t=n_t, d_model=dm),
        grid=(nb_rows, n_tiles),
        in_specs=[row, vec, vec, _layer_spec(win.shape, layer), _layer_spec(wout.shape, layer),
                  _layer_spec(wskip.shape, layer), _layer_spec(lam2.shape, layer), vec,
                  _layer_spec(wglu.shape, layer), _const_spec((1, 2 * dm)), state],
        out_specs=[row, state],
        out_shape=[jax.ShapeDtypeStruct(x.shape, F32),
                   jax.ShapeDtypeStruct(init.shape, F32)],
        scratch_shapes=[pltpu.VMEM((n_seq, SUBLANE, 2 * HALF), F32), pltpu.VMEM((rows, dm2), F32), slab]
        + ([slab] if n_seq > 1 else []),
        compiler_params=_params(2),
        name="s5_mixer",
    )(x, g2, g3, win, wout, wskip, lam2, d, wglu, bglu, init)


def _state_to_rows(re, im):
    b = re.shape[0]
    return jnp.concatenate([re.reshape(b, -1, HALF), im.reshape(b, -1, HALF)], axis=-1)


def _rows_to_state(s, n_groups):
    b = s.shape[0]
    return (s[:, :, :HALF].reshape(b, n_groups, STATE_DIM), s[:, :, HALF:].reshape(b, n_groups, STATE_DIM))


def _t5_bucket_np(dist):
    n = np.maximum(dist, 0)
    max_exact = NUM_BUCKETS // 2
    nf = np.maximum(n, 1).astype(np.float32)
    large = max_exact + (np.log(nf / np.float32(max_exact)) / np.float32(math.log(MAX_DISTANCE / max_exact))
                         * np.float32(NUM_BUCKETS - max_exact)).astype(np.int32)
    large = np.minimum(large, NUM_BUCKETS - 1)
    return np.where(n < max_exact, n, large).astype(np.int32)


def _bucket_table(n_q, n_k, q_offset, n_q_pad, n_k_pad, first_key=0):
    dist = (np.arange(n_q)[:, None] + q_offset) - np.arange(n_k)[None, :]
    valid = (dist >= 0) & (dist < WINDOW) & (np.arange(n_k)[None, :] >= first_key)
    table = np.full((n_q_pad, n_k_pad), -1, np.int32)
    table[:n_q, :n_k] = np.where(valid, _t5_bucket_np(dist), -1)
    return table


def _bias_kernel(rel_ref, idx_ref, o_ref):
    h = pl.program_id(0)
    idx = idx_ref[...]
    acc = jnp.full(idx.shape, NEG_INF, F32)
    for b in range(NUM_BUCKETS):
        acc = jnp.where(idx == b, rel_ref[b, h], acc)
    o_ref[0] = acc


def _bias_planes(rel_bias, table):
    n_heads = rel_bias.shape[1]
    r, c = table.shape
    return pl.pallas_call(
        _bias_kernel,
        grid=(n_heads,),
        in_specs=[pl.BlockSpec(memory_space=pltpu.SMEM), pl.BlockSpec((r, c), lambda h: (0, 0))],
        out_specs=pl.BlockSpec((1, r, c), lambda h: (h, 0, 0)),
        out_shape=jax.ShapeDtypeStruct((n_heads, r, c), F32),
        compiler_params=_params(),
        name="attn_bias",
    )(rel_bias, jnp.asarray(table))


def _kv_head_rows(kv, g):
    lane = lax.broadcasted_iota(jnp.int32, kv.shape, 1)
    swapped = pltpu.roll(kv, HEAD_DIM, axis=1)
    even, odd = (kv, swapped) if g == 0 else (swapped, kv)
    return jnp.where(lane < HEAD_DIM, even, 0.0), jnp.where(lane >= HEAD_DIM, odd, 0.0)


def _pair_layout_t(planes, n_kv, n_q):
    h, _, n_k = planes.shape
    pairs = h // n_kv // 2
    p = planes[:, :n_q].reshape(n_kv, pairs, 2, n_q, n_k).transpose(0, 2, 4, 1, 3)
    return p.reshape(n_kv, 2 * n_k, pairs * n_q)


def _sink_cols(sinks, n_kv, n_q):
    h = sinks.shape[0]
    pairs = h // n_kv // 2
    s = sinks.reshape(n_kv, pairs, 2).transpose(0, 2, 1)[..., None]
    return jnp.broadcast_to(s, (n_kv, 2, pairs, n_q)).reshape(n_kv, 2, pairs * n_q)


def _attend_t(chains):
    scores = []
    for q_t, k_even, k_odd, _, _, bias, _ in chains:
        ke = jnp.concatenate([k_even, k_odd], axis=0).astype(BF16)
        scores.append(jnp.dot(ke, q_t, preferred_element_type=F32) + bias)
    weights = []
    for s, chain in zip(scores, chains):
        sink = chain[6]
        n_k = s.shape[0] // 2
        probs, inv = [], []
        for par in range(2):
            sp = s[par * n_k:(par + 1) * n_k]
            sk = sink[par:par + 1]
            mx = jnp.maximum(jnp.max(sp, axis=0, keepdims=True), sk)
            p = jnp.exp(sp - mx)
            den = jnp.sum(p, axis=0, keepdims=True) + jnp.exp(sk - mx)
            probs.append(p.astype(BF16))
            inv.append(1.0 / den)
        weights.append((jnp.concatenate(probs, axis=0), inv))
    outs = []
    for (p_t, inv), chain in zip(weights, chains):
        ve = jnp.concatenate([chain[3], chain[4]], axis=0).astype(BF16)
        o = lax.dot_general(ve, p_t, (((0,), (0,)), ((), ())), preferred_element_type=F32)
        row = lax.broadcasted_iota(jnp.int32, o.shape, 0)
        outs.append(o * jnp.where(row < HEAD_DIM, inv[0], inv[1]))
    return outs


def _prompt_attn_kernel(x_ref, kvc_ref, kvp_ref, bias_ref, sink_ref, g2_ref, g3_ref, wq_ref, bq_ref,
                        wo_ref, bo_ref, o_ref, *, n_kv, n_sub, steps_per_seq, scale):
    blk = WINDOW
    x = x_ref[...]
    u = _rms(x, g2_ref[...]).astype(BF16)
    q_t = lax.dot_general(wq_ref[...], u, (((1,), (1,)), ((), ())), preferred_element_type=F32)
    q_t = ((q_t + jnp.concatenate([bq_ref[...]] * n_sub, axis=1)) * scale).astype(BF16)
    kv = jnp.concatenate([kvp_ref[...], kvc_ref[...]], axis=0)
    pairs = q_t.shape[0] // LANE // n_kv
    first = jnp.where(pl.program_id(0) % steps_per_seq == 0, 1, 0)
    k_rows = [_kv_head_rows(kv[:, :n_kv * HEAD_DIM], g) for g in range(n_kv)]
    v_rows = [_kv_head_rows(kv[:, n_kv * HEAD_DIM:], g) for g in range(n_kv)]
    chains = []
    for s in range(n_sub):
        keys = slice(s * blk, (s + 2) * blk)
        for g in range(n_kv):
            qp = jnp.concatenate([q_t[(g * pairs + i) * LANE:(g * pairs + i + 1) * LANE, s * blk:(s + 1) * blk]
                                  for i in range(pairs)], axis=1)
            bias = bias_ref[first, g] if s == 0 else bias_ref[0, g]
            chains.append((qp, k_rows[g][0][keys], k_rows[g][1][keys], v_rows[g][0][keys],
                           v_rows[g][1][keys], bias, sink_ref[g]))
    outs = _attend_t(chains)
    o_t = jnp.concatenate(
        [jnp.concatenate([outs[s * n_kv + g][:, i * blk:(i + 1) * blk] for s in range(n_sub)], axis=1)
         for g in range(n_kv) for i in range(pairs)], axis=0).astype(BF16)
    a = lax.dot_general(o_t, wo_ref[...], (((0,), (0,)), ((), ())), preferred_element_type=F32) + bo_ref[...]
    o_ref[...] = x + _rms(a, g3_ref[...])


def _prompt_attention(x, kv, bias, sink, g2, g3, wq_t, bq_col, wo, bo, layer, *, blocks_per_seq, n_kv):
    t, d = x.shape
    blk = WINDOW
    n_sub = ATTN_BLOCKS_PER_STEP
    assert n_kv * HEAD_DIM == LANE and blocks_per_seq % n_sub == 0
    row = pl.BlockSpec((n_sub * blk, d), lambda i: (i, 0))
    kvw = kv.shape[1]
    return pl.pallas_call(
        functools.partial(_prompt_attn_kernel, n_kv=n_kv, n_sub=n_sub,
                          steps_per_seq=blocks_per_seq // n_sub, scale=1.0 / math.sqrt(HEAD_DIM)),
        grid=(t // (n_sub * blk),),
        in_specs=[row,
                  pl.BlockSpec((n_sub * blk, kvw), lambda i: (i, 0)),
                  pl.BlockSpec((blk, kvw), lambda i: (jnp.maximum(i * n_sub - 1, 0), 0)),
                  _const_spec(bias.shape),
                  _const_spec(sink.shape), _const_spec((1, d)), _const_spec((1, d)),
                  _layer_spec(wq_t.shape, layer), _const_spec(bq_col.shape),
                  _layer_spec(wo.shape, layer), _const_spec((1, d))],
        out_specs=row,
        out_shape=jax.ShapeDtypeStruct((t, d), F32),
        compiler_params=_params(),
        name="prompt_attention",
    )(x, kv, kv, bias, sink, g2, g3, wq_t, bq_col, wo, bo)


def _sample_attn_kernel(q_ref, k_ref, v_ref, bias_ref, sink_ref, o_ref, *, n_kv, scale):
    chains = []
    for b in range(q_ref.shape[0]):
        k = k_ref[b]
        v = v_ref[b]
        for g in range(n_kv):
            chains.append(((q_ref[b, g] * scale).astype(BF16), *_kv_head_rows(k, g), *_kv_head_rows(v, g),
                           bias_ref[g], sink_ref[g]))
    outs = _attend_t(chains)
    for b in range(q_ref.shape[0]):
        for g in range(n_kv):
            o_ref[b, g] = outs[b * n_kv + g]


def _sample_attention(q_t, k_full, v_full, bias, sink, *, n_kv):
    nb, _, _, n_l = q_t.shape
    n_k = k_full.shape[1]
    bb = SUBLANE
    return pl.pallas_call(
        functools.partial(_sample_attn_kernel, n_kv=n_kv, scale=1.0 / math.sqrt(HEAD_DIM)),
        grid=(nb // bb,),
        in_specs=[pl.BlockSpec((bb, n_kv, LANE, n_l), lambda i: (i, 0, 0, 0)),
                  pl.BlockSpec((bb, n_k, LANE), lambda i: (i, 0, 0)),
                  pl.BlockSpec((bb, n_k, LANE), lambda i: (i, 0, 0)),
                  _const_spec(bias.shape), _const_spec(sink.shape)],
        out_specs=pl.BlockSpec((bb, n_kv, LANE, n_l), lambda i: (i, 0, 0, 0)),
        out_shape=jax.ShapeDtypeStruct(q_t.shape, F32),
        compiler_params=_params(),
        name="sample_attention",
    )(q_t, k_full, v_full, bias, sink)


def kernel(x_prompt, x_sample, state_ssm_re, state_ssm_im, cache_win_k, cache_win_v, norm_g, ffn1_w_gu, ffn1_w_down, ffn2_w_gu, ffn2_w_down, ssm_lambda_re, ssm_lambda_im, ssm_log_dt, ssm_b_re, ssm_b_im, ssm_c_re, ssm_c_im, ssm_d, ssm_w_glu, ssm_b_glu, kv_norm_g, w_kv, b_kv, attn_w_q, attn_b_q, attn_sinks, attn_w_o, attn_b_o, rel_bias):
    bsz, seq, dm = x_prompt.shape
    dec_b, dec_s, _ = x_sample.shape
    depth = norm_g.shape[0]
    n_a = ssm_lambda_re.shape[0]
    n_groups = ssm_lambda_re.shape[1]
    n_heads = attn_sinks.shape[1]
    n_kv = cache_win_k.shape[2]
    n_past = cache_win_k.shape[1]
    pairs = n_heads // n_kv // 2
    n_k_pad = 2 * WINDOW
    assert seq % ROW_TILE == 0 and seq % WINDOW == 0 and dec_s % 2 == 0
    assert dec_b * dec_s == ROW_TILE and n_past + dec_s <= n_k_pad and dec_b % SUBLANE == 0
    assert n_kv == 2 and dm == n_heads * HEAD_DIM and n_groups == SUBLANE * GROUPS_PER_BLOCK

    bf = lambda w: w.astype(BF16)
    row = lambda v: v.reshape(1, -1)
    s5_w = _discretize(ssm_lambda_re, ssm_lambda_im, ssm_log_dt, ssm_b_re, ssm_b_im, ssm_c_re, ssm_c_im)
    ffn_w = (bf(ffn1_w_gu[0])[None], bf(ffn1_w_down[0])[None])
    wglu, wkv, wq, wo = bf(ssm_w_glu), bf(w_kv)[None], bf(attn_w_q), bf(attn_w_o)

    tab_p = _bucket_table(WINDOW, 2 * WINDOW, WINDOW, WINDOW, 2 * WINDOW)
    tab_p0 = _bucket_table(WINDOW, 2 * WINDOW, WINDOW, WINDOW, 2 * WINDOW, first_key=WINDOW)
    tab_s = _bucket_table(dec_s, n_past + dec_s, n_past, SUBLANE, n_k_pad)
    wq_t = wq.transpose(0, 2, 1)
    bias_p = jnp.stack([_pair_layout_t(_bias_planes(rel_bias, tab_p), n_kv, WINDOW),
                        _pair_layout_t(_bias_planes(rel_bias, tab_p0), n_kv, WINDOW)])
    bias_s = _pair_layout_t(_bias_planes(rel_bias, tab_s), n_kv, dec_s)

    xp = x_prompt.reshape(bsz * seq, dm)
    xs = x_sample.reshape(dec_b * dec_s, dm)
    zero_state = jnp.zeros((bsz, SUBLANE, 2 * HALF), F32)

    ends_p, ends_s = [], []
    kv_p = k_full = v_full = None
    half = n_kv * HEAD_DIM
    for l in range(depth):
        g = [row(norm_g[l, i]) for i in range(norm_g.shape[1])]
        if l == n_a:
            kv_p = _norm_proj(xp, row(kv_norm_g), wkv, row(b_kv), 0)
            kv_s = _norm_proj(xs, row(kv_norm_g), wkv, row(b_kv), 0).reshape(dec_b, dec_s, 2 * half)
            k_full = jnp.concatenate([cache_win_k.reshape(dec_b, n_past, half), kv_s[:, :, :half]], axis=1)
            v_full = jnp.concatenate([cache_win_v.reshape(dec_b, n_past, half), kv_s[:, :, half:]], axis=1)
        pair_io = dict(pair_in=0 < l < n_a, pair_out=l < n_a)
        xs = _ffn(xs, g[0], g[1], *ffn_w, **pair_io)
        xp, *ffn_w = _ffn(xp, g[0], g[1], *ffn_w, **pair_io, cast_next=(ffn2_w_gu, ffn2_w_down, l))
        if l < n_a:
            common = (g[2], g[3], *s5_w, row(ssm_d[l]), wglu, row(ssm_b_glu[l]))
            xp3, fin_p = _s5_layer(xp.reshape(bsz, seq // 2, 2 * dm), *common, zero_state, l,
                                   n_seq=1, n_t=ROW_TILE // 2)
            xs3, fin_s = _s5_layer(xs.reshape(1, dec_b * dec_s // 2, 2 * dm), *common,
                                   _state_to_rows(state_ssm_re[l], state_ssm_im[l]), l,
                                   n_seq=dec_b, n_t=dec_s // 2)
            xp, xs = xp3.reshape(bsz * seq // 2, 2 * dm), xs3.reshape(dec_b * dec_s // 2, 2 * dm)
            ends_p.append(_rows_to_state(fin_p, n_groups))
            ends_s.append(_rows_to_state(fin_s, n_groups))
        else:
            bl = l - n_a
            bq_l, bo_l = row(attn_b_q[bl]), row(attn_b_o[bl])
            bq_col = jnp.broadcast_to(attn_b_q[bl][:, None], (dm, WINDOW))
            xp = _prompt_attention(xp, kv_p, bias_p, _sink_cols(attn_sinks[bl], n_kv, WINDOW), g[2], g[3],
                                   wq_t, bq_col, wo, bo_l, bl, blocks_per_seq=seq // WINDOW, n_kv=n_kv)
            q = _norm_proj(xs, g[2], wq, bq_l, bl)
            q = q.reshape(dec_b, dec_s, n_kv, pairs, LANE).transpose(0, 2, 4, 3, 1)
            q = q.reshape(dec_b, n_kv, LANE, pairs * dec_s)
            pad = n_k_pad - (n_past + dec_s)
            o = _sample_attention(q, jnp.pad(k_full, ((0, 0), (0, pad), (0, 0))),
                                  jnp.pad(v_full, ((0, 0), (0, pad), (0, 0))),
                                  bias_s, _sink_cols(attn_sinks[bl], n_kv, dec_s), n_kv=n_kv)
            o = o.reshape(dec_b, n_kv, LANE, pairs, dec_s).transpose(0, 4, 1, 3, 2).reshape(dec_b * dec_s, dm)
            xs = _out_proj_residual(xs, o, g[3], wo, bo_l, bl)
        pair_io = dict(pair_in=l < n_a, pair_out=l < n_a - 1)
        xs = _ffn(xs, g[4], g[5], *ffn_w, **pair_io)
        if l + 1 < depth:
            xp, *ffn_w = _ffn(xp, g[4], g[5], *ffn_w, **pair_io, cast_next=(ffn1_w_gu, ffn1_w_down, l + 1))
        else:
            xp = _ffn(xp, g[4], g[5], *ffn_w, **pair_io)

    kv_p3 = kv_p.reshape(bsz, seq, 2 * half)
    new_k_p = kv_p3[:, -WINDOW:, :half].reshape(bsz, WINDOW, n_kv, HEAD_DIM)
    new_v_p = kv_p3[:, -WINDOW:, half:].reshape(bsz, WINDOW, n_kv, HEAD_DIM)
    new_k_s = k_full[:, -WINDOW:].reshape(dec_b, WINDOW, n_kv, HEAD_DIM)
    new_v_s = v_full[:, -WINDOW:].reshape(dec_b, WINDOW, n_kv, HEAD_DIM)
    return (xp.reshape(bsz, seq, dm), xs.reshape(dec_b, dec_s, dm),
            jnp.stack([e[0] for e in ends_p]), jnp.stack([e[1] for e in ends_p]),
            new_k_p, new_v_p,
            jnp.stack([e[0] for e in ends_s]), jnp.stack([e[1] for e in ends_s]),
            new_k_s, new_v_s)
```

```python
import functools
import math

import numpy as np
import jax
import jax.numpy as jnp
from jax import lax
from jax.experimental import pallas as pl
from jax.experimental.pallas import tpu as pltpu

F32 = jnp.float32
BF16 = jnp.bfloat16

LANE = 128
SUBLANE = 8
VMEM_LIMIT_BYTES = 56 * 1024 * 1024

RMS_EPS = 1e-6
GROUP_SIZE = 16
STATE_DIM = 64
HEAD_DIM = 64
WINDOW = 128
NUM_BUCKETS = 32
MAX_DISTANCE = WINDOW
GROUPS_PER_BLOCK = LANE // GROUP_SIZE
HALF = GROUPS_PER_BLOCK * STATE_DIM
N_SLABS = 2 * HALF // LANE
N_PAIRS = N_SLABS // 2
ROW_TILE = 512
SAMPLE_SPLIT = 2
FFN_CHAINS = 4
ATTN_BLOCKS_PER_STEP = 4
NEG_INF = float("-inf")
LOG2_E = math.log2(math.e)
QUERY_SCALE = LOG2_E / math.sqrt(HEAD_DIM)


def _params(n_axes=1):
    return pltpu.CompilerParams(dimension_semantics=("arbitrary",) * n_axes,
                                vmem_limit_bytes=VMEM_LIMIT_BYTES)


def _const_spec(shape):
    nd = len(shape)
    return pl.BlockSpec(shape, lambda *_: (0,) * nd, pipeline_mode=pl.Buffered(1))


def _layer_spec(shape, layer):
    nd = len(shape)
    return pl.BlockSpec((None,) + tuple(shape[1:]), lambda *_: (layer,) + (0,) * (nd - 1),
                        pipeline_mode=pl.Buffered(1))


def _rms(x, g):
    ms = jnp.mean(x * x, axis=-1, keepdims=True)
    return x * lax.rsqrt(ms + RMS_EPS) * g


def _aligned(row):
    return row if isinstance(row, int) else pl.multiple_of(row, SUBLANE)


def _sigmoid(x):
    return 1.0 / (1.0 + jnp.exp(-x))


def _cast_chunk(rows, n_steps):
    tile = 2 * SUBLANE
    for chunk in range(tile, rows + 1, tile):
        if rows % chunk == 0 and rows // chunk <= n_steps:
            return chunk
    raise ValueError((rows, n_steps))


def _ffn_kernel(*refs, d_ff, pair_in, pair_out, cast_steps):
    if cast_steps:
        x_ref, ga_ref, gb_ref, wgu_ref, wd_ref, nwgu_ref, nwd_ref, o_ref, cwgu_ref, cwd_ref, *slab_ref = refs
        for src, dst, steps in ((nwgu_ref, cwgu_ref, cast_steps[0]), (nwd_ref, cwd_ref, cast_steps[1])):
            @pl.when(pl.program_id(0) < steps)
            def _(src=src, dst=dst):
                dst[...] = src[...].astype(BF16)
    else:
        x_ref, ga_ref, gb_ref, wgu_ref, wd_ref, o_ref, *slab_ref = refs
    d = ga_ref.shape[1]
    if pair_in:
        x = jnp.concatenate([x_ref[:, :d], x_ref[:, d:]], axis=0)
    else:
        x = x_ref[...]
    half = x.shape[0] // 2
    rc = x.shape[0] // FFN_CHAINS
    xc = [x[c * rc:(c + 1) * rc] for c in range(FFN_CHAINS)]
    xn = [_rms(v, ga_ref[...]).astype(BF16) for v in xc]
    gu = [jnp.dot(v, wgu_ref[...], preferred_element_type=F32) for v in xn]
    h = [(v[:, :d_ff] * _sigmoid(v[:, :d_ff]) * v[:, d_ff:]).astype(BF16) for v in gu]
    y = [jnp.dot(v, wd_ref[...], preferred_element_type=F32) for v in h]
    out = jnp.concatenate([v + 0.5 * _rms(w, gb_ref[...]) for v, w in zip(xc, y)], axis=0)
    if pair_in == pair_out:
        o_ref[...] = jnp.concatenate([out[:half], out[half:]], axis=1) if pair_in else out
        return
    (slab,) = slab_ref
    even = pl.ds(0, half, stride=2)
    odd = pl.ds(1, half, stride=2)
    for j in range(d // LANE):
        lanes = slice(j * LANE, (j + 1) * LANE)
        if pair_out:
            slab[j] = out[:, lanes]
            o_ref[:, lanes] = slab[j, even, :]
            o_ref[:, d + j * LANE:d + (j + 1) * LANE] = slab[j, odd, :]
        else:
            slab[j, even, :] = out[:half, lanes]
            slab[j, odd, :] = out[half:, lanes]
            o_ref[:, lanes] = slab[j]


def _ffn(x, ga, gb, wgu, wd, pair_in=False, pair_out=False, cast_next=None):
    d = ga.shape[1]
    t = x.shape[0] * (2 if pair_in else 1)
    d_ff = wd.shape[1]
    tm = min(ROW_TILE, t)
    n_steps = t // tm

    def rows(paired):
        return pl.BlockSpec((tm // 2, 2 * d) if paired else (tm, d), lambda i: (i, 0))

    in_specs = [rows(pair_in), _const_spec((1, d)), _const_spec((1, d)),
                _layer_spec(wgu.shape, 0), _layer_spec(wd.shape, 0)]
    out_specs = [rows(pair_out)]
    out_shape = [jax.ShapeDtypeStruct((t // 2, 2 * d) if pair_out else (t, d), F32)]
    operands = [x, ga, gb, wgu, wd]
    cast_steps = None
    if cast_next is not None:
        *next_w, layer = cast_next
        cast_steps = []
        for w in next_w:
            chunk = _cast_chunk(w.shape[1], n_steps)
            last = w.shape[1] // chunk - 1
            cast_steps.append(last + 1)
            in_specs.append(pl.BlockSpec((None, chunk, w.shape[2]),
                                         lambda i, last=last: (layer, jnp.minimum(i, last), 0)))
            out_specs.append(pl.BlockSpec((None, chunk, w.shape[2]),
                                          lambda i, last=last: (0, jnp.minimum(i, last), 0)))
            out_shape.append(jax.ShapeDtypeStruct((1,) + w.shape[1:], BF16))
            operands.append(w)
    res = pl.pallas_call(
        functools.partial(_ffn_kernel, d_ff=d_ff, pair_in=pair_in, pair_out=pair_out, cast_steps=cast_steps),
        grid=(n_steps,),
        in_specs=in_specs,
        out_specs=out_specs,
        out_shape=out_shape,
        scratch_shapes=[pltpu.VMEM((d // LANE, tm, LANE), F32)] if pair_in != pair_out else [],
        compiler_params=_params(),
        name="ffn",
    )(*operands)
    return res if cast_next is not None else res[0]


def _proj_kernel(x_ref, g_ref, w_ref, b_ref, o_ref):
    xn = _rms(x_ref[...], g_ref[...]).astype(BF16)
    o_ref[...] = jnp.dot(xn, w_ref[...], preferred_element_type=F32) + b_ref[...]


def _norm_proj(x, g, w, b, layer):
    t, d = x.shape
    n = w.shape[-1]
    tm = min(ROW_TILE, t)
    return pl.pallas_call(
        _proj_kernel,
        grid=(t // tm,),
        in_specs=[pl.BlockSpec((tm, d), lambda i: (i, 0)), _const_spec((1, d)),
                  _layer_spec(w.shape, layer), _const_spec((1, n))],
        out_specs=pl.BlockSpec((tm, n), lambda i: (i, 0)),
        out_shape=jax.ShapeDtypeStruct((t, n), F32),
        compiler_params=_params(),
        name="norm_proj",
    )(x, g, w, b)


def _out_proj_kernel(x_ref, o_ref_in, g_ref, w_ref, b_ref, out_ref):
    a = jnp.dot(o_ref_in[...].astype(BF16), w_ref[...], preferred_element_type=F32) + b_ref[...]
    out_ref[...] = x_ref[...] + _rms(a, g_ref[...])


def _out_proj_residual(x, o, g, w, b, layer):
    t, d = x.shape
    tm = min(ROW_TILE, t)
    row = pl.BlockSpec((tm, d), lambda i: (i, 0))
    return pl.pallas_call(
        _out_proj_kernel,
        grid=(t // tm,),
        in_specs=[row, pl.BlockSpec((tm, o.shape[1]), lambda i: (i, 0)), _const_spec((1, d)),
                  _layer_spec(w.shape, layer), _const_spec((1, d))],
        out_specs=row,
        out_shape=jax.ShapeDtypeStruct((t, d), F32),
        compiler_params=_params(),
        name="out_proj_residual",
    )(x, o, g, w, b)


def _lam_bar(lam_re, lam_im, log_dt):
    dt = jnp.exp(log_dt)
    mag = jnp.exp(lam_re * dt)
    return mag * jnp.cos(lam_im * dt), mag * jnp.sin(lam_im * dt)


def _discretize_kernel(lre_ref, lim_ref, ldt_ref, lre_col_ref, lim_col_ref, ldt_col_ref,
                       bre_ref, bim_ref, cre_ref, cim_ref, win_ref, wout_ref, wskip_ref, lam2_ref):
    a = lre_ref[0, 0]
    b = lim_ref[0, 0]
    lbr, lbi = _lam_bar(a, b, ldt_ref[0, 0])
    den = a * a + b * b
    cr = ((lbr - 1.0) * a + lbi * b) / den
    ci = (lbi * a - (lbr - 1.0) * b) / den
    bre = bre_ref[0, 0]
    bim = bim_ref[0, 0]
    wbr = cr * bre - ci * bim
    wbi = cr * bim + ci * bre
    win_ref[0, 0, :LANE, :HALF] = (lbr * wbr - lbi * wbi).astype(BF16)
    win_ref[0, 0, :LANE, HALF:] = (lbr * wbi + lbi * wbr).astype(BF16)
    win_ref[0, 0, LANE:, :HALF] = wbr.astype(BF16)
    win_ref[0, 0, LANE:, HALF:] = wbi.astype(BF16)
    lam2_ref[0, 0, :, :HALF] = lbr * lbr - lbi * lbi
    lam2_ref[0, 0, :, HALF:] = 2.0 * lbr * lbi
    lcr, lci = _lam_bar(lre_col_ref[0, 0], lim_col_ref[0, 0], ldt_col_ref[0, 0])
    cre = cre_ref[0, 0]
    cim = cim_ref[0, 0]
    zeros = jnp.zeros((HALF, LANE), BF16)
    wout_ref[0, 0, 0 * HALF:1 * HALF, :LANE] = (cre * lcr - cim * lci).astype(BF16)
    wout_ref[0, 0, 1 * HALF:2 * HALF, :LANE] = (-(cre * lci + cim * lcr)).astype(BF16)
    wout_ref[0, 0, 2 * HALF:3 * HALF, :LANE] = zeros
    wout_ref[0, 0, 3 * HALF:4 * HALF, :LANE] = zeros
    wout_ref[0, 0, 0 * HALF:1 * HALF, LANE:] = zeros
    wout_ref[0, 0, 1 * HALF:2 * HALF, LANE:] = zeros
    wout_ref[0, 0, 2 * HALF:3 * HALF, LANE:] = cre.astype(BF16)
    wout_ref[0, 0, 3 * HALF:4 * HALF, LANE:] = (-cim).astype(BF16)
    skip = (jnp.dot(wbr, cre, precision=lax.Precision.HIGHEST, preferred_element_type=F32)
            - jnp.dot(wbi, cim, precision=lax.Precision.HIGHEST, preferred_element_type=F32))
    wskip_ref[0, 0, :, :LANE] = skip.astype(BF16)
    wskip_ref[0, 0, :, LANE:] = jnp.zeros((LANE, LANE), BF16)


def _block_diag_in(b):
    nl, g, p, h = b.shape
    nb = g // GROUPS_PER_BLOCK
    bt = b.transpose(0, 1, 3, 2).reshape(nl, nb, GROUPS_PER_BLOCK, h, p)
    eye = jnp.eye(GROUPS_PER_BLOCK, dtype=b.dtype)
    out = bt[:, :, :, :, None, :] * eye[None, None, :, None, :, None]
    return out.reshape(nl, nb, GROUPS_PER_BLOCK * h, GROUPS_PER_BLOCK * p)


def _block_diag_out(c):
    nl, g, h, p = c.shape
    nb = g // GROUPS_PER_BLOCK
    ct = c.transpose(0, 1, 3, 2).reshape(nl, nb, GROUPS_PER_BLOCK, p, h)
    eye = jnp.eye(GROUPS_PER_BLOCK, dtype=c.dtype)
    out = ct[:, :, :, :, None, :] * eye[None, None, :, None, :, None]
    return out.reshape(nl, nb, GROUPS_PER_BLOCK * p, GROUPS_PER_BLOCK * h)


def _discretize(lam_re, lam_im, log_dt, b_re, b_im, c_re, c_im):
    nl, g, p = lam_re.shape
    nb = g // GROUPS_PER_BLOCK

    def rows(v):
        return v.reshape(nl, nb, 1, HALF)

    def cols(v):
        return v.reshape(nl, nb, HALF, 1)

    def spec(r, c):
        return pl.BlockSpec((1, 1, r, c), lambda l, k: (l, k, 0, 0))

    win, wout, wskip, lam2 = pl.pallas_call(
        _discretize_kernel,
        grid=(nl, nb),
        in_specs=[spec(1, HALF)] * 3 + [spec(HALF, 1)] * 3 + [spec(LANE, HALF)] * 2 + [spec(HALF, LANE)] * 2,
        out_specs=[spec(2 * LANE, 2 * HALF), spec(4 * HALF, 2 * LANE), spec(LANE, 2 * LANE), spec(1, 2 * HALF)],
        out_shape=[jax.ShapeDtypeStruct((nl, nb, 2 * LANE, 2 * HALF), BF16),
                   jax.ShapeDtypeStruct((nl, nb, 4 * HALF, 2 * LANE), BF16),
                   jax.ShapeDtypeStruct((nl, nb, LANE, 2 * LANE), BF16),
                   jax.ShapeDtypeStruct((nl, nb, 1, 2 * HALF), F32)],
        compiler_params=_params(2),
        name="s5_discretize",
    )(rows(lam_re), rows(lam_im), rows(log_dt), cols(lam_re), cols(lam_im), cols(log_dt),
      _block_diag_in(b_re), _block_diag_in(b_im), _block_diag_out(c_re), _block_diag_out(c_im))
    return win, wout, wskip, lam2.reshape(nl, nb, 2 * HALF)


def _slab_pitch(rows):
    assert rows % SUBLANE == 0
    return rows + SUBLANE // 2


def _split_state(st):
    return (tuple(st[:, p * LANE:(p + 1) * LANE] for p in range(N_PAIRS)),
            tuple(st[:, HALF + p * LANE:HALF + (p + 1) * LANE] for p in range(N_PAIRS)))


def _s5_mixer_kernel(x_ref, g2_ref, g3_ref, win_ref, wout_ref, wskip_ref, lam_ref, d_ref, wglu_ref, bglu_ref,
                     init_ref, o_ref, final_ref, state_ref, y_ref, s_ref, prev_ref=None,
                     *, n_blocks, n_seq, n_t, d_model):
    tile = pl.program_id(1)
    rows = n_seq * n_t
    pitch = _slab_pitch(rows)

    @pl.when(tile == 0)
    def _():
        state_ref[...] = init_ref[...]

    lre, lim = _split_state(lam_ref[...])
    xe = x_ref[0, :, :d_model]
    xo = x_ref[0, :, d_model:]
    ue = _rms(xe, g2_ref[...])
    uo = _rms(xo, g2_ref[...])
    ueb = ue.astype(BF16)
    uob = uo.astype(BF16)


    for k in range(n_blocks):
        lhs = jnp.concatenate([ueb[:, k * LANE:(k + 1) * LANE], uob[:, k * LANE:(k + 1) * LANE]], axis=1)
        v = jnp.dot(lhs, win_ref[k], preferred_element_type=F32)
        for j in range(N_SLABS):
            s_ref[j, pl.ds(k * pitch + 1, rows), :] = v[:, j * LANE:(j + 1) * LANE]

    def seq_body(seq, carry):
        xre, xim = _split_state(state_ref[seq])
        if n_seq == 1:
            before = pl.ds(0, SUBLANE, stride=pitch)
            for p in range(N_PAIRS):
                s_ref[p, before, :] = xre[p]
                s_ref[N_PAIRS + p, before, :] = xim[p]

        def step(t, st):
            xre, xim = st
            m = seq * n_t + t
            pair = pl.ds(m + 1, SUBLANE, stride=pitch)
            nre, nim = [], []
            for p in range(N_PAIRS):
                if n_seq > 1:
                    prev_ref[p, pl.ds(m, SUBLANE, stride=pitch), :] = xre[p]
                    prev_ref[N_PAIRS + p, pl.ds(m, SUBLANE, stride=pitch), :] = xim[p]
                re = lre[p] * xre[p] - lim[p] * xim[p] + s_ref[p, pair, :]
                im = lre[p] * xim[p] + lim[p] * xre[p] + s_ref[N_PAIRS + p, pair, :]
                s_ref[p, pair, :] = re
                s_ref[N_PAIRS + p, pair, :] = im
                nre.append(re)
                nim.append(im)
            return tuple(nre), tuple(nim)

        xre, xim = lax.fori_loop(0, n_t, step, (xre, xim), unroll=min(n_t, 8))
        state_ref[seq] = jnp.concatenate(list(xre) + list(xim), axis=1)
        return carry

    if n_seq == 1:
        seq_body(0, 0)
    else:
        lax.fori_loop(0, n_seq, seq_body, 0)

    before_ref = s_ref if n_seq == 1 else prev_ref
    for k in range(n_blocks):
        xb = jnp.concatenate([before_ref[j, pl.ds(k * pitch, rows), :] for j in range(N_SLABS)], axis=1)
        xa = jnp.concatenate([s_ref[j, pl.ds(k * pitch + 1, rows), :] for j in range(N_SLABS)], axis=1)
        yk = jnp.dot(jnp.concatenate([xb, xa], axis=1).astype(BF16), wout_ref[k], preferred_element_type=F32)
        yk = yk + jnp.dot(ueb[:, k * LANE:(k + 1) * LANE], wskip_ref[k], preferred_element_type=F32)
        y_ref[:, k * LANE:(k + 1) * LANE] = yk[:, :LANE]
        y_ref[:, d_model + k * LANE:d_model + (k + 1) * LANE] = yk[:, LANE:]
    y = jnp.concatenate([y_ref[:, :d_model] + d_ref[...] * ue, y_ref[:, d_model:] + d_ref[...] * uo], axis=0)
    rc = y.shape[0] // FFN_CHAINS
    gy = [jax.nn.gelu(y[c * rc:(c + 1) * rc]).astype(BF16) for c in range(FFN_CHAINS)]
    h = [jnp.dot(v, wglu_ref[...], preferred_element_type=F32) + bglu_ref[...] for v in gy]
    r = jnp.concatenate([_rms(v[:, :d_model] * _sigmoid(v[:, d_model:]), g3_ref[...]) for v in h], axis=0)
    o_ref[0] = jnp.concatenate([xe + r[:rows], xo + r[rows:]], axis=1)

    @pl.when(tile == pl.num_programs(1) - 1)
    def _():
        final_ref[...] = state_ref[...]


def _s5_layer(x, g2, g3, win, wout, wskip, lam2, d, wglu, bglu, init, layer, *, n_seq, n_t):
    nb_rows, s2, dm2 = x.shape
    dm = dm2 // 2
    n_blocks = win.shape[1]
    assert n_blocks == SUBLANE
    rows = n_seq * n_t
    n_tiles = s2 // rows
    assert n_seq == 1 or n_tiles == 1
    row = pl.BlockSpec((1, rows, dm2), lambda b, i: (b, i, 0))
    vec = _const_spec((1, dm))
    state = pl.BlockSpec((n_seq, SUBLANE, 2 * HALF), lambda b, i: (b, 0, 0))
    slab = pltpu.VMEM((N_SLABS, n_blocks * _slab_pitch(rows), LANE), F32)
    return pl.pallas_call(
        functools.partial(_s5_mixer_kernel, n_blocks=n_blocks, n_seq=n_seq, n_t=n_t, d_model=dm),
        grid=(nb_rows, n_tiles),
        in_specs=[row, vec, vec, _layer_spec(win.shape, layer), _layer_spec(wout.shape, layer),
                  _layer_spec(wskip.shape, layer), _layer_spec(lam2.shape, layer), vec,
                  _layer_spec(wglu.shape, layer), _const_spec((1, 2 * dm)),
                  pl.BlockSpec(state.block_shape, state.index_map, pipeline_mode=pl.Buffered(1))],
        out_specs=[row, state],
        out_shape=[jax.ShapeDtypeStruct(x.shape, F32),
                   jax.ShapeDtypeStruct(init.shape, F32)],
        scratch_shapes=[pltpu.VMEM((n_seq, SUBLANE, 2 * HALF), F32), pltpu.VMEM((rows, dm2), F32), slab]
        + ([slab] if n_seq > 1 else []),
        compiler_params=_params(2),
        name="s5_mixer",
    )(x, g2, g3, win, wout, wskip, lam2, d, wglu, bglu, init)


def _state_to_rows(re, im):
    b = re.shape[0]
    return jnp.concatenate([re.reshape(b, -1, HALF), im.reshape(b, -1, HALF)], axis=-1)


def _rows_to_state(s, n_groups):
    b = s.shape[0]
    return (s[:, :, :HALF].reshape(b, n_groups, STATE_DIM), s[:, :, HALF:].reshape(b, n_groups, STATE_DIM))


def _t5_bucket_np(dist):
    n = np.maximum(dist, 0)
    max_exact = NUM_BUCKETS // 2
    nf = np.maximum(n, 1).astype(np.float32)
    large = max_exact + (np.log(nf / np.float32(max_exact)) / np.float32(math.log(MAX_DISTANCE / max_exact))
                         * np.float32(NUM_BUCKETS - max_exact)).astype(np.int32)
    large = np.minimum(large, NUM_BUCKETS - 1)
    return np.where(n < max_exact, n, large).astype(np.int32)


def _bucket_table(n_q, n_k, q_offset, n_q_pad, n_k_pad, first_key=0):
    dist = (np.arange(n_q)[:, None] + q_offset) - np.arange(n_k)[None, :]
    valid = (dist >= 0) & (dist < WINDOW) & (np.arange(n_k)[None, :] >= first_key)
    table = np.full((n_q_pad, n_k_pad), -1, np.int32)
    table[:n_q, :n_k] = np.where(valid, _t5_bucket_np(dist), -1)
    return table


def _bias_kernel(rel_ref, idx_ref, o_ref):
    h = pl.program_id(0)
    idx = idx_ref[...]
    acc = jnp.full(idx.shape, NEG_INF, F32)
    for b in range(NUM_BUCKETS):
        acc = jnp.where(idx == b, rel_ref[b, h] * LOG2_E, acc)
    o_ref[0] = acc


def _bias_planes(rel_bias, table):
    n_heads = rel_bias.shape[1]
    r, c = table.shape
    return pl.pallas_call(
        _bias_kernel,
        grid=(n_heads,),
        in_specs=[pl.BlockSpec(memory_space=pltpu.SMEM), pl.BlockSpec((r, c), lambda h: (0, 0))],
        out_specs=pl.BlockSpec((1, r, c), lambda h: (h, 0, 0)),
        out_shape=jax.ShapeDtypeStruct((n_heads, r, c), F32),
        compiler_params=_params(),
        name="attn_bias",
    )(rel_bias, jnp.asarray(table))


def _kv_head_rows(kv, g):
    lane = lax.broadcasted_iota(jnp.int32, kv.shape, 1)
    swapped = pltpu.roll(kv, HEAD_DIM, axis=1)
    even, odd = (kv, swapped) if g == 0 else (swapped, kv)
    return jnp.where(lane < HEAD_DIM, even, 0.0), jnp.where(lane >= HEAD_DIM, odd, 0.0)


def _pair_layout_t(planes, n_kv, n_q):
    h, _, n_k = planes.shape
    pairs = h // n_kv // 2
    p = planes[:, :n_q].reshape(n_kv, pairs, 2, n_q, n_k).transpose(0, 2, 4, 1, 3)
    return p.reshape(n_kv, 2 * n_k, pairs * n_q)


def _sink_cols(sinks, n_kv, n_q):
    h = sinks.shape[0]
    pairs = h // n_kv // 2
    s = sinks.reshape(n_kv, pairs, 2).transpose(0, 2, 1)[..., None]
    return jnp.broadcast_to(s, (n_kv, 2, pairs, n_q)).reshape(n_kv, 2, pairs * n_q)


def _attend_t(chains):
    scores = []
    for q_t, k_even, k_odd, _, _, bias, _ in chains:
        ke = jnp.concatenate([k_even, k_odd], axis=0).astype(BF16)
        scores.append(jnp.dot(ke, q_t, preferred_element_type=F32) + bias)
    weights = []
    for s, chain in zip(scores, chains):
        sink = chain[6]
        n_k = s.shape[0] // 2
        probs, inv = [], []
        for par in range(2):
            sp = s[par * n_k:(par + 1) * n_k]
            sk = sink[par:par + 1] * LOG2_E
            mx = jnp.maximum(jnp.max(sp, axis=0, keepdims=True), sk)
            p = jnp.exp2(sp - mx)
            den = jnp.sum(p, axis=0, keepdims=True) + jnp.exp2(sk - mx)
            probs.append(p.astype(BF16))
            inv.append(1.0 / den)
        weights.append((jnp.concatenate(probs, axis=0), inv))
    outs = []
    for (p_t, inv), chain in zip(weights, chains):
        ve = jnp.concatenate([chain[3], chain[4]], axis=0).astype(BF16)
        o = lax.dot_general(ve, p_t, (((0,), (0,)), ((), ())), preferred_element_type=F32)
        row = lax.broadcasted_iota(jnp.int32, o.shape, 0)
        outs.append(o * jnp.where(row < HEAD_DIM, inv[0], inv[1]))
    return outs


def _prompt_attn_kernel(x_ref, kvc_ref, kvp_ref, bias_ref, sink_ref, g2_ref, g3_ref, wq_ref, bq_ref,
                        wo_ref, bo_ref, o_ref, *, n_kv, n_sub, steps_per_seq, scale):
    blk = WINDOW
    x = x_ref[...]
    u = _rms(x, g2_ref[...]).astype(BF16)
    q_t = lax.dot_general(wq_ref[...], u, (((1,), (1,)), ((), ())), preferred_element_type=F32)
    q_t = ((q_t + jnp.concatenate([bq_ref[...]] * n_sub, axis=1)) * scale).astype(BF16)
    kv = jnp.concatenate([kvp_ref[...], kvc_ref[...]], axis=0)
    pairs = q_t.shape[0] // LANE // n_kv
    first = jnp.where(pl.program_id(0) % steps_per_seq == 0, 1, 0)
    k_rows = [_kv_head_rows(kv[:, :n_kv * HEAD_DIM], g) for g in range(n_kv)]
    v_rows = [_kv_head_rows(kv[:, n_kv * HEAD_DIM:], g) for g in range(n_kv)]
    chains = []
    for s in range(n_sub):
        keys = slice(s * blk, (s + 2) * blk)
        for g in range(n_kv):
            qp = jnp.concatenate([q_t[(g * pairs + i) * LANE:(g * pairs + i + 1) * LANE, s * blk:(s + 1) * blk]
                                  for i in range(pairs)], axis=1)
            bias = bias_ref[first, g] if s == 0 else bias_ref[0, g]
            chains.append((qp, k_rows[g][0][keys], k_rows[g][1][keys], v_rows[g][0][keys],
                           v_rows[g][1][keys], bias, sink_ref[g]))
    outs = _attend_t(chains)
    o_t = jnp.concatenate(
        [jnp.concatenate([outs[s * n_kv + g][:, i * blk:(i + 1) * blk] for s in range(n_sub)], axis=1)
         for g in range(n_kv) for i in range(pairs)], axis=0).astype(BF16)
    a = lax.dot_general(o_t, wo_ref[...], (((0,), (0,)), ((), ())), preferred_element_type=F32) + bo_ref[...]
    o_ref[...] = x + _rms(a, g3_ref[...])


def _prompt_attention(x, kv, bias, sink, g2, g3, wq_t, bq_col, wo, bo, layer, *, blocks_per_seq, n_kv):
    t, d = x.shape
    blk = WINDOW
    n_sub = ATTN_BLOCKS_PER_STEP
    assert n_kv * HEAD_DIM == LANE and blocks_per_seq % n_sub == 0
    row = pl.BlockSpec((n_sub * blk, d), lambda i: (i, 0))
    kvw = kv.shape[1]
    return pl.pallas_call(
        functools.partial(_prompt_attn_kernel, n_kv=n_kv, n_sub=n_sub,
                          steps_per_seq=blocks_per_seq // n_sub, scale=QUERY_SCALE),
        grid=(t // (n_sub * blk),),
        in_specs=[row,
                  pl.BlockSpec((n_sub * blk, kvw), lambda i: (i, 0)),
                  pl.BlockSpec((blk, kvw), lambda i: (jnp.maximum(i * n_sub - 1, 0), 0)),
                  _const_spec(bias.shape),
                  _const_spec(sink.shape), _const_spec((1, d)), _const_spec((1, d)),
                  _layer_spec(wq_t.shape, layer), _const_spec(bq_col.shape),
                  _layer_spec(wo.shape, layer), _const_spec((1, d))],
        out_specs=row,
        out_shape=jax.ShapeDtypeStruct((t, d), F32),
        compiler_params=_params(),
        name="prompt_attention",
    )(x, kv, kv, bias, sink, g2, g3, wq_t, bq_col, wo, bo)


def _sample_attn_kernel(q_ref, k_ref, v_ref, bias_ref, sink_ref, o_ref, *, n_kv, scale):
    chains = []
    for b in range(q_ref.shape[0]):
        k = k_ref[b]
        v = v_ref[b]
        for g in range(n_kv):
            chains.append(((q_ref[b, g] * scale).astype(BF16), *_kv_head_rows(k, g), *_kv_head_rows(v, g),
                           bias_ref[g], sink_ref[g]))
    outs = _attend_t(chains)
    for b in range(q_ref.shape[0]):
        for g in range(n_kv):
            o_ref[b, g] = outs[b * n_kv + g]


def _sample_attention(q_t, k_full, v_full, bias, sink, *, n_kv):
    nb, _, _, n_l = q_t.shape
    n_k = k_full.shape[1]
    bb = SUBLANE
    return pl.pallas_call(
        functools.partial(_sample_attn_kernel, n_kv=n_kv, scale=QUERY_SCALE),
        grid=(nb // bb,),
        in_specs=[pl.BlockSpec((bb, n_kv, LANE, n_l), lambda i: (i, 0, 0, 0)),
                  pl.BlockSpec((bb, n_k, LANE), lambda i: (i, 0, 0)),
                  pl.BlockSpec((bb, n_k, LANE), lambda i: (i, 0, 0)),
                  _const_spec(bias.shape), _const_spec(sink.shape)],
        out_specs=pl.BlockSpec((bb, n_kv, LANE, n_l), lambda i: (i, 0, 0, 0)),
        out_shape=jax.ShapeDtypeStruct(q_t.shape, F32),
        compiler_params=_params(),
        name="sample_attention",
    )(q_t, k_full, v_full, bias, sink)


def kernel(x_prompt, x_sample, state_ssm_re, state_ssm_im, cache_win_k, cache_win_v, norm_g, ffn1_w_gu, ffn1_w_down, ffn2_w_gu, ffn2_w_down, ssm_lambda_re, ssm_lambda_im, ssm_log_dt, ssm_b_re, ssm_b_im, ssm_c_re, ssm_c_im, ssm_d, ssm_w_glu, ssm_b_glu, kv_norm_g, w_kv, b_kv, attn_w_q, attn_b_q, attn_sinks, attn_w_o, attn_b_o, rel_bias):
    bsz, seq, dm = x_prompt.shape
    dec_b, dec_s, _ = x_sample.shape
    depth = norm_g.shape[0]
    n_a = ssm_lambda_re.shape[0]
    n_groups = ssm_lambda_re.shape[1]
    n_heads = attn_sinks.shape[1]
    n_kv = cache_win_k.shape[2]
    n_past = cache_win_k.shape[1]
    pairs = n_heads // n_kv // 2
    n_k_pad = 2 * WINDOW
    assert seq % ROW_TILE == 0 and seq % WINDOW == 0 and dec_s % 2 == 0
    assert dec_b * dec_s == ROW_TILE and n_past + dec_s <= n_k_pad and dec_b % SUBLANE == 0
    assert n_kv == 2 and dm == n_heads * HEAD_DIM and n_groups == SUBLANE * GROUPS_PER_BLOCK

    bf = lambda w: w.astype(BF16)
    row = lambda v: v.reshape(1, -1)
    s5_w = _discretize(ssm_lambda_re, ssm_lambda_im, ssm_log_dt, ssm_b_re, ssm_b_im, ssm_c_re, ssm_c_im)
    ffn_w = (bf(ffn1_w_gu[0])[None], bf(ffn1_w_down[0])[None])
    wglu, wkv, wq, wo = bf(ssm_w_glu), bf(w_kv)[None], bf(attn_w_q), bf(attn_w_o)

    tab_p = _bucket_table(WINDOW, 2 * WINDOW, WINDOW, WINDOW, 2 * WINDOW)
    tab_p0 = _bucket_table(WINDOW, 2 * WINDOW, WINDOW, WINDOW, 2 * WINDOW, first_key=WINDOW)
    tab_s = _bucket_table(dec_s, n_past + dec_s, n_past, SUBLANE, n_k_pad)
    wq_t = wq.transpose(0, 2, 1)
    bias_p = jnp.stack([_pair_layout_t(_bias_planes(rel_bias, tab_p), n_kv, WINDOW),
                        _pair_layout_t(_bias_planes(rel_bias, tab_p0), n_kv, WINDOW)])
    bias_s = _pair_layout_t(_bias_planes(rel_bias, tab_s), n_kv, dec_s)

    xp = x_prompt.reshape(bsz * seq, dm)
    xs = x_sample.reshape(dec_b * dec_s, dm)
    zero_state = jnp.zeros((bsz, SUBLANE, 2 * HALF), F32)

    ends_p, ends_s = [], []
    kv_p = k_full = v_full = None
    half = n_kv * HEAD_DIM
    for l in range(depth):
        g = [row(norm_g[l, i]) for i in range(norm_g.shape[1])]
        if l == n_a:
            kv_p = _norm_proj(xp, row(kv_norm_g), wkv, row(b_kv), 0)
            kv_s = _norm_proj(xs, row(kv_norm_g), wkv, row(b_kv), 0).reshape(dec_b, dec_s, 2 * half)
            k_full = jnp.concatenate([cache_win_k.reshape(dec_b, n_past, half), kv_s[:, :, :half]], axis=1)
            v_full = jnp.concatenate([cache_win_v.reshape(dec_b, n_past, half), kv_s[:, :, half:]], axis=1)
        pair_io = dict(pair_in=0 < l < n_a, pair_out=l < n_a)
        xs = _ffn(xs, g[0], g[1], *ffn_w, **pair_io)
        xp, *ffn_w = _ffn(xp, g[0], g[1], *ffn_w, **pair_io, cast_next=(ffn2_w_gu, ffn2_w_down, l))
        if l < n_a:
            common = (g[2], g[3], *s5_w, row(ssm_d[l]), wglu, row(ssm_b_glu[l]))
            xp3, fin_p = _s5_layer(xp.reshape(bsz, seq // 2, 2 * dm), *common, zero_state, l,
                                   n_seq=1, n_t=ROW_TILE // 2)
            xs3, fin_s = _s5_layer(xs.reshape(SAMPLE_SPLIT, dec_b * dec_s // (2 * SAMPLE_SPLIT), 2 * dm), *common,
                                   _state_to_rows(state_ssm_re[l], state_ssm_im[l]), l,
                                   n_seq=dec_b // SAMPLE_SPLIT, n_t=dec_s // 2)
            xp, xs = xp3.reshape(bsz * seq // 2, 2 * dm), xs3.reshape(dec_b * dec_s // 2, 2 * dm)
            ends_p.append(_rows_to_state(fin_p, n_groups))
            ends_s.append(_rows_to_state(fin_s, n_groups))
        else:
            bl = l - n_a
            bq_l, bo_l = row(attn_b_q[bl]), row(attn_b_o[bl])
            bq_col = jnp.broadcast_to(attn_b_q[bl][:, None], (dm, WINDOW))
            xp = _prompt_attention(xp, kv_p, bias_p, _sink_cols(attn_sinks[bl], n_kv, WINDOW), g[2], g[3],
                                   wq_t, bq_col, wo, bo_l, bl, blocks_per_seq=seq // WINDOW, n_kv=n_kv)
            q = _norm_proj(xs, g[2], wq, bq_l, bl)
            q = q.reshape(dec_b, dec_s, n_kv, pairs, LANE).transpose(0, 2, 4, 3, 1)
            q = q.reshape(dec_b, n_kv, LANE, pairs * dec_s)
            pad = n_k_pad - (n_past + dec_s)
            o = _sample_attention(q, jnp.pad(k_full, ((0, 0), (0, pad), (0, 0))),
                                  jnp.pad(v_full, ((0, 0), (0, pad), (0, 0))),
                                  bias_s, _sink_cols(attn_sinks[bl], n_kv, dec_s), n_kv=n_kv)
            o = o.reshape(dec_b, n_kv, LANE, pairs, dec_s).transpose(0, 4, 1, 3, 2).reshape(dec_b * dec_s, dm)
            xs = _out_proj_residual(xs, o, g[3], wo, bo_l, bl)
        pair_io = dict(pair_in=l < n_a, pair_out=l < n_a - 1)
        xs = _ffn(xs, g[4], g[5], *ffn_w, **pair_io)
        if l + 1 < depth:
            xp, *ffn_w = _ffn(xp, g[4], g[5], *ffn_w, **pair_io, cast_next=(ffn1_w_gu, ffn1_w_down, l + 1))
        else:
            xp = _ffn(xp, g[4], g[5], *ffn_w, **pair_io)

    kv_p3 = kv_p.reshape(bsz, seq, 2 * half)
    new_k_p = kv_p3[:, -WINDOW:, :half].reshape(bsz, WINDOW, n_kv, HEAD_DIM)
    new_v_p = kv_p3[:, -WINDOW:, half:].reshape(bsz, WINDOW, n_kv, HEAD_DIM)
    new_k_s = k_full[:, -WINDOW:].reshape(dec_b, WINDOW, n_kv, HEAD_DIM)
    new_v_s = v_full[:, -WINDOW:].reshape(dec_b, WINDOW, n_kv, HEAD_DIM)
    return (xp.reshape(bsz, seq, dm), xs.reshape(dec_b, dec_s, dm),
            jnp.stack([e[0] for e in ends_p]), jnp.stack([e[1] for e in ends_p]),
            new_k_p, new_v_p,
            jnp.stack([e[0] for e in ends_s]), jnp.stack([e[1] for e in ends_s]),
            new_k_s, new_v_s)
```

```python
import functools
import math

import numpy as np
import jax
import jax.numpy as jnp
from jax import lax
from jax.experimental import pallas as pl
from jax.experimental.pallas import tpu as pltpu

F32 = jnp.float32
BF16 = jnp.bfloat16

LANE = 128
SUBLANE = 8
VMEM_LIMIT_BYTES = 56 * 1024 * 1024

RMS_EPS = 1e-6
GROUP_SIZE = 16
STATE_DIM = 64
HEAD_DIM = 64
WINDOW = 128
NUM_BUCKETS = 32
MAX_DISTANCE = WINDOW
GROUPS_PER_BLOCK = LANE // GROUP_SIZE
HALF = GROUPS_PER_BLOCK * STATE_DIM
N_SLABS = 2 * HALF // LANE
N_PAIRS = N_SLABS // 2
ROW_TILE = 512
SAMPLE_SPLIT = 2
FFN_CHAINS = 4
ATTN_BLOCKS_PER_STEP = 4
NEG_INF = float("-inf")
LOG2_E = math.log2(math.e)
QUERY_SCALE = LOG2_E / math.sqrt(HEAD_DIM)


def _params(n_axes=1):
    return pltpu.CompilerParams(dimension_semantics=("arbitrary",) * n_axes,
                                vmem_limit_bytes=VMEM_LIMIT_BYTES)


def _const_spec(shape):
    nd = len(shape)
    return pl.BlockSpec(shape, lambda *_: (0,) * nd, pipeline_mode=pl.Buffered(1))


def _layer_spec(shape, layer):
    nd = len(shape)
    return pl.BlockSpec((None,) + tuple(shape[1:]), lambda *_: (layer,) + (0,) * (nd - 1),
                        pipeline_mode=pl.Buffered(1))


def _rms(x, g):
    ms = jnp.mean(x * x, axis=-1, keepdims=True)
    return x * lax.rsqrt(ms + RMS_EPS) * g


def _aligned(row):
    return row if isinstance(row, int) else pl.multiple_of(row, SUBLANE)


def _sigmoid(x):
    return 1.0 / (1.0 + jnp.exp(-x))


def _cast_chunk(rows, n_steps):
    tile = 2 * SUBLANE
    for chunk in range(tile, rows + 1, tile):
        if rows % chunk == 0 and rows // chunk <= n_steps:
            return chunk
    raise ValueError((rows, n_steps))


def _ffn_kernel(*refs, d_ff, pair_in, pair_out, cast_steps):
    if cast_steps:
        x_ref, ga_ref, gb_ref, wgu_ref, wd_ref, nwgu_ref, nwd_ref, o_ref, cwgu_ref, cwd_ref, *slab_ref = refs
        for src, dst, steps in ((nwgu_ref, cwgu_ref, cast_steps[0]), (nwd_ref, cwd_ref, cast_steps[1])):
            @pl.when(pl.program_id(0) < steps)
            def _(src=src, dst=dst):
                dst[...] = src[...].astype(BF16)
    else:
        x_ref, ga_ref, gb_ref, wgu_ref, wd_ref, o_ref, *slab_ref = refs
    d = ga_ref.shape[1]
    if pair_in:
        x = jnp.concatenate([x_ref[:, :d], x_ref[:, d:]], axis=0)
    else:
        x = x_ref[...]
    half = x.shape[0] // 2
    rc = x.shape[0] // FFN_CHAINS
    xc = [x[c * rc:(c + 1) * rc] for c in range(FFN_CHAINS)]
    xn = [_rms(v, ga_ref[...]).astype(BF16) for v in xc]
    gu = [jnp.dot(v, wgu_ref[...], preferred_element_type=F32) for v in xn]
    h = [(v[:, :d_ff] * _sigmoid(v[:, :d_ff]) * v[:, d_ff:]).astype(BF16) for v in gu]
    y = [jnp.dot(v, wd_ref[...], preferred_element_type=F32) for v in h]
    out = jnp.concatenate([v + 0.5 * _rms(w, gb_ref[...]) for v, w in zip(xc, y)], axis=0)
    if pair_in == pair_out:
        o_ref[...] = jnp.concatenate([out[:half], out[half:]], axis=1) if pair_in else out
        return
    (slab,) = slab_ref
    even = pl.ds(0, half, stride=2)
    odd = pl.ds(1, half, stride=2)
    for j in range(d // LANE):
        lanes = slice(j * LANE, (j + 1) * LANE)
        if pair_out:
            slab[j] = out[:, lanes]
            o_ref[:, lanes] = slab[j, even, :]
            o_ref[:, d + j * LANE:d + (j + 1) * LANE] = slab[j, odd, :]
        else:
            slab[j, even, :] = out[:half, lanes]
            slab[j, odd, :] = out[half:, lanes]
            o_ref[:, lanes] = slab[j]


def _ffn(x, ga, gb, wgu, wd, pair_in=False, pair_out=False, cast_next=None):
    d = ga.shape[1]
    t = x.shape[0] * (2 if pair_in else 1)
    d_ff = wd.shape[1]
    tm = min(ROW_TILE, t)
    n_steps = t // tm

    def rows(paired):
        return pl.BlockSpec((tm // 2, 2 * d) if paired else (tm, d), lambda i: (i, 0))

    in_specs = [rows(pair_in), _const_spec((1, d)), _const_spec((1, d)),
                _layer_spec(wgu.shape, 0), _layer_spec(wd.shape, 0)]
    out_specs = [rows(pair_out)]
    out_shape = [jax.ShapeDtypeStruct((t // 2, 2 * d) if pair_out else (t, d), F32)]
    operands = [x, ga, gb, wgu, wd]
    cast_steps = None
    if cast_next is not None:
        *next_w, layer = cast_next
        cast_steps = []
        for w in next_w:
            chunk = _cast_chunk(w.shape[1], n_steps)
            last = w.shape[1] // chunk - 1
            cast_steps.append(last + 1)
            in_specs.append(pl.BlockSpec((None, chunk, w.shape[2]),
                                         lambda i, last=last: (layer, jnp.minimum(i, last), 0)))
            out_specs.append(pl.BlockSpec((None, chunk, w.shape[2]),
                                          lambda i, last=last: (0, jnp.minimum(i, last), 0)))
            out_shape.append(jax.ShapeDtypeStruct((1,) + w.shape[1:], BF16))
            operands.append(w)
    res = pl.pallas_call(
        functools.partial(_ffn_kernel, d_ff=d_ff, pair_in=pair_in, pair_out=pair_out, cast_steps=cast_steps),
        grid=(n_steps,),
        in_specs=in_specs,
        out_specs=out_specs,
        out_shape=out_shape,
        scratch_shapes=[pltpu.VMEM((d // LANE, tm, LANE), F32)] if pair_in != pair_out else [],
        compiler_params=_params(),
        name="ffn",
    )(*operands)
    return res if cast_next is not None else res[0]


def _proj_kernel(x_ref, g_ref, w_ref, b_ref, o_ref):
    xn = _rms(x_ref[...], g_ref[...]).astype(BF16)
    o_ref[...] = jnp.dot(xn, w_ref[...], preferred_element_type=F32) + b_ref[...]


def _norm_proj(x, g, w, b, layer):
    t, d = x.shape
    n = w.shape[-1]
    tm = min(ROW_TILE, t)
    return pl.pallas_call(
        _proj_kernel,
        grid=(t // tm,),
        in_specs=[pl.BlockSpec((tm, d), lambda i: (i, 0)), _const_spec((1, d)),
                  _layer_spec(w.shape, layer), _const_spec((1, n))],
        out_specs=pl.BlockSpec((tm, n), lambda i: (i, 0)),
        out_shape=jax.ShapeDtypeStruct((t, n), F32),
        compiler_params=_params(),
        name="norm_proj",
    )(x, g, w, b)


def _out_proj_kernel(x_ref, o_ref_in, g_ref, w_ref, b_ref, out_ref):
    a = jnp.dot(o_ref_in[...].astype(BF16), w_ref[...], preferred_element_type=F32) + b_ref[...]
    out_ref[...] = x_ref[...] + _rms(a, g_ref[...])


def _out_proj_residual(x, o, g, w, b, layer):
    t, d = x.shape
    tm = min(ROW_TILE, t)
    row = pl.BlockSpec((tm, d), lambda i: (i, 0))
    return pl.pallas_call(
        _out_proj_kernel,
        grid=(t // tm,),
        in_specs=[row, pl.BlockSpec((tm, o.shape[1]), lambda i: (i, 0)), _const_spec((1, d)),
                  _layer_spec(w.shape, layer), _const_spec((1, d))],
        out_specs=row,
        out_shape=jax.ShapeDtypeStruct((t, d), F32),
        compiler_params=_params(),
        name="out_proj_residual",
    )(x, o, g, w, b)


def _lam_bar(lam_re, lam_im, log_dt):
    dt = jnp.exp(log_dt)
    mag = jnp.exp(lam_re * dt)
    return mag * jnp.cos(lam_im * dt), mag * jnp.sin(lam_im * dt)


def _discretize_kernel(lre_ref, lim_ref, ldt_ref, lre_col_ref, lim_col_ref, ldt_col_ref,
                       bre_ref, bim_ref, cre_ref, cim_ref, win_ref, wout_ref, wskip_ref, lam2_ref):
    a = lre_ref[0, 0]
    b = lim_ref[0, 0]
    lbr, lbi = _lam_bar(a, b, ldt_ref[0, 0])
    den = a * a + b * b
    cr = ((lbr - 1.0) * a + lbi * b) / den
    ci = (lbi * a - (lbr - 1.0) * b) / den
    bre = bre_ref[0, 0]
    bim = bim_ref[0, 0]
    wbr = cr * bre - ci * bim
    wbi = cr * bim + ci * bre
    lwbr = lbr * wbr - lbi * wbi
    lwbi = lbr * wbi + lbi * wbr
    win_ref[0, 0, :LANE, :HALF] = lwbr.astype(BF16)
    win_ref[0, 0, :LANE, HALF:] = lwbi.astype(BF16)
    win_ref[0, 0, LANE:, :HALF] = wbr.astype(BF16)
    win_ref[0, 0, LANE:, HALF:] = wbi.astype(BF16)
    lam2_ref[0, 0, :, :HALF] = lbr * lbr - lbi * lbi
    lam2_ref[0, 0, :, HALF:] = 2.0 * lbr * lbi
    lcr, lci = _lam_bar(lre_col_ref[0, 0], lim_col_ref[0, 0], ldt_col_ref[0, 0])
    cre = cre_ref[0, 0]
    cim = cim_ref[0, 0]
    c1r = cre * lcr - cim * lci
    c1i = cre * lci + cim * lcr
    wout_ref[0, 0, :HALF, :LANE] = c1r.astype(BF16)
    wout_ref[0, 0, HALF:, :LANE] = (-c1i).astype(BF16)
    wout_ref[0, 0, :HALF, LANE:] = (c1r * lcr - c1i * lci).astype(BF16)
    wout_ref[0, 0, HALF:, LANE:] = (-(c1r * lci + c1i * lcr)).astype(BF16)
    def through(br, bi):
        return (jnp.dot(br, cre, precision=lax.Precision.HIGHEST, preferred_element_type=F32)
                - jnp.dot(bi, cim, precision=lax.Precision.HIGHEST, preferred_element_type=F32)).astype(BF16)

    cb = through(wbr, wbi)
    wskip_ref[0, 0, :LANE, :LANE] = cb
    wskip_ref[0, 0, :LANE, LANE:] = through(lwbr, lwbi)
    wskip_ref[0, 0, LANE:, :LANE] = jnp.zeros((LANE, LANE), BF16)
    wskip_ref[0, 0, LANE:, LANE:] = cb


def _block_diag_in(b):
    nl, g, p, h = b.shape
    nb = g // GROUPS_PER_BLOCK
    bt = b.transpose(0, 1, 3, 2).reshape(nl, nb, GROUPS_PER_BLOCK, h, p)
    eye = jnp.eye(GROUPS_PER_BLOCK, dtype=b.dtype)
    out = bt[:, :, :, :, None, :] * eye[None, None, :, None, :, None]
    return out.reshape(nl, nb, GROUPS_PER_BLOCK * h, GROUPS_PER_BLOCK * p)


def _block_diag_out(c):
    nl, g, h, p = c.shape
    nb = g // GROUPS_PER_BLOCK
    ct = c.transpose(0, 1, 3, 2).reshape(nl, nb, GROUPS_PER_BLOCK, p, h)
    eye = jnp.eye(GROUPS_PER_BLOCK, dtype=c.dtype)
    out = ct[:, :, :, :, None, :] * eye[None, None, :, None, :, None]
    return out.reshape(nl, nb, GROUPS_PER_BLOCK * p, GROUPS_PER_BLOCK * h)


def _discretize(lam_re, lam_im, log_dt, b_re, b_im, c_re, c_im):
    nl, g, p = lam_re.shape
    nb = g // GROUPS_PER_BLOCK

    def rows(v):
        return v.reshape(nl, nb, 1, HALF)

    def cols(v):
        return v.reshape(nl, nb, HALF, 1)

    def spec(r, c):
        return pl.BlockSpec((1, 1, r, c), lambda l, k: (l, k, 0, 0))

    win, wout, wskip, lam2 = pl.pallas_call(
        _discretize_kernel,
        grid=(nl, nb),
        in_specs=[spec(1, HALF)] * 3 + [spec(HALF, 1)] * 3 + [spec(LANE, HALF)] * 2 + [spec(HALF, LANE)] * 2,
        out_specs=[spec(2 * LANE, 2 * HALF), spec(2 * HALF, 2 * LANE), spec(2 * LANE, 2 * LANE), spec(1, 2 * HALF)],
        out_shape=[jax.ShapeDtypeStruct((nl, nb, 2 * LANE, 2 * HALF), BF16),
                   jax.ShapeDtypeStruct((nl, nb, 2 * HALF, 2 * LANE), BF16),
                   jax.ShapeDtypeStruct((nl, nb, 2 * LANE, 2 * LANE), BF16),
                   jax.ShapeDtypeStruct((nl, nb, 1, 2 * HALF), F32)],
        compiler_params=_params(2),
        name="s5_discretize",
    )(rows(lam_re), rows(lam_im), rows(log_dt), cols(lam_re), cols(lam_im), cols(log_dt),
      _block_diag_in(b_re), _block_diag_in(b_im), _block_diag_out(c_re), _block_diag_out(c_im))
    return win, wout, wskip, lam2.reshape(nl, nb, 2 * HALF)


def _slab_pitch(rows):
    assert rows % SUBLANE == 0
    return rows + SUBLANE // 2


def _split_state(st):
    return (tuple(st[:, p * LANE:(p + 1) * LANE] for p in range(N_PAIRS)),
            tuple(st[:, HALF + p * LANE:HALF + (p + 1) * LANE] for p in range(N_PAIRS)))


def _s5_mixer_kernel(x_ref, g2_ref, g3_ref, win_ref, wout_ref, wskip_ref, lam_ref, d_ref, wglu_ref, bglu_ref,
                     init_ref, o_ref, final_ref, state_ref, y_ref, s_ref,
                     *, n_blocks, n_seq, n_t, d_model):
    tile = pl.program_id(1)
    rows = n_seq * n_t
    pitch = _slab_pitch(rows)

    @pl.when(tile == 0)
    def _():
        state_ref[...] = init_ref[...]

    lre, lim = _split_state(lam_ref[...])
    xe = x_ref[0, :, :d_model]
    xo = x_ref[0, :, d_model:]
    ue = _rms(xe, g2_ref[...])
    uo = _rms(xo, g2_ref[...])
    ueb = ue.astype(BF16)
    uob = uo.astype(BF16)


    u_pair = [jnp.concatenate([ueb[:, k * LANE:(k + 1) * LANE], uob[:, k * LANE:(k + 1) * LANE]], axis=1)
              for k in range(n_blocks)]
    for k in range(n_blocks):
        v = jnp.dot(u_pair[k], win_ref[k], preferred_element_type=F32)
        for j in range(N_SLABS):
            s_ref[j, pl.ds(k * pitch, rows), :] = v[:, j * LANE:(j + 1) * LANE]

    def seq_body(seq, carry):
        xre, xim = _split_state(state_ref[seq])

        def step(t, st):
            xre, xim = st
            pair = pl.ds(seq * n_t + t, SUBLANE, stride=pitch)
            nre, nim = [], []
            for p in range(N_PAIRS):
                vre = s_ref[p, pair, :]
                vim = s_ref[N_PAIRS + p, pair, :]
                s_ref[p, pair, :] = xre[p]
                s_ref[N_PAIRS + p, pair, :] = xim[p]
                nre.append(lre[p] * xre[p] - lim[p] * xim[p] + vre)
                nim.append(lre[p] * xim[p] + lim[p] * xre[p] + vim)
            return tuple(nre), tuple(nim)

        xre, xim = lax.fori_loop(0, n_t, step, (xre, xim), unroll=min(n_t, 8))
        state_ref[seq] = jnp.concatenate(list(xre) + list(xim), axis=1)
        return carry

    if n_seq == 1:
        seq_body(0, 0)
    else:
        lax.fori_loop(0, n_seq, seq_body, 0)

    for k in range(n_blocks):
        xb = jnp.concatenate([s_ref[j, pl.ds(k * pitch, rows), :] for j in range(N_SLABS)], axis=1)
        yk = (jnp.dot(xb.astype(BF16), wout_ref[k], preferred_element_type=F32)
              + jnp.dot(u_pair[k], wskip_ref[k], preferred_element_type=F32))
        y_ref[:, k * LANE:(k + 1) * LANE] = yk[:, :LANE]
        y_ref[:, d_model + k * LANE:d_model + (k + 1) * LANE] = yk[:, LANE:]
    y = jnp.concatenate([y_ref[:, :d_model] + d_ref[...] * ue, y_ref[:, d_model:] + d_ref[...] * uo], axis=0)
    rc = y.shape[0] // FFN_CHAINS
    gy = [jax.nn.gelu(y[c * rc:(c + 1) * rc]).astype(BF16) for c in range(FFN_CHAINS)]
    h = [jnp.dot(v, wglu_ref[...], preferred_element_type=F32) + bglu_ref[...] for v in gy]
    r = jnp.concatenate([_rms(v[:, :d_model] * _sigmoid(v[:, d_model:]), g3_ref[...]) for v in h], axis=0)
    o_ref[0] = jnp.concatenate([xe + r[:rows], xo + r[rows:]], axis=1)

    @pl.when(tile == pl.num_programs(1) - 1)
    def _():
        final_ref[...] = state_ref[...]


def _s5_layer(x, g2, g3, win, wout, wskip, lam2, d, wglu, bglu, init, layer, *, n_seq, n_t):
    nb_rows, s2, dm2 = x.shape
    dm = dm2 // 2
    n_blocks = win.shape[1]
    assert n_blocks == SUBLANE
    rows = n_seq * n_t
    n_tiles = s2 // rows
    assert n_seq == 1 or n_tiles == 1
    row = pl.BlockSpec((1, rows, dm2), lambda b, i: (b, i, 0))
    vec = _const_spec((1, dm))
    state = pl.BlockSpec((n_seq, SUBLANE, 2 * HALF), lambda b, i: (b, 0, 0))
    slab = pltpu.VMEM((N_SLABS, n_blocks * _slab_pitch(rows), LANE), F32)
    return pl.pallas_call(
        functools.partial(_s5_mixer_kernel, n_blocks=n_blocks, n_seq=n_seq, n_t=n_t, d_model=dm),
        grid=(nb_rows, n_tiles),
        in_specs=[row, vec, vec, _layer_spec(win.shape, layer), _layer_spec(wout.shape, layer),
                  _layer_spec(wskip.shape, layer), _layer_spec(lam2.shape, layer), vec,
                  _layer_spec(wglu.shape, layer), _const_spec((1, 2 * dm)),
                  pl.BlockSpec(state.block_shape, state.index_map, pipeline_mode=pl.Buffered(1))],
        out_specs=[row, state],
        out_shape=[jax.ShapeDtypeStruct(x.shape, F32),
                   jax.ShapeDtypeStruct(init.shape, F32)],
        scratch_shapes=[pltpu.VMEM((n_seq, SUBLANE, 2 * HALF), F32), pltpu.VMEM((rows, dm2), F32), slab],
        compiler_params=_params(2),
        name="s5_mixer",
    )(x, g2, g3, win, wout, wskip, lam2, d, wglu, bglu, init)


def _state_to_rows(re, im):
    b = re.shape[0]
    return jnp.concatenate([re.reshape(b, -1, HALF), im.reshape(b, -1, HALF)], axis=-1)


def _rows_to_state(s, n_groups):
    b = s.shape[0]
    return (s[:, :, :HALF].reshape(b, n_groups, STATE_DIM), s[:, :, HALF:].reshape(b, n_groups, STATE_DIM))


def _t5_bucket_np(dist):
    n = np.maximum(dist, 0)
    max_exact = NUM_BUCKETS // 2
    nf = np.maximum(n, 1).astype(np.float32)
    large = max_exact + (np.log(nf / np.float32(max_exact)) / np.float32(math.log(MAX_DISTANCE / max_exact))
                         * np.float32(NUM_BUCKETS - max_exact)).astype(np.int32)
    large = np.minimum(large, NUM_BUCKETS - 1)
    return np.where(n < max_exact, n, large).astype(np.int32)


def _bucket_table(n_q, n_k, q_offset, n_q_pad, n_k_pad, first_key=0):
    dist = (np.arange(n_q)[:, None] + q_offset) - np.arange(n_k)[None, :]
    valid = (dist >= 0) & (dist < WINDOW) & (np.arange(n_k)[None, :] >= first_key)
    table = np.full((n_q_pad, n_k_pad), -1, np.int32)
    table[:n_q, :n_k] = np.where(valid, _t5_bucket_np(dist), -1)
    return table


def _bias_kernel(rel_ref, idx_ref, o_ref):
    h = pl.program_id(0)
    idx = idx_ref[...]
    acc = jnp.full(idx.shape, NEG_INF, F32)
    for b in range(NUM_BUCKETS):
        acc = jnp.where(idx == b, rel_ref[b, h] * LOG2_E, acc)
    o_ref[0] = acc


def _bias_planes(rel_bias, table):
    n_heads = rel_bias.shape[1]
    r, c = table.shape
    return pl.pallas_call(
        _bias_kernel,
        grid=(n_heads,),
        in_specs=[pl.BlockSpec(memory_space=pltpu.SMEM), pl.BlockSpec((r, c), lambda h: (0, 0))],
        out_specs=pl.BlockSpec((1, r, c), lambda h: (h, 0, 0)),
        out_shape=jax.ShapeDtypeStruct((n_heads, r, c), F32),
        compiler_params=_params(),
        name="attn_bias",
    )(rel_bias, jnp.asarray(table))


def _kv_head_rows(kv, g):
    lane = lax.broadcasted_iota(jnp.int32, kv.shape, 1)
    swapped = pltpu.roll(kv, HEAD_DIM, axis=1)
    even, odd = (kv, swapped) if g == 0 else (swapped, kv)
    return jnp.where(lane < HEAD_DIM, even, 0.0), jnp.where(lane >= HEAD_DIM, odd, 0.0)


def _pair_layout_t(planes, n_kv, n_q):
    h, _, n_k = planes.shape
    pairs = h // n_kv // 2
    p = planes[:, :n_q].reshape(n_kv, pairs, 2, n_q, n_k).transpose(0, 2, 4, 1, 3)
    return p.reshape(n_kv, 2 * n_k, pairs * n_q)


def _sink_cols(sinks, n_kv, n_q):
    h = sinks.shape[0]
    pairs = h // n_kv // 2
    s = sinks.reshape(n_kv, pairs, 2).transpose(0, 2, 1)[..., None]
    return jnp.broadcast_to(s, (n_kv, 2, pairs, n_q)).reshape(n_kv, 2, pairs * n_q)


def _attend_t(chains):
    scores = []
    for q_t, k_even, k_odd, _, _, bias, _ in chains:
        ke = jnp.concatenate([k_even, k_odd], axis=0).astype(BF16)
        scores.append(jnp.dot(ke, q_t, preferred_element_type=F32) + bias)
    weights = []
    for s, chain in zip(scores, chains):
        sink = chain[6]
        n_k = s.shape[0] // 2
        probs, inv = [], []
        for par in range(2):
            sp = s[par * n_k:(par + 1) * n_k]
            sk = sink[par:par + 1] * LOG2_E
            mx = jnp.maximum(jnp.max(sp, axis=0, keepdims=True), sk)
            p = jnp.exp2(sp - mx)
            den = jnp.sum(p, axis=0, keepdims=True) + jnp.exp2(sk - mx)
            probs.append(p.astype(BF16))
            inv.append(1.0 / den)
        weights.append((jnp.concatenate(probs, axis=0), inv))
    outs = []
    for (p_t, inv), chain in zip(weights, chains):
        ve = jnp.concatenate([chain[3], chain[4]], axis=0).astype(BF16)
        o = lax.dot_general(ve, p_t, (((0,), (0,)), ((), ())), preferred_element_type=F32)
        row = lax.broadcasted_iota(jnp.int32, o.shape, 0)
        outs.append(o * jnp.where(row < HEAD_DIM, inv[0], inv[1]))
    return outs


def _prompt_attn_kernel(x_ref, kvc_ref, kvp_ref, bias_ref, sink_ref, g2_ref, g3_ref, wq_ref, bq_ref,
                        wo_ref, bo_ref, o_ref, *, n_kv, n_sub, steps_per_seq, scale):
    blk = WINDOW
    x = x_ref[...]
    u = _rms(x, g2_ref[...]).astype(BF16)
    q_t = lax.dot_general(wq_ref[...], u, (((1,), (1,)), ((), ())), preferred_element_type=F32)
    q_t = ((q_t + jnp.concatenate([bq_ref[...]] * n_sub, axis=1)) * scale).astype(BF16)
    kv = jnp.concatenate([kvp_ref[...], kvc_ref[...]], axis=0)
    pairs = q_t.shape[0] // LANE // n_kv
    first = jnp.where(pl.program_id(0) % steps_per_seq == 0, 1, 0)
    k_rows = [_kv_head_rows(kv[:, :n_kv * HEAD_DIM], g) for g in range(n_kv)]
    v_rows = [_kv_head_rows(kv[:, n_kv * HEAD_DIM:], g) for g in range(n_kv)]
    chains = []
    for s in range(n_sub):
        keys = slice(s * blk, (s + 2) * blk)
        for g in range(n_kv):
            qp = jnp.concatenate([q_t[(g * pairs + i) * LANE:(g * pairs + i + 1) * LANE, s * blk:(s + 1) * blk]
                                  for i in range(pairs)], axis=1)
            bias = bias_ref[first, g] if s == 0 else bias_ref[0, g]
            chains.append((qp, k_rows[g][0][keys], k_rows[g][1][keys], v_rows[g][0][keys],
                           v_rows[g][1][keys], bias, sink_ref[g]))
    outs = _attend_t(chains)
    o_t = jnp.concatenate(
        [jnp.concatenate([outs[s * n_kv + g][:, i * blk:(i + 1) * blk] for s in range(n_sub)], axis=1)
         for g in range(n_kv) for i in range(pairs)], axis=0).astype(BF16)
    a = lax.dot_general(o_t, wo_ref[...], (((0,), (0,)), ((), ())), preferred_element_type=F32) + bo_ref[...]
    o_ref[...] = x + _rms(a, g3_ref[...])


def _prompt_attention(x, kv, bias, sink, g2, g3, wq_t, bq_col, wo, bo, layer, *, blocks_per_seq, n_kv):
    t, d = x.shape
    blk = WINDOW
    n_sub = ATTN_BLOCKS_PER_STEP
    assert n_kv * HEAD_DIM == LANE and blocks_per_seq % n_sub == 0
    row = pl.BlockSpec((n_sub * blk, d), lambda i: (i, 0))
    kvw = kv.shape[1]
    return pl.pallas_call(
        functools.partial(_prompt_attn_kernel, n_kv=n_kv, n_sub=n_sub,
                          steps_per_seq=blocks_per_seq // n_sub, scale=QUERY_SCALE),
        grid=(t // (n_sub * blk),),
        in_specs=[row,
                  pl.BlockSpec((n_sub * blk, kvw), lambda i: (i, 0)),
                  pl.BlockSpec((blk, kvw), lambda i: (jnp.maximum(i * n_sub - 1, 0), 0)),
                  _const_spec(bias.shape),
                  _const_spec(sink.shape), _const_spec((1, d)), _const_spec((1, d)),
                  _layer_spec(wq_t.shape, layer), _const_spec(bq_col.shape),
                  _layer_spec(wo.shape, layer), _const_spec((1, d))],
        out_specs=row,
        out_shape=jax.ShapeDtypeStruct((t, d), F32),
        compiler_params=_params(),
        name="prompt_attention",
    )(x, kv, kv, bias, sink, g2, g3, wq_t, bq_col, wo, bo)


def _sample_attn_kernel(q_ref, k_ref, v_ref, bias_ref, sink_ref, o_ref, *, n_kv, scale):
    chains = []
    for b in range(q_ref.shape[0]):
        k = k_ref[b]
        v = v_ref[b]
        for g in range(n_kv):
            chains.append(((q_ref[b, g] * scale).astype(BF16), *_kv_head_rows(k, g), *_kv_head_rows(v, g),
                           bias_ref[g], sink_ref[g]))
    outs = _attend_t(chains)
    for b in range(q_ref.shape[0]):
        for g in range(n_kv):
            o_ref[b, g] = outs[b * n_kv + g]


def _sample_attention(q_t, k_full, v_full, bias, sink, *, n_kv):
    nb, _, _, n_l = q_t.shape
    n_k = k_full.shape[1]
    bb = SUBLANE
    return pl.pallas_call(
        functools.partial(_sample_attn_kernel, n_kv=n_kv, scale=QUERY_SCALE),
        grid=(nb // bb,),
        in_specs=[pl.BlockSpec((bb, n_kv, LANE, n_l), lambda i: (i, 0, 0, 0)),
                  pl.BlockSpec((bb, n_k, LANE), lambda i: (i, 0, 0)),
                  pl.BlockSpec((bb, n_k, LANE), lambda i: (i, 0, 0)),
                  _const_spec(bias.shape), _const_spec(sink.shape)],
        out_specs=pl.BlockSpec((bb, n_kv, LANE, n_l), lambda i: (i, 0, 0, 0)),
        out_shape=jax.ShapeDtypeStruct(q_t.shape, F32),
        compiler_params=_params(),
        name="sample_attention",
    )(q_t, k_full, v_full, bias, sink)


def kernel(x_prompt, x_sample, state_ssm_re, state_ssm_im, cache_win_k, cache_win_v, norm_g, ffn1_w_gu, ffn1_w_down, ffn2_w_gu, ffn2_w_down, ssm_lambda_re, ssm_lambda_im, ssm_log_dt, ssm_b_re, ssm_b_im, ssm_c_re, ssm_c_im, ssm_d, ssm_w_glu, ssm_b_glu, kv_norm_g, w_kv, b_kv, attn_w_q, attn_b_q, attn_sinks, attn_w_o, attn_b_o, rel_bias):
    bsz, seq, dm = x_prompt.shape
    dec_b, dec_s, _ = x_sample.shape
    depth = norm_g.shape[0]
    n_a = ssm_lambda_re.shape[0]
    n_groups = ssm_lambda_re.shape[1]
    n_heads = attn_sinks.shape[1]
    n_kv = cache_win_k.shape[2]
    n_past = cache_win_k.shape[1]
    pairs = n_heads // n_kv // 2
    n_k_pad = 2 * WINDOW
    assert seq % ROW_TILE == 0 and seq % WINDOW == 0 and dec_s % 2 == 0
    assert dec_b * dec_s == ROW_TILE and n_past + dec_s <= n_k_pad and dec_b % SUBLANE == 0
    assert n_kv == 2 and dm == n_heads * HEAD_DIM and n_groups == SUBLANE * GROUPS_PER_BLOCK

    bf = lambda w: w.astype(BF16)
    row = lambda v: v.reshape(1, -1)
    s5_w = _discretize(ssm_lambda_re, ssm_lambda_im, ssm_log_dt, ssm_b_re, ssm_b_im, ssm_c_re, ssm_c_im)
    ffn_w = (bf(ffn1_w_gu[0])[None], bf(ffn1_w_down[0])[None])
    wglu, wkv, wq, wo = bf(ssm_w_glu), bf(w_kv)[None], bf(attn_w_q), bf(attn_w_o)

    tab_p = _bucket_table(WINDOW, 2 * WINDOW, WINDOW, WINDOW, 2 * WINDOW)
    tab_p0 = _bucket_table(WINDOW, 2 * WINDOW, WINDOW, WINDOW, 2 * WINDOW, first_key=WINDOW)
    tab_s = _bucket_table(dec_s, n_past + dec_s, n_past, SUBLANE, n_k_pad)
    wq_t = wq.transpose(0, 2, 1)
    bias_p = jnp.stack([_pair_layout_t(_bias_planes(rel_bias, tab_p), n_kv, WINDOW),
                        _pair_layout_t(_bias_planes(rel_bias, tab_p0), n_kv, WINDOW)])
    bias_s = _pair_layout_t(_bias_planes(rel_bias, tab_s), n_kv, dec_s)

    xp = x_prompt.reshape(bsz * seq, dm)
    xs = x_sample.reshape(dec_b * dec_s, dm)
    zero_state = jnp.zeros((bsz, SUBLANE, 2 * HALF), F32)

    ends_p, ends_s = [], []
    kv_p = k_full = v_full = None
    half = n_kv * HEAD_DIM
    for l in range(depth):
        g = [row(norm_g[l, i]) for i in range(norm_g.shape[1])]
        if l == n_a:
            kv_p = _norm_proj(xp, row(kv_norm_g), wkv, row(b_kv), 0)
            kv_s = _norm_proj(xs, row(kv_norm_g), wkv, row(b_kv), 0).reshape(dec_b, dec_s, 2 * half)
            k_full = jnp.concatenate([cache_win_k.reshape(dec_b, n_past, half), kv_s[:, :, :half]], axis=1)
            v_full = jnp.concatenate([cache_win_v.reshape(dec_b, n_past, half), kv_s[:, :, half:]], axis=1)
        pair_io = dict(pair_in=0 < l < n_a, pair_out=l < n_a)
        xs = _ffn(xs, g[0], g[1], *ffn_w, **pair_io)
        xp, *ffn_w = _ffn(xp, g[0], g[1], *ffn_w, **pair_io, cast_next=(ffn2_w_gu, ffn2_w_down, l))
        if l < n_a:
            common = (g[2], g[3], *s5_w, row(ssm_d[l]), wglu, row(ssm_b_glu[l]))
            xp3, fin_p = _s5_layer(xp.reshape(bsz, seq // 2, 2 * dm), *common, zero_state, l,
                                   n_seq=1, n_t=ROW_TILE // 2)
            xs3, fin_s = _s5_layer(xs.reshape(SAMPLE_SPLIT, dec_b * dec_s // (2 * SAMPLE_SPLIT), 2 * dm), *common,
                                   _state_to_rows(state_ssm_re[l], state_ssm_im[l]), l,
                                   n_seq=dec_b // SAMPLE_SPLIT, n_t=dec_s // 2)
            xp, xs = xp3.reshape(bsz * seq // 2, 2 * dm), xs3.reshape(dec_b * dec_s // 2, 2 * dm)
            ends_p.append(_rows_to_state(fin_p, n_groups))
            ends_s.append(_rows_to_state(fin_s, n_groups))
        else:
            bl = l - n_a
            bq_l, bo_l = row(attn_b_q[bl]), row(attn_b_o[bl])
            bq_col = jnp.broadcast_to(attn_b_q[bl][:, None], (dm, WINDOW))
            xp = _prompt_attention(xp, kv_p, bias_p, _sink_cols(attn_sinks[bl], n_kv, WINDOW), g[2], g[3],
                                   wq_t, bq_col, wo, bo_l, bl, blocks_per_seq=seq // WINDOW, n_kv=n_kv)
            q = _norm_proj(xs, g[2], wq, bq_l, bl)
            q = q.reshape(dec_b, dec_s, n_kv, pairs, LANE).transpose(0, 2, 4, 3, 1)
            q = q.reshape(dec_b, n_kv, LANE, pairs * dec_s)
            pad = n_k_pad - (n_past + dec_s)
            o = _sample_attention(q, jnp.pad(k_full, ((0, 0), (0, pad), (0, 0))),
                                  jnp.pad(v_full, ((0, 0), (0, pad), (0, 0))),
                                  bias_s, _sink_cols(attn_sinks[bl], n_kv, dec_s), n_kv=n_kv)
            o = o.reshape(dec_b, n_kv, LANE, pairs, dec_s).transpose(0, 4, 1, 3, 2).reshape(dec_b * dec_s, dm)
            xs = _out_proj_residual(xs, o, g[3], wo, bo_l, bl)
        pair_io = dict(pair_in=l < n_a, pair_out=l < n_a - 1)
        xs = _ffn(xs, g[4], g[5], *ffn_w, **pair_io)
        if l + 1 < depth:
            xp, *ffn_w = _ffn(xp, g[4], g[5], *ffn_w, **pair_io, cast_next=(ffn1_w_gu, ffn1_w_down, l + 1))
        else:
            xp = _ffn(xp, g[4], g[5], *ffn_w, **pair_io)

    kv_p3 = kv_p.reshape(bsz, seq, 2 * half)
    new_k_p = kv_p3[:, -WINDOW:, :half].reshape(bsz, WINDOW, n_kv, HEAD_DIM)
    new_v_p = kv_p3[:, -WINDOW:, half:].reshape(bsz, WINDOW, n_kv, HEAD_DIM)
    new_k_s = k_full[:, -WINDOW:].reshape(dec_b, WINDOW, n_kv, HEAD_DIM)
    new_v_s = v_full[:, -WINDOW:].reshape(dec_b, WINDOW, n_kv, HEAD_DIM)
    return (xp.reshape(bsz, seq, dm), xs.reshape(dec_b, dec_s, dm),
            jnp.stack([e[0] for e in ends_p]), jnp.stack([e[1] for e in ends_p]),
            new_k_p, new_v_p,
            jnp.stack([e[0] for e in ends_s]), jnp.stack([e[1] for e in ends_s]),
            new_k_s, new_v_s)
```

```python
import functools
import math

import numpy as np
import jax
import jax.numpy as jnp
from jax import lax
from jax.experimental import pallas as pl
from jax.experimental.pallas import tpu as pltpu

F32 = jnp.float32
BF16 = jnp.bfloat16

LANE = 128
SUBLANE = 8
VMEM_LIMIT_BYTES = 56 * 1024 * 1024

RMS_EPS = 1e-6
GROUP_SIZE = 16
STATE_DIM = 64
HEAD_DIM = 64
WINDOW = 128
NUM_BUCKETS = 32
MAX_DISTANCE = WINDOW
GROUPS_PER_BLOCK = LANE // GROUP_SIZE
HALF = GROUPS_PER_BLOCK * STATE_DIM
N_SLABS = 2 * HALF // LANE
N_PAIRS = N_SLABS // 2
ROW_TILE = 512
SAMPLE_SPLIT = 2
FFN_CHAINS = 4
ATTN_BLOCKS_PER_STEP = 4
NEG_INF = float("-inf")
LOG2_E = math.log2(math.e)
QUERY_SCALE = LOG2_E / math.sqrt(HEAD_DIM)


def _params(n_axes=1):
    return pltpu.CompilerParams(dimension_semantics=("arbitrary",) * n_axes,
                                vmem_limit_bytes=VMEM_LIMIT_BYTES)


def _const_spec(shape):
    nd = len(shape)
    return pl.BlockSpec(shape, lambda *_: (0,) * nd, pipeline_mode=pl.Buffered(1))


def _layer_spec(shape, layer):
    nd = len(shape)
    return pl.BlockSpec((None,) + tuple(shape[1:]), lambda *_: (layer,) + (0,) * (nd - 1),
                        pipeline_mode=pl.Buffered(1))


def _rms(x, g):
    ms = jnp.mean(x * x, axis=-1, keepdims=True)
    return x * lax.rsqrt(ms + RMS_EPS) * g


def _aligned(row):
    return row if isinstance(row, int) else pl.multiple_of(row, SUBLANE)


def _sigmoid(x):
    return 1.0 / (1.0 + jnp.exp(-x))


def _cast_chunk(rows, n_steps):
    tile = 2 * SUBLANE
    for chunk in range(tile, rows + 1, tile):
        if rows % chunk == 0 and rows // chunk <= n_steps:
            return chunk
    raise ValueError((rows, n_steps))


def _ffn_kernel(*refs, d_ff, pair_in, pair_out, cast_steps, with_kv):
    refs = list(refs)
    x_ref, ga_ref, gb_ref, wgu_ref, wd_ref = refs[:5]
    del refs[:5]
    kvg_ref, wkv_ref, bkv_ref = (refs.pop(0), refs.pop(0), refs.pop(0)) if with_kv else (None,) * 3
    cast_src = (refs.pop(0), refs.pop(0)) if cast_steps else ()
    o_ref = refs.pop(0)
    kv_ref = refs.pop(0) if with_kv else None
    cast_dst = (refs.pop(0), refs.pop(0)) if cast_steps else ()
    slab_ref = refs
    for src, dst, steps in zip(cast_src, cast_dst, cast_steps or ()):
        @pl.when(pl.program_id(0) < steps)
        def _(src=src, dst=dst):
            dst[...] = src[...].astype(BF16)
    d = ga_ref.shape[1]
    if pair_in:
        x = jnp.concatenate([x_ref[:, :d], x_ref[:, d:]], axis=0)
    else:
        x = x_ref[...]
    half = x.shape[0] // 2
    rc = x.shape[0] // FFN_CHAINS
    xc = [x[c * rc:(c + 1) * rc] for c in range(FFN_CHAINS)]
    xn = [_rms(v, ga_ref[...]).astype(BF16) for v in xc]
    gu = [jnp.dot(v, wgu_ref[...], preferred_element_type=F32) for v in xn]
    h = [(v[:, :d_ff] * _sigmoid(v[:, :d_ff]) * v[:, d_ff:]).astype(BF16) for v in gu]
    y = [jnp.dot(v, wd_ref[...], preferred_element_type=F32) for v in h]
    out = jnp.concatenate([v + 0.5 * _rms(w, gb_ref[...]) for v, w in zip(xc, y)], axis=0)
    if pair_in == pair_out:
        o_ref[...] = jnp.concatenate([out[:half], out[half:]], axis=1) if pair_in else out
    else:
        (slab,) = slab_ref
        even = pl.ds(0, half, stride=2)
        odd = pl.ds(1, half, stride=2)
        for j in range(d // LANE):
            lanes = slice(j * LANE, (j + 1) * LANE)
            if pair_out:
                slab[j] = out[:, lanes]
                o_ref[:, lanes] = slab[j, even, :]
                o_ref[:, d + j * LANE:d + (j + 1) * LANE] = slab[j, odd, :]
            else:
                slab[j, even, :] = out[:half, lanes]
                slab[j, odd, :] = out[half:, lanes]
                o_ref[:, lanes] = slab[j]
    if with_kv:
        assert not pair_out
        kv_ref[...] = jnp.dot(_rms(o_ref[...], kvg_ref[...]).astype(BF16), wkv_ref[...],
                              preferred_element_type=F32) + bkv_ref[...]


def _ffn(x, ga, gb, wgu, wd, pair_in=False, pair_out=False, cast_next=None, kv=None):
    d = ga.shape[1]
    t = x.shape[0] * (2 if pair_in else 1)
    d_ff = wd.shape[1]
    tm = min(ROW_TILE, t)
    n_steps = t // tm

    def rows(paired):
        return pl.BlockSpec((tm // 2, 2 * d) if paired else (tm, d), lambda i: (i, 0))

    in_specs = [rows(pair_in), _const_spec((1, d)), _const_spec((1, d)),
                _layer_spec(wgu.shape, 0), _layer_spec(wd.shape, 0)]
    out_specs = [rows(pair_out)]
    out_shape = [jax.ShapeDtypeStruct((t // 2, 2 * d) if pair_out else (t, d), F32)]
    operands = [x, ga, gb, wgu, wd]
    if kv is not None:
        kv_g, kv_w, kv_b = kv
        n_kv_out = kv_w.shape[-1]
        in_specs += [_const_spec((1, d)), _layer_spec(kv_w.shape, 0), _const_spec((1, n_kv_out))]
        out_specs.append(pl.BlockSpec((tm, n_kv_out), lambda i: (i, 0)))
        out_shape.append(jax.ShapeDtypeStruct((t, n_kv_out), F32))
        operands += [kv_g, kv_w, kv_b]
    cast_steps = None
    if cast_next is not None:
        *next_w, layer = cast_next
        cast_steps = []
        for w in next_w:
            chunk = _cast_chunk(w.shape[1], n_steps)
            last = w.shape[1] // chunk - 1
            cast_steps.append(last + 1)
            in_specs.append(pl.BlockSpec((None, chunk, w.shape[2]),
                                         lambda i, last=last: (layer, jnp.minimum(i, last), 0)))
            out_specs.append(pl.BlockSpec((None, chunk, w.shape[2]),
                                          lambda i, last=last: (0, jnp.minimum(i, last), 0)))
            out_shape.append(jax.ShapeDtypeStruct((1,) + w.shape[1:], BF16))
            operands.append(w)
    res = pl.pallas_call(
        functools.partial(_ffn_kernel, d_ff=d_ff, pair_in=pair_in, pair_out=pair_out, cast_steps=cast_steps,
                          with_kv=kv is not None),
        grid=(n_steps,),
        in_specs=in_specs,
        out_specs=out_specs,
        out_shape=out_shape,
        scratch_shapes=[pltpu.VMEM((d // LANE, tm, LANE), F32)] if pair_in != pair_out else [],
        compiler_params=_params(),
        name="ffn",
    )(*operands)
    return res[0] if len(res) == 1 else res


def _proj_kernel(x_ref, g_ref, w_ref, b_ref, o_ref):
    xn = _rms(x_ref[...], g_ref[...]).astype(BF16)
    o_ref[...] = jnp.dot(xn, w_ref[...], preferred_element_type=F32) + b_ref[...]


def _norm_proj(x, g, w, b, layer):
    t, d = x.shape
    n = w.shape[-1]
    tm = min(ROW_TILE, t)
    return pl.pallas_call(
        _proj_kernel,
        grid=(t // tm,),
        in_specs=[pl.BlockSpec((tm, d), lambda i: (i, 0)), _const_spec((1, d)),
                  _layer_spec(w.shape, layer), _const_spec((1, n))],
        out_specs=pl.BlockSpec((tm, n), lambda i: (i, 0)),
        out_shape=jax.ShapeDtypeStruct((t, n), F32),
        compiler_params=_params(),
        name="norm_proj",
    )(x, g, w, b)


def _out_proj_kernel(x_ref, o_ref_in, g_ref, w_ref, b_ref, out_ref):
    a = jnp.dot(o_ref_in[...].astype(BF16), w_ref[...], preferred_element_type=F32) + b_ref[...]
    out_ref[...] = x_ref[...] + _rms(a, g_ref[...])


def _out_proj_residual(x, o, g, w, b, layer):
    t, d = x.shape
    tm = min(ROW_TILE, t)
    row = pl.BlockSpec((tm, d), lambda i: (i, 0))
    return pl.pallas_call(
        _out_proj_kernel,
        grid=(t // tm,),
        in_specs=[row, pl.BlockSpec((tm, o.shape[1]), lambda i: (i, 0)), _const_spec((1, d)),
                  _layer_spec(w.shape, layer), _const_spec((1, d))],
        out_specs=row,
        out_shape=jax.ShapeDtypeStruct((t, d), F32),
        compiler_params=_params(),
        name="out_proj_residual",
    )(x, o, g, w, b)


def _lam_bar(lam_re, lam_im, log_dt):
    dt = jnp.exp(log_dt)
    mag = jnp.exp(lam_re * dt)
    return mag * jnp.cos(lam_im * dt), mag * jnp.sin(lam_im * dt)


def _discretize_kernel(lre_ref, lim_ref, ldt_ref, lre_col_ref, lim_col_ref, ldt_col_ref,
                       bre_ref, bim_ref, cre_ref, cim_ref, win_ref, wout_ref, wskip_ref, lam2_ref):
    a = lre_ref[0, 0]
    b = lim_ref[0, 0]
    lbr, lbi = _lam_bar(a, b, ldt_ref[0, 0])
    den = a * a + b * b
    cr = ((lbr - 1.0) * a + lbi * b) / den
    ci = (lbi * a - (lbr - 1.0) * b) / den
    bre = bre_ref[0, 0]
    bim = bim_ref[0, 0]
    wbr = cr * bre - ci * bim
    wbi = cr * bim + ci * bre
    lwbr = lbr * wbr - lbi * wbi
    lwbi = lbr * wbi + lbi * wbr
    win_ref[0, 0, :LANE, :HALF] = lwbr.astype(BF16)
    win_ref[0, 0, :LANE, HALF:] = lwbi.astype(BF16)
    win_ref[0, 0, LANE:, :HALF] = wbr.astype(BF16)
    win_ref[0, 0, LANE:, HALF:] = wbi.astype(BF16)
    lam2_ref[0, 0, :, :HALF] = lbr * lbr - lbi * lbi
    lam2_ref[0, 0, :, HALF:] = 2.0 * lbr * lbi
    lcr, lci = _lam_bar(lre_col_ref[0, 0], lim_col_ref[0, 0], ldt_col_ref[0, 0])
    cre = cre_ref[0, 0]
    cim = cim_ref[0, 0]
    c1r = cre * lcr - cim * lci
    c1i = cre * lci + cim * lcr
    wout_ref[0, 0, :HALF, :LANE] = c1r.astype(BF16)
    wout_ref[0, 0, HALF:, :LANE] = (-c1i).astype(BF16)
    wout_ref[0, 0, :HALF, LANE:] = (c1r * lcr - c1i * lci).astype(BF16)
    wout_ref[0, 0, HALF:, LANE:] = (-(c1r * lci + c1i * lcr)).astype(BF16)
    def through(br, bi):
        return (jnp.dot(br, cre, precision=lax.Precision.HIGHEST, preferred_element_type=F32)
                - jnp.dot(bi, cim, precision=lax.Precision.HIGHEST, preferred_element_type=F32)).astype(BF16)

    cb = through(wbr, wbi)
    wskip_ref[0, 0, :LANE, :LANE] = cb
    wskip_ref[0, 0, :LANE, LANE:] = through(lwbr, lwbi)
    wskip_ref[0, 0, LANE:, :LANE] = jnp.zeros((LANE, LANE), BF16)
    wskip_ref[0, 0, LANE:, LANE:] = cb


def _block_diag_in(b):
    nl, g, p, h = b.shape
    nb = g // GROUPS_PER_BLOCK
    bt = b.transpose(0, 1, 3, 2).reshape(nl, nb, GROUPS_PER_BLOCK, h, p)
    eye = jnp.eye(GROUPS_PER_BLOCK, dtype=b.dtype)
    out = bt[:, :, :, :, None, :] * eye[None, None, :, None, :, None]
    return out.reshape(nl, nb, GROUPS_PER_BLOCK * h, GROUPS_PER_BLOCK * p)


def _block_diag_out(c):
    nl, g, h, p = c.shape
    nb = g // GROUPS_PER_BLOCK
    ct = c.transpose(0, 1, 3, 2).reshape(nl, nb, GROUPS_PER_BLOCK, p, h)
    eye = jnp.eye(GROUPS_PER_BLOCK, dtype=c.dtype)
    out = ct[:, :, :, :, None, :] * eye[None, None, :, None, :, None]
    return out.reshape(nl, nb, GROUPS_PER_BLOCK * p, GROUPS_PER_BLOCK * h)


def _discretize(lam_re, lam_im, log_dt, b_re, b_im, c_re, c_im):
    nl, g, p = lam_re.shape
    nb = g // GROUPS_PER_BLOCK

    def rows(v):
        return v.reshape(nl, nb, 1, HALF)

    def cols(v):
        return v.reshape(nl, nb, HALF, 1)

    def spec(r, c):
        return pl.BlockSpec((1, 1, r, c), lambda l, k: (l, k, 0, 0))

    win, wout, wskip, lam2 = pl.pallas_call(
        _discretize_kernel,
        grid=(nl, nb),
        in_specs=[spec(1, HALF)] * 3 + [spec(HALF, 1)] * 3 + [spec(LANE, HALF)] * 2 + [spec(HALF, LANE)] * 2,
        out_specs=[spec(2 * LANE, 2 * HALF), spec(2 * HALF, 2 * LANE), spec(2 * LANE, 2 * LANE), spec(1, 2 * HALF)],
        out_shape=[jax.ShapeDtypeStruct((nl, nb, 2 * LANE, 2 * HALF), BF16),
                   jax.ShapeDtypeStruct((nl, nb, 2 * HALF, 2 * LANE), BF16),
                   jax.ShapeDtypeStruct((nl, nb, 2 * LANE, 2 * LANE), BF16),
                   jax.ShapeDtypeStruct((nl, nb, 1, 2 * HALF), F32)],
        compiler_params=_params(2),
        name="s5_discretize",
    )(rows(lam_re), rows(lam_im), rows(log_dt), cols(lam_re), cols(lam_im), cols(log_dt),
      _block_diag_in(b_re), _block_diag_in(b_im), _block_diag_out(c_re), _block_diag_out(c_im))
    return win, wout, wskip, lam2.reshape(nl, nb, 2 * HALF)


def _slab_pitch(rows):
    assert rows % SUBLANE == 0
    return rows + SUBLANE // 2


def _split_state(st):
    return (tuple(st[:, p * LANE:(p + 1) * LANE] for p in range(N_PAIRS)),
            tuple(st[:, HALF + p * LANE:HALF + (p + 1) * LANE] for p in range(N_PAIRS)))


def _s5_mixer_kernel(x_ref, g2_ref, g3_ref, win_ref, wout_ref, wskip_ref, lam_ref, d_ref, wglu_ref, bglu_ref,
                     init_ref, o_ref, final_ref, state_ref, y_ref, s_ref,
                     *, n_blocks, n_seq, n_t, d_model):
    tile = pl.program_id(1)
    rows = n_seq * n_t
    pitch = _slab_pitch(rows)

    @pl.when(tile == 0)
    def _():
        state_ref[...] = init_ref[...]

    lre, lim = _split_state(lam_ref[...])
    xe = x_ref[0, :, :d_model]
    xo = x_ref[0, :, d_model:]
    ue = _rms(xe, g2_ref[...])
    uo = _rms(xo, g2_ref[...])
    ueb = ue.astype(BF16)
    uob = uo.astype(BF16)


    u_pair = [jnp.concatenate([ueb[:, k * LANE:(k + 1) * LANE], uob[:, k * LANE:(k + 1) * LANE]], axis=1)
              for k in range(n_blocks)]
    for k in range(n_blocks):
        v = jnp.dot(u_pair[k], win_ref[k], preferred_element_type=F32)
        for j in range(N_SLABS):
            s_ref[j, pl.ds(k * pitch, rows), :] = v[:, j * LANE:(j + 1) * LANE]

    def seq_body(seq, carry):
        xre, xim = _split_state(state_ref[seq])

        def step(t, st):
            xre, xim = st
            pair = pl.ds(seq * n_t + t, SUBLANE, stride=pitch)
            nre, nim = [], []
            for p in range(N_PAIRS):
                vre = s_ref[p, pair, :]
                vim = s_ref[N_PAIRS + p, pair, :]
                s_ref[p, pair, :] = xre[p]
                s_ref[N_PAIRS + p, pair, :] = xim[p]
                nre.append(lre[p] * xre[p] - lim[p] * xim[p] + vre)
                nim.append(lre[p] * xim[p] + lim[p] * xre[p] + vim)
            return tuple(nre), tuple(nim)

        xre, xim = lax.fori_loop(0, n_t, step, (xre, xim), unroll=min(n_t, 8))
        state_ref[seq] = jnp.concatenate(list(xre) + list(xim), axis=1)
        return carry

    if n_seq == 1:
        seq_body(0, 0)
    else:
        lax.fori_loop(0, n_seq, seq_body, 0)

    for k in range(n_blocks):
        xb = jnp.concatenate([s_ref[j, pl.ds(k * pitch, rows), :] for j in range(N_SLABS)], axis=1)
        yk = (jnp.dot(xb.astype(BF16), wout_ref[k], preferred_element_type=F32)
              + jnp.dot(u_pair[k], wskip_ref[k], preferred_element_type=F32))
        y_ref[:, k * LANE:(k + 1) * LANE] = yk[:, :LANE]
        y_ref[:, d_model + k * LANE:d_model + (k + 1) * LANE] = yk[:, LANE:]
    y = jnp.concatenate([y_ref[:, :d_model] + d_ref[...] * ue, y_ref[:, d_model:] + d_ref[...] * uo], axis=0)
    rc = y.shape[0] // FFN_CHAINS
    gy = [jax.nn.gelu(y[c * rc:(c + 1) * rc]).astype(BF16) for c in range(FFN_CHAINS)]
    h = [jnp.dot(v, wglu_ref[...], preferred_element_type=F32) + bglu_ref[...] for v in gy]
    r = jnp.concatenate([_rms(v[:, :d_model] * _sigmoid(v[:, d_model:]), g3_ref[...]) for v in h], axis=0)
    o_ref[0] = jnp.concatenate([xe + r[:rows], xo + r[rows:]], axis=1)

    @pl.when(tile == pl.num_programs(1) - 1)
    def _():
        final_ref[...] = state_ref[...]


def _s5_layer(x, g2, g3, win, wout, wskip, lam2, d, wglu, bglu, init, layer, *, n_seq, n_t):
    nb_rows, s2, dm2 = x.shape
    dm = dm2 // 2
    n_blocks = win.shape[1]
    assert n_blocks == SUBLANE
    rows = n_seq * n_t
    n_tiles = s2 // rows
    assert n_seq == 1 or n_tiles == 1
    row = pl.BlockSpec((1, rows, dm2), lambda b, i: (b, i, 0))
    vec = _const_spec((1, dm))
    state = pl.BlockSpec((n_seq, SUBLANE, 2 * HALF), lambda b, i: (b, 0, 0))
    slab = pltpu.VMEM((N_SLABS, n_blocks * _slab_pitch(rows), LANE), F32)
    return pl.pallas_call(
        functools.partial(_s5_mixer_kernel, n_blocks=n_blocks, n_seq=n_seq, n_t=n_t, d_model=dm),
        grid=(nb_rows, n_tiles),
        in_specs=[row, vec, vec, _layer_spec(win.shape, layer), _layer_spec(wout.shape, layer),
                  _layer_spec(wskip.shape, layer), _layer_spec(lam2.shape, layer), vec,
                  _layer_spec(wglu.shape, layer), _const_spec((1, 2 * dm)),
                  pl.BlockSpec(state.block_shape, state.index_map, pipeline_mode=pl.Buffered(1))],
        out_specs=[row, state],
        out_shape=[jax.ShapeDtypeStruct(x.shape, F32),
                   jax.ShapeDtypeStruct(init.shape, F32)],
        scratch_shapes=[pltpu.VMEM((n_seq, SUBLANE, 2 * HALF), F32), pltpu.VMEM((rows, dm2), F32), slab],
        compiler_params=_params(2),
        name="s5_mixer",
    )(x, g2, g3, win, wout, wskip, lam2, d, wglu, bglu, init)


def _state_to_rows(re, im):
    b = re.shape[0]
    return jnp.concatenate([re.reshape(b, -1, HALF), im.reshape(b, -1, HALF)], axis=-1)


def _rows_to_state(s, n_groups):
    b = s.shape[0]
    return (s[:, :, :HALF].reshape(b, n_groups, STATE_DIM), s[:, :, HALF:].reshape(b, n_groups, STATE_DIM))


def _t5_bucket_np(dist):
    n = np.maximum(dist, 0)
    max_exact = NUM_BUCKETS // 2
    nf = np.maximum(n, 1).astype(np.float32)
    large = max_exact + (np.log(nf / np.float32(max_exact)) / np.float32(math.log(MAX_DISTANCE / max_exact))
                         * np.float32(NUM_BUCKETS - max_exact)).astype(np.int32)
    large = np.minimum(large, NUM_BUCKETS - 1)
    return np.where(n < max_exact, n, large).astype(np.int32)


def _bucket_table(n_q, n_k, q_offset, n_q_pad, n_k_pad, first_key=0):
    dist = (np.arange(n_q)[:, None] + q_offset) - np.arange(n_k)[None, :]
    valid = (dist >= 0) & (dist < WINDOW) & (np.arange(n_k)[None, :] >= first_key)
    table = np.full((n_q_pad, n_k_pad), -1, np.int32)
    table[:n_q, :n_k] = np.where(valid, _t5_bucket_np(dist), -1)
    return table


def _bias_kernel(rel_ref, idx_ref, o_ref):
    h = pl.program_id(0)
    idx = idx_ref[...]
    acc = jnp.full(idx.shape, NEG_INF, F32)
    for b in range(NUM_BUCKETS):
        acc = jnp.where(idx == b, rel_ref[b, h] * LOG2_E, acc)
    o_ref[0] = acc


def _bias_planes(rel_bias, table):
    n_heads = rel_bias.shape[1]
    r, c = table.shape
    return pl.pallas_call(
        _bias_kernel,
        grid=(n_heads,),
        in_specs=[pl.BlockSpec(memory_space=pltpu.SMEM), pl.BlockSpec((r, c), lambda h: (0, 0))],
        out_specs=pl.BlockSpec((1, r, c), lambda h: (h, 0, 0)),
        out_shape=jax.ShapeDtypeStruct((n_heads, r, c), F32),
        compiler_params=_params(),
        name="attn_bias",
    )(rel_bias, jnp.asarray(table))


def _kv_head_rows(kv, g):
    lane = lax.broadcasted_iota(jnp.int32, kv.shape, 1)
    swapped = pltpu.roll(kv, HEAD_DIM, axis=1)
    even, odd = (kv, swapped) if g == 0 else (swapped, kv)
    return jnp.where(lane < HEAD_DIM, even, 0.0), jnp.where(lane >= HEAD_DIM, odd, 0.0)


def _pair_layout_t(planes, n_kv, n_q):
    h, _, n_k = planes.shape
    pairs = h // n_kv // 2
    p = planes[:, :n_q].reshape(n_kv, pairs, 2, n_q, n_k).transpose(0, 2, 4, 1, 3)
    return p.reshape(n_kv, 2 * n_k, pairs * n_q)


def _sink_cols(sinks, n_kv, n_q):
    h = sinks.shape[0]
    pairs = h // n_kv // 2
    s = sinks.reshape(n_kv, pairs, 2).transpose(0, 2, 1)[..., None]
    return jnp.broadcast_to(s, (n_kv, 2, pairs, n_q)).reshape(n_kv, 2, pairs * n_q)


def _attend_t(chains):
    scores = []
    for q_t, k_even, k_odd, _, _, bias, _ in chains:
        ke = jnp.concatenate([k_even, k_odd], axis=0).astype(BF16)
        scores.append(jnp.dot(ke, q_t, preferred_element_type=F32) + bias)
    weights = []
    for s, chain in zip(scores, chains):
        sink = chain[6]
        n_k = s.shape[0] // 2
        probs, inv = [], []
        for par in range(2):
            sp = s[par * n_k:(par + 1) * n_k]
            sk = sink[par:par + 1] * LOG2_E
            mx = jnp.maximum(jnp.max(sp, axis=0, keepdims=True), sk)
            p = jnp.exp2(sp - mx)
            den = jnp.sum(p, axis=0, keepdims=True) + jnp.exp2(sk - mx)
            probs.append(p.astype(BF16))
            inv.append(1.0 / den)
        weights.append((jnp.concatenate(probs, axis=0), inv))
    outs = []
    for (p_t, inv), chain in zip(weights, chains):
        ve = jnp.concatenate([chain[3], chain[4]], axis=0).astype(BF16)
        o = lax.dot_general(ve, p_t, (((0,), (0,)), ((), ())), preferred_element_type=F32)
        row = lax.broadcasted_iota(jnp.int32, o.shape, 0)
        outs.append(o * jnp.where(row < HEAD_DIM, inv[0], inv[1]))
    return outs


def _prompt_attn_kernel(x_ref, kvc_ref, kvp_ref, bias_ref, sink_ref, g2_ref, g3_ref, wq_ref, bq_ref,
                        wo_ref, bo_ref, o_ref, *, n_kv, n_sub, steps_per_seq, scale):
    blk = WINDOW
    x = x_ref[...]
    u = _rms(x, g2_ref[...]).astype(BF16)
    q_t = lax.dot_general(wq_ref[...], u, (((1,), (1,)), ((), ())), preferred_element_type=F32)
    q_t = ((q_t + jnp.concatenate([bq_ref[...]] * n_sub, axis=1)) * scale).astype(BF16)
    kv = jnp.concatenate([kvp_ref[...], kvc_ref[...]], axis=0)
    pairs = q_t.shape[0] // LANE // n_kv
    first = jnp.where(pl.program_id(0) % steps_per_seq == 0, 1, 0)
    k_rows = [_kv_head_rows(kv[:, :n_kv * HEAD_DIM], g) for g in range(n_kv)]
    v_rows = [_kv_head_rows(kv[:, n_kv * HEAD_DIM:], g) for g in range(n_kv)]
    chains = []
    for s in range(n_sub):
        keys = slice(s * blk, (s + 2) * blk)
        for g in range(n_kv):
            qp = jnp.concatenate([q_t[(g * pairs + i) * LANE:(g * pairs + i + 1) * LANE, s * blk:(s + 1) * blk]
                                  for i in range(pairs)], axis=1)
            bias = bias_ref[first, g] if s == 0 else bias_ref[0, g]
            chains.append((qp, k_rows[g][0][keys], k_rows[g][1][keys], v_rows[g][0][keys],
                           v_rows[g][1][keys], bias, sink_ref[g]))
    outs = _attend_t(chains)
    o_t = jnp.concatenate(
        [jnp.concatenate([outs[s * n_kv + g][:, i * blk:(i + 1) * blk] for s in range(n_sub)], axis=1)
         for g in range(n_kv) for i in range(pairs)], axis=0).astype(BF16)
    a = lax.dot_general(o_t, wo_ref[...], (((0,), (0,)), ((), ())), preferred_element_type=F32) + bo_ref[...]
    o_ref[...] = x + _rms(a, g3_ref[...])


def _prompt_attention(x, kv, bias, sink, g2, g3, wq_t, bq_col, wo, bo, layer, *, blocks_per_seq, n_kv):
    t, d = x.shape
    blk = WINDOW
    n_sub = ATTN_BLOCKS_PER_STEP
    assert n_kv * HEAD_DIM == LANE and blocks_per_seq % n_sub == 0
    row = pl.BlockSpec((n_sub * blk, d), lambda i: (i, 0))
    kvw = kv.shape[1]
    return pl.pallas_call(
        functools.partial(_prompt_attn_kernel, n_kv=n_kv, n_sub=n_sub,
                          steps_per_seq=blocks_per_seq // n_sub, scale=QUERY_SCALE),
        grid=(t // (n_sub * blk),),
        in_specs=[row,
                  pl.BlockSpec((n_sub * blk, kvw), lambda i: (i, 0)),
                  pl.BlockSpec((blk, kvw), lambda i: (jnp.maximum(i * n_sub - 1, 0), 0)),
                  _const_spec(bias.shape),
                  _const_spec(sink.shape), _const_spec((1, d)), _const_spec((1, d)),
                  _layer_spec(wq_t.shape, layer), _const_spec(bq_col.shape),
                  _layer_spec(wo.shape, layer), _const_spec((1, d))],
        out_specs=row,
        out_shape=jax.ShapeDtypeStruct((t, d), F32),
        compiler_params=_params(),
        name="prompt_attention",
    )(x, kv, kv, bias, sink, g2, g3, wq_t, bq_col, wo, bo)


def _sample_attn_kernel(q_ref, k_ref, v_ref, bias_ref, sink_ref, o_ref, *, n_kv, scale):
    chains = []
    for b in range(q_ref.shape[0]):
        k = k_ref[b]
        v = v_ref[b]
        for g in range(n_kv):
            chains.append(((q_ref[b, g] * scale).astype(BF16), *_kv_head_rows(k, g), *_kv_head_rows(v, g),
                           bias_ref[g], sink_ref[g]))
    outs = _attend_t(chains)
    for b in range(q_ref.shape[0]):
        for g in range(n_kv):
            o_ref[b, g] = outs[b * n_kv + g]


def _sample_attention(q_t, k_full, v_full, bias, sink, *, n_kv):
    nb, _, _, n_l = q_t.shape
    n_k = k_full.shape[1]
    bb = SUBLANE
    return pl.pallas_call(
        functools.partial(_sample_attn_kernel, n_kv=n_kv, scale=QUERY_SCALE),
        grid=(nb // bb,),
        in_specs=[pl.BlockSpec((bb, n_kv, LANE, n_l), lambda i: (i, 0, 0, 0)),
                  pl.BlockSpec((bb, n_k, LANE), lambda i: (i, 0, 0)),
                  pl.BlockSpec((bb, n_k, LANE), lambda i: (i, 0, 0)),
                  _const_spec(bias.shape), _const_spec(sink.shape)],
        out_specs=pl.BlockSpec((bb, n_kv, LANE, n_l), lambda i: (i, 0, 0, 0)),
        out_shape=jax.ShapeDtypeStruct(q_t.shape, F32),
        compiler_params=_params(),
        name="sample_attention",
    )(q_t, k_full, v_full, bias, sink)


def kernel(x_prompt, x_sample, state_ssm_re, state_ssm_im, cache_win_k, cache_win_v, norm_g, ffn1_w_gu, ffn1_w_down, ffn2_w_gu, ffn2_w_down, ssm_lambda_re, ssm_lambda_im, ssm_log_dt, ssm_b_re, ssm_b_im, ssm_c_re, ssm_c_im, ssm_d, ssm_w_glu, ssm_b_glu, kv_norm_g, w_kv, b_kv, attn_w_q, attn_b_q, attn_sinks, attn_w_o, attn_b_o, rel_bias):
    bsz, seq, dm = x_prompt.shape
    dec_b, dec_s, _ = x_sample.shape
    depth = norm_g.shape[0]
    n_a = ssm_lambda_re.shape[0]
    n_groups = ssm_lambda_re.shape[1]
    n_heads = attn_sinks.shape[1]
    n_kv = cache_win_k.shape[2]
    n_past = cache_win_k.shape[1]
    pairs = n_heads // n_kv // 2
    n_k_pad = 2 * WINDOW
    assert seq % ROW_TILE == 0 and seq % WINDOW == 0 and dec_s % 2 == 0
    assert dec_b * dec_s == ROW_TILE and n_past + dec_s <= n_k_pad and dec_b % SUBLANE == 0
    assert n_kv == 2 and dm == n_heads * HEAD_DIM and n_groups == SUBLANE * GROUPS_PER_BLOCK

    bf = lambda w: w.astype(BF16)
    row = lambda v: v.reshape(1, -1)
    s5_w = _discretize(ssm_lambda_re, ssm_lambda_im, ssm_log_dt, ssm_b_re, ssm_b_im, ssm_c_re, ssm_c_im)
    ffn_w = (bf(ffn1_w_gu[0])[None], bf(ffn1_w_down[0])[None])
    wglu, wkv, wq, wo = bf(ssm_w_glu), bf(w_kv)[None], bf(attn_w_q), bf(attn_w_o)

    tab_p = _bucket_table(WINDOW, 2 * WINDOW, WINDOW, WINDOW, 2 * WINDOW)
    tab_p0 = _bucket_table(WINDOW, 2 * WINDOW, WINDOW, WINDOW, 2 * WINDOW, first_key=WINDOW)
    tab_s = _bucket_table(dec_s, n_past + dec_s, n_past, SUBLANE, n_k_pad)
    wq_t = wq.transpose(0, 2, 1)
    bias_p = jnp.stack([_pair_layout_t(_bias_planes(rel_bias, tab_p), n_kv, WINDOW),
                        _pair_layout_t(_bias_planes(rel_bias, tab_p0), n_kv, WINDOW)])
    bias_s = _pair_layout_t(_bias_planes(rel_bias, tab_s), n_kv, dec_s)

    xp = x_prompt.reshape(bsz * seq, dm)
    xs = x_sample.reshape(dec_b * dec_s, dm)
    zero_state = jnp.zeros((bsz, SUBLANE, 2 * HALF), F32)

    ends_p, ends_s = [], []
    kv_p = k_full = v_full = None
    half = n_kv * HEAD_DIM
    for l in range(depth):
        g = [row(norm_g[l, i]) for i in range(norm_g.shape[1])]
        pair_io = dict(pair_in=0 < l < n_a, pair_out=l < n_a)
        xs = _ffn(xs, g[0], g[1], *ffn_w, **pair_io)
        xp, *ffn_w = _ffn(xp, g[0], g[1], *ffn_w, **pair_io, cast_next=(ffn2_w_gu, ffn2_w_down, l))
        if l < n_a:
            common = (g[2], g[3], *s5_w, row(ssm_d[l]), wglu, row(ssm_b_glu[l]))
            xp3, fin_p = _s5_layer(xp.reshape(bsz, seq // 2, 2 * dm), *common, zero_state, l,
                                   n_seq=1, n_t=ROW_TILE // 2)
            xs3, fin_s = _s5_layer(xs.reshape(SAMPLE_SPLIT, dec_b * dec_s // (2 * SAMPLE_SPLIT), 2 * dm), *common,
                                   _state_to_rows(state_ssm_re[l], state_ssm_im[l]), l,
                                   n_seq=dec_b // SAMPLE_SPLIT, n_t=dec_s // 2)
            xp, xs = xp3.reshape(bsz * seq // 2, 2 * dm), xs3.reshape(dec_b * dec_s // 2, 2 * dm)
            ends_p.append(_rows_to_state(fin_p, n_groups))
            ends_s.append(_rows_to_state(fin_s, n_groups))
        else:
            bl = l - n_a
            bq_l, bo_l = row(attn_b_q[bl]), row(attn_b_o[bl])
            bq_col = jnp.broadcast_to(attn_b_q[bl][:, None], (dm, WINDOW))
            xp = _prompt_attention(xp, kv_p, bias_p, _sink_cols(attn_sinks[bl], n_kv, WINDOW), g[2], g[3],
                                   wq_t, bq_col, wo, bo_l, bl, blocks_per_seq=seq // WINDOW, n_kv=n_kv)
            q = _norm_proj(xs, g[2], wq, bq_l, bl)
            q = q.reshape(dec_b, dec_s, n_kv, pairs, LANE).transpose(0, 2, 4, 3, 1)
            q = q.reshape(dec_b, n_kv, LANE, pairs * dec_s)
            pad = n_k_pad - (n_past + dec_s)
            o = _sample_attention(q, jnp.pad(k_full, ((0, 0), (0, pad), (0, 0))),
                                  jnp.pad(v_full, ((0, 0), (0, pad), (0, 0))),
                                  bias_s, _sink_cols(attn_sinks[bl], n_kv, dec_s), n_kv=n_kv)
            o = o.reshape(dec_b, n_kv, LANE, pairs, dec_s).transpose(0, 4, 1, 3, 2).reshape(dec_b * dec_s, dm)
            xs = _out_proj_residual(xs, o, g[3], wo, bo_l, bl)
        pair_io = dict(pair_in=l < n_a, pair_out=l < n_a - 1)
        cast_next = (ffn1_w_gu, ffn1_w_down, l + 1) if l + 1 < depth else None
        kv = (row(kv_norm_g), wkv, row(b_kv)) if l == n_a - 1 else None
        res_s = _ffn(xs, g[4], g[5], *ffn_w, **pair_io, kv=kv)
        res_p = _ffn(xp, g[4], g[5], *ffn_w, **pair_io, kv=kv, cast_next=cast_next)
        if cast_next is not None:
            *res_p, ffn_w[0], ffn_w[1] = res_p
            res_p = res_p[0] if len(res_p) == 1 else res_p
        if kv is not None:
            (xp, kv_p), (xs, kv_s) = res_p, res_s
            kv_s = kv_s.reshape(dec_b, dec_s, 2 * half)
            k_full = jnp.concatenate([cache_win_k.reshape(dec_b, n_past, half), kv_s[:, :, :half]], axis=1)
            v_full = jnp.concatenate([cache_win_v.reshape(dec_b, n_past, half), kv_s[:, :, half:]], axis=1)
        else:
            xp, xs = res_p, res_s

    kv_p3 = kv_p.reshape(bsz, seq, 2 * half)
    new_k_p = kv_p3[:, -WINDOW:, :half].reshape(bsz, WINDOW, n_kv, HEAD_DIM)
    new_v_p = kv_p3[:, -WINDOW:, half:].reshape(bsz, WINDOW, n_kv, HEAD_DIM)
    new_k_s = k_full[:, -WINDOW:].reshape(dec_b, WINDOW, n_kv, HEAD_DIM)
    new_v_s = v_full[:, -WINDOW:].reshape(dec_b, WINDOW, n_kv, HEAD_DIM)
    return (xp.reshape(bsz, seq, dm), xs.reshape(dec_b, dec_s, dm),
            jnp.stack([e[0] for e in ends_p]), jnp.stack([e[1] for e in ends_p]),
            new_k_p, new_v_p,
            jnp.stack([e[0] for e in ends_s]), jnp.stack([e[1] for e in ends_s]),
            new_k_s, new_v_s)
```

```python
import functools
import math

import numpy as np
import jax
import jax.numpy as jnp
from jax import lax
from jax.experimental import pallas as pl
from jax.experimental.pallas import tpu as pltpu

F32 = jnp.float32
BF16 = jnp.bfloat16

LANE = 128
SUBLANE = 8
VMEM_LIMIT_BYTES = 56 * 1024 * 1024

RMS_EPS = 1e-6
GROUP_SIZE = 16
STATE_DIM = 64
HEAD_DIM = 64
WINDOW = 128
NUM_BUCKETS = 32
MAX_DISTANCE = WINDOW
GROUPS_PER_BLOCK = LANE // GROUP_SIZE
HALF = GROUPS_PER_BLOCK * STATE_DIM
N_SLABS = 2 * HALF // LANE
N_PAIRS = N_SLABS // 2
ROW_TILE = 512
SAMPLE_SPLIT = 2
FFN_CHAINS = 4
ATTN_BLOCKS_PER_STEP = 4
NEG_INF = float("-inf")
LOG2_E = math.log2(math.e)
QUERY_SCALE = LOG2_E / math.sqrt(HEAD_DIM)


def _params(n_axes=1):
    return pltpu.CompilerParams(dimension_semantics=("arbitrary",) * n_axes,
                                vmem_limit_bytes=VMEM_LIMIT_BYTES)


def _const_spec(shape):
    nd = len(shape)
    return pl.BlockSpec(shape, lambda *_: (0,) * nd, pipeline_mode=pl.Buffered(1))


def _layer_spec(shape, layer):
    nd = len(shape)
    return pl.BlockSpec((None,) + tuple(shape[1:]), lambda *_: (layer,) + (0,) * (nd - 1),
                        pipeline_mode=pl.Buffered(1))


def _rms(x, g):
    ms = jnp.mean(x * x, axis=-1, keepdims=True)
    return x * lax.rsqrt(ms + RMS_EPS) * g


def _aligned(row):
    return row if isinstance(row, int) else pl.multiple_of(row, SUBLANE)


def _sigmoid(x):
    return 1.0 / (1.0 + jnp.exp(-x))


def _cast_chunk(rows, n_steps):
    tile = 2 * SUBLANE
    for chunk in range(tile, rows + 1, tile):
        if rows % chunk == 0 and rows // chunk <= n_steps:
            return chunk
    raise ValueError((rows, n_steps))


def _ffn_kernel(*refs, d_ff, pair_in, pair_out, cast_steps, with_kv):
    refs = list(refs)
    x_ref, xr_ref, ga_ref, gb_ref, wgu_ref, wd_ref = refs[:6]
    del refs[:6]
    kvg_ref, wkv_ref, bkv_ref = (refs.pop(0), refs.pop(0), refs.pop(0)) if with_kv else (None,) * 3
    cast_src = (refs.pop(0), refs.pop(0)) if cast_steps else ()
    o_ref, or_ref = refs.pop(0), refs.pop(0)
    kv_ref, kvr_ref = (refs.pop(0), refs.pop(0)) if with_kv else (None, None)
    cast_dst = (refs.pop(0), refs.pop(0)) if cast_steps else ()
    slab_ref = refs
    for src, dst, steps in zip(cast_src, cast_dst, cast_steps or ()):
        @pl.when(pl.program_id(0) < steps)
        def _(src=src, dst=dst):
            dst[...] = src[...].astype(BF16)
    args = (ga_ref, gb_ref, wgu_ref, wd_ref, kvg_ref, wkv_ref, bkv_ref, slab_ref)
    _ffn_tile(x_ref, o_ref, kv_ref, *args, d_ff=d_ff, pair_in=pair_in, pair_out=pair_out)

    @pl.when(pl.program_id(0) == pl.num_programs(0) - 1)
    def _():
        _ffn_tile(xr_ref, or_ref, kvr_ref, *args, d_ff=d_ff, pair_in=pair_in, pair_out=pair_out)


def _ffn_tile(x_ref, o_ref, kv_ref, ga_ref, gb_ref, wgu_ref, wd_ref, kvg_ref, wkv_ref, bkv_ref, slab_ref,
              *, d_ff, pair_in, pair_out):
    d = ga_ref.shape[1]
    if pair_in:
        x = jnp.concatenate([x_ref[:, :d], x_ref[:, d:]], axis=0)
    else:
        x = x_ref[...]
    half = x.shape[0] // 2
    rc = x.shape[0] // FFN_CHAINS
    xc = [x[c * rc:(c + 1) * rc] for c in range(FFN_CHAINS)]
    xn = [_rms(v, ga_ref[...]).astype(BF16) for v in xc]
    gu = [jnp.dot(v, wgu_ref[...], preferred_element_type=F32) for v in xn]
    h = [(v[:, :d_ff] * _sigmoid(v[:, :d_ff]) * v[:, d_ff:]).astype(BF16) for v in gu]
    y = [jnp.dot(v, wd_ref[...], preferred_element_type=F32) for v in h]
    out = jnp.concatenate([v + 0.5 * _rms(w, gb_ref[...]) for v, w in zip(xc, y)], axis=0)
    if pair_in == pair_out:
        o_ref[...] = jnp.concatenate([out[:half], out[half:]], axis=1) if pair_in else out
    else:
        (slab,) = slab_ref
        even = pl.ds(0, half, stride=2)
        odd = pl.ds(1, half, stride=2)
        for j in range(d // LANE):
            lanes = slice(j * LANE, (j + 1) * LANE)
            if pair_out:
                slab[j] = out[:, lanes]
                o_ref[:, lanes] = slab[j, even, :]
                o_ref[:, d + j * LANE:d + (j + 1) * LANE] = slab[j, odd, :]
            else:
                slab[j, even, :] = out[:half, lanes]
                slab[j, odd, :] = out[half:, lanes]
                o_ref[:, lanes] = slab[j]
    if kv_ref is not None:
        assert not pair_out
        kv_ref[...] = jnp.dot(_rms(o_ref[...], kvg_ref[...]).astype(BF16), wkv_ref[...],
                              preferred_element_type=F32) + bkv_ref[...]


def _ffn(x, x_rider, ga, gb, wgu, wd, pair_in=False, pair_out=False, cast_next=None, kv=None):
    d = ga.shape[1]
    t = x.shape[0] * (2 if pair_in else 1)
    d_ff = wd.shape[1]
    tm = ROW_TILE
    n_steps = t // tm
    assert t % tm == 0 and x_rider.shape[0] * (2 if pair_in else 1) == tm

    def tile(paired):
        return (tm // 2, 2 * d) if paired else (tm, d)

    def rows(paired):
        return pl.BlockSpec(tile(paired), lambda i: (i, 0))

    def rider(shape):
        return pl.BlockSpec(shape, lambda i: (0, 0))

    in_specs = [rows(pair_in), _const_spec(tile(pair_in)), _const_spec((1, d)), _const_spec((1, d)),
                _layer_spec(wgu.shape, 0), _layer_spec(wd.shape, 0)]
    out_specs = [rows(pair_out), rider(tile(pair_out))]
    out_shape = [jax.ShapeDtypeStruct((t // 2, 2 * d) if pair_out else (t, d), F32),
                 jax.ShapeDtypeStruct(tile(pair_out), F32)]
    operands = [x, x_rider, ga, gb, wgu, wd]
    if kv is not None:
        kv_g, kv_w, kv_b = kv
        n_kv_out = kv_w.shape[-1]
        in_specs += [_const_spec((1, d)), _layer_spec(kv_w.shape, 0), _const_spec((1, n_kv_out))]
        out_specs += [pl.BlockSpec((tm, n_kv_out), lambda i: (i, 0)), rider((tm, n_kv_out))]
        out_shape += [jax.ShapeDtypeStruct((t, n_kv_out), F32), jax.ShapeDtypeStruct((tm, n_kv_out), F32)]
        operands += [kv_g, kv_w, kv_b]
    cast_steps = None
    if cast_next is not None:
        *next_w, layer = cast_next
        cast_steps = []
        for w in next_w:
            chunk = _cast_chunk(w.shape[1], n_steps)
            last = w.shape[1] // chunk - 1
            cast_steps.append(last + 1)
            in_specs.append(pl.BlockSpec((None, chunk, w.shape[2]),
                                         lambda i, last=last: (layer, jnp.minimum(i, last), 0)))
            out_specs.append(pl.BlockSpec((None, chunk, w.shape[2]),
                                          lambda i, last=last: (0, jnp.minimum(i, last), 0)))
            out_shape.append(jax.ShapeDtypeStruct((1,) + w.shape[1:], BF16))
            operands.append(w)
    return pl.pallas_call(
        functools.partial(_ffn_kernel, d_ff=d_ff, pair_in=pair_in, pair_out=pair_out, cast_steps=cast_steps,
                          with_kv=kv is not None),
        grid=(n_steps,),
        in_specs=in_specs,
        out_specs=out_specs,
        out_shape=out_shape,
        scratch_shapes=[pltpu.VMEM((d // LANE, tm, LANE), F32)] if pair_in != pair_out else [],
        compiler_params=_params(),
        name="ffn",
    )(*operands)


def _proj_kernel(x_ref, g_ref, w_ref, b_ref, o_ref):
    xn = _rms(x_ref[...], g_ref[...]).astype(BF16)
    o_ref[...] = jnp.dot(xn, w_ref[...], preferred_element_type=F32) + b_ref[...]


def _norm_proj(x, g, w, b, layer):
    t, d = x.shape
    n = w.shape[-1]
    tm = min(ROW_TILE, t)
    return pl.pallas_call(
        _proj_kernel,
        grid=(t // tm,),
        in_specs=[pl.BlockSpec((tm, d), lambda i: (i, 0)), _const_spec((1, d)),
                  _layer_spec(w.shape, layer), _const_spec((1, n))],
        out_specs=pl.BlockSpec((tm, n), lambda i: (i, 0)),
        out_shape=jax.ShapeDtypeStruct((t, n), F32),
        compiler_params=_params(),
        name="norm_proj",
    )(x, g, w, b)


def _out_proj_kernel(x_ref, o_ref_in, g_ref, w_ref, b_ref, out_ref):
    a = jnp.dot(o_ref_in[...].astype(BF16), w_ref[...], preferred_element_type=F32) + b_ref[...]
    out_ref[...] = x_ref[...] + _rms(a, g_ref[...])


def _out_proj_residual(x, o, g, w, b, layer):
    t, d = x.shape
    tm = min(ROW_TILE, t)
    row = pl.BlockSpec((tm, d), lambda i: (i, 0))
    return pl.pallas_call(
        _out_proj_kernel,
        grid=(t // tm,),
        in_specs=[row, pl.BlockSpec((tm, o.shape[1]), lambda i: (i, 0)), _const_spec((1, d)),
                  _layer_spec(w.shape, layer), _const_spec((1, d))],
        out_specs=row,
        out_shape=jax.ShapeDtypeStruct((t, d), F32),
        compiler_params=_params(),
        name="out_proj_residual",
    )(x, o, g, w, b)


def _lam_bar(lam_re, lam_im, log_dt):
    dt = jnp.exp(log_dt)
    mag = jnp.exp(lam_re * dt)
    return mag * jnp.cos(lam_im * dt), mag * jnp.sin(lam_im * dt)


def _discretize_kernel(lre_ref, lim_ref, ldt_ref, lre_col_ref, lim_col_ref, ldt_col_ref,
                       bre_ref, bim_ref, cre_ref, cim_ref, win_ref, wout_ref, wskip_ref, lam2_ref):
    a = lre_ref[0, 0]
    b = lim_ref[0, 0]
    lbr, lbi = _lam_bar(a, b, ldt_ref[0, 0])
    den = a * a + b * b
    cr = ((lbr - 1.0) * a + lbi * b) / den
    ci = (lbi * a - (lbr - 1.0) * b) / den
    bre = bre_ref[0, 0]
    bim = bim_ref[0, 0]
    wbr = cr * bre - ci * bim
    wbi = cr * bim + ci * bre
    lwbr = lbr * wbr - lbi * wbi
    lwbi = lbr * wbi + lbi * wbr
    win_ref[0, 0, :LANE, :HALF] = lwbr.astype(BF16)
    win_ref[0, 0, :LANE, HALF:] = lwbi.astype(BF16)
    win_ref[0, 0, LANE:, :HALF] = wbr.astype(BF16)
    win_ref[0, 0, LANE:, HALF:] = wbi.astype(BF16)
    lam2_ref[0, 0, :, :HALF] = lbr * lbr - lbi * lbi
    lam2_ref[0, 0, :, HALF:] = 2.0 * lbr * lbi
    lcr, lci = _lam_bar(lre_col_ref[0, 0], lim_col_ref[0, 0], ldt_col_ref[0, 0])
    cre = cre_ref[0, 0]
    cim = cim_ref[0, 0]
    c1r = cre * lcr - cim * lci
    c1i = cre * lci + cim * lcr
    wout_ref[0, 0, :HALF, :LANE] = c1r.astype(BF16)
    wout_ref[0, 0, HALF:, :LANE] = (-c1i).astype(BF16)
    wout_ref[0, 0, :HALF, LANE:] = (c1r * lcr - c1i * lci).astype(BF16)
    wout_ref[0, 0, HALF:, LANE:] = (-(c1r * lci + c1i * lcr)).astype(BF16)
    def through(br, bi):
        return (jnp.dot(br, cre, precision=lax.Precision.HIGHEST, preferred_element_type=F32)
                - jnp.dot(bi, cim, precision=lax.Precision.HIGHEST, preferred_element_type=F32)).astype(BF16)

    cb = through(wbr, wbi)
    wskip_ref[0, 0, :LANE, :LANE] = cb
    wskip_ref[0, 0, :LANE, LANE:] = through(lwbr, lwbi)
    wskip_ref[0, 0, LANE:, :LANE] = jnp.zeros((LANE, LANE), BF16)
    wskip_ref[0, 0, LANE:, LANE:] = cb


def _block_diag_in(b):
    nl, g, p, h = b.shape
    nb = g // GROUPS_PER_BLOCK
    bt = b.transpose(0, 1, 3, 2).reshape(nl, nb, GROUPS_PER_BLOCK, h, p)
    eye = jnp.eye(GROUPS_PER_BLOCK, dtype=b.dtype)
    out = bt[:, :, :, :, None, :] * eye[None, None, :, None, :, None]
    return out.reshape(nl, nb, GROUPS_PER_BLOCK * h, GROUPS_PER_BLOCK * p)


def _block_diag_out(c):
    nl, g, h, p = c.shape
    nb = g // GROUPS_PER_BLOCK
    ct = c.transpose(0, 1, 3, 2).reshape(nl, nb, GROUPS_PER_BLOCK, p, h)
    eye = jnp.eye(GROUPS_PER_BLOCK, dtype=c.dtype)
    out = ct[:, :, :, :, None, :] * eye[None, None, :, None, :, None]
    return out.reshape(nl, nb, GROUPS_PER_BLOCK * p, GROUPS_PER_BLOCK * h)


def _discretize(lam_re, lam_im, log_dt, b_re, b_im, c_re, c_im):
    nl, g, p = lam_re.shape
    nb = g // GROUPS_PER_BLOCK

    def rows(v):
        return v.reshape(nl, nb, 1, HALF)

    def cols(v):
        return v.reshape(nl, nb, HALF, 1)

    def spec(r, c):
        return pl.BlockSpec((1, 1, r, c), lambda l, k: (l, k, 0, 0))

    win, wout, wskip, lam2 = pl.pallas_call(
        _discretize_kernel,
        grid=(nl, nb),
        in_specs=[spec(1, HALF)] * 3 + [spec(HALF, 1)] * 3 + [spec(LANE, HALF)] * 2 + [spec(HALF, LANE)] * 2,
        out_specs=[spec(2 * LANE, 2 * HALF), spec(2 * HALF, 2 * LANE), spec(2 * LANE, 2 * LANE), spec(1, 2 * HALF)],
        out_shape=[jax.ShapeDtypeStruct((nl, nb, 2 * LANE, 2 * HALF), BF16),
                   jax.ShapeDtypeStruct((nl, nb, 2 * HALF, 2 * LANE), BF16),
                   jax.ShapeDtypeStruct((nl, nb, 2 * LANE, 2 * LANE), BF16),
                   jax.ShapeDtypeStruct((nl, nb, 1, 2 * HALF), F32)],
        compiler_params=_params(2),
        name="s5_discretize",
    )(rows(lam_re), rows(lam_im), rows(log_dt), cols(lam_re), cols(lam_im), cols(log_dt),
      _block_diag_in(b_re), _block_diag_in(b_im), _block_diag_out(c_re), _block_diag_out(c_im))
    return win, wout, wskip, lam2.reshape(nl, nb, 2 * HALF)


def _slab_pitch(rows):
    assert rows % SUBLANE == 0
    return rows + SUBLANE // 2


def _split_state(st):
    return (tuple(st[:, p * LANE:(p + 1) * LANE] for p in range(N_PAIRS)),
            tuple(st[:, HALF + p * LANE:HALF + (p + 1) * LANE] for p in range(N_PAIRS)))


def _s5_mixer_kernel(x_ref, g2_ref, g3_ref, win_ref, wout_ref, wskip_ref, lam_ref, d_ref, wglu_ref, bglu_ref,
                     init_ref, o_ref, final_ref, state_ref, y_ref, s_ref,
                     *, n_blocks, n_seq, n_t, d_model):
    tile = pl.program_id(1)
    rows = n_seq * n_t
    pitch = _slab_pitch(rows)

    @pl.when(tile == 0)
    def _():
        state_ref[...] = init_ref[...]

    lre, lim = _split_state(lam_ref[...])
    xe = x_ref[0, :, :d_model]
    xo = x_ref[0, :, d_model:]
    ue = _rms(xe, g2_ref[...])
    uo = _rms(xo, g2_ref[...])
    ueb = ue.astype(BF16)
    uob = uo.astype(BF16)


    u_pair = [jnp.concatenate([ueb[:, k * LANE:(k + 1) * LANE], uob[:, k * LANE:(k + 1) * LANE]], axis=1)
              for k in range(n_blocks)]
    for k in range(n_blocks):
        v = jnp.dot(u_pair[k], win_ref[k], preferred_element_type=F32)
        for j in range(N_SLABS):
            s_ref[j, pl.ds(k * pitch, rows), :] = v[:, j * LANE:(j + 1) * LANE]

    def seq_body(seq, carry):
        xre, xim = _split_state(state_ref[seq])

        def step(t, st):
            xre, xim = st
            pair = pl.ds(seq * n_t + t, SUBLANE, stride=pitch)
            nre, nim = [], []
            for p in range(N_PAIRS):
                vre = s_ref[p, pair, :]
                vim = s_ref[N_PAIRS + p, pair, :]
                s_ref[p, pair, :] = xre[p]
                s_ref[N_PAIRS + p, pair, :] = xim[p]
                nre.append(lre[p] * xre[p] - lim[p] * xim[p] + vre)
                nim.append(lre[p] * xim[p] + lim[p] * xre[p] + vim)
            return tuple(nre), tuple(nim)

        xre, xim = lax.fori_loop(0, n_t, step, (xre, xim), unroll=min(n_t, 8))
        state_ref[seq] = jnp.concatenate(list(xre) + list(xim), axis=1)
        return carry

    if n_seq == 1:
        seq_body(0, 0)
    else:
        lax.fori_loop(0, n_seq, seq_body, 0)

    for k in range(n_blocks):
        xb = jnp.concatenate([s_ref[j, pl.ds(k * pitch, rows), :] for j in range(N_SLABS)], axis=1)
        yk = (jnp.dot(xb.astype(BF16), wout_ref[k], preferred_element_type=F32)
              + jnp.dot(u_pair[k], wskip_ref[k], preferred_element_type=F32))
        y_ref[:, k * LANE:(k + 1) * LANE] = yk[:, :LANE]
        y_ref[:, d_model + k * LANE:d_model + (k + 1) * LANE] = yk[:, LANE:]
    y = jnp.concatenate([y_ref[:, :d_model] + d_ref[...] * ue, y_ref[:, d_model:] + d_ref[...] * uo], axis=0)
    rc = y.shape[0] // FFN_CHAINS
    gy = [jax.nn.gelu(y[c * rc:(c + 1) * rc]).astype(BF16) for c in range(FFN_CHAINS)]
    h = [jnp.dot(v, wglu_ref[...], preferred_element_type=F32) + bglu_ref[...] for v in gy]
    r = jnp.concatenate([_rms(v[:, :d_model] * _sigmoid(v[:, d_model:]), g3_ref[...]) for v in h], axis=0)
    o_ref[0] = jnp.concatenate([xe + r[:rows], xo + r[rows:]], axis=1)

    @pl.when(tile == pl.num_programs(1) - 1)
    def _():
        final_ref[...] = state_ref[...]


def _s5_layer(x, g2, g3, win, wout, wskip, lam2, d, wglu, bglu, init, layer, *, n_seq, n_t):
    nb_rows, s2, dm2 = x.shape
    dm = dm2 // 2
    n_blocks = win.shape[1]
    assert n_blocks == SUBLANE
    rows = n_seq * n_t
    n_tiles = s2 // rows
    assert n_seq == 1 or n_tiles == 1
    row = pl.BlockSpec((1, rows, dm2), lambda b, i: (b, i, 0))
    vec = _const_spec((1, dm))
    state = pl.BlockSpec((n_seq, SUBLANE, 2 * HALF), lambda b, i: (b, 0, 0))
    slab = pltpu.VMEM((N_SLABS, n_blocks * _slab_pitch(rows), LANE), F32)
    return pl.pallas_call(
        functools.partial(_s5_mixer_kernel, n_blocks=n_blocks, n_seq=n_seq, n_t=n_t, d_model=dm),
        grid=(nb_rows, n_tiles),
        in_specs=[row, vec, vec, _layer_spec(win.shape, layer), _layer_spec(wout.shape, layer),
                  _layer_spec(wskip.shape, layer), _layer_spec(lam2.shape, layer), vec,
                  _layer_spec(wglu.shape, layer), _const_spec((1, 2 * dm)),
                  pl.BlockSpec(state.block_shape, state.index_map, pipeline_mode=pl.Buffered(1))],
        out_specs=[row, state],
        out_shape=[jax.ShapeDtypeStruct(x.shape, F32),
                   jax.ShapeDtypeStruct(init.shape, F32)],
        scratch_shapes=[pltpu.VMEM((n_seq, SUBLANE, 2 * HALF), F32), pltpu.VMEM((rows, dm2), F32), slab],
        compiler_params=_params(2),
        name="s5_mixer",
    )(x, g2, g3, win, wout, wskip, lam2, d, wglu, bglu, init)


def _state_to_rows(re, im):
    b = re.shape[0]
    return jnp.concatenate([re.reshape(b, -1, HALF), im.reshape(b, -1, HALF)], axis=-1)


def _rows_to_state(s, n_groups):
    b = s.shape[0]
    return (s[:, :, :HALF].reshape(b, n_groups, STATE_DIM), s[:, :, HALF:].reshape(b, n_groups, STATE_DIM))


def _t5_bucket_np(dist):
    n = np.maximum(dist, 0)
    max_exact = NUM_BUCKETS // 2
    nf = np.maximum(n, 1).astype(np.float32)
    large = max_exact + (np.log(nf / np.float32(max_exact)) / np.float32(math.log(MAX_DISTANCE / max_exact))
                         * np.float32(NUM_BUCKETS - max_exact)).astype(np.int32)
    large = np.minimum(large, NUM_BUCKETS - 1)
    return np.where(n < max_exact, n, large).astype(np.int32)


def _bucket_table(n_q, n_k, q_offset, n_q_pad, n_k_pad, first_key=0):
    dist = (np.arange(n_q)[:, None] + q_offset) - np.arange(n_k)[None, :]
    valid = (dist >= 0) & (dist < WINDOW) & (np.arange(n_k)[None, :] >= first_key)
    table = np.full((n_q_pad, n_k_pad), -1, np.int32)
    table[:n_q, :n_k] = np.where(valid, _t5_bucket_np(dist), -1)
    return table


def _bias_kernel(rel_ref, idx_ref, o_ref):
    h = pl.program_id(0)
    idx = idx_ref[...]
    acc = jnp.full(idx.shape, NEG_INF, F32)
    for b in range(NUM_BUCKETS):
        acc = jnp.where(idx == b, rel_ref[b, h] * LOG2_E, acc)
    o_ref[0] = acc


def _bias_planes(rel_bias, table):
    n_heads = rel_bias.shape[1]
    r, c = table.shape
    return pl.pallas_call(
        _bias_kernel,
        grid=(n_heads,),
        in_specs=[pl.BlockSpec(memory_space=pltpu.SMEM), pl.BlockSpec((r, c), lambda h: (0, 0))],
        out_specs=pl.BlockSpec((1, r, c), lambda h: (h, 0, 0)),
        out_shape=jax.ShapeDtypeStruct((n_heads, r, c), F32),
        compiler_params=_params(),
        name="attn_bias",
    )(rel_bias, jnp.asarray(table))


def _kv_head_rows(kv, g):
    lane = lax.broadcasted_iota(jnp.int32, kv.shape, 1)
    swapped = pltpu.roll(kv, HEAD_DIM, axis=1)
    even, odd = (kv, swapped) if g == 0 else (swapped, kv)
    return jnp.where(lane < HEAD_DIM, even, 0.0), jnp.where(lane >= HEAD_DIM, odd, 0.0)


def _pair_layout_t(planes, n_kv, n_q):
    h, _, n_k = planes.shape
    pairs = h // n_kv // 2
    p = planes[:, :n_q].reshape(n_kv, pairs, 2, n_q, n_k).transpose(0, 2, 4, 1, 3)
    return p.reshape(n_kv, 2 * n_k, pairs * n_q)


def _sink_cols(sinks, n_kv, n_q):
    h = sinks.shape[0]
    pairs = h // n_kv // 2
    s = sinks.reshape(n_kv, pairs, 2).transpose(0, 2, 1)[..., None]
    return jnp.broadcast_to(s, (n_kv, 2, pairs, n_q)).reshape(n_kv, 2, pairs * n_q)


def _attend_t(chains):
    scores = []
    for q_t, k_even, k_odd, _, _, bias, _ in chains:
        ke = jnp.concatenate([k_even, k_odd], axis=0).astype(BF16)
        scores.append(jnp.dot(ke, q_t, preferred_element_type=F32) + bias)
    weights = []
    for s, chain in zip(scores, chains):
        sink = chain[6]
        n_k = s.shape[0] // 2
        probs, inv = [], []
        for par in range(2):
            sp = s[par * n_k:(par + 1) * n_k]
            sk = sink[par:par + 1] * LOG2_E
            mx = jnp.maximum(jnp.max(sp, axis=0, keepdims=True), sk)
            p = jnp.exp2(sp - mx)
            den = jnp.sum(p, axis=0, keepdims=True) + jnp.exp2(sk - mx)
            probs.append(p.astype(BF16))
            inv.append(1.0 / den)
        weights.append((jnp.concatenate(probs, axis=0), inv))
    outs = []
    for (p_t, inv), chain in zip(weights, chains):
        ve = jnp.concatenate([chain[3], chain[4]], axis=0).astype(BF16)
        o = lax.dot_general(ve, p_t, (((0,), (0,)), ((), ())), preferred_element_type=F32)
        row = lax.broadcasted_iota(jnp.int32, o.shape, 0)
        outs.append(o * jnp.where(row < HEAD_DIM, inv[0], inv[1]))
    return outs


def _prompt_attn_kernel(x_ref, kvc_ref, kvp_ref, bias_ref, sink_ref, g2_ref, g3_ref, wq_ref, bq_ref,
                        wo_ref, bo_ref, o_ref, *, n_kv, n_sub, steps_per_seq, scale):
    blk = WINDOW
    x = x_ref[...]
    u = _rms(x, g2_ref[...]).astype(BF16)
    q_t = lax.dot_general(wq_ref[...], u, (((1,), (1,)), ((), ())), preferred_element_type=F32)
    q_t = ((q_t + jnp.concatenate([bq_ref[...]] * n_sub, axis=1)) * scale).astype(BF16)
    kv = jnp.concatenate([kvp_ref[...], kvc_ref[...]], axis=0)
    pairs = q_t.shape[0] // LANE // n_kv
    first = jnp.where(pl.program_id(0) % steps_per_seq == 0, 1, 0)
    k_rows = [_kv_head_rows(kv[:, :n_kv * HEAD_DIM], g) for g in range(n_kv)]
    v_rows = [_kv_head_rows(kv[:, n_kv * HEAD_DIM:], g) for g in range(n_kv)]
    chains = []
    for s in range(n_sub):
        keys = slice(s * blk, (s + 2) * blk)
        for g in range(n_kv):
            qp = jnp.concatenate([q_t[(g * pairs + i) * LANE:(g * pairs + i + 1) * LANE, s * blk:(s + 1) * blk]
                                  for i in range(pairs)], axis=1)
            bias = bias_ref[first, g] if s == 0 else bias_ref[0, g]
            chains.append((qp, k_rows[g][0][keys], k_rows[g][1][keys], v_rows[g][0][keys],
                           v_rows[g][1][keys], bias, sink_ref[g]))
    outs = _attend_t(chains)
    o_t = jnp.concatenate(
        [jnp.concatenate([outs[s * n_kv + g][:, i * blk:(i + 1) * blk] for s in range(n_sub)], axis=1)
         for g in range(n_kv) for i in range(pairs)], axis=0).astype(BF16)
    a = lax.dot_general(o_t, wo_ref[...], (((0,), (0,)), ((), ())), preferred_element_type=F32) + bo_ref[...]
    o_ref[...] = x + _rms(a, g3_ref[...])


def _prompt_attention(x, kv, bias, sink, g2, g3, wq_t, bq_col, wo, bo, layer, *, blocks_per_seq, n_kv):
    t, d = x.shape
    blk = WINDOW
    n_sub = ATTN_BLOCKS_PER_STEP
    assert n_kv * HEAD_DIM == LANE and blocks_per_seq % n_sub == 0
    row = pl.BlockSpec((n_sub * blk, d), lambda i: (i, 0))
    kvw = kv.shape[1]
    return pl.pallas_call(
        functools.partial(_prompt_attn_kernel, n_kv=n_kv, n_sub=n_sub,
                          steps_per_seq=blocks_per_seq // n_sub, scale=QUERY_SCALE),
        grid=(t // (n_sub * blk),),
        in_specs=[row,
                  pl.BlockSpec((n_sub * blk, kvw), lambda i: (i, 0)),
                  pl.BlockSpec((blk, kvw), lambda i: (jnp.maximum(i * n_sub - 1, 0), 0)),
                  _const_spec(bias.shape),
                  _const_spec(sink.shape), _const_spec((1, d)), _const_spec((1, d)),
                  _layer_spec(wq_t.shape, layer), _const_spec(bq_col.shape),
                  _layer_spec(wo.shape, layer), _const_spec((1, d))],
        out_specs=row,
        out_shape=jax.ShapeDtypeStruct((t, d), F32),
        compiler_params=_params(),
        name="prompt_attention",
    )(x, kv, kv, bias, sink, g2, g3, wq_t, bq_col, wo, bo)


def _sample_attn_kernel(q_ref, kc_ref, vc_ref, kvn_ref, bias_ref, sink_ref, o_ref, *, n_kv, scale):
    half = n_kv * HEAD_DIM
    chains = []
    for b in range(q_ref.shape[0]):
        k = jnp.concatenate([kc_ref[b], kvn_ref[b, :, :half]], axis=0)
        v = jnp.concatenate([vc_ref[b], kvn_ref[b, :, half:]], axis=0)
        for g in range(n_kv):
            chains.append(((q_ref[b, g] * scale).astype(BF16), *_kv_head_rows(k, g), *_kv_head_rows(v, g),
                           bias_ref[g], sink_ref[g]))
    outs = _attend_t(chains)
    for b in range(q_ref.shape[0]):
        for g in range(n_kv):
            o_ref[b, g] = outs[b * n_kv + g]


def _sample_attention(q_t, k_cache, v_cache, kv_new, bias, sink, *, n_kv):
    nb, _, _, n_l = q_t.shape
    n_past = k_cache.shape[1]
    bb = SUBLANE
    return pl.pallas_call(
        functools.partial(_sample_attn_kernel, n_kv=n_kv, scale=QUERY_SCALE),
        grid=(nb // bb,),
        in_specs=[pl.BlockSpec((bb, n_kv, LANE, n_l), lambda i: (i, 0, 0, 0)),
                  pl.BlockSpec((bb, n_past, LANE), lambda i: (i, 0, 0)),
                  pl.BlockSpec((bb, n_past, LANE), lambda i: (i, 0, 0)),
                  pl.BlockSpec((bb,) + kv_new.shape[1:], lambda i: (i, 0, 0)),
                  _const_spec(bias.shape), _const_spec(sink.shape)],
        out_specs=pl.BlockSpec((bb, n_kv, LANE, n_l), lambda i: (i, 0, 0, 0)),
        out_shape=jax.ShapeDtypeStruct(q_t.shape, F32),
        compiler_params=_params(),
        name="sample_attention",
    )(q_t, k_cache, v_cache, kv_new, bias, sink)


def kernel(x_prompt, x_sample, state_ssm_re, state_ssm_im, cache_win_k, cache_win_v, norm_g, ffn1_w_gu, ffn1_w_down, ffn2_w_gu, ffn2_w_down, ssm_lambda_re, ssm_lambda_im, ssm_log_dt, ssm_b_re, ssm_b_im, ssm_c_re, ssm_c_im, ssm_d, ssm_w_glu, ssm_b_glu, kv_norm_g, w_kv, b_kv, attn_w_q, attn_b_q, attn_sinks, attn_w_o, attn_b_o, rel_bias):
    bsz, seq, dm = x_prompt.shape
    dec_b, dec_s, _ = x_sample.shape
    depth = norm_g.shape[0]
    n_a = ssm_lambda_re.shape[0]
    n_groups = ssm_lambda_re.shape[1]
    n_heads = attn_sinks.shape[1]
    n_kv = cache_win_k.shape[2]
    n_past = cache_win_k.shape[1]
    pairs = n_heads // n_kv // 2
    n_k_pad = n_past + SUBLANE
    assert seq % ROW_TILE == 0 and seq % WINDOW == 0 and dec_s % 2 == 0
    assert dec_b * dec_s == ROW_TILE and dec_s <= SUBLANE and n_past % SUBLANE == 0
    assert dec_b % (SUBLANE * SAMPLE_SPLIT) == 0
    assert n_kv == 2 and dm == n_heads * HEAD_DIM and n_groups == SUBLANE * GROUPS_PER_BLOCK

    bf = lambda w: w.astype(BF16)
    row = lambda v: v.reshape(1, -1)
    s5_w = _discretize(ssm_lambda_re, ssm_lambda_im, ssm_log_dt, ssm_b_re, ssm_b_im, ssm_c_re, ssm_c_im)
    ffn_w = (bf(ffn1_w_gu[0])[None], bf(ffn1_w_down[0])[None])
    wglu, wkv, wq, wo = bf(ssm_w_glu), bf(w_kv)[None], bf(attn_w_q), bf(attn_w_o)

    tab_p = _bucket_table(WINDOW, 2 * WINDOW, WINDOW, WINDOW, 2 * WINDOW)
    tab_p0 = _bucket_table(WINDOW, 2 * WINDOW, WINDOW, WINDOW, 2 * WINDOW, first_key=WINDOW)
    tab_s = _bucket_table(dec_s, n_past + dec_s, n_past, SUBLANE, n_k_pad)
    wq_t = wq.transpose(0, 2, 1)
    bias_p = jnp.stack([_pair_layout_t(_bias_planes(rel_bias, tab_p), n_kv, WINDOW),
                        _pair_layout_t(_bias_planes(rel_bias, tab_p0), n_kv, WINDOW)])
    bias_s = _pair_layout_t(_bias_planes(rel_bias, tab_s), n_kv, dec_s)

    xp = x_prompt.reshape(bsz * seq, dm)
    xs = x_sample.reshape(dec_b * dec_s, dm)
    zero_state = jnp.zeros((bsz, SUBLANE, 2 * HALF), F32)

    ends_p, ends_s = [], []
    kv_p = kv_s = kv_new = None
    k_cache = cache_win_k.reshape(dec_b, n_past, n_kv * HEAD_DIM)
    v_cache = cache_win_v.reshape(dec_b, n_past, n_kv * HEAD_DIM)
    half = n_kv * HEAD_DIM
    for l in range(depth):
        g = [row(norm_g[l, i]) for i in range(norm_g.shape[1])]
        pair_io = dict(pair_in=0 < l < n_a, pair_out=l < n_a)
        xp, xs, *ffn_w = _ffn(xp, xs, g[0], g[1], *ffn_w, **pair_io, cast_next=(ffn2_w_gu, ffn2_w_down, l))
        if l < n_a:
            common = (g[2], g[3], *s5_w, row(ssm_d[l]), wglu, row(ssm_b_glu[l]))
            xp3, fin_p = _s5_layer(xp.reshape(bsz, seq // 2, 2 * dm), *common, zero_state, l,
                                   n_seq=1, n_t=ROW_TILE // 2)
            xs3, fin_s = _s5_layer(xs.reshape(SAMPLE_SPLIT, dec_b * dec_s // (2 * SAMPLE_SPLIT), 2 * dm), *common,
                                   _state_to_rows(state_ssm_re[l], state_ssm_im[l]), l,
                                   n_seq=dec_b // SAMPLE_SPLIT, n_t=dec_s // 2)
            xp, xs = xp3.reshape(bsz * seq // 2, 2 * dm), xs3.reshape(dec_b * dec_s // 2, 2 * dm)
            ends_p.append(_rows_to_state(fin_p, n_groups))
            ends_s.append(_rows_to_state(fin_s, n_groups))
        else:
            bl = l - n_a
            bq_l, bo_l = row(attn_b_q[bl]), row(attn_b_o[bl])
            bq_col = jnp.broadcast_to(attn_b_q[bl][:, None], (dm, WINDOW))
            xp = _prompt_attention(xp, kv_p, bias_p, _sink_cols(attn_sinks[bl], n_kv, WINDOW), g[2], g[3],
                                   wq_t, bq_col, wo, bo_l, bl, blocks_per_seq=seq // WINDOW, n_kv=n_kv)
            q = _norm_proj(xs, g[2], wq, bq_l, bl)
            q = q.reshape(dec_b, dec_s, n_kv, pairs, LANE).transpose(0, 2, 4, 3, 1)
            q = q.reshape(dec_b, n_kv, LANE, pairs * dec_s)
            o = _sample_attention(q, k_cache, v_cache, kv_new, bias_s, _sink_cols(attn_sinks[bl], n_kv, dec_s), n_kv=n_kv)
            o = o.reshape(dec_b, n_kv, LANE, pairs, dec_s).transpose(0, 4, 1, 3, 2).reshape(dec_b * dec_s, dm)
            xs = _out_proj_residual(xs, o, g[3], wo, bo_l, bl)
        pair_io = dict(pair_in=l < n_a, pair_out=l < n_a - 1)
        cast_next = (ffn1_w_gu, ffn1_w_down, l + 1) if l + 1 < depth else None
        if l == n_a - 1:
            xp, xs, kv_p, kv_s, *ffn_w = _ffn(xp, xs, g[4], g[5], *ffn_w, **pair_io, cast_next=cast_next,
                                              kv=(row(kv_norm_g), wkv, row(b_kv)))
            kv_s = kv_s.reshape(dec_b, dec_s, 2 * half)
            kv_new = jnp.pad(kv_s, ((0, 0), (0, SUBLANE - dec_s), (0, 0)))
        else:
            xp, xs, *ffn_w = _ffn(xp, xs, g[4], g[5], *ffn_w, **pair_io, cast_next=cast_next)

    kv_p3 = kv_p.reshape(bsz, seq, 2 * half)
    new_k_p = kv_p3[:, -WINDOW:, :half].reshape(bsz, WINDOW, n_kv, HEAD_DIM)
    new_v_p = kv_p3[:, -WINDOW:, half:].reshape(bsz, WINDOW, n_kv, HEAD_DIM)
    new_k_s = jnp.concatenate([k_cache, kv_s[:, :, :half]], axis=1)[:, -WINDOW:].reshape(dec_b, WINDOW, n_kv, HEAD_DIM)
    new_v_s = jnp.concatenate([v_cache, kv_s[:, :, half:]], axis=1)[:, -WINDOW:].reshape(dec_b, WINDOW, n_kv, HEAD_DIM)
    return (xp.reshape(bsz, seq, dm), xs.reshape(dec_b, dec_s, dm),
            jnp.stack([e[0] for e in ends_p]), jnp.stack([e[1] for e in ends_p]),
            new_k_p, new_v_p,
            jnp.stack([e[0] for e in ends_s]), jnp.stack([e[1] for e in ends_s]),
            new_k_s, new_v_s)
```

```python
import functools
import math

import numpy as np
import jax
import jax.numpy as jnp
from jax import lax
from jax.experimental import pallas as pl
from jax.experimental.pallas import tpu as pltpu

F32 = jnp.float32
BF16 = jnp.bfloat16

LANE = 128
SUBLANE = 8
VMEM_LIMIT_BYTES = 56 * 1024 * 1024

RMS_EPS = 1e-6
GROUP_SIZE = 16
STATE_DIM = 64
HEAD_DIM = 64
WINDOW = 128
NUM_BUCKETS = 32
MAX_DISTANCE = WINDOW
GROUPS_PER_BLOCK = LANE // GROUP_SIZE
HALF = GROUPS_PER_BLOCK * STATE_DIM
N_SLABS = 2 * HALF // LANE
N_PAIRS = N_SLABS // 2
ROW_TILE = 512
SAMPLE_SPLIT = 2
FFN_CHAINS = 2
ATTN_BLOCKS_PER_STEP = 4
NEG_INF = float("-inf")
LOG2_E = math.log2(math.e)
QUERY_SCALE = LOG2_E / math.sqrt(HEAD_DIM)


def _params(n_axes=1):
    return pltpu.CompilerParams(dimension_semantics=("arbitrary",) * n_axes,
                                vmem_limit_bytes=VMEM_LIMIT_BYTES)


def _const_spec(shape):
    nd = len(shape)
    return pl.BlockSpec(shape, lambda *_: (0,) * nd, pipeline_mode=pl.Buffered(1))


def _layer_spec(shape, layer):
    nd = len(shape)
    return pl.BlockSpec((None,) + tuple(shape[1:]), lambda *_: (layer,) + (0,) * (nd - 1),
                        pipeline_mode=pl.Buffered(1))


def _rms(x, g):
    ms = jnp.mean(x * x, axis=-1, keepdims=True)
    return x * lax.rsqrt(ms + RMS_EPS) * g


def _aligned(row):
    return row if isinstance(row, int) else pl.multiple_of(row, SUBLANE)


def _sigmoid(x):
    return 1.0 / (1.0 + jnp.exp(-x))


def _cast_chunk(rows, n_steps):
    tile = 2 * SUBLANE
    for chunk in range(tile, rows + 1, tile):
        if rows % chunk == 0 and rows // chunk <= n_steps:
            return chunk
    raise ValueError((rows, n_steps))


def _ffn_kernel(*refs, d_ff, pair_in, pair_out, cast_steps, with_kv):
    refs = list(refs)
    x_ref, xr_ref, ga_ref, gb_ref, wgu_ref, wd_ref = refs[:6]
    del refs[:6]
    kvg_ref, wkv_ref, bkv_ref = (refs.pop(0), refs.pop(0), refs.pop(0)) if with_kv else (None,) * 3
    cast_src = (refs.pop(0), refs.pop(0)) if cast_steps else ()
    o_ref, or_ref = refs.pop(0), refs.pop(0)
    kv_ref, kvr_ref = (refs.pop(0), refs.pop(0)) if with_kv else (None, None)
    cast_dst = (refs.pop(0), refs.pop(0)) if cast_steps else ()
    slab_ref = refs
    for src, dst, steps in zip(cast_src, cast_dst, cast_steps or ()):
        @pl.when(pl.program_id(0) < steps)
        def _(src=src, dst=dst):
            dst[...] = src[...].astype(BF16)
    args = (ga_ref, gb_ref, wgu_ref, wd_ref, kvg_ref, wkv_ref, bkv_ref, slab_ref)
    _ffn_tile(x_ref, o_ref, kv_ref, *args, d_ff=d_ff, pair_in=pair_in, pair_out=pair_out)

    @pl.when(pl.program_id(0) == pl.num_programs(0) - 1)
    def _():
        _ffn_tile(xr_ref, or_ref, kvr_ref, *args, d_ff=d_ff, pair_in=pair_in, pair_out=pair_out)


def _ffn_tile(x_ref, o_ref, kv_ref, ga_ref, gb_ref, wgu_ref, wd_ref, kvg_ref, wkv_ref, bkv_ref, slab_ref,
              *, d_ff, pair_in, pair_out):
    d = ga_ref.shape[1]
    if pair_in:
        x = jnp.concatenate([x_ref[:, :d], x_ref[:, d:]], axis=0)
    else:
        x = x_ref[...]
    half = x.shape[0] // 2
    rc = x.shape[0] // FFN_CHAINS
    xc = [x[c * rc:(c + 1) * rc] for c in range(FFN_CHAINS)]
    xn = [_rms(v, ga_ref[...]).astype(BF16) for v in xc]
    gu = [jnp.dot(v, wgu_ref[...], preferred_element_type=F32) for v in xn]
    h = [(v[:, :d_ff] * _sigmoid(v[:, :d_ff]) * v[:, d_ff:]).astype(BF16) for v in gu]
    y = [jnp.dot(v, wd_ref[...], preferred_element_type=F32) for v in h]
    out = jnp.concatenate([v + 0.5 * _rms(w, gb_ref[...]) for v, w in zip(xc, y)], axis=0)
    if pair_in == pair_out:
        o_ref[...] = jnp.concatenate([out[:half], out[half:]], axis=1) if pair_in else out
    else:
        (slab,) = slab_ref
        even = pl.ds(0, half, stride=2)
        odd = pl.ds(1, half, stride=2)
        for j in range(d // LANE):
            lanes = slice(j * LANE, (j + 1) * LANE)
            if pair_out:
                slab[j] = out[:, lanes]
                o_ref[:, lanes] = slab[j, even, :]
                o_ref[:, d + j * LANE:d + (j + 1) * LANE] = slab[j, odd, :]
            else:
                slab[j, even, :] = out[:half, lanes]
                slab[j, odd, :] = out[half:, lanes]
                o_ref[:, lanes] = slab[j]
    if kv_ref is not None:
        assert not pair_out
        kv_ref[...] = jnp.dot(_rms(o_ref[...], kvg_ref[...]).astype(BF16), wkv_ref[...],
                              preferred_element_type=F32) + bkv_ref[...]


def _ffn(x, x_rider, ga, gb, wgu, wd, pair_in=False, pair_out=False, cast_next=None, kv=None):
    d = ga.shape[1]
    t = x.shape[0] * (2 if pair_in else 1)
    d_ff = wd.shape[1]
    tm = ROW_TILE
    n_steps = t // tm
    assert t % tm == 0 and x_rider.shape[0] * (2 if pair_in else 1) == tm

    def tile(paired):
        return (tm // 2, 2 * d) if paired else (tm, d)

    def rows(paired):
        return pl.BlockSpec(tile(paired), lambda i: (i, 0))

    def rider(shape):
        return pl.BlockSpec(shape, lambda i: (0, 0))

    in_specs = [rows(pair_in), _const_spec(tile(pair_in)), _const_spec((1, d)), _const_spec((1, d)),
                _layer_spec(wgu.shape, 0), _layer_spec(wd.shape, 0)]
    out_specs = [rows(pair_out), rider(tile(pair_out))]
    out_shape = [jax.ShapeDtypeStruct((t // 2, 2 * d) if pair_out else (t, d), F32),
                 jax.ShapeDtypeStruct(tile(pair_out), F32)]
    operands = [x, x_rider, ga, gb, wgu, wd]
    if kv is not None:
        kv_g, kv_w, kv_b = kv
        n_kv_out = kv_w.shape[-1]
        in_specs += [_const_spec((1, d)), _layer_spec(kv_w.shape, 0), _const_spec((1, n_kv_out))]
        out_specs += [pl.BlockSpec((tm, n_kv_out), lambda i: (i, 0)), rider((tm, n_kv_out))]
        out_shape += [jax.ShapeDtypeStruct((t, n_kv_out), F32), jax.ShapeDtypeStruct((tm, n_kv_out), F32)]
        operands += [kv_g, kv_w, kv_b]
    cast_steps = None
    if cast_next is not None:
        *next_w, layer = cast_next
        cast_steps = []
        for w in next_w:
            chunk = _cast_chunk(w.shape[1], n_steps)
            last = w.shape[1] // chunk - 1
            cast_steps.append(last + 1)
            in_specs.append(pl.BlockSpec((None, chunk, w.shape[2]),
                                         lambda i, last=last: (layer, jnp.minimum(i, last), 0)))
            out_specs.append(pl.BlockSpec((None, chunk, w.shape[2]),
                                          lambda i, last=last: (0, jnp.minimum(i, last), 0)))
            out_shape.append(jax.ShapeDtypeStruct((1,) + w.shape[1:], BF16))
            operands.append(w)
    return pl.pallas_call(
        functools.partial(_ffn_kernel, d_ff=d_ff, pair_in=pair_in, pair_out=pair_out, cast_steps=cast_steps,
                          with_kv=kv is not None),
        grid=(n_steps,),
        in_specs=in_specs,
        out_specs=out_specs,
        out_shape=out_shape,
        scratch_shapes=[pltpu.VMEM((d // LANE, tm, LANE), F32)] if pair_in != pair_out else [],
        compiler_params=_params(),
        name="ffn",
    )(*operands)


def _proj_kernel(x_ref, g_ref, w_ref, b_ref, o_ref):
    xn = _rms(x_ref[...], g_ref[...]).astype(BF16)
    o_ref[...] = jnp.dot(xn, w_ref[...], preferred_element_type=F32) + b_ref[...]


def _norm_proj(x, g, w, b, layer):
    t, d = x.shape
    n = w.shape[-1]
    tm = min(ROW_TILE, t)
    return pl.pallas_call(
        _proj_kernel,
        grid=(t // tm,),
        in_specs=[pl.BlockSpec((tm, d), lambda i: (i, 0)), _const_spec((1, d)),
                  _layer_spec(w.shape, layer), _const_spec((1, n))],
        out_specs=pl.BlockSpec((tm, n), lambda i: (i, 0)),
        out_shape=jax.ShapeDtypeStruct((t, n), F32),
        compiler_params=_params(),
        name="norm_proj",
    )(x, g, w, b)


def _out_proj_kernel(x_ref, o_ref_in, g_ref, w_ref, b_ref, out_ref):
    a = jnp.dot(o_ref_in[...].astype(BF16), w_ref[...], preferred_element_type=F32) + b_ref[...]
    out_ref[...] = x_ref[...] + _rms(a, g_ref[...])


def _out_proj_residual(x, o, g, w, b, layer):
    t, d = x.shape
    tm = min(ROW_TILE, t)
    row = pl.BlockSpec((tm, d), lambda i: (i, 0))
    return pl.pallas_call(
        _out_proj_kernel,
        grid=(t // tm,),
        in_specs=[row, pl.BlockSpec((tm, o.shape[1]), lambda i: (i, 0)), _const_spec((1, d)),
                  _layer_spec(w.shape, layer), _const_spec((1, d))],
        out_specs=row,
        out_shape=jax.ShapeDtypeStruct((t, d), F32),
        compiler_params=_params(),
        name="out_proj_residual",
    )(x, o, g, w, b)


def _lam_bar(lam_re, lam_im, log_dt):
    dt = jnp.exp(log_dt)
    mag = jnp.exp(lam_re * dt)
    return mag * jnp.cos(lam_im * dt), mag * jnp.sin(lam_im * dt)


def _discretize_kernel(lre_ref, lim_ref, ldt_ref, lre_col_ref, lim_col_ref, ldt_col_ref,
                       bre_ref, bim_ref, cre_ref, cim_ref, win_ref, wout_ref, wskip_ref, lam2_ref):
    a = lre_ref[0, 0]
    b = lim_ref[0, 0]
    lbr, lbi = _lam_bar(a, b, ldt_ref[0, 0])
    den = a * a + b * b
    cr = ((lbr - 1.0) * a + lbi * b) / den
    ci = (lbi * a - (lbr - 1.0) * b) / den
    bre = bre_ref[0, 0]
    bim = bim_ref[0, 0]
    wbr = cr * bre - ci * bim
    wbi = cr * bim + ci * bre
    lwbr = lbr * wbr - lbi * wbi
    lwbi = lbr * wbi + lbi * wbr
    win_ref[0, 0, :LANE, :HALF] = lwbr.astype(BF16)
    win_ref[0, 0, :LANE, HALF:] = lwbi.astype(BF16)
    win_ref[0, 0, LANE:, :HALF] = wbr.astype(BF16)
    win_ref[0, 0, LANE:, HALF:] = wbi.astype(BF16)
    lam2_ref[0, 0, :, :HALF] = lbr * lbr - lbi * lbi
    lam2_ref[0, 0, :, HALF:] = 2.0 * lbr * lbi
    lcr, lci = _lam_bar(lre_col_ref[0, 0], lim_col_ref[0, 0], ldt_col_ref[0, 0])
    cre = cre_ref[0, 0]
    cim = cim_ref[0, 0]
    c1r = cre * lcr - cim * lci
    c1i = cre * lci + cim * lcr
    wout_ref[0, 0, :HALF, :LANE] = c1r.astype(BF16)
    wout_ref[0, 0, HALF:, :LANE] = (-c1i).astype(BF16)
    wout_ref[0, 0, :HALF, LANE:] = (c1r * lcr - c1i * lci).astype(BF16)
    wout_ref[0, 0, HALF:, LANE:] = (-(c1r * lci + c1i * lcr)).astype(BF16)
    def through(br, bi):
        return (jnp.dot(br, cre, precision=lax.Precision.HIGHEST, preferred_element_type=F32)
                - jnp.dot(bi, cim, precision=lax.Precision.HIGHEST, preferred_element_type=F32)).astype(BF16)

    cb = through(wbr, wbi)
    wskip_ref[0, 0, :LANE, :LANE] = cb
    wskip_ref[0, 0, :LANE, LANE:] = through(lwbr, lwbi)
    wskip_ref[0, 0, LANE:, :LANE] = jnp.zeros((LANE, LANE), BF16)
    wskip_ref[0, 0, LANE:, LANE:] = cb


def _block_diag_in(b):
    nl, g, p, h = b.shape
    nb = g // GROUPS_PER_BLOCK
    bt = b.transpose(0, 1, 3, 2).reshape(nl, nb, GROUPS_PER_BLOCK, h, p)
    eye = jnp.eye(GROUPS_PER_BLOCK, dtype=b.dtype)
    out = bt[:, :, :, :, None, :] * eye[None, None, :, None, :, None]
    return out.reshape(nl, nb, GROUPS_PER_BLOCK * h, GROUPS_PER_BLOCK * p)


def _block_diag_out(c):
    nl, g, h, p = c.shape
    nb = g // GROUPS_PER_BLOCK
    ct = c.transpose(0, 1, 3, 2).reshape(nl, nb, GROUPS_PER_BLOCK, p, h)
    eye = jnp.eye(GROUPS_PER_BLOCK, dtype=c.dtype)
    out = ct[:, :, :, :, None, :] * eye[None, None, :, None, :, None]
    return out.reshape(nl, nb, GROUPS_PER_BLOCK * p, GROUPS_PER_BLOCK * h)


def _discretize(lam_re, lam_im, log_dt, b_re, b_im, c_re, c_im):
    nl, g, p = lam_re.shape
    nb = g // GROUPS_PER_BLOCK

    def rows(v):
        return v.reshape(nl, nb, 1, HALF)

    def cols(v):
        return v.reshape(nl, nb, HALF, 1)

    def spec(r, c):
        return pl.BlockSpec((1, 1, r, c), lambda l, k: (l, k, 0, 0))

    win, wout, wskip, lam2 = pl.pallas_call(
        _discretize_kernel,
        grid=(nl, nb),
        in_specs=[spec(1, HALF)] * 3 + [spec(HALF, 1)] * 3 + [spec(LANE, HALF)] * 2 + [spec(HALF, LANE)] * 2,
        out_specs=[spec(2 * LANE, 2 * HALF), spec(2 * HALF, 2 * LANE), spec(2 * LANE, 2 * LANE), spec(1, 2 * HALF)],
        out_shape=[jax.ShapeDtypeStruct((nl, nb, 2 * LANE, 2 * HALF), BF16),
                   jax.ShapeDtypeStruct((nl, nb, 2 * HALF, 2 * LANE), BF16),
                   jax.ShapeDtypeStruct((nl, nb, 2 * LANE, 2 * LANE), BF16),
                   jax.ShapeDtypeStruct((nl, nb, 1, 2 * HALF), F32)],
        compiler_params=_params(2),
        name="s5_discretize",
    )(rows(lam_re), rows(lam_im), rows(log_dt), cols(lam_re), cols(lam_im), cols(log_dt),
      _block_diag_in(b_re), _block_diag_in(b_im), _block_diag_out(c_re), _block_diag_out(c_im))
    return win, wout, wskip, lam2.reshape(nl, nb, 2 * HALF)


def _slab_pitch(rows):
    assert rows % SUBLANE == 0
    return rows + SUBLANE // 2


def _split_state(st):
    return (tuple(st[:, p * LANE:(p + 1) * LANE] for p in range(N_PAIRS)),
            tuple(st[:, HALF + p * LANE:HALF + (p + 1) * LANE] for p in range(N_PAIRS)))


def _s5_mixer_kernel(x_ref, g2_ref, g3_ref, win_ref, wout_ref, wskip_ref, lam_ref, d_ref, wglu_ref, bglu_ref,
                     init_ref, o_ref, final_ref, state_ref, y_ref, s_ref,
                     *, n_blocks, n_seq, n_t, d_model):
    tile = pl.program_id(1)
    rows = n_seq * n_t
    pitch = _slab_pitch(rows)

    @pl.when(tile == 0)
    def _():
        state_ref[...] = init_ref[...]

    lre, lim = _split_state(lam_ref[...])
    xe = x_ref[0, :, :d_model]
    xo = x_ref[0, :, d_model:]
    ue = _rms(xe, g2_ref[...])
    uo = _rms(xo, g2_ref[...])
    ueb = ue.astype(BF16)
    uob = uo.astype(BF16)


    u_pair = [jnp.concatenate([ueb[:, k * LANE:(k + 1) * LANE], uob[:, k * LANE:(k + 1) * LANE]], axis=1)
              for k in range(n_blocks)]
    for k in range(n_blocks):
        v = jnp.dot(u_pair[k], win_ref[k], preferred_element_type=F32)
        for j in range(N_SLABS):
            s_ref[j, pl.ds(k * pitch, rows), :] = v[:, j * LANE:(j + 1) * LANE]

    def seq_body(seq, carry):
        xre, xim = _split_state(state_ref[seq])

        def step(t, st):
            xre, xim = st
            pair = pl.ds(seq * n_t + t, SUBLANE, stride=pitch)
            nre, nim = [], []
            for p in range(N_PAIRS):
                vre = s_ref[p, pair, :]
                vim = s_ref[N_PAIRS + p, pair, :]
                s_ref[p, pair, :] = xre[p]
                s_ref[N_PAIRS + p, pair, :] = xim[p]
                nre.append(lre[p] * xre[p] - lim[p] * xim[p] + vre)
                nim.append(lre[p] * xim[p] + lim[p] * xre[p] + vim)
            return tuple(nre), tuple(nim)

        xre, xim = lax.fori_loop(0, n_t, step, (xre, xim), unroll=min(n_t, 8))
        state_ref[seq] = jnp.concatenate(list(xre) + list(xim), axis=1)
        return carry

    if n_seq == 1:
        seq_body(0, 0)
    else:
        lax.fori_loop(0, n_seq, seq_body, 0)

    for k in range(n_blocks):
        xb = jnp.concatenate([s_ref[j, pl.ds(k * pitch, rows), :] for j in range(N_SLABS)], axis=1)
        yk = (jnp.dot(xb.astype(BF16), wout_ref[k], preferred_element_type=F32)
              + jnp.dot(u_pair[k], wskip_ref[k], preferred_element_type=F32))
        y_ref[:, k * LANE:(k + 1) * LANE] = yk[:, :LANE]
        y_ref[:, d_model + k * LANE:d_model + (k + 1) * LANE] = yk[:, LANE:]
    y = jnp.concatenate([y_ref[:, :d_model] + d_ref[...] * ue, y_ref[:, d_model:] + d_ref[...] * uo], axis=0)
    rc = y.shape[0] // FFN_CHAINS
    gy = [jax.nn.gelu(y[c * rc:(c + 1) * rc]).astype(BF16) for c in range(FFN_CHAINS)]
    h = [jnp.dot(v, wglu_ref[...], preferred_element_type=F32) + bglu_ref[...] for v in gy]
    r = jnp.concatenate([_rms(v[:, :d_model] * _sigmoid(v[:, d_model:]), g3_ref[...]) for v in h], axis=0)
    o_ref[0] = jnp.concatenate([xe + r[:rows], xo + r[rows:]], axis=1)

    @pl.when(tile == pl.num_programs(1) - 1)
    def _():
        final_ref[...] = state_ref[...]


def _s5_layer(x, g2, g3, win, wout, wskip, lam2, d, wglu, bglu, init, layer, *, n_seq, n_t):
    nb_rows, s2, dm2 = x.shape
    dm = dm2 // 2
    n_blocks = win.shape[1]
    assert n_blocks == SUBLANE
    rows = n_seq * n_t
    n_tiles = s2 // rows
    assert n_seq == 1 or n_tiles == 1
    row = pl.BlockSpec((1, rows, dm2), lambda b, i: (b, i, 0))
    vec = _const_spec((1, dm))
    state = pl.BlockSpec((n_seq, SUBLANE, 2 * HALF), lambda b, i: (b, 0, 0))
    slab = pltpu.VMEM((N_SLABS, n_blocks * _slab_pitch(rows), LANE), F32)
    return pl.pallas_call(
        functools.partial(_s5_mixer_kernel, n_blocks=n_blocks, n_seq=n_seq, n_t=n_t, d_model=dm),
        grid=(nb_rows, n_tiles),
        in_specs=[row, vec, vec, _layer_spec(win.shape, layer), _layer_spec(wout.shape, layer),
                  _layer_spec(wskip.shape, layer), _layer_spec(lam2.shape, layer), vec,
                  _layer_spec(wglu.shape, layer), _const_spec((1, 2 * dm)),
                  pl.BlockSpec(state.block_shape, state.index_map, pipeline_mode=pl.Buffered(1))],
        out_specs=[row, state],
        out_shape=[jax.ShapeDtypeStruct(x.shape, F32),
                   jax.ShapeDtypeStruct(init.shape, F32)],
        scratch_shapes=[pltpu.VMEM((n_seq, SUBLANE, 2 * HALF), F32), pltpu.VMEM((rows, dm2), F32), slab],
        compiler_params=_params(2),
        name="s5_mixer",
    )(x, g2, g3, win, wout, wskip, lam2, d, wglu, bglu, init)


def _state_to_rows(re, im):
    b = re.shape[0]
    return jnp.concatenate([re.reshape(b, -1, HALF), im.reshape(b, -1, HALF)], axis=-1)


def _rows_to_state(s, n_groups):
    b = s.shape[0]
    return (s[:, :, :HALF].reshape(b, n_groups, STATE_DIM), s[:, :, HALF:].reshape(b, n_groups, STATE_DIM))


def _t5_bucket_np(dist):
    n = np.maximum(dist, 0)
    max_exact = NUM_BUCKETS // 2
    nf = np.maximum(n, 1).astype(np.float32)
    large = max_exact + (np.log(nf / np.float32(max_exact)) / np.float32(math.log(MAX_DISTANCE / max_exact))
                         * np.float32(NUM_BUCKETS - max_exact)).astype(np.int32)
    large = np.minimum(large, NUM_BUCKETS - 1)
    return np.where(n < max_exact, n, large).astype(np.int32)


def _bucket_table(n_q, n_k, q_offset, n_q_pad, n_k_pad, first_key=0):
    dist = (np.arange(n_q)[:, None] + q_offset) - np.arange(n_k)[None, :]
    valid = (dist >= 0) & (dist < WINDOW) & (np.arange(n_k)[None, :] >= first_key)
    table = np.full((n_q_pad, n_k_pad), -1, np.int32)
    table[:n_q, :n_k] = np.where(valid, _t5_bucket_np(dist), -1)
    return table


def _bias_kernel(rel_ref, idx_ref, o_ref):
    h = pl.program_id(0)
    idx = idx_ref[...]
    acc = jnp.full(idx.shape, NEG_INF, F32)
    for b in range(NUM_BUCKETS):
        acc = jnp.where(idx == b, rel_ref[b, h] * LOG2_E, acc)
    o_ref[0] = acc


def _bias_planes(rel_bias, table):
    n_heads = rel_bias.shape[1]
    r, c = table.shape
    return pl.pallas_call(
        _bias_kernel,
        grid=(n_heads,),
        in_specs=[pl.BlockSpec(memory_space=pltpu.SMEM), pl.BlockSpec((r, c), lambda h: (0, 0))],
        out_specs=pl.BlockSpec((1, r, c), lambda h: (h, 0, 0)),
        out_shape=jax.ShapeDtypeStruct((n_heads, r, c), F32),
        compiler_params=_params(),
        name="attn_bias",
    )(rel_bias, jnp.asarray(table))


def _kv_head_rows(kv, g):
    lane = lax.broadcasted_iota(jnp.int32, kv.shape, 1)
    swapped = pltpu.roll(kv, HEAD_DIM, axis=1)
    even, odd = (kv, swapped) if g == 0 else (swapped, kv)
    return jnp.where(lane < HEAD_DIM, even, 0.0), jnp.where(lane >= HEAD_DIM, odd, 0.0)


def _pair_layout_t(planes, n_kv, n_q):
    h, _, n_k = planes.shape
    pairs = h // n_kv // 2
    p = planes[:, :n_q].reshape(n_kv, pairs, 2, n_q, n_k).transpose(0, 2, 4, 1, 3)
    return p.reshape(n_kv, 2 * n_k, pairs * n_q)


def _sink_cols(sinks, n_kv, n_q):
    h = sinks.shape[0]
    pairs = h // n_kv // 2
    s = sinks.reshape(n_kv, pairs, 2).transpose(0, 2, 1)[..., None]
    return jnp.broadcast_to(s, (n_kv, 2, pairs, n_q)).reshape(n_kv, 2, pairs * n_q)


def _attend_t(chains):
    scores = []
    for q_t, k_even, k_odd, _, _, bias, _ in chains:
        ke = jnp.concatenate([k_even, k_odd], axis=0).astype(BF16)
        scores.append(jnp.dot(ke, q_t, preferred_element_type=F32) + bias)
    weights = []
    for s, chain in zip(scores, chains):
        sink = chain[6]
        n_k = s.shape[0] // 2
        probs, inv = [], []
        for par in range(2):
            sp = s[par * n_k:(par + 1) * n_k]
            sk = sink[par:par + 1] * LOG2_E
            mx = jnp.maximum(jnp.max(sp, axis=0, keepdims=True), sk)
            p = jnp.exp2(sp - mx)
            den = jnp.sum(p, axis=0, keepdims=True) + jnp.exp2(sk - mx)
            probs.append(p.astype(BF16))
            inv.append(1.0 / den)
        weights.append((jnp.concatenate(probs, axis=0), inv))
    outs = []
    for (p_t, inv), chain in zip(weights, chains):
        ve = jnp.concatenate([chain[3], chain[4]], axis=0).astype(BF16)
        o = lax.dot_general(ve, p_t, (((0,), (0,)), ((), ())), preferred_element_type=F32)
        row = lax.broadcasted_iota(jnp.int32, o.shape, 0)
        outs.append(o * jnp.where(row < HEAD_DIM, inv[0], inv[1]))
    return outs


def _prompt_attn_kernel(x_ref, kvc_ref, kvp_ref, bias_ref, sink_ref, g2_ref, g3_ref, wq_ref, bq_ref,
                        wo_ref, bo_ref, o_ref, *, n_kv, n_sub, steps_per_seq, scale):
    blk = WINDOW
    x = x_ref[...]
    u = _rms(x, g2_ref[...]).astype(BF16)
    q_t = lax.dot_general(wq_ref[...], u, (((1,), (1,)), ((), ())), preferred_element_type=F32)
    q_t = ((q_t + jnp.concatenate([bq_ref[...]] * n_sub, axis=1)) * scale).astype(BF16)
    kv = jnp.concatenate([kvp_ref[...], kvc_ref[...]], axis=0)
    pairs = q_t.shape[0] // LANE // n_kv
    first = jnp.where(pl.program_id(0) % steps_per_seq == 0, 1, 0)
    k_rows = [_kv_head_rows(kv[:, :n_kv * HEAD_DIM], g) for g in range(n_kv)]
    v_rows = [_kv_head_rows(kv[:, n_kv * HEAD_DIM:], g) for g in range(n_kv)]
    chains = []
    for s in range(n_sub):
        keys = slice(s * blk, (s + 2) * blk)
        for g in range(n_kv):
            qp = jnp.concatenate([q_t[(g * pairs + i) * LANE:(g * pairs + i + 1) * LANE, s * blk:(s + 1) * blk]
                                  for i in range(pairs)], axis=1)
            bias = bias_ref[first, g] if s == 0 else bias_ref[0, g]
            chains.append((qp, k_rows[g][0][keys], k_rows[g][1][keys], v_rows[g][0][keys],
                           v_rows[g][1][keys], bias, sink_ref[g]))
    outs = _attend_t(chains)
    o_t = jnp.concatenate(
        [jnp.concatenate([outs[s * n_kv + g][:, i * blk:(i + 1) * blk] for s in range(n_sub)], axis=1)
         for g in range(n_kv) for i in range(pairs)], axis=0).astype(BF16)
    a = lax.dot_general(o_t, wo_ref[...], (((0,), (0,)), ((), ())), preferred_element_type=F32) + bo_ref[...]
    o_ref[...] = x + _rms(a, g3_ref[...])


def _prompt_attention(x, kv, bias, sink, g2, g3, wq_t, bq_col, wo, bo, layer, *, blocks_per_seq, n_kv):
    t, d = x.shape
    blk = WINDOW
    n_sub = ATTN_BLOCKS_PER_STEP
    assert n_kv * HEAD_DIM == LANE and blocks_per_seq % n_sub == 0
    row = pl.BlockSpec((n_sub * blk, d), lambda i: (i, 0))
    kvw = kv.shape[1]
    return pl.pallas_call(
        functools.partial(_prompt_attn_kernel, n_kv=n_kv, n_sub=n_sub,
                          steps_per_seq=blocks_per_seq // n_sub, scale=QUERY_SCALE),
        grid=(t // (n_sub * blk),),
        in_specs=[row,
                  pl.BlockSpec((n_sub * blk, kvw), lambda i: (i, 0)),
                  pl.BlockSpec((blk, kvw), lambda i: (jnp.maximum(i * n_sub - 1, 0), 0)),
                  _const_spec(bias.shape),
                  _const_spec(sink.shape), _const_spec((1, d)), _const_spec((1, d)),
                  _layer_spec(wq_t.shape, layer), _const_spec(bq_col.shape),
                  _layer_spec(wo.shape, layer), _const_spec((1, d))],
        out_specs=row,
        out_shape=jax.ShapeDtypeStruct((t, d), F32),
        compiler_params=_params(),
        name="prompt_attention",
    )(x, kv, kv, bias, sink, g2, g3, wq_t, bq_col, wo, bo)


def _sample_attn_kernel(q_ref, kc_ref, vc_ref, kvn_ref, bias_ref, sink_ref, o_ref, *, n_kv, scale):
    half = n_kv * HEAD_DIM
    chains = []
    for b in range(q_ref.shape[0]):
        k = jnp.concatenate([kc_ref[b], kvn_ref[b, :, :half]], axis=0)
        v = jnp.concatenate([vc_ref[b], kvn_ref[b, :, half:]], axis=0)
        for g in range(n_kv):
            chains.append(((q_ref[b, g] * scale).astype(BF16), *_kv_head_rows(k, g), *_kv_head_rows(v, g),
                           bias_ref[g], sink_ref[g]))
    outs = _attend_t(chains)
    for b in range(q_ref.shape[0]):
        for g in range(n_kv):
            o_ref[b, g] = outs[b * n_kv + g]


def _sample_attention(q_t, k_cache, v_cache, kv_new, bias, sink, *, n_kv):
    nb, _, _, n_l = q_t.shape
    n_past = k_cache.shape[1]
    bb = SUBLANE
    return pl.pallas_call(
        functools.partial(_sample_attn_kernel, n_kv=n_kv, scale=QUERY_SCALE),
        grid=(nb // bb,),
        in_specs=[pl.BlockSpec((bb, n_kv, LANE, n_l), lambda i: (i, 0, 0, 0)),
                  pl.BlockSpec((bb, n_past, LANE), lambda i: (i, 0, 0)),
                  pl.BlockSpec((bb, n_past, LANE), lambda i: (i, 0, 0)),
                  pl.BlockSpec((bb,) + kv_new.shape[1:], lambda i: (i, 0, 0)),
                  _const_spec(bias.shape), _const_spec(sink.shape)],
        out_specs=pl.BlockSpec((bb, n_kv, LANE, n_l), lambda i: (i, 0, 0, 0)),
        out_shape=jax.ShapeDtypeStruct(q_t.shape, F32),
        compiler_params=_params(),
        name="sample_attention",
    )(q_t, k_cache, v_cache, kv_new, bias, sink)


def kernel(x_prompt, x_sample, state_ssm_re, state_ssm_im, cache_win_k, cache_win_v, norm_g, ffn1_w_gu, ffn1_w_down, ffn2_w_gu, ffn2_w_down, ssm_lambda_re, ssm_lambda_im, ssm_log_dt, ssm_b_re, ssm_b_im, ssm_c_re, ssm_c_im, ssm_d, ssm_w_glu, ssm_b_glu, kv_norm_g, w_kv, b_kv, attn_w_q, attn_b_q, attn_sinks, attn_w_o, attn_b_o, rel_bias):
    bsz, seq, dm = x_prompt.shape
    dec_b, dec_s, _ = x_sample.shape
    depth = norm_g.shape[0]
    n_a = ssm_lambda_re.shape[0]
    n_groups = ssm_lambda_re.shape[1]
    n_heads = attn_sinks.shape[1]
    n_kv = cache_win_k.shape[2]
    n_past = cache_win_k.shape[1]
    pairs = n_heads // n_kv // 2
    n_k_pad = n_past + SUBLANE
    assert seq % ROW_TILE == 0 and seq % WINDOW == 0 and dec_s % 2 == 0
    assert dec_b * dec_s == ROW_TILE and dec_s <= SUBLANE and n_past % SUBLANE == 0
    assert dec_b % (SUBLANE * SAMPLE_SPLIT) == 0
    assert n_kv == 2 and dm == n_heads * HEAD_DIM and n_groups == SUBLANE * GROUPS_PER_BLOCK

    bf = lambda w: w.astype(BF16)
    row = lambda v: v.reshape(1, -1)
    s5_w = _discretize(ssm_lambda_re, ssm_lambda_im, ssm_log_dt, ssm_b_re, ssm_b_im, ssm_c_re, ssm_c_im)
    ffn_w = (bf(ffn1_w_gu[0])[None], bf(ffn1_w_down[0])[None])
    wglu, wkv, wq, wo = bf(ssm_w_glu), bf(w_kv)[None], bf(attn_w_q), bf(attn_w_o)

    tab_p = _bucket_table(WINDOW, 2 * WINDOW, WINDOW, WINDOW, 2 * WINDOW)
    tab_p0 = _bucket_table(WINDOW, 2 * WINDOW, WINDOW, WINDOW, 2 * WINDOW, first_key=WINDOW)
    tab_s = _bucket_table(dec_s, n_past + dec_s, n_past, SUBLANE, n_k_pad)
    wq_t = wq.transpose(0, 2, 1)
    bias_p = jnp.stack([_pair_layout_t(_bias_planes(rel_bias, tab_p), n_kv, WINDOW),
                        _pair_layout_t(_bias_planes(rel_bias, tab_p0), n_kv, WINDOW)])
    bias_s = _pair_layout_t(_bias_planes(rel_bias, tab_s), n_kv, dec_s)

    xp = x_prompt.reshape(bsz * seq, dm)
    xs = x_sample.reshape(dec_b * dec_s, dm)
    zero_state = jnp.zeros((bsz, SUBLANE, 2 * HALF), F32)

    ends_p, ends_s = [], []
    kv_p = kv_s = kv_new = None
    k_cache = cache_win_k.reshape(dec_b, n_past, n_kv * HEAD_DIM)
    v_cache = cache_win_v.reshape(dec_b, n_past, n_kv * HEAD_DIM)
    half = n_kv * HEAD_DIM
    for l in range(depth):
        g = [row(norm_g[l, i]) for i in range(norm_g.shape[1])]
        pair_io = dict(pair_in=0 < l < n_a, pair_out=l < n_a)
        xp, xs, *ffn_w = _ffn(xp, xs, g[0], g[1], *ffn_w, **pair_io, cast_next=(ffn2_w_gu, ffn2_w_down, l))
        if l < n_a:
            common = (g[2], g[3], *s5_w, row(ssm_d[l]), wglu, row(ssm_b_glu[l]))
            xp3, fin_p = _s5_layer(xp.reshape(bsz, seq // 2, 2 * dm), *common, zero_state, l,
                                   n_seq=1, n_t=ROW_TILE // 2)
            xs3, fin_s = _s5_layer(xs.reshape(SAMPLE_SPLIT, dec_b * dec_s // (2 * SAMPLE_SPLIT), 2 * dm), *common,
                                   _state_to_rows(state_ssm_re[l], state_ssm_im[l]), l,
                                   n_seq=dec_b // SAMPLE_SPLIT, n_t=dec_s // 2)
            xp, xs = xp3.reshape(bsz * seq // 2, 2 * dm), xs3.reshape(dec_b * dec_s // 2, 2 * dm)
            ends_p.append(_rows_to_state(fin_p, n_groups))
            ends_s.append(_rows_to_state(fin_s, n_groups))
        else:
            bl = l - n_a
            bq_l, bo_l = row(attn_b_q[bl]), row(attn_b_o[bl])
            bq_col = jnp.broadcast_to(attn_b_q[bl][:, None], (dm, WINDOW))
            xp = _prompt_attention(xp, kv_p, bias_p, _sink_cols(attn_sinks[bl], n_kv, WINDOW), g[2], g[3],
                                   wq_t, bq_col, wo, bo_l, bl, blocks_per_seq=seq // WINDOW, n_kv=n_kv)
            q = _norm_proj(xs, g[2], wq, bq_l, bl)
            q = q.reshape(dec_b, dec_s, n_kv, pairs, LANE).transpose(0, 2, 4, 3, 1)
            q = q.reshape(dec_b, n_kv, LANE, pairs * dec_s)
            o = _sample_attention(q, k_cache, v_cache, kv_new, bias_s, _sink_cols(attn_sinks[bl], n_kv, dec_s), n_kv=n_kv)
            o = o.reshape(dec_b, n_kv, LANE, pairs, dec_s).transpose(0, 4, 1, 3, 2).reshape(dec_b * dec_s, dm)
            xs = _out_proj_residual(xs, o, g[3], wo, bo_l, bl)
        pair_io = dict(pair_in=l < n_a, pair_out=l < n_a - 1)
        cast_next = (ffn1_w_gu, ffn1_w_down, l + 1) if l + 1 < depth else None
        if l == n_a - 1:
            xp, xs, kv_p, kv_s, *ffn_w = _ffn(xp, xs, g[4], g[5], *ffn_w, **pair_io, cast_next=cast_next,
                                              kv=(row(kv_norm_g), wkv, row(b_kv)))
            kv_s = kv_s.reshape(dec_b, dec_s, 2 * half)
            kv_new = jnp.pad(kv_s, ((0, 0), (0, SUBLANE - dec_s), (0, 0)))
        else:
            xp, xs, *ffn_w = _ffn(xp, xs, g[4], g[5], *ffn_w, **pair_io, cast_next=cast_next)

    kv_p3 = kv_p.reshape(bsz, seq, 2 * half)
    new_k_p = kv_p3[:, -WINDOW:, :half].reshape(bsz, WINDOW, n_kv, HEAD_DIM)
    new_v_p = kv_p3[:, -WINDOW:, half:].reshape(bsz, WINDOW, n_kv, HEAD_DIM)
    new_k_s = jnp.concatenate([k_cache, kv_s[:, :, :half]], axis=1)[:, -WINDOW:].reshape(dec_b, WINDOW, n_kv, HEAD_DIM)
    new_v_s = jnp.concatenate([v_cache, kv_s[:, :, half:]], axis=1)[:, -WINDOW:].reshape(dec_b, WINDOW, n_kv, HEAD_DIM)
    return (xp.reshape(bsz, seq, dm), xs.reshape(dec_b, dec_s, dm),
            jnp.stack([e[0] for e in ends_p]), jnp.stack([e[1] for e in ends_p]),
            new_k_p, new_v_p,
            jnp.stack([e[0] for e in ends_s]), jnp.stack([e[1] for e in ends_s]),
            new_k_s, new_v_s)
```

```python
import functools
import math

import numpy as np
import jax
import jax.numpy as jnp
from jax import lax
from jax.experimental import pallas as pl
from jax.experimental.pallas import tpu as pltpu

F32 = jnp.float32
BF16 = jnp.bfloat16

LANE = 128
SUBLANE = 8
VMEM_LIMIT_BYTES = 56 * 1024 * 1024

RMS_EPS = 1e-6
GROUP_SIZE = 16
STATE_DIM = 64
HEAD_DIM = 64
WINDOW = 128
NUM_BUCKETS = 32
MAX_DISTANCE = WINDOW
GROUPS_PER_BLOCK = LANE // GROUP_SIZE
HALF = GROUPS_PER_BLOCK * STATE_DIM
N_SLABS = 2 * HALF // LANE
N_PAIRS = N_SLABS // 2
ROW_TILE = 512
SAMPLE_SPLIT = 2
FFN_CHAINS = 2
ATTN_BLOCKS_PER_STEP = 4
NEG_INF = float("-inf")
LOG2_E = math.log2(math.e)
QUERY_SCALE = LOG2_E / math.sqrt(HEAD_DIM)


def _params(n_axes=1):
    return pltpu.CompilerParams(dimension_semantics=("arbitrary",) * n_axes,
                                vmem_limit_bytes=VMEM_LIMIT_BYTES)


def _const_spec(shape):
    nd = len(shape)
    return pl.BlockSpec(shape, lambda *_: (0,) * nd, pipeline_mode=pl.Buffered(1))


def _layer_spec(shape, layer):
    nd = len(shape)
    return pl.BlockSpec((None,) + tuple(shape[1:]), lambda *_: (layer,) + (0,) * (nd - 1),
                        pipeline_mode=pl.Buffered(1))


def _rms(x, g):
    ms = jnp.mean(x * x, axis=-1, keepdims=True)
    return x * lax.rsqrt(ms + RMS_EPS) * g


def _aligned(row):
    return row if isinstance(row, int) else pl.multiple_of(row, SUBLANE)


def _sigmoid(x):
    return 1.0 / (1.0 + jnp.exp(-x))


def _cast_chunk(rows, n_steps):
    tile = 2 * SUBLANE
    for chunk in range(tile, rows + 1, tile):
        if rows % chunk == 0 and rows // chunk <= n_steps:
            return chunk
    raise ValueError((rows, n_steps))


def _ffn_kernel(*refs, d_ff, pair_in, pair_out, cast_steps, with_kv):
    refs = list(refs)
    x_ref, xr_ref, ga_ref, gb_ref, wgu_ref, wd_ref = refs[:6]
    del refs[:6]
    kvg_ref, wkv_ref, bkv_ref = (refs.pop(0), refs.pop(0), refs.pop(0)) if with_kv else (None,) * 3
    cast_src = (refs.pop(0), refs.pop(0)) if cast_steps else ()
    o_ref, or_ref = refs.pop(0), refs.pop(0)
    kv_ref, kvr_ref = (refs.pop(0), refs.pop(0)) if with_kv else (None, None)
    cast_dst = (refs.pop(0), refs.pop(0)) if cast_steps else ()
    slab_ref = refs
    for src, dst, steps in zip(cast_src, cast_dst, cast_steps or ()):
        @pl.when(pl.program_id(0) < steps)
        def _(src=src, dst=dst):
            dst[...] = src[...].astype(BF16)
    args = (ga_ref, gb_ref, wgu_ref, wd_ref, kvg_ref, wkv_ref, bkv_ref, slab_ref)
    _ffn_tile(x_ref, o_ref, kv_ref, *args, d_ff=d_ff, pair_in=pair_in, pair_out=pair_out)

    @pl.when(pl.program_id(0) == pl.num_programs(0) - 1)
    def _():
        _ffn_tile(xr_ref, or_ref, kvr_ref, *args, d_ff=d_ff, pair_in=pair_in, pair_out=pair_out)


def _ffn_tile(x_ref, o_ref, kv_ref, ga_ref, gb_ref, wgu_ref, wd_ref, kvg_ref, wkv_ref, bkv_ref, slab_ref,
              *, d_ff, pair_in, pair_out):
    d = ga_ref.shape[1]
    if pair_in:
        x = jnp.concatenate([x_ref[:, :d], x_ref[:, d:]], axis=0)
    else:
        x = x_ref[...]
    half = x.shape[0] // 2
    rc = x.shape[0] // FFN_CHAINS
    xc = [x[c * rc:(c + 1) * rc] for c in range(FFN_CHAINS)]
    xn = [_rms(v, ga_ref[...]).astype(BF16) for v in xc]
    gu = [jnp.dot(v, wgu_ref[...], preferred_element_type=F32) for v in xn]
    h = [(v[:, :d_ff] * _sigmoid(v[:, :d_ff]) * v[:, d_ff:]).astype(BF16) for v in gu]
    y = [jnp.dot(v, wd_ref[...], preferred_element_type=F32) for v in h]
    out = jnp.concatenate([v + 0.5 * _rms(w, gb_ref[...]) for v, w in zip(xc, y)], axis=0)
    if pair_in == pair_out:
        o_ref[...] = jnp.concatenate([out[:half], out[half:]], axis=1) if pair_in else out
    else:
        (slab,) = slab_ref
        even = pl.ds(0, half, stride=2)
        odd = pl.ds(1, half, stride=2)
        for j in range(d // LANE):
            lanes = slice(j * LANE, (j + 1) * LANE)
            if pair_out:
                slab[j] = out[:, lanes]
                o_ref[:, lanes] = slab[j, even, :]
                o_ref[:, d + j * LANE:d + (j + 1) * LANE] = slab[j, odd, :]
            else:
                slab[j, even, :] = out[:half, lanes]
                slab[j, odd, :] = out[half:, lanes]
                o_ref[:, lanes] = slab[j]
    if kv_ref is not None:
        assert not pair_out
        kv_ref[...] = jnp.dot(_rms(o_ref[...], kvg_ref[...]).astype(BF16), wkv_ref[...],
                              preferred_element_type=F32) + bkv_ref[...]


def _ffn(x, x_rider, ga, gb, wgu, wd, pair_in=False, pair_out=False, cast_next=None, kv=None):
    d = ga.shape[1]
    t = x.shape[0] * (2 if pair_in else 1)
    d_ff = wd.shape[1]
    tm = ROW_TILE
    n_steps = t // tm
    assert t % tm == 0 and x_rider.shape[0] * (2 if pair_in else 1) == tm

    def tile(paired):
        return (tm // 2, 2 * d) if paired else (tm, d)

    def rows(paired):
        return pl.BlockSpec(tile(paired), lambda i: (i, 0))

    def rider(shape):
        return pl.BlockSpec(shape, lambda i: (0, 0))

    in_specs = [rows(pair_in), _const_spec(tile(pair_in)), _const_spec((1, d)), _const_spec((1, d)),
                _layer_spec(wgu.shape, 0), _layer_spec(wd.shape, 0)]
    out_specs = [rows(pair_out), rider(tile(pair_out))]
    out_shape = [jax.ShapeDtypeStruct((t // 2, 2 * d) if pair_out else (t, d), F32),
                 jax.ShapeDtypeStruct(tile(pair_out), F32)]
    operands = [x, x_rider, ga, gb, wgu, wd]
    if kv is not None:
        kv_g, kv_w, kv_b = kv
        n_kv_out = kv_w.shape[-1]
        in_specs += [_const_spec((1, d)), _layer_spec(kv_w.shape, 0), _const_spec((1, n_kv_out))]
        out_specs += [pl.BlockSpec((tm, n_kv_out), lambda i: (i, 0)), rider((tm, n_kv_out))]
        out_shape += [jax.ShapeDtypeStruct((t, n_kv_out), F32), jax.ShapeDtypeStruct((tm, n_kv_out), F32)]
        operands += [kv_g, kv_w, kv_b]
    cast_steps = None
    if cast_next is not None:
        *next_w, layer = cast_next
        cast_steps = []
        for w in next_w:
            chunk = _cast_chunk(w.shape[1], n_steps)
            last = w.shape[1] // chunk - 1
            cast_steps.append(last + 1)
            in_specs.append(pl.BlockSpec((None, chunk, w.shape[2]),
                                         lambda i, last=last: (layer, jnp.minimum(i, last), 0)))
            out_specs.append(pl.BlockSpec((None, chunk, w.shape[2]),
                                          lambda i, last=last: (0, jnp.minimum(i, last), 0)))
            out_shape.append(jax.ShapeDtypeStruct((1,) + w.shape[1:], BF16))
            operands.append(w)
    return pl.pallas_call(
        functools.partial(_ffn_kernel, d_ff=d_ff, pair_in=pair_in, pair_out=pair_out, cast_steps=cast_steps,
                          with_kv=kv is not None),
        grid=(n_steps,),
        in_specs=in_specs,
        out_specs=out_specs,
        out_shape=out_shape,
        scratch_shapes=[pltpu.VMEM((d // LANE, tm, LANE), F32)] if pair_in != pair_out else [],
        compiler_params=_params(),
        name="ffn",
    )(*operands)


def _proj_kernel(x_ref, g_ref, w_ref, b_ref, o_ref):
    xn = _rms(x_ref[...], g_ref[...]).astype(BF16)
    o_ref[...] = jnp.dot(xn, w_ref[...], preferred_element_type=F32) + b_ref[...]


def _norm_proj(x, g, w, b, layer):
    t, d = x.shape
    n = w.shape[-1]
    tm = min(ROW_TILE, t)
    return pl.pallas_call(
        _proj_kernel,
        grid=(t // tm,),
        in_specs=[pl.BlockSpec((tm, d), lambda i: (i, 0)), _const_spec((1, d)),
                  _layer_spec(w.shape, layer), _const_spec((1, n))],
        out_specs=pl.BlockSpec((tm, n), lambda i: (i, 0)),
        out_shape=jax.ShapeDtypeStruct((t, n), F32),
        compiler_params=_params(),
        name="norm_proj",
    )(x, g, w, b)


def _out_proj_kernel(x_ref, o_ref_in, g_ref, w_ref, b_ref, out_ref):
    a = jnp.dot(o_ref_in[...].astype(BF16), w_ref[...], preferred_element_type=F32) + b_ref[...]
    out_ref[...] = x_ref[...] + _rms(a, g_ref[...])


def _out_proj_residual(x, o, g, w, b, layer):
    t, d = x.shape
    tm = min(ROW_TILE, t)
    row = pl.BlockSpec((tm, d), lambda i: (i, 0))
    return pl.pallas_call(
        _out_proj_kernel,
        grid=(t // tm,),
        in_specs=[row, pl.BlockSpec((tm, o.shape[1]), lambda i: (i, 0)), _const_spec((1, d)),
                  _layer_spec(w.shape, layer), _const_spec((1, d))],
        out_specs=row,
        out_shape=jax.ShapeDtypeStruct((t, d), F32),
        compiler_params=_params(),
        name="out_proj_residual",
    )(x, o, g, w, b)


def _lam_bar(lam_re, lam_im, log_dt):
    dt = jnp.exp(log_dt)
    mag = jnp.exp(lam_re * dt)
    return mag * jnp.cos(lam_im * dt), mag * jnp.sin(lam_im * dt)


def _discretize_kernel(lre_ref, lim_ref, ldt_ref, lre_col_ref, lim_col_ref, ldt_col_ref,
                       bre_ref, bim_ref, cre_ref, cim_ref, win_ref, wout_ref, wskip_ref, lam2_ref):
    a = lre_ref[0, 0]
    b = lim_ref[0, 0]
    lbr, lbi = _lam_bar(a, b, ldt_ref[0, 0])
    den = a * a + b * b
    cr = ((lbr - 1.0) * a + lbi * b) / den
    ci = (lbi * a - (lbr - 1.0) * b) / den
    bre = bre_ref[0, 0]
    bim = bim_ref[0, 0]
    wbr = cr * bre - ci * bim
    wbi = cr * bim + ci * bre
    lwbr = lbr * wbr - lbi * wbi
    lwbi = lbr * wbi + lbi * wbr
    win_ref[0, 0, :LANE, :HALF] = lwbr.astype(BF16)
    win_ref[0, 0, :LANE, HALF:] = lwbi.astype(BF16)
    win_ref[0, 0, LANE:, :HALF] = wbr.astype(BF16)
    win_ref[0, 0, LANE:, HALF:] = wbi.astype(BF16)
    lam2_ref[0, 0, :, :HALF] = lbr * lbr - lbi * lbi
    lam2_ref[0, 0, :, HALF:] = 2.0 * lbr * lbi
    lcr, lci = _lam_bar(lre_col_ref[0, 0], lim_col_ref[0, 0], ldt_col_ref[0, 0])
    cre = cre_ref[0, 0]
    cim = cim_ref[0, 0]
    c1r = cre * lcr - cim * lci
    c1i = cre * lci + cim * lcr
    wout_ref[0, 0, :HALF, :LANE] = c1r.astype(BF16)
    wout_ref[0, 0, HALF:, :LANE] = (-c1i).astype(BF16)
    wout_ref[0, 0, :HALF, LANE:] = (c1r * lcr - c1i * lci).astype(BF16)
    wout_ref[0, 0, HALF:, LANE:] = (-(c1r * lci + c1i * lcr)).astype(BF16)
    def through(br, bi):
        return (jnp.dot(br, cre, precision=lax.Precision.HIGHEST, preferred_element_type=F32)
                - jnp.dot(bi, cim, precision=lax.Precision.HIGHEST, preferred_element_type=F32)).astype(BF16)

    cb = through(wbr, wbi)
    wskip_ref[0, 0, :LANE, :LANE] = cb
    wskip_ref[0, 0, :LANE, LANE:] = through(lwbr, lwbi)
    wskip_ref[0, 0, LANE:, :LANE] = jnp.zeros((LANE, LANE), BF16)
    wskip_ref[0, 0, LANE:, LANE:] = cb


def _block_diag_in(b):
    nl, g, p, h = b.shape
    nb = g // GROUPS_PER_BLOCK
    bt = b.transpose(0, 1, 3, 2).reshape(nl, nb, GROUPS_PER_BLOCK, h, p)
    eye = jnp.eye(GROUPS_PER_BLOCK, dtype=b.dtype)
    out = bt[:, :, :, :, None, :] * eye[None, None, :, None, :, None]
    return out.reshape(nl, nb, GROUPS_PER_BLOCK * h, GROUPS_PER_BLOCK * p)


def _block_diag_out(c):
    nl, g, h, p = c.shape
    nb = g // GROUPS_PER_BLOCK
    ct = c.transpose(0, 1, 3, 2).reshape(nl, nb, GROUPS_PER_BLOCK, p, h)
    eye = jnp.eye(GROUPS_PER_BLOCK, dtype=c.dtype)
    out = ct[:, :, :, :, None, :] * eye[None, None, :, None, :, None]
    return out.reshape(nl, nb, GROUPS_PER_BLOCK * p, GROUPS_PER_BLOCK * h)


def _discretize(lam_re, lam_im, log_dt, b_re, b_im, c_re, c_im):
    nl, g, p = lam_re.shape
    nb = g // GROUPS_PER_BLOCK

    def rows(v):
        return v.reshape(nl, nb, 1, HALF)

    def cols(v):
        return v.reshape(nl, nb, HALF, 1)

    def spec(r, c):
        return pl.BlockSpec((1, 1, r, c), lambda l, k: (l, k, 0, 0))

    win, wout, wskip, lam2 = pl.pallas_call(
        _discretize_kernel,
        grid=(nl, nb),
        in_specs=[spec(1, HALF)] * 3 + [spec(HALF, 1)] * 3 + [spec(LANE, HALF)] * 2 + [spec(HALF, LANE)] * 2,
        out_specs=[spec(2 * LANE, 2 * HALF), spec(2 * HALF, 2 * LANE), spec(2 * LANE, 2 * LANE), spec(1, 2 * HALF)],
        out_shape=[jax.ShapeDtypeStruct((nl, nb, 2 * LANE, 2 * HALF), BF16),
                   jax.ShapeDtypeStruct((nl, nb, 2 * HALF, 2 * LANE), BF16),
                   jax.ShapeDtypeStruct((nl, nb, 2 * LANE, 2 * LANE), BF16),
                   jax.ShapeDtypeStruct((nl, nb, 1, 2 * HALF), F32)],
        compiler_params=_params(2),
        name="s5_discretize",
    )(rows(lam_re), rows(lam_im), rows(log_dt), cols(lam_re), cols(lam_im), cols(log_dt),
      _block_diag_in(b_re), _block_diag_in(b_im), _block_diag_out(c_re), _block_diag_out(c_im))
    return win, wout, wskip, lam2.reshape(nl, nb, 2 * HALF)


def _slab_pitch(rows):
    assert rows % SUBLANE == 0
    return rows + SUBLANE // 2


def _split_state(st):
    return (tuple(st[:, p * LANE:(p + 1) * LANE] for p in range(N_PAIRS)),
            tuple(st[:, HALF + p * LANE:HALF + (p + 1) * LANE] for p in range(N_PAIRS)))


def _s5_mixer_kernel(x_ref, g2_ref, g3_ref, win_ref, wout_ref, wskip_ref, lam_ref, d_ref, wglu_ref, bglu_ref,
                     init_ref, o_ref, final_ref, state_ref, y_ref, s_ref,
                     *, n_blocks, n_seq, n_t, d_model):
    tile = pl.program_id(1)
    rows = n_seq * n_t
    pitch = _slab_pitch(rows)

    @pl.when(tile == 0)
    def _():
        state_ref[...] = init_ref[...]

    lre, lim = _split_state(lam_ref[...])
    xe = x_ref[0, :, :d_model]
    xo = x_ref[0, :, d_model:]
    ue = _rms(xe, g2_ref[...])
    uo = _rms(xo, g2_ref[...])
    ueb = ue.astype(BF16)
    uob = uo.astype(BF16)


    u_pair = [jnp.concatenate([ueb[:, k * LANE:(k + 1) * LANE], uob[:, k * LANE:(k + 1) * LANE]], axis=1)
              for k in range(n_blocks)]
    for k in range(n_blocks):
        v = jnp.dot(u_pair[k], win_ref[k], preferred_element_type=F32)
        for j in range(N_SLABS):
            s_ref[j, pl.ds(k * pitch, rows), :] = v[:, j * LANE:(j + 1) * LANE]

    def seq_body(seq, carry):
        xre, xim = _split_state(state_ref[seq])

        def step(t, st):
            xre, xim = st
            pair = pl.ds(seq * n_t + t, SUBLANE, stride=pitch)
            nre, nim = [], []
            for p in range(N_PAIRS):
                vre = s_ref[p, pair, :]
                vim = s_ref[N_PAIRS + p, pair, :]
                s_ref[p, pair, :] = xre[p]
                s_ref[N_PAIRS + p, pair, :] = xim[p]
                nre.append(lre[p] * xre[p] - lim[p] * xim[p] + vre)
                nim.append(lre[p] * xim[p] + lim[p] * xre[p] + vim)
            return tuple(nre), tuple(nim)

        xre, xim = lax.fori_loop(0, n_t, step, (xre, xim), unroll=min(n_t, 8))
        state_ref[seq] = jnp.concatenate(list(xre) + list(xim), axis=1)
        return carry

    if n_seq == 1:
        seq_body(0, 0)
    else:
        lax.fori_loop(0, n_seq, seq_body, 0)

    for k in range(n_blocks):
        xb = jnp.concatenate([s_ref[j, pl.ds(k * pitch, rows), :] for j in range(N_SLABS)], axis=1)
        yk = (jnp.dot(xb.astype(BF16), wout_ref[k], preferred_element_type=F32)
              + jnp.dot(u_pair[k], wskip_ref[k], preferred_element_type=F32))
        y_ref[:, k * LANE:(k + 1) * LANE] = yk[:, :LANE]
        y_ref[:, d_model + k * LANE:d_model + (k + 1) * LANE] = yk[:, LANE:]
    y = jnp.concatenate([y_ref[:, :d_model] + d_ref[...] * ue, y_ref[:, d_model:] + d_ref[...] * uo], axis=0)
    h = jnp.dot(jax.nn.gelu(y).astype(BF16), wglu_ref[...], preferred_element_type=F32) + bglu_ref[...]
    r = _rms(h[:, :d_model] * _sigmoid(h[:, d_model:]), g3_ref[...])
    o_ref[0] = jnp.concatenate([xe + r[:rows], xo + r[rows:]], axis=1)

    @pl.when(tile == pl.num_programs(1) - 1)
    def _():
        final_ref[...] = state_ref[...]


def _s5_layer(x, g2, g3, win, wout, wskip, lam2, d, wglu, bglu, init, layer, *, n_seq, n_t):
    nb_rows, s2, dm2 = x.shape
    dm = dm2 // 2
    n_blocks = win.shape[1]
    assert n_blocks == SUBLANE
    rows = n_seq * n_t
    n_tiles = s2 // rows
    assert n_seq == 1 or n_tiles == 1
    row = pl.BlockSpec((1, rows, dm2), lambda b, i: (b, i, 0))
    vec = _const_spec((1, dm))
    state = pl.BlockSpec((n_seq, SUBLANE, 2 * HALF), lambda b, i: (b, 0, 0))
    slab = pltpu.VMEM((N_SLABS, n_blocks * _slab_pitch(rows), LANE), F32)
    return pl.pallas_call(
        functools.partial(_s5_mixer_kernel, n_blocks=n_blocks, n_seq=n_seq, n_t=n_t, d_model=dm),
        grid=(nb_rows, n_tiles),
        in_specs=[row, vec, vec, _layer_spec(win.shape, layer), _layer_spec(wout.shape, layer),
                  _layer_spec(wskip.shape, layer), _layer_spec(lam2.shape, layer), vec,
                  _layer_spec(wglu.shape, layer), _const_spec((1, 2 * dm)),
                  pl.BlockSpec(state.block_shape, state.index_map, pipeline_mode=pl.Buffered(1))],
        out_specs=[row, state],
        out_shape=[jax.ShapeDtypeStruct(x.shape, F32),
                   jax.ShapeDtypeStruct(init.shape, F32)],
        scratch_shapes=[pltpu.VMEM((n_seq, SUBLANE, 2 * HALF), F32), pltpu.VMEM((rows, dm2), F32), slab],
        compiler_params=_params(2),
        name="s5_mixer",
    )(x, g2, g3, win, wout, wskip, lam2, d, wglu, bglu, init)


def _state_to_rows(re, im):
    b = re.shape[0]
    return jnp.concatenate([re.reshape(b, -1, HALF), im.reshape(b, -1, HALF)], axis=-1)


def _rows_to_state(s, n_groups):
    b = s.shape[0]
    return (s[:, :, :HALF].reshape(b, n_groups, STATE_DIM), s[:, :, HALF:].reshape(b, n_groups, STATE_DIM))


def _t5_bucket_np(dist):
    n = np.maximum(dist, 0)
    max_exact = NUM_BUCKETS // 2
    nf = np.maximum(n, 1).astype(np.float32)
    large = max_exact + (np.log(nf / np.float32(max_exact)) / np.float32(math.log(MAX_DISTANCE / max_exact))
                         * np.float32(NUM_BUCKETS - max_exact)).astype(np.int32)
    large = np.minimum(large, NUM_BUCKETS - 1)
    return np.where(n < max_exact, n, large).astype(np.int32)


def _bucket_table(n_q, n_k, q_offset, n_q_pad, n_k_pad, first_key=0):
    dist = (np.arange(n_q)[:, None] + q_offset) - np.arange(n_k)[None, :]
    valid = (dist >= 0) & (dist < WINDOW) & (np.arange(n_k)[None, :] >= first_key)
    table = np.full((n_q_pad, n_k_pad), -1, np.int32)
    table[:n_q, :n_k] = np.where(valid, _t5_bucket_np(dist), -1)
    return table


def _bias_kernel(rel_ref, idx_ref, o_ref):
    h = pl.program_id(0)
    idx = idx_ref[...]
    acc = jnp.full(idx.shape, NEG_INF, F32)
    for b in range(NUM_BUCKETS):
        acc = jnp.where(idx == b, rel_ref[b, h] * LOG2_E, acc)
    o_ref[0] = acc


def _bias_planes(rel_bias, table):
    n_heads = rel_bias.shape[1]
    r, c = table.shape
    return pl.pallas_call(
        _bias_kernel,
        grid=(n_heads,),
        in_specs=[pl.BlockSpec(memory_space=pltpu.SMEM), pl.BlockSpec((r, c), lambda h: (0, 0))],
        out_specs=pl.BlockSpec((1, r, c), lambda h: (h, 0, 0)),
        out_shape=jax.ShapeDtypeStruct((n_heads, r, c), F32),
        compiler_params=_params(),
        name="attn_bias",
    )(rel_bias, jnp.asarray(table))


def _bias_tiles_kernel(rel_ref, idx_ref, o_ref, *, n_kv, pairs):
    n_var, n_k, n_q = idx_ref.shape
    for v in range(n_var):
        idx = idx_ref[v]
        for g in range(n_kv):
            for pair in range(pairs):
                for par in range(2):
                    h = (g * pairs + pair) * 2 + par
                    acc = jnp.full(idx.shape, NEG_INF, F32)
                    for b in range(NUM_BUCKETS):
                        acc = jnp.where(idx == b, rel_ref[b, h] * LOG2_E, acc)
                    o_ref[v, g, par * n_k:(par + 1) * n_k, pair * n_q:(pair + 1) * n_q] = acc


def _bias_tiles(rel_bias, tables, n_kv):
    n_heads = rel_bias.shape[1]
    pairs = n_heads // n_kv // 2
    idx = jnp.asarray(np.stack([t.T for t in tables]))
    n_var, n_k, n_q = idx.shape
    return pl.pallas_call(
        functools.partial(_bias_tiles_kernel, n_kv=n_kv, pairs=pairs),
        in_specs=[pl.BlockSpec(memory_space=pltpu.SMEM), pl.BlockSpec(memory_space=pltpu.VMEM)],
        out_specs=pl.BlockSpec(memory_space=pltpu.VMEM),
        out_shape=jax.ShapeDtypeStruct((n_var, n_kv, 2 * n_k, pairs * n_q), F32),
        compiler_params=pltpu.CompilerParams(vmem_limit_bytes=VMEM_LIMIT_BYTES),
        name="attn_bias_tiles",
    )(rel_bias, idx)


def _kv_head_rows(kv, g):
    lane = lax.broadcasted_iota(jnp.int32, kv.shape, 1)
    swapped = pltpu.roll(kv, HEAD_DIM, axis=1)
    even, odd = (kv, swapped) if g == 0 else (swapped, kv)
    return jnp.where(lane < HEAD_DIM, even, 0.0), jnp.where(lane >= HEAD_DIM, odd, 0.0)


def _pair_layout_t(planes, n_kv, n_q):
    h, _, n_k = planes.shape
    pairs = h // n_kv // 2
    p = planes[:, :n_q].reshape(n_kv, pairs, 2, n_q, n_k).transpose(0, 2, 4, 1, 3)
    return p.reshape(n_kv, 2 * n_k, pairs * n_q)


def _sink_cols(sinks, n_kv, n_q):
    h = sinks.shape[0]
    pairs = h // n_kv // 2
    s = sinks.reshape(n_kv, pairs, 2).transpose(0, 2, 1)[..., None]
    return jnp.broadcast_to(s, (n_kv, 2, pairs, n_q)).reshape(n_kv, 2, pairs * n_q)


def _attend_t(chains):
    scores = []
    for q_t, k_even, k_odd, _, _, bias, _ in chains:
        ke = jnp.concatenate([k_even, k_odd], axis=0).astype(BF16)
        scores.append(jnp.dot(ke, q_t, preferred_element_type=F32) + bias)
    weights = []
    for s, chain in zip(scores, chains):
        sink = chain[6]
        n_k = s.shape[0] // 2
        probs, inv = [], []
        for par in range(2):
            sp = s[par * n_k:(par + 1) * n_k]
            sk = sink[par:par + 1] * LOG2_E
            mx = jnp.maximum(jnp.max(sp, axis=0, keepdims=True), sk)
            p = jnp.exp2(sp - mx)
            den = jnp.sum(p, axis=0, keepdims=True) + jnp.exp2(sk - mx)
            probs.append(p.astype(BF16))
            inv.append(1.0 / den)
        weights.append((jnp.concatenate(probs, axis=0), inv))
    outs = []
    for (p_t, inv), chain in zip(weights, chains):
        ve = jnp.concatenate([chain[3], chain[4]], axis=0).astype(BF16)
        o = lax.dot_general(ve, p_t, (((0,), (0,)), ((), ())), preferred_element_type=F32)
        row = lax.broadcasted_iota(jnp.int32, o.shape, 0)
        outs.append(o * jnp.where(row < HEAD_DIM, inv[0], inv[1]))
    return outs


def _prompt_attn_kernel(x_ref, kvc_ref, kvp_ref, bias_ref, sink_ref, g2_ref, g3_ref, wq_ref, bq_ref,
                        wo_ref, bo_ref, o_ref, *, n_kv, n_sub, steps_per_seq, scale):
    blk = WINDOW
    x = x_ref[...]
    u = _rms(x, g2_ref[...]).astype(BF16)
    q_t = lax.dot_general(wq_ref[...], u, (((1,), (1,)), ((), ())), preferred_element_type=F32)
    q_t = ((q_t + jnp.concatenate([bq_ref[...]] * n_sub, axis=1)) * scale).astype(BF16)
    kv = jnp.concatenate([kvp_ref[...], kvc_ref[...]], axis=0)
    pairs = q_t.shape[0] // LANE // n_kv
    first = jnp.where(pl.program_id(0) % steps_per_seq == 0, 1, 0)
    k_rows = [_kv_head_rows(kv[:, :n_kv * HEAD_DIM], g) for g in range(n_kv)]
    v_rows = [_kv_head_rows(kv[:, n_kv * HEAD_DIM:], g) for g in range(n_kv)]
    chains = []
    for s in range(n_sub):
        keys = slice(s * blk, (s + 2) * blk)
        for g in range(n_kv):
            qp = jnp.concatenate([q_t[(g * pairs + i) * LANE:(g * pairs + i + 1) * LANE, s * blk:(s + 1) * blk]
                                  for i in range(pairs)], axis=1)
            bias = bias_ref[first, g] if s == 0 else bias_ref[0, g]
            chains.append((qp, k_rows[g][0][keys], k_rows[g][1][keys], v_rows[g][0][keys],
                           v_rows[g][1][keys], bias, sink_ref[g]))
    outs = _attend_t(chains)
    o_t = jnp.concatenate(
        [jnp.concatenate([outs[s * n_kv + g][:, i * blk:(i + 1) * blk] for s in range(n_sub)], axis=1)
         for g in range(n_kv) for i in range(pairs)], axis=0).astype(BF16)
    a = lax.dot_general(o_t, wo_ref[...], (((0,), (0,)), ((), ())), preferred_element_type=F32) + bo_ref[...]
    o_ref[...] = x + _rms(a, g3_ref[...])


def _prompt_attention(x, kv, bias, sink, g2, g3, wq_t, bq_col, wo, bo, layer, *, blocks_per_seq, n_kv):
    t, d = x.shape
    blk = WINDOW
    n_sub = ATTN_BLOCKS_PER_STEP
    assert n_kv * HEAD_DIM == LANE and blocks_per_seq % n_sub == 0
    row = pl.BlockSpec((n_sub * blk, d), lambda i: (i, 0))
    kvw = kv.shape[1]
    return pl.pallas_call(
        functools.partial(_prompt_attn_kernel, n_kv=n_kv, n_sub=n_sub,
                          steps_per_seq=blocks_per_seq // n_sub, scale=QUERY_SCALE),
        grid=(t // (n_sub * blk),),
        in_specs=[row,
                  pl.BlockSpec((n_sub * blk, kvw), lambda i: (i, 0)),
                  pl.BlockSpec((blk, kvw), lambda i: (jnp.maximum(i * n_sub - 1, 0), 0)),
                  _const_spec(bias.shape),
                  _const_spec(sink.shape), _const_spec((1, d)), _const_spec((1, d)),
                  _layer_spec(wq_t.shape, layer), _const_spec(bq_col.shape),
                  _layer_spec(wo.shape, layer), _const_spec((1, d))],
        out_specs=row,
        out_shape=jax.ShapeDtypeStruct((t, d), F32),
        compiler_params=_params(),
        name="prompt_attention",
    )(x, kv, kv, bias, sink, g2, g3, wq_t, bq_col, wo, bo)


def _sample_attn_kernel(q_ref, kc_ref, vc_ref, kvn_ref, bias_ref, sink_ref, o_ref, *, n_kv, scale):
    half = n_kv * HEAD_DIM
    chains = []
    for b in range(q_ref.shape[0]):
        k = jnp.concatenate([kc_ref[b], kvn_ref[b, :, :half]], axis=0)
        v = jnp.concatenate([vc_ref[b], kvn_ref[b, :, half:]], axis=0)
        for g in range(n_kv):
            chains.append(((q_ref[b, g] * scale).astype(BF16), *_kv_head_rows(k, g), *_kv_head_rows(v, g),
                           bias_ref[g], sink_ref[g]))
    outs = _attend_t(chains)
    for b in range(q_ref.shape[0]):
        for g in range(n_kv):
            o_ref[b, g] = outs[b * n_kv + g]


def _sample_attention(q_t, k_cache, v_cache, kv_new, bias, sink, *, n_kv):
    nb, _, _, n_l = q_t.shape
    n_past = k_cache.shape[1]
    bb = SUBLANE
    return pl.pallas_call(
        functools.partial(_sample_attn_kernel, n_kv=n_kv, scale=QUERY_SCALE),
        grid=(nb // bb,),
        in_specs=[pl.BlockSpec((bb, n_kv, LANE, n_l), lambda i: (i, 0, 0, 0)),
                  pl.BlockSpec((bb, n_past, LANE), lambda i: (i, 0, 0)),
                  pl.BlockSpec((bb, n_past, LANE), lambda i: (i, 0, 0)),
                  pl.BlockSpec((bb,) + kv_new.shape[1:], lambda i: (i, 0, 0)),
                  _const_spec(bias.shape), _const_spec(sink.shape)],
        out_specs=pl.BlockSpec((bb, n_kv, LANE, n_l), lambda i: (i, 0, 0, 0)),
        out_shape=jax.ShapeDtypeStruct(q_t.shape, F32),
        compiler_params=_params(),
        name="sample_attention",
    )(q_t, k_cache, v_cache, kv_new, bias, sink)


def kernel(x_prompt, x_sample, state_ssm_re, state_ssm_im, cache_win_k, cache_win_v, norm_g, ffn1_w_gu, ffn1_w_down, ffn2_w_gu, ffn2_w_down, ssm_lambda_re, ssm_lambda_im, ssm_log_dt, ssm_b_re, ssm_b_im, ssm_c_re, ssm_c_im, ssm_d, ssm_w_glu, ssm_b_glu, kv_norm_g, w_kv, b_kv, attn_w_q, attn_b_q, attn_sinks, attn_w_o, attn_b_o, rel_bias):
    bsz, seq, dm = x_prompt.shape
    dec_b, dec_s, _ = x_sample.shape
    depth = norm_g.shape[0]
    n_a = ssm_lambda_re.shape[0]
    n_groups = ssm_lambda_re.shape[1]
    n_heads = attn_sinks.shape[1]
    n_kv = cache_win_k.shape[2]
    n_past = cache_win_k.shape[1]
    pairs = n_heads // n_kv // 2
    n_k_pad = n_past + SUBLANE
    assert seq % ROW_TILE == 0 and seq % WINDOW == 0 and dec_s % 2 == 0
    assert dec_b * dec_s == ROW_TILE and dec_s <= SUBLANE and n_past % SUBLANE == 0
    assert dec_b % (SUBLANE * SAMPLE_SPLIT) == 0
    assert n_kv == 2 and dm == n_heads * HEAD_DIM and n_groups == SUBLANE * GROUPS_PER_BLOCK

    bf = lambda w: w.astype(BF16)
    row = lambda v: v.reshape(1, -1)
    s5_w = _discretize(ssm_lambda_re, ssm_lambda_im, ssm_log_dt, ssm_b_re, ssm_b_im, ssm_c_re, ssm_c_im)
    ffn_w = (bf(ffn1_w_gu[0])[None], bf(ffn1_w_down[0])[None])
    wglu, wkv, wq, wo = bf(ssm_w_glu), bf(w_kv)[None], bf(attn_w_q), bf(attn_w_o)

    tab_p = _bucket_table(WINDOW, 2 * WINDOW, WINDOW, WINDOW, 2 * WINDOW)
    tab_p0 = _bucket_table(WINDOW, 2 * WINDOW, WINDOW, WINDOW, 2 * WINDOW, first_key=WINDOW)
    tab_s = _bucket_table(dec_s, n_past + dec_s, n_past, SUBLANE, n_k_pad)
    wq_t = wq.transpose(0, 2, 1)
    bias_p = _bias_tiles(rel_bias, [tab_p, tab_p0], n_kv)
    bias_s = _pair_layout_t(_bias_planes(rel_bias, tab_s), n_kv, dec_s)

    xp = x_prompt.reshape(bsz * seq, dm)
    xs = x_sample.reshape(dec_b * dec_s, dm)
    zero_state = jnp.zeros((bsz, SUBLANE, 2 * HALF), F32)

    ends_p, ends_s = [], []
    kv_p = kv_s = kv_new = None
    k_cache = cache_win_k.reshape(dec_b, n_past, n_kv * HEAD_DIM)
    v_cache = cache_win_v.reshape(dec_b, n_past, n_kv * HEAD_DIM)
    half = n_kv * HEAD_DIM
    for l in range(depth):
        g = [row(norm_g[l, i]) for i in range(norm_g.shape[1])]
        pair_io = dict(pair_in=0 < l < n_a, pair_out=l < n_a)
        xp, xs, *ffn_w = _ffn(xp, xs, g[0], g[1], *ffn_w, **pair_io, cast_next=(ffn2_w_gu, ffn2_w_down, l))
        if l < n_a:
            common = (g[2], g[3], *s5_w, row(ssm_d[l]), wglu, row(ssm_b_glu[l]))
            xp3, fin_p = _s5_layer(xp.reshape(bsz, seq // 2, 2 * dm), *common, zero_state, l,
                                   n_seq=1, n_t=ROW_TILE // 2)
            xs3, fin_s = _s5_layer(xs.reshape(SAMPLE_SPLIT, dec_b * dec_s // (2 * SAMPLE_SPLIT), 2 * dm), *common,
                                   _state_to_rows(state_ssm_re[l], state_ssm_im[l]), l,
                                   n_seq=dec_b // SAMPLE_SPLIT, n_t=dec_s // 2)
            xp, xs = xp3.reshape(bsz * seq // 2, 2 * dm), xs3.reshape(dec_b * dec_s // 2, 2 * dm)
            ends_p.append(_rows_to_state(fin_p, n_groups))
            ends_s.append(_rows_to_state(fin_s, n_groups))
        else:
            bl = l - n_a
            bq_l, bo_l = row(attn_b_q[bl]), row(attn_b_o[bl])
            bq_col = jnp.broadcast_to(attn_b_q[bl][:, None], (dm, WINDOW))
            xp = _prompt_attention(xp, kv_p, bias_p, _sink_cols(attn_sinks[bl], n_kv, WINDOW), g[2], g[3],
                                   wq_t, bq_col, wo, bo_l, bl, blocks_per_seq=seq // WINDOW, n_kv=n_kv)
            q = _norm_proj(xs, g[2], wq, bq_l, bl)
            q = q.reshape(dec_b, dec_s, n_kv, pairs, LANE).transpose(0, 2, 4, 3, 1)
            q = q.reshape(dec_b, n_kv, LANE, pairs * dec_s)
            o = _sample_attention(q, k_cache, v_cache, kv_new, bias_s, _sink_cols(attn_sinks[bl], n_kv, dec_s), n_kv=n_kv)
            o = o.reshape(dec_b, n_kv, LANE, pairs, dec_s).transpose(0, 4, 1, 3, 2).reshape(dec_b * dec_s, dm)
            xs = _out_proj_residual(xs, o, g[3], wo, bo_l, bl)
        pair_io = dict(pair_in=l < n_a, pair_out=l < n_a - 1)
        cast_next = (ffn1_w_gu, ffn1_w_down, l + 1) if l + 1 < depth else None
        if l == n_a - 1:
            xp, xs, kv_p, kv_s, *ffn_w = _ffn(xp, xs, g[4], g[5], *ffn_w, **pair_io, cast_next=cast_next,
                                              kv=(row(kv_norm_g), wkv, row(b_kv)))
            kv_s = kv_s.reshape(dec_b, dec_s, 2 * half)
            kv_new = jnp.pad(kv_s, ((0, 0), (0, SUBLANE - dec_s), (0, 0)))
        else:
            xp, xs, *ffn_w = _ffn(xp, xs, g[4], g[5], *ffn_w, **pair_io, cast_next=cast_next)

    kv_p3 = kv_p.reshape(bsz, seq, 2 * half)
    new_k_p = kv_p3[:, -WINDOW:, :half].reshape(bsz, WINDOW, n_kv, HEAD_DIM)
    new_v_p = kv_p3[:, -WINDOW:, half:].reshape(bsz, WINDOW, n_kv, HEAD_DIM)
    new_k_s = jnp.concatenate([k_cache, kv_s[:, :, :half]], axis=1)[:, -WINDOW:].reshape(dec_b, WINDOW, n_kv, HEAD_DIM)
    new_v_s = jnp.concatenate([v_cache, kv_s[:, :, half:]], axis=1)[:, -WINDOW:].reshape(dec_b, WINDOW, n_kv, HEAD_DIM)
    return (xp.reshape(bsz, seq, dm), xs.reshape(dec_b, dec_s, dm),
            jnp.stack([e[0] for e in ends_p]), jnp.stack([e[1] for e in ends_p]),
            new_k_p, new_v_p,
            jnp.stack([e[0] for e in ends_s]), jnp.stack([e[1] for e in ends_s]),
            new_k_s, new_v_s)
```

```python
import functools
import math

import numpy as np
import jax
import jax.numpy as jnp
from jax import lax
from jax.experimental import pallas as pl
from jax.experimental.pallas import tpu as pltpu

F32 = jnp.float32
BF16 = jnp.bfloat16

LANE = 128
SUBLANE = 8
VMEM_LIMIT_BYTES = 56 * 1024 * 1024

RMS_EPS = 1e-6
GROUP_SIZE = 16
STATE_DIM = 64
HEAD_DIM = 64
WINDOW = 128
NUM_BUCKETS = 32
MAX_DISTANCE = WINDOW
GROUPS_PER_BLOCK = LANE // GROUP_SIZE
HALF = GROUPS_PER_BLOCK * STATE_DIM
N_SLABS = 2 * HALF // LANE
N_PAIRS = N_SLABS // 2
ROW_TILE = 512
SAMPLE_SPLIT = 2
FFN_CHAINS = 2
ATTN_BLOCKS_PER_STEP = 4
NEG_INF = float("-inf")
LOG2_E = math.log2(math.e)
QUERY_SCALE = LOG2_E / math.sqrt(HEAD_DIM)


def _params(n_axes=1):
    return pltpu.CompilerParams(dimension_semantics=("arbitrary",) * n_axes,
                                vmem_limit_bytes=VMEM_LIMIT_BYTES)


def _const_spec(shape):
    nd = len(shape)
    return pl.BlockSpec(shape, lambda *_: (0,) * nd, pipeline_mode=pl.Buffered(1))


def _layer_spec(shape, layer):
    nd = len(shape)
    return pl.BlockSpec((None,) + tuple(shape[1:]), lambda *_: (layer,) + (0,) * (nd - 1),
                        pipeline_mode=pl.Buffered(1))


def _rms(x, g):
    ms = jnp.mean(x * x, axis=-1, keepdims=True)
    return x * lax.rsqrt(ms + RMS_EPS) * g


def _aligned(row):
    return row if isinstance(row, int) else pl.multiple_of(row, SUBLANE)


def _sigmoid(x):
    return 1.0 / (1.0 + jnp.exp(-x))


def _cast_chunk(rows, n_steps):
    tile = 2 * SUBLANE
    for chunk in range(tile, rows + 1, tile):
        if rows % chunk == 0 and rows // chunk <= n_steps:
            return chunk
    raise ValueError((rows, n_steps))


def _ffn_kernel(*refs, d_ff, pair_in, pair_out, cast_steps, with_kv):
    refs = list(refs)
    x_ref, xr_ref, ga_ref, gb_ref, wgu_ref, wd_ref = refs[:6]
    del refs[:6]
    kvg_ref, wkv_ref, bkv_ref = (refs.pop(0), refs.pop(0), refs.pop(0)) if with_kv else (None,) * 3
    cast_src = (refs.pop(0), refs.pop(0)) if cast_steps else ()
    o_ref, or_ref = refs.pop(0), refs.pop(0)
    kv_ref, kvr_ref = (refs.pop(0), refs.pop(0)) if with_kv else (None, None)
    cast_dst = (refs.pop(0), refs.pop(0)) if cast_steps else ()
    slab_ref = refs
    for src, dst, steps in zip(cast_src, cast_dst, cast_steps or ()):
        @pl.when(pl.program_id(0) < steps)
        def _(src=src, dst=dst):
            dst[...] = src[...].astype(BF16)
    args = (ga_ref, gb_ref, wgu_ref, wd_ref, kvg_ref, wkv_ref, bkv_ref, slab_ref)
    _ffn_tile(x_ref, o_ref, kv_ref, *args, d_ff=d_ff, pair_in=pair_in, pair_out=pair_out)

    @pl.when(pl.program_id(0) == pl.num_programs(0) - 1)
    def _():
        _ffn_tile(xr_ref, or_ref, kvr_ref, *args, d_ff=d_ff, pair_in=pair_in, pair_out=pair_out)


def _ffn_tile(x_ref, o_ref, kv_ref, ga_ref, gb_ref, wgu_ref, wd_ref, kvg_ref, wkv_ref, bkv_ref, slab_ref,
              *, d_ff, pair_in, pair_out):
    d = ga_ref.shape[1]
    if pair_in:
        x = jnp.concatenate([x_ref[:, :d], x_ref[:, d:]], axis=0)
    else:
        x = x_ref[...]
    half = x.shape[0] // 2
    rc = x.shape[0] // FFN_CHAINS
    xc = [x[c * rc:(c + 1) * rc] for c in range(FFN_CHAINS)]
    xn = [_rms(v, ga_ref[...]).astype(BF16) for v in xc]
    gu = [jnp.dot(v, wgu_ref[...], preferred_element_type=F32) for v in xn]
    h = [(v[:, :d_ff] * _sigmoid(v[:, :d_ff]) * v[:, d_ff:]).astype(BF16) for v in gu]
    y = [jnp.dot(v, wd_ref[...], preferred_element_type=F32) for v in h]
    out = jnp.concatenate([v + 0.5 * _rms(w, gb_ref[...]) for v, w in zip(xc, y)], axis=0)
    if pair_in == pair_out:
        o_ref[...] = jnp.concatenate([out[:half], out[half:]], axis=1) if pair_in else out
    else:
        (slab,) = slab_ref
        even = pl.ds(0, half, stride=2)
        odd = pl.ds(1, half, stride=2)
        for j in range(d // LANE):
            lanes = slice(j * LANE, (j + 1) * LANE)
            if pair_out:
                slab[j] = out[:, lanes]
                o_ref[:, lanes] = slab[j, even, :]
                o_ref[:, d + j * LANE:d + (j + 1) * LANE] = slab[j, odd, :]
            else:
                slab[j, even, :] = out[:half, lanes]
                slab[j, odd, :] = out[half:, lanes]
                o_ref[:, lanes] = slab[j]
    if kv_ref is not None:
        assert not pair_out
        kv_ref[...] = jnp.dot(_rms(o_ref[...], kvg_ref[...]).astype(BF16), wkv_ref[...],
                              preferred_element_type=F32) + bkv_ref[...]


def _ffn(x, x_rider, ga, gb, wgu, wd, pair_in=False, pair_out=False, cast_next=None, kv=None):
    d = ga.shape[1]
    t = x.shape[0] * (2 if pair_in else 1)
    d_ff = wd.shape[1]
    tm = ROW_TILE
    n_steps = t // tm
    assert t % tm == 0 and x_rider.shape[0] * (2 if pair_in else 1) == tm

    def tile(paired):
        return (tm // 2, 2 * d) if paired else (tm, d)

    def rows(paired):
        return pl.BlockSpec(tile(paired), lambda i: (i, 0))

    def rider(shape):
        return pl.BlockSpec(shape, lambda i: (0, 0))

    in_specs = [rows(pair_in), _const_spec(tile(pair_in)), _const_spec((1, d)), _const_spec((1, d)),
                _layer_spec(wgu.shape, 0), _layer_spec(wd.shape, 0)]
    out_specs = [rows(pair_out), rider(tile(pair_out))]
    out_shape = [jax.ShapeDtypeStruct((t // 2, 2 * d) if pair_out else (t, d), F32),
                 jax.ShapeDtypeStruct(tile(pair_out), F32)]
    operands = [x, x_rider, ga, gb, wgu, wd]
    if kv is not None:
        kv_g, kv_w, kv_b = kv
        n_kv_out = kv_w.shape[-1]
        in_specs += [_const_spec((1, d)), _layer_spec(kv_w.shape, 0), _const_spec((1, n_kv_out))]
        out_specs += [pl.BlockSpec((tm, n_kv_out), lambda i: (i, 0)), rider((tm, n_kv_out))]
        out_shape += [jax.ShapeDtypeStruct((t, n_kv_out), F32), jax.ShapeDtypeStruct((tm, n_kv_out), F32)]
        operands += [kv_g, kv_w, kv_b]
    cast_steps = None
    if cast_next is not None:
        *next_w, layer = cast_next
        cast_steps = []
        for w in next_w:
            chunk = _cast_chunk(w.shape[1], n_steps)
            last = w.shape[1] // chunk - 1
            cast_steps.append(last + 1)
            in_specs.append(pl.BlockSpec((None, chunk, w.shape[2]),
                                         lambda i, last=last: (layer, jnp.minimum(i, last), 0)))
            out_specs.append(pl.BlockSpec((None, chunk, w.shape[2]),
                                          lambda i, last=last: (0, jnp.minimum(i, last), 0)))
            out_shape.append(jax.ShapeDtypeStruct((1,) + w.shape[1:], BF16))
            operands.append(w)
    return pl.pallas_call(
        functools.partial(_ffn_kernel, d_ff=d_ff, pair_in=pair_in, pair_out=pair_out, cast_steps=cast_steps,
                          with_kv=kv is not None),
        grid=(n_steps,),
        in_specs=in_specs,
        out_specs=out_specs,
        out_shape=out_shape,
        scratch_shapes=[pltpu.VMEM((d // LANE, tm, LANE), F32)] if pair_in != pair_out else [],
        compiler_params=_params(),
        name="ffn",
    )(*operands)


def _proj_kernel(x_ref, g_ref, w_ref, b_ref, o_ref):
    xn = _rms(x_ref[...], g_ref[...]).astype(BF16)
    o_ref[...] = jnp.dot(xn, w_ref[...], preferred_element_type=F32) + b_ref[...]


def _norm_proj(x, g, w, b, layer):
    t, d = x.shape
    n = w.shape[-1]
    tm = min(ROW_TILE, t)
    return pl.pallas_call(
        _proj_kernel,
        grid=(t // tm,),
        in_specs=[pl.BlockSpec((tm, d), lambda i: (i, 0)), _const_spec((1, d)),
                  _layer_spec(w.shape, layer), _const_spec((1, n))],
        out_specs=pl.BlockSpec((tm, n), lambda i: (i, 0)),
        out_shape=jax.ShapeDtypeStruct((t, n), F32),
        compiler_params=_params(),
        name="norm_proj",
    )(x, g, w, b)


def _out_proj_kernel(x_ref, o_ref_in, g_ref, w_ref, b_ref, out_ref):
    a = jnp.dot(o_ref_in[...].astype(BF16), w_ref[...], preferred_element_type=F32) + b_ref[...]
    out_ref[...] = x_ref[...] + _rms(a, g_ref[...])


def _out_proj_residual(x, o, g, w, b, layer):
    t, d = x.shape
    tm = min(ROW_TILE, t)
    row = pl.BlockSpec((tm, d), lambda i: (i, 0))
    return pl.pallas_call(
        _out_proj_kernel,
        grid=(t // tm,),
        in_specs=[row, pl.BlockSpec((tm, o.shape[1]), lambda i: (i, 0)), _const_spec((1, d)),
                  _layer_spec(w.shape, layer), _const_spec((1, d))],
        out_specs=row,
        out_shape=jax.ShapeDtypeStruct((t, d), F32),
        compiler_params=_params(),
        name="out_proj_residual",
    )(x, o, g, w, b)


def _lam_bar(lam_re, lam_im, log_dt):
    dt = jnp.exp(log_dt)
    mag = jnp.exp(lam_re * dt)
    return mag * jnp.cos(lam_im * dt), mag * jnp.sin(lam_im * dt)


def _discretize_kernel(lre_ref, lim_ref, ldt_ref, lre_col_ref, lim_col_ref, ldt_col_ref,
                       bre_ref, bim_ref, cre_ref, cim_ref, win_ref, wout_ref, wskip_ref, lam2_ref):
    a = lre_ref[0, 0]
    b = lim_ref[0, 0]
    lbr, lbi = _lam_bar(a, b, ldt_ref[0, 0])
    den = a * a + b * b
    cr = ((lbr - 1.0) * a + lbi * b) / den
    ci = (lbi * a - (lbr - 1.0) * b) / den
    bre = bre_ref[0, 0]
    bim = bim_ref[0, 0]
    wbr = cr * bre - ci * bim
    wbi = cr * bim + ci * bre
    lwbr = lbr * wbr - lbi * wbi
    lwbi = lbr * wbi + lbi * wbr
    win_ref[0, 0, :LANE, :HALF] = lwbr.astype(BF16)
    win_ref[0, 0, :LANE, HALF:] = lwbi.astype(BF16)
    win_ref[0, 0, LANE:, :HALF] = wbr.astype(BF16)
    win_ref[0, 0, LANE:, HALF:] = wbi.astype(BF16)
    lam2_ref[0, 0, :, :HALF] = lbr * lbr - lbi * lbi
    lam2_ref[0, 0, :, HALF:] = 2.0 * lbr * lbi
    lcr, lci = _lam_bar(lre_col_ref[0, 0], lim_col_ref[0, 0], ldt_col_ref[0, 0])
    cre = cre_ref[0, 0]
    cim = cim_ref[0, 0]
    c1r = cre * lcr - cim * lci
    c1i = cre * lci + cim * lcr
    wout_ref[0, 0, :HALF, :LANE] = c1r.astype(BF16)
    wout_ref[0, 0, HALF:, :LANE] = (-c1i).astype(BF16)
    wout_ref[0, 0, :HALF, LANE:] = (c1r * lcr - c1i * lci).astype(BF16)
    wout_ref[0, 0, HALF:, LANE:] = (-(c1r * lci + c1i * lcr)).astype(BF16)
    def through(br, bi):
        return (jnp.dot(br, cre, precision=lax.Precision.HIGHEST, preferred_element_type=F32)
                - jnp.dot(bi, cim, precision=lax.Precision.HIGHEST, preferred_element_type=F32)).astype(BF16)

    cb = through(wbr, wbi)
    wskip_ref[0, 0, :LANE, :LANE] = cb
    wskip_ref[0, 0, :LANE, LANE:] = through(lwbr, lwbi)
    wskip_ref[0, 0, LANE:, :LANE] = jnp.zeros((LANE, LANE), BF16)
    wskip_ref[0, 0, LANE:, LANE:] = cb


def _block_diag_in(b):
    nl, g, p, h = b.shape
    nb = g // GROUPS_PER_BLOCK
    bt = b.transpose(0, 1, 3, 2).reshape(nl, nb, GROUPS_PER_BLOCK, h, p)
    eye = jnp.eye(GROUPS_PER_BLOCK, dtype=b.dtype)
    out = bt[:, :, :, :, None, :] * eye[None, None, :, None, :, None]
    return out.reshape(nl, nb, GROUPS_PER_BLOCK * h, GROUPS_PER_BLOCK * p)


def _block_diag_out(c):
    nl, g, h, p = c.shape
    nb = g // GROUPS_PER_BLOCK
    ct = c.transpose(0, 1, 3, 2).reshape(nl, nb, GROUPS_PER_BLOCK, p, h)
    eye = jnp.eye(GROUPS_PER_BLOCK, dtype=c.dtype)
    out = ct[:, :, :, :, None, :] * eye[None, None, :, None, :, None]
    return out.reshape(nl, nb, GROUPS_PER_BLOCK * p, GROUPS_PER_BLOCK * h)


def _discretize(lam_re, lam_im, log_dt, b_re, b_im, c_re, c_im):
    nl, g, p = lam_re.shape
    nb = g // GROUPS_PER_BLOCK

    def rows(v):
        return v.reshape(nl, nb, 1, HALF)

    def cols(v):
        return v.reshape(nl, nb, HALF, 1)

    def spec(r, c):
        return pl.BlockSpec((1, 1, r, c), lambda l, k: (l, k, 0, 0))

    win, wout, wskip, lam2 = pl.pallas_call(
        _discretize_kernel,
        grid=(nl, nb),
        in_specs=[spec(1, HALF)] * 3 + [spec(HALF, 1)] * 3 + [spec(LANE, HALF)] * 2 + [spec(HALF, LANE)] * 2,
        out_specs=[spec(2 * LANE, 2 * HALF), spec(2 * HALF, 2 * LANE), spec(2 * LANE, 2 * LANE), spec(1, 2 * HALF)],
        out_shape=[jax.ShapeDtypeStruct((nl, nb, 2 * LANE, 2 * HALF), BF16),
                   jax.ShapeDtypeStruct((nl, nb, 2 * HALF, 2 * LANE), BF16),
                   jax.ShapeDtypeStruct((nl, nb, 2 * LANE, 2 * LANE), BF16),
                   jax.ShapeDtypeStruct((nl, nb, 1, 2 * HALF), F32)],
        compiler_params=_params(2),
        name="s5_discretize",
    )(rows(lam_re), rows(lam_im), rows(log_dt), cols(lam_re), cols(lam_im), cols(log_dt),
      _block_diag_in(b_re), _block_diag_in(b_im), _block_diag_out(c_re), _block_diag_out(c_im))
    return win, wout, wskip, lam2.reshape(nl, nb, 2 * HALF)


def _slab_pitch(rows):
    assert rows % SUBLANE == 0
    return rows + SUBLANE // 2


def _split_state(st):
    return (tuple(st[:, p * LANE:(p + 1) * LANE] for p in range(N_PAIRS)),
            tuple(st[:, HALF + p * LANE:HALF + (p + 1) * LANE] for p in range(N_PAIRS)))


def _s5_mixer_kernel(x_ref, g2_ref, g3_ref, win_ref, wout_ref, wskip_ref, lam_ref, d_ref, wglu_ref, bglu_ref,
                     init_re_ref, init_im_ref, o_ref, final_re_ref, final_im_ref, state_ref, y_ref, s_ref,
                     *, n_blocks, n_seq, n_t, d_model):
    tile = pl.program_id(1)
    rows = n_seq * n_t
    pitch = _slab_pitch(rows)

    @pl.when(tile == 0)
    def _():
        state_ref[:, :, :HALF] = init_re_ref[...]
        state_ref[:, :, HALF:] = init_im_ref[...]

    lre, lim = _split_state(lam_ref[...])
    xe = x_ref[0, :, :d_model]
    xo = x_ref[0, :, d_model:]
    ue = _rms(xe, g2_ref[...])
    uo = _rms(xo, g2_ref[...])
    ueb = ue.astype(BF16)
    uob = uo.astype(BF16)


    u_pair = [jnp.concatenate([ueb[:, k * LANE:(k + 1) * LANE], uob[:, k * LANE:(k + 1) * LANE]], axis=1)
              for k in range(n_blocks)]
    for k in range(n_blocks):
        v = jnp.dot(u_pair[k], win_ref[k], preferred_element_type=F32)
        for j in range(N_SLABS):
            s_ref[j, pl.ds(k * pitch, rows), :] = v[:, j * LANE:(j + 1) * LANE]

    def seq_body(seq, carry):
        xre, xim = _split_state(state_ref[seq])

        def step(t, st):
            xre, xim = st
            pair = pl.ds(seq * n_t + t, SUBLANE, stride=pitch)
            nre, nim = [], []
            for p in range(N_PAIRS):
                vre = s_ref[p, pair, :]
                vim = s_ref[N_PAIRS + p, pair, :]
                s_ref[p, pair, :] = xre[p]
                s_ref[N_PAIRS + p, pair, :] = xim[p]
                nre.append(lre[p] * xre[p] - lim[p] * xim[p] + vre)
                nim.append(lre[p] * xim[p] + lim[p] * xre[p] + vim)
            return tuple(nre), tuple(nim)

        xre, xim = lax.fori_loop(0, n_t, step, (xre, xim), unroll=min(n_t, 8))
        state_ref[seq] = jnp.concatenate(list(xre) + list(xim), axis=1)
        return carry

    if n_seq == 1:
        seq_body(0, 0)
    else:
        lax.fori_loop(0, n_seq, seq_body, 0)

    for k in range(n_blocks):
        xb = jnp.concatenate([s_ref[j, pl.ds(k * pitch, rows), :] for j in range(N_SLABS)], axis=1)
        yk = (jnp.dot(xb.astype(BF16), wout_ref[k], preferred_element_type=F32)
              + jnp.dot(u_pair[k], wskip_ref[k], preferred_element_type=F32))
        y_ref[:, k * LANE:(k + 1) * LANE] = yk[:, :LANE]
        y_ref[:, d_model + k * LANE:d_model + (k + 1) * LANE] = yk[:, LANE:]
    y = jnp.concatenate([y_ref[:, :d_model] + d_ref[...] * ue, y_ref[:, d_model:] + d_ref[...] * uo], axis=0)
    h = jnp.dot(jax.nn.gelu(y).astype(BF16), wglu_ref[...], preferred_element_type=F32) + bglu_ref[...]
    r = _rms(h[:, :d_model] * _sigmoid(h[:, d_model:]), g3_ref[...])
    o_ref[0] = jnp.concatenate([xe + r[:rows], xo + r[rows:]], axis=1)

    @pl.when(tile == pl.num_programs(1) - 1)
    def _():
        final_re_ref[...] = state_ref[:, :, :HALF]
        final_im_ref[...] = state_ref[:, :, HALF:]


def _s5_layer(x, g2, g3, win, wout, wskip, lam2, d, wglu, bglu, init_re, init_im, layer, *, n_seq, n_t):
    nb_rows, s2, dm2 = x.shape
    dm = dm2 // 2
    n_blocks = win.shape[1]
    assert n_blocks == SUBLANE
    rows = n_seq * n_t
    n_tiles = s2 // rows
    assert n_seq == 1 or n_tiles == 1
    row = pl.BlockSpec((1, rows, dm2), lambda b, i: (b, i, 0))
    vec = _const_spec((1, dm))
    state = pl.BlockSpec((n_seq, SUBLANE, HALF), lambda b, i: (b, 0, 0))
    state_in = pl.BlockSpec(state.block_shape, state.index_map, pipeline_mode=pl.Buffered(1))
    state_shape = jax.ShapeDtypeStruct(init_re.shape, F32)
    slab = pltpu.VMEM((N_SLABS, n_blocks * _slab_pitch(rows), LANE), F32)
    return pl.pallas_call(
        functools.partial(_s5_mixer_kernel, n_blocks=n_blocks, n_seq=n_seq, n_t=n_t, d_model=dm),
        grid=(nb_rows, n_tiles),
        in_specs=[row, vec, vec, _layer_spec(win.shape, layer), _layer_spec(wout.shape, layer),
                  _layer_spec(wskip.shape, layer), _layer_spec(lam2.shape, layer), vec,
                  _layer_spec(wglu.shape, layer), _const_spec((1, 2 * dm)), state_in, state_in],
        out_specs=[row, state, state],
        out_shape=[jax.ShapeDtypeStruct(x.shape, F32), state_shape, state_shape],
        scratch_shapes=[pltpu.VMEM((n_seq, SUBLANE, 2 * HALF), F32), pltpu.VMEM((rows, dm2), F32), slab],
        compiler_params=_params(2),
        name="s5_mixer",
    )(x, g2, g3, win, wout, wskip, lam2, d, wglu, bglu, init_re, init_im)


def _state_rows(s):
    return s.reshape(s.shape[0], -1, HALF)


def _t5_bucket_np(dist):
    n = np.maximum(dist, 0)
    max_exact = NUM_BUCKETS // 2
    nf = np.maximum(n, 1).astype(np.float32)
    large = max_exact + (np.log(nf / np.float32(max_exact)) / np.float32(math.log(MAX_DISTANCE / max_exact))
                         * np.float32(NUM_BUCKETS - max_exact)).astype(np.int32)
    large = np.minimum(large, NUM_BUCKETS - 1)
    return np.where(n < max_exact, n, large).astype(np.int32)


def _bucket_table(n_q, n_k, q_offset, n_q_pad, n_k_pad, first_key=0):
    dist = (np.arange(n_q)[:, None] + q_offset) - np.arange(n_k)[None, :]
    valid = (dist >= 0) & (dist < WINDOW) & (np.arange(n_k)[None, :] >= first_key)
    table = np.full((n_q_pad, n_k_pad), -1, np.int32)
    table[:n_q, :n_k] = np.where(valid, _t5_bucket_np(dist), -1)
    return table


def _bias_kernel(rel_ref, idx_ref, o_ref):
    h = pl.program_id(0)
    idx = idx_ref[...]
    acc = jnp.full(idx.shape, NEG_INF, F32)
    for b in range(NUM_BUCKETS):
        acc = jnp.where(idx == b, rel_ref[b, h] * LOG2_E, acc)
    o_ref[0] = acc


def _bias_planes(rel_bias, table):
    n_heads = rel_bias.shape[1]
    r, c = table.shape
    return pl.pallas_call(
        _bias_kernel,
        grid=(n_heads,),
        in_specs=[pl.BlockSpec(memory_space=pltpu.SMEM), pl.BlockSpec((r, c), lambda h: (0, 0))],
        out_specs=pl.BlockSpec((1, r, c), lambda h: (h, 0, 0)),
        out_shape=jax.ShapeDtypeStruct((n_heads, r, c), F32),
        compiler_params=_params(),
        name="attn_bias",
    )(rel_bias, jnp.asarray(table))


def _bias_tiles_kernel(rel_ref, idx_ref, o_ref, *, n_kv, pairs):
    n_var, n_k, n_q = idx_ref.shape
    for v in range(n_var):
        idx = idx_ref[v]
        for g in range(n_kv):
            for pair in range(pairs):
                for par in range(2):
                    h = (g * pairs + pair) * 2 + par
                    acc = jnp.full(idx.shape, NEG_INF, F32)
                    for b in range(NUM_BUCKETS):
                        acc = jnp.where(idx == b, rel_ref[b, h] * LOG2_E, acc)
                    o_ref[v, g, par * n_k:(par + 1) * n_k, pair * n_q:(pair + 1) * n_q] = acc


def _bias_tiles(rel_bias, tables, n_kv):
    n_heads = rel_bias.shape[1]
    pairs = n_heads // n_kv // 2
    idx = jnp.asarray(np.stack([t.T for t in tables]))
    n_var, n_k, n_q = idx.shape
    return pl.pallas_call(
        functools.partial(_bias_tiles_kernel, n_kv=n_kv, pairs=pairs),
        in_specs=[pl.BlockSpec(memory_space=pltpu.SMEM), pl.BlockSpec(memory_space=pltpu.VMEM)],
        out_specs=pl.BlockSpec(memory_space=pltpu.VMEM),
        out_shape=jax.ShapeDtypeStruct((n_var, n_kv, 2 * n_k, pairs * n_q), F32),
        compiler_params=pltpu.CompilerParams(vmem_limit_bytes=VMEM_LIMIT_BYTES),
        name="attn_bias_tiles",
    )(rel_bias, idx)


def _kv_head_rows(kv, g):
    lane = lax.broadcasted_iota(jnp.int32, kv.shape, 1)
    swapped = pltpu.roll(kv, HEAD_DIM, axis=1)
    even, odd = (kv, swapped) if g == 0 else (swapped, kv)
    return jnp.where(lane < HEAD_DIM, even, 0.0), jnp.where(lane >= HEAD_DIM, odd, 0.0)


def _pair_layout_t(planes, n_kv, n_q):
    h, _, n_k = planes.shape
    pairs = h // n_kv // 2
    p = planes[:, :n_q].reshape(n_kv, pairs, 2, n_q, n_k).transpose(0, 2, 4, 1, 3)
    return p.reshape(n_kv, 2 * n_k, pairs * n_q)


def _sink_cols(sinks, n_kv, n_q):
    h = sinks.shape[0]
    pairs = h // n_kv // 2
    s = sinks.reshape(n_kv, pairs, 2).transpose(0, 2, 1)[..., None]
    return jnp.broadcast_to(s, (n_kv, 2, pairs, n_q)).reshape(n_kv, 2, pairs * n_q)


def _attend_t(chains):
    scores = []
    for q_t, k_even, k_odd, _, _, bias, _ in chains:
        ke = jnp.concatenate([k_even, k_odd], axis=0).astype(BF16)
        scores.append(jnp.dot(ke, q_t, preferred_element_type=F32) + bias)
    weights = []
    for s, chain in zip(scores, chains):
        sink = chain[6]
        n_k = s.shape[0] // 2
        probs, inv = [], []
        for par in range(2):
            sp = s[par * n_k:(par + 1) * n_k]
            sk = sink[par:par + 1] * LOG2_E
            mx = jnp.maximum(jnp.max(sp, axis=0, keepdims=True), sk)
            p = jnp.exp2(sp - mx)
            den = jnp.sum(p, axis=0, keepdims=True) + jnp.exp2(sk - mx)
            probs.append(p.astype(BF16))
            inv.append(1.0 / den)
        weights.append((jnp.concatenate(probs, axis=0), inv))
    outs = []
    for (p_t, inv), chain in zip(weights, chains):
        ve = jnp.concatenate([chain[3], chain[4]], axis=0).astype(BF16)
        o = lax.dot_general(ve, p_t, (((0,), (0,)), ((), ())), preferred_element_type=F32)
        row = lax.broadcasted_iota(jnp.int32, o.shape, 0)
        outs.append(o * jnp.where(row < HEAD_DIM, inv[0], inv[1]))
    return outs


def _prompt_attn_kernel(x_ref, kvc_ref, kvp_ref, bias_ref, sink_ref, g2_ref, g3_ref, wq_ref, bq_ref,
                        wo_ref, bo_ref, o_ref, *, n_kv, n_sub, steps_per_seq, scale):
    blk = WINDOW
    x = x_ref[...]
    u = _rms(x, g2_ref[...]).astype(BF16)
    q_t = lax.dot_general(wq_ref[...], u, (((1,), (1,)), ((), ())), preferred_element_type=F32)
    q_t = ((q_t + jnp.concatenate([bq_ref[...]] * n_sub, axis=1)) * scale).astype(BF16)
    kv = jnp.concatenate([kvp_ref[...], kvc_ref[...]], axis=0)
    pairs = q_t.shape[0] // LANE // n_kv
    first = jnp.where(pl.program_id(0) % steps_per_seq == 0, 1, 0)
    k_rows = [_kv_head_rows(kv[:, :n_kv * HEAD_DIM], g) for g in range(n_kv)]
    v_rows = [_kv_head_rows(kv[:, n_kv * HEAD_DIM:], g) for g in range(n_kv)]
    chains = []
    for s in range(n_sub):
        keys = slice(s * blk, (s + 2) * blk)
        for g in range(n_kv):
            qp = jnp.concatenate([q_t[(g * pairs + i) * LANE:(g * pairs + i + 1) * LANE, s * blk:(s + 1) * blk]
                                  for i in range(pairs)], axis=1)
            bias = bias_ref[first, g] if s == 0 else bias_ref[0, g]
            chains.append((qp, k_rows[g][0][keys], k_rows[g][1][keys], v_rows[g][0][keys],
                           v_rows[g][1][keys], bias, sink_ref[g]))
    outs = _attend_t(chains)
    o_t = jnp.concatenate(
        [jnp.concatenate([outs[s * n_kv + g][:, i * blk:(i + 1) * blk] for s in range(n_sub)], axis=1)
         for g in range(n_kv) for i in range(pairs)], axis=0).astype(BF16)
    a = lax.dot_general(o_t, wo_ref[...], (((0,), (0,)), ((), ())), preferred_element_type=F32) + bo_ref[...]
    o_ref[...] = x + _rms(a, g3_ref[...])


def _prompt_attention(x, kv, bias, sink, g2, g3, wq_t, bq_col, wo, bo, layer, *, blocks_per_seq, n_kv):
    t, d = x.shape
    blk = WINDOW
    n_sub = ATTN_BLOCKS_PER_STEP
    assert n_kv * HEAD_DIM == LANE and blocks_per_seq % n_sub == 0
    row = pl.BlockSpec((n_sub * blk, d), lambda i: (i, 0))
    kvw = kv.shape[1]
    return pl.pallas_call(
        functools.partial(_prompt_attn_kernel, n_kv=n_kv, n_sub=n_sub,
                          steps_per_seq=blocks_per_seq // n_sub, scale=QUERY_SCALE),
        grid=(t // (n_sub * blk),),
        in_specs=[row,
                  pl.BlockSpec((n_sub * blk, kvw), lambda i: (i, 0)),
                  pl.BlockSpec((blk, kvw), lambda i: (jnp.maximum(i * n_sub - 1, 0), 0)),
                  _const_spec(bias.shape),
                  _const_spec(sink.shape), _const_spec((1, d)), _const_spec((1, d)),
                  _layer_spec(wq_t.shape, layer), _const_spec(bq_col.shape),
                  _layer_spec(wo.shape, layer), _const_spec((1, d))],
        out_specs=row,
        out_shape=jax.ShapeDtypeStruct((t, d), F32),
        compiler_params=_params(),
        name="prompt_attention",
    )(x, kv, kv, bias, sink, g2, g3, wq_t, bq_col, wo, bo)


def _sample_attn_kernel(q_ref, kc_ref, vc_ref, kvn_ref, bias_ref, sink_ref, o_ref, *, n_kv, scale):
    half = n_kv * HEAD_DIM
    chains = []
    for b in range(q_ref.shape[0]):
        k = jnp.concatenate([kc_ref[b], kvn_ref[b, :, :half]], axis=0)
        v = jnp.concatenate([vc_ref[b], kvn_ref[b, :, half:]], axis=0)
        for g in range(n_kv):
            chains.append(((q_ref[b, g] * scale).astype(BF16), *_kv_head_rows(k, g), *_kv_head_rows(v, g),
                           bias_ref[g], sink_ref[g]))
    outs = _attend_t(chains)
    for b in range(q_ref.shape[0]):
        for g in range(n_kv):
            o_ref[b, g] = outs[b * n_kv + g]


def _sample_attention(q_t, k_cache, v_cache, kv_new, bias, sink, *, n_kv):
    nb, _, _, n_l = q_t.shape
    n_past = k_cache.shape[1]
    bb = SUBLANE
    return pl.pallas_call(
        functools.partial(_sample_attn_kernel, n_kv=n_kv, scale=QUERY_SCALE),
        grid=(nb // bb,),
        in_specs=[pl.BlockSpec((bb, n_kv, LANE, n_l), lambda i: (i, 0, 0, 0)),
                  pl.BlockSpec((bb, n_past, LANE), lambda i: (i, 0, 0)),
                  pl.BlockSpec((bb, n_past, LANE), lambda i: (i, 0, 0)),
                  pl.BlockSpec((bb,) + kv_new.shape[1:], lambda i: (i, 0, 0)),
                  _const_spec(bias.shape), _const_spec(sink.shape)],
        out_specs=pl.BlockSpec((bb, n_kv, LANE, n_l), lambda i: (i, 0, 0, 0)),
        out_shape=jax.ShapeDtypeStruct(q_t.shape, F32),
        compiler_params=_params(),
        name="sample_attention",
    )(q_t, k_cache, v_cache, kv_new, bias, sink)


def kernel(x_prompt, x_sample, state_ssm_re, state_ssm_im, cache_win_k, cache_win_v, norm_g, ffn1_w_gu, ffn1_w_down, ffn2_w_gu, ffn2_w_down, ssm_lambda_re, ssm_lambda_im, ssm_log_dt, ssm_b_re, ssm_b_im, ssm_c_re, ssm_c_im, ssm_d, ssm_w_glu, ssm_b_glu, kv_norm_g, w_kv, b_kv, attn_w_q, attn_b_q, attn_sinks, attn_w_o, attn_b_o, rel_bias):
    bsz, seq, dm = x_prompt.shape
    dec_b, dec_s, _ = x_sample.shape
    depth = norm_g.shape[0]
    n_a = ssm_lambda_re.shape[0]
    n_groups = ssm_lambda_re.shape[1]
    n_heads = attn_sinks.shape[1]
    n_kv = cache_win_k.shape[2]
    n_past = cache_win_k.shape[1]
    pairs = n_heads // n_kv // 2
    n_k_pad = n_past + SUBLANE
    assert seq % ROW_TILE == 0 and seq % WINDOW == 0 and dec_s % 2 == 0
    assert dec_b * dec_s == ROW_TILE and dec_s <= SUBLANE and n_past % SUBLANE == 0
    assert dec_b % (SUBLANE * SAMPLE_SPLIT) == 0
    assert n_kv == 2 and dm == n_heads * HEAD_DIM and n_groups == SUBLANE * GROUPS_PER_BLOCK

    bf = lambda w: w.astype(BF16)
    row = lambda v: v.reshape(1, -1)
    s5_w = _discretize(ssm_lambda_re, ssm_lambda_im, ssm_log_dt, ssm_b_re, ssm_b_im, ssm_c_re, ssm_c_im)
    ffn_w = (bf(ffn1_w_gu[0])[None], bf(ffn1_w_down[0])[None])
    wglu, wkv, wq, wo = bf(ssm_w_glu), bf(w_kv)[None], bf(attn_w_q), bf(attn_w_o)

    tab_p = _bucket_table(WINDOW, 2 * WINDOW, WINDOW, WINDOW, 2 * WINDOW)
    tab_p0 = _bucket_table(WINDOW, 2 * WINDOW, WINDOW, WINDOW, 2 * WINDOW, first_key=WINDOW)
    tab_s = _bucket_table(dec_s, n_past + dec_s, n_past, SUBLANE, n_k_pad)
    wq_t = wq.transpose(0, 2, 1)
    bias_p = _bias_tiles(rel_bias, [tab_p, tab_p0], n_kv)
    bias_s = _pair_layout_t(_bias_planes(rel_bias, tab_s), n_kv, dec_s)

    xp = x_prompt.reshape(bsz * seq, dm)
    xs = x_sample.reshape(dec_b * dec_s, dm)
    zero_state = jnp.zeros((bsz, SUBLANE, HALF), F32)

    ends_p, ends_s = [], []
    kv_p = kv_s = kv_new = None
    k_cache = cache_win_k.reshape(dec_b, n_past, n_kv * HEAD_DIM)
    v_cache = cache_win_v.reshape(dec_b, n_past, n_kv * HEAD_DIM)
    half = n_kv * HEAD_DIM
    for l in range(depth):
        g = [row(norm_g[l, i]) for i in range(norm_g.shape[1])]
        pair_io = dict(pair_in=0 < l < n_a, pair_out=l < n_a)
        xp, xs, *ffn_w = _ffn(xp, xs, g[0], g[1], *ffn_w, **pair_io, cast_next=(ffn2_w_gu, ffn2_w_down, l))
        if l < n_a:
            common = (g[2], g[3], *s5_w, row(ssm_d[l]), wglu, row(ssm_b_glu[l]))
            xp3, *fin_p = _s5_layer(xp.reshape(bsz, seq // 2, 2 * dm), *common, zero_state, zero_state, l,
                                    n_seq=1, n_t=ROW_TILE // 2)
            xs3, *fin_s = _s5_layer(xs.reshape(SAMPLE_SPLIT, dec_b * dec_s // (2 * SAMPLE_SPLIT), 2 * dm), *common,
                                    _state_rows(state_ssm_re[l]), _state_rows(state_ssm_im[l]), l,
                                    n_seq=dec_b // SAMPLE_SPLIT, n_t=dec_s // 2)
            xp, xs = xp3.reshape(bsz * seq // 2, 2 * dm), xs3.reshape(dec_b * dec_s // 2, 2 * dm)
            ends_p.append([s.reshape(bsz, n_groups, STATE_DIM) for s in fin_p])
            ends_s.append([s.reshape(dec_b, n_groups, STATE_DIM) for s in fin_s])
        else:
            bl = l - n_a
            bq_l, bo_l = row(attn_b_q[bl]), row(attn_b_o[bl])
            bq_col = jnp.broadcast_to(attn_b_q[bl][:, None], (dm, WINDOW))
            xp = _prompt_attention(xp, kv_p, bias_p, _sink_cols(attn_sinks[bl], n_kv, WINDOW), g[2], g[3],
                                   wq_t, bq_col, wo, bo_l, bl, blocks_per_seq=seq // WINDOW, n_kv=n_kv)
            q = _norm_proj(xs, g[2], wq, bq_l, bl)
            q = q.reshape(dec_b, dec_s, n_kv, pairs, LANE).transpose(0, 2, 4, 3, 1)
            q = q.reshape(dec_b, n_kv, LANE, pairs * dec_s)
            o = _sample_attention(q, k_cache, v_cache, kv_new, bias_s, _sink_cols(attn_sinks[bl], n_kv, dec_s), n_kv=n_kv)
            o = o.reshape(dec_b, n_kv, LANE, pairs, dec_s).transpose(0, 4, 1, 3, 2).reshape(dec_b * dec_s, dm)
            xs = _out_proj_residual(xs, o, g[3], wo, bo_l, bl)
        pair_io = dict(pair_in=l < n_a, pair_out=l < n_a - 1)
        cast_next = (ffn1_w_gu, ffn1_w_down, l + 1) if l + 1 < depth else None
        if l == n_a - 1:
            xp, xs, kv_p, kv_s, *ffn_w = _ffn(xp, xs, g[4], g[5], *ffn_w, **pair_io, cast_next=cast_next,
                                              kv=(row(kv_norm_g), wkv, row(b_kv)))
            kv_s = kv_s.reshape(dec_b, dec_s, 2 * half)
            kv_new = jnp.pad(kv_s, ((0, 0), (0, SUBLANE - dec_s), (0, 0)))
        else:
            xp, xs, *ffn_w = _ffn(xp, xs, g[4], g[5], *ffn_w, **pair_io, cast_next=cast_next)

    kv_p3 = kv_p.reshape(bsz, seq, 2 * half)
    new_k_p = kv_p3[:, -WINDOW:, :half].reshape(bsz, WINDOW, n_kv, HEAD_DIM)
    new_v_p = kv_p3[:, -WINDOW:, half:].reshape(bsz, WINDOW, n_kv, HEAD_DIM)
    new_k_s = jnp.concatenate([k_cache, kv_s[:, :, :half]], axis=1)[:, -WINDOW:].reshape(dec_b, WINDOW, n_kv, HEAD_DIM)
    new_v_s = jnp.concatenate([v_cache, kv_s[:, :, half:]], axis=1)[:, -WINDOW:].reshape(dec_b, WINDOW, n_kv, HEAD_DIM)
    return (xp.reshape(bsz, seq, dm), xs.reshape(dec_b, dec_s, dm),
            jnp.stack([e[0] for e in ends_p]), jnp.stack([e[1] for e in ends_p]),
            new_k_p, new_v_p,
            jnp.stack([e[0] for e in ends_s]), jnp.stack([e[1] for e in ends_s]),
            new_k_s, new_v_s)
```

```python
import functools
import math

import numpy as np
import jax
import jax.numpy as jnp
from jax import lax
from jax.experimental import pallas as pl
from jax.experimental.pallas import tpu as pltpu

F32 = jnp.float32
BF16 = jnp.bfloat16

LANE = 128
SUBLANE = 8
VMEM_LIMIT_BYTES = 56 * 1024 * 1024

RMS_EPS = 1e-6
GROUP_SIZE = 16
STATE_DIM = 64
HEAD_DIM = 64
WINDOW = 128
NUM_BUCKETS = 32
MAX_DISTANCE = WINDOW
GROUPS_PER_BLOCK = LANE // GROUP_SIZE
HALF = GROUPS_PER_BLOCK * STATE_DIM
N_SLABS = 2 * HALF // LANE
N_PAIRS = N_SLABS // 2
ROW_TILE = 512
SAMPLE_SPLIT = 2
FFN_CHAINS = 2
ATTN_BLOCKS_PER_STEP = 8
NEG_INF = float("-inf")
LOG2_E = math.log2(math.e)
QUERY_SCALE = LOG2_E / math.sqrt(HEAD_DIM)


def _params(n_axes=1):
    return pltpu.CompilerParams(dimension_semantics=("arbitrary",) * n_axes,
                                vmem_limit_bytes=VMEM_LIMIT_BYTES)


def _const_spec(shape):
    nd = len(shape)
    return pl.BlockSpec(shape, lambda *_: (0,) * nd, pipeline_mode=pl.Buffered(1))


def _layer_spec(shape, layer):
    nd = len(shape)
    return pl.BlockSpec((None,) + tuple(shape[1:]), lambda *_: (layer,) + (0,) * (nd - 1),
                        pipeline_mode=pl.Buffered(1))


def _rms(x, g):
    ms = jnp.mean(x * x, axis=-1, keepdims=True)
    return x * lax.rsqrt(ms + RMS_EPS) * g


def _aligned(row):
    return row if isinstance(row, int) else pl.multiple_of(row, SUBLANE)


def _sigmoid(x):
    return 1.0 / (1.0 + jnp.exp(-x))


def _cast_chunk(rows, n_steps):
    tile = 2 * SUBLANE
    for chunk in range(tile, rows + 1, tile):
        if rows % chunk == 0 and rows // chunk <= n_steps:
            return chunk
    raise ValueError((rows, n_steps))


def _ffn_kernel(*refs, d_ff, pair_in, pair_out, cast_steps, with_kv):
    refs = list(refs)
    x_ref, xr_ref, ga_ref, gb_ref, wgu_ref, wd_ref = refs[:6]
    del refs[:6]
    kvg_ref, wkv_ref, bkv_ref = (refs.pop(0), refs.pop(0), refs.pop(0)) if with_kv else (None,) * 3
    cast_src = (refs.pop(0), refs.pop(0)) if cast_steps else ()
    o_ref, or_ref = refs.pop(0), refs.pop(0)
    kv_ref, kvr_ref = (refs.pop(0), refs.pop(0)) if with_kv else (None, None)
    cast_dst = (refs.pop(0), refs.pop(0)) if cast_steps else ()
    slab_ref = refs
    for src, dst, steps in zip(cast_src, cast_dst, cast_steps or ()):
        @pl.when(pl.program_id(0) < steps)
        def _(src=src, dst=dst):
            dst[...] = src[...].astype(BF16)
    args = (ga_ref, gb_ref, wgu_ref, wd_ref, kvg_ref, wkv_ref, bkv_ref, slab_ref)
    _ffn_tile(x_ref, o_ref, kv_ref, *args, d_ff=d_ff, pair_in=pair_in, pair_out=pair_out)

    @pl.when(pl.program_id(0) == pl.num_programs(0) - 1)
    def _():
        _ffn_tile(xr_ref, or_ref, kvr_ref, *args, d_ff=d_ff, pair_in=pair_in, pair_out=pair_out)


def _ffn_tile(x_ref, o_ref, kv_ref, ga_ref, gb_ref, wgu_ref, wd_ref, kvg_ref, wkv_ref, bkv_ref, slab_ref,
              *, d_ff, pair_in, pair_out):
    d = ga_ref.shape[1]
    if pair_in:
        x = jnp.concatenate([x_ref[:, :d], x_ref[:, d:]], axis=0)
    else:
        x = x_ref[...]
    half = x.shape[0] // 2
    rc = x.shape[0] // FFN_CHAINS
    xc = [x[c * rc:(c + 1) * rc] for c in range(FFN_CHAINS)]
    xn = [_rms(v, ga_ref[...]).astype(BF16) for v in xc]
    gu = [jnp.dot(v, wgu_ref[...], preferred_element_type=F32) for v in xn]
    h = [(v[:, :d_ff] * _sigmoid(v[:, :d_ff]) * v[:, d_ff:]).astype(BF16) for v in gu]
    y = [jnp.dot(v, wd_ref[...], preferred_element_type=F32) for v in h]
    out = jnp.concatenate([v + 0.5 * _rms(w, gb_ref[...]) for v, w in zip(xc, y)], axis=0)
    if pair_in == pair_out:
        o_ref[...] = jnp.concatenate([out[:half], out[half:]], axis=1) if pair_in else out
    else:
        (slab,) = slab_ref
        even = pl.ds(0, half, stride=2)
        odd = pl.ds(1, half, stride=2)
        for j in range(d // LANE):
            lanes = slice(j * LANE, (j + 1) * LANE)
            if pair_out:
                slab[j] = out[:, lanes]
                o_ref[:, lanes] = slab[j, even, :]
                o_ref[:, d + j * LANE:d + (j + 1) * LANE] = slab[j, odd, :]
            else:
                slab[j, even, :] = out[:half, lanes]
                slab[j, odd, :] = out[half:, lanes]
                o_ref[:, lanes] = slab[j]
    if kv_ref is not None:
        assert not pair_out
        kv_ref[...] = jnp.dot(_rms(o_ref[...], kvg_ref[...]).astype(BF16), wkv_ref[...],
                              preferred_element_type=F32) + bkv_ref[...]


def _ffn(x, x_rider, ga, gb, wgu, wd, pair_in=False, pair_out=False, cast_next=None, kv=None):
    d = ga.shape[1]
    t = x.shape[0] * (2 if pair_in else 1)
    d_ff = wd.shape[1]
    tm = ROW_TILE
    n_steps = t // tm
    assert t % tm == 0 and x_rider.shape[0] * (2 if pair_in else 1) == tm

    def tile(paired):
        return (tm // 2, 2 * d) if paired else (tm, d)

    def rows(paired):
        return pl.BlockSpec(tile(paired), lambda i: (i, 0))

    def rider(shape):
        return pl.BlockSpec(shape, lambda i: (0, 0))

    in_specs = [rows(pair_in), _const_spec(tile(pair_in)), _const_spec((1, d)), _const_spec((1, d)),
                _layer_spec(wgu.shape, 0), _layer_spec(wd.shape, 0)]
    out_specs = [rows(pair_out), rider(tile(pair_out))]
    out_shape = [jax.ShapeDtypeStruct((t // 2, 2 * d) if pair_out else (t, d), F32),
                 jax.ShapeDtypeStruct(tile(pair_out), F32)]
    operands = [x, x_rider, ga, gb, wgu, wd]
    if kv is not None:
        kv_g, kv_w, kv_b = kv
        n_kv_out = kv_w.shape[-1]
        in_specs += [_const_spec((1, d)), _layer_spec(kv_w.shape, 0), _const_spec((1, n_kv_out))]
        out_specs += [pl.BlockSpec((tm, n_kv_out), lambda i: (i, 0)), rider((tm, n_kv_out))]
        out_shape += [jax.ShapeDtypeStruct((t, n_kv_out), F32), jax.ShapeDtypeStruct((tm, n_kv_out), F32)]
        operands += [kv_g, kv_w, kv_b]
    cast_steps = None
    if cast_next is not None:
        *next_w, layer = cast_next
        cast_steps = []
        for w in next_w:
            chunk = _cast_chunk(w.shape[1], n_steps)
            last = w.shape[1] // chunk - 1
            cast_steps.append(last + 1)
            in_specs.append(pl.BlockSpec((None, chunk, w.shape[2]),
                                         lambda i, last=last: (layer, jnp.minimum(i, last), 0)))
            out_specs.append(pl.BlockSpec((None, chunk, w.shape[2]),
                                          lambda i, last=last: (0, jnp.minimum(i, last), 0)))
            out_shape.append(jax.ShapeDtypeStruct((1,) + w.shape[1:], BF16))
            operands.append(w)
    return pl.pallas_call(
        functools.partial(_ffn_kernel, d_ff=d_ff, pair_in=pair_in, pair_out=pair_out, cast_steps=cast_steps,
                          with_kv=kv is not None),
        grid=(n_steps,),
        in_specs=in_specs,
        out_specs=out_specs,
        out_shape=out_shape,
        scratch_shapes=[pltpu.VMEM((d // LANE, tm, LANE), F32)] if pair_in != pair_out else [],
        compiler_params=_params(),
        name="ffn",
    )(*operands)


def _proj_kernel(x_ref, g_ref, w_ref, b_ref, o_ref):
    xn = _rms(x_ref[...], g_ref[...]).astype(BF16)
    o_ref[...] = jnp.dot(xn, w_ref[...], preferred_element_type=F32) + b_ref[...]


def _norm_proj(x, g, w, b, layer):
    t, d = x.shape
    n = w.shape[-1]
    tm = min(ROW_TILE, t)
    return pl.pallas_call(
        _proj_kernel,
        grid=(t // tm,),
        in_specs=[pl.BlockSpec((tm, d), lambda i: (i, 0)), _const_spec((1, d)),
                  _layer_spec(w.shape, layer), _const_spec((1, n))],
        out_specs=pl.BlockSpec((tm, n), lambda i: (i, 0)),
        out_shape=jax.ShapeDtypeStruct((t, n), F32),
        compiler_params=_params(),
        name="norm_proj",
    )(x, g, w, b)


def _out_proj_kernel(x_ref, o_ref_in, g_ref, w_ref, b_ref, out_ref):
    a = jnp.dot(o_ref_in[...].astype(BF16), w_ref[...], preferred_element_type=F32) + b_ref[...]
    out_ref[...] = x_ref[...] + _rms(a, g_ref[...])


def _out_proj_residual(x, o, g, w, b, layer):
    t, d = x.shape
    tm = min(ROW_TILE, t)
    row = pl.BlockSpec((tm, d), lambda i: (i, 0))
    return pl.pallas_call(
        _out_proj_kernel,
        grid=(t // tm,),
        in_specs=[row, pl.BlockSpec((tm, o.shape[1]), lambda i: (i, 0)), _const_spec((1, d)),
                  _layer_spec(w.shape, layer), _const_spec((1, d))],
        out_specs=row,
        out_shape=jax.ShapeDtypeStruct((t, d), F32),
        compiler_params=_params(),
        name="out_proj_residual",
    )(x, o, g, w, b)


def _lam_bar(lam_re, lam_im, log_dt):
    dt = jnp.exp(log_dt)
    mag = jnp.exp(lam_re * dt)
    return mag * jnp.cos(lam_im * dt), mag * jnp.sin(lam_im * dt)


def _discretize_kernel(lre_ref, lim_ref, ldt_ref, lre_col_ref, lim_col_ref, ldt_col_ref,
                       bre_ref, bim_ref, cre_ref, cim_ref, win_ref, wout_ref, wskip_ref, lam2_ref):
    a = lre_ref[0, 0]
    b = lim_ref[0, 0]
    lbr, lbi = _lam_bar(a, b, ldt_ref[0, 0])
    den = a * a + b * b
    cr = ((lbr - 1.0) * a + lbi * b) / den
    ci = (lbi * a - (lbr - 1.0) * b) / den
    bre = bre_ref[0, 0]
    bim = bim_ref[0, 0]
    wbr = cr * bre - ci * bim
    wbi = cr * bim + ci * bre
    lwbr = lbr * wbr - lbi * wbi
    lwbi = lbr * wbi + lbi * wbr
    win_ref[0, 0, :LANE, :HALF] = lwbr.astype(BF16)
    win_ref[0, 0, :LANE, HALF:] = lwbi.astype(BF16)
    win_ref[0, 0, LANE:, :HALF] = wbr.astype(BF16)
    win_ref[0, 0, LANE:, HALF:] = wbi.astype(BF16)
    lam2_ref[0, 0, :, :HALF] = lbr * lbr - lbi * lbi
    lam2_ref[0, 0, :, HALF:] = 2.0 * lbr * lbi
    lcr, lci = _lam_bar(lre_col_ref[0, 0], lim_col_ref[0, 0], ldt_col_ref[0, 0])
    cre = cre_ref[0, 0]
    cim = cim_ref[0, 0]
    c1r = cre * lcr - cim * lci
    c1i = cre * lci + cim * lcr
    wout_ref[0, 0, :HALF, :LANE] = c1r.astype(BF16)
    wout_ref[0, 0, HALF:, :LANE] = (-c1i).astype(BF16)
    wout_ref[0, 0, :HALF, LANE:] = (c1r * lcr - c1i * lci).astype(BF16)
    wout_ref[0, 0, HALF:, LANE:] = (-(c1r * lci + c1i * lcr)).astype(BF16)
    def through(br, bi):
        return (jnp.dot(br, cre, precision=lax.Precision.HIGHEST, preferred_element_type=F32)
                - jnp.dot(bi, cim, precision=lax.Precision.HIGHEST, preferred_element_type=F32)).astype(BF16)

    cb = through(wbr, wbi)
    wskip_ref[0, 0, :LANE, :LANE] = cb
    wskip_ref[0, 0, :LANE, LANE:] = through(lwbr, lwbi)
    wskip_ref[0, 0, LANE:, :LANE] = jnp.zeros((LANE, LANE), BF16)
    wskip_ref[0, 0, LANE:, LANE:] = cb


def _block_diag_in(b):
    nl, g, p, h = b.shape
    nb = g // GROUPS_PER_BLOCK
    bt = b.transpose(0, 1, 3, 2).reshape(nl, nb, GROUPS_PER_BLOCK, h, p)
    eye = jnp.eye(GROUPS_PER_BLOCK, dtype=b.dtype)
    out = bt[:, :, :, :, None, :] * eye[None, None, :, None, :, None]
    return out.reshape(nl, nb, GROUPS_PER_BLOCK * h, GROUPS_PER_BLOCK * p)


def _block_diag_out(c):
    nl, g, h, p = c.shape
    nb = g // GROUPS_PER_BLOCK
    ct = c.transpose(0, 1, 3, 2).reshape(nl, nb, GROUPS_PER_BLOCK, p, h)
    eye = jnp.eye(GROUPS_PER_BLOCK, dtype=c.dtype)
    out = ct[:, :, :, :, None, :] * eye[None, None, :, None, :, None]
    return out.reshape(nl, nb, GROUPS_PER_BLOCK * p, GROUPS_PER_BLOCK * h)


def _discretize(lam_re, lam_im, log_dt, b_re, b_im, c_re, c_im):
    nl, g, p = lam_re.shape
    nb = g // GROUPS_PER_BLOCK

    def rows(v):
        return v.reshape(nl, nb, 1, HALF)

    def cols(v):
        return v.reshape(nl, nb, HALF, 1)

    def spec(r, c):
        return pl.BlockSpec((1, 1, r, c), lambda l, k: (l, k, 0, 0))

    win, wout, wskip, lam2 = pl.pallas_call(
        _discretize_kernel,
        grid=(nl, nb),
        in_specs=[spec(1, HALF)] * 3 + [spec(HALF, 1)] * 3 + [spec(LANE, HALF)] * 2 + [spec(HALF, LANE)] * 2,
        out_specs=[spec(2 * LANE, 2 * HALF), spec(2 * HALF, 2 * LANE), spec(2 * LANE, 2 * LANE), spec(1, 2 * HALF)],
        out_shape=[jax.ShapeDtypeStruct((nl, nb, 2 * LANE, 2 * HALF), BF16),
                   jax.ShapeDtypeStruct((nl, nb, 2 * HALF, 2 * LANE), BF16),
                   jax.ShapeDtypeStruct((nl, nb, 2 * LANE, 2 * LANE), BF16),
                   jax.ShapeDtypeStruct((nl, nb, 1, 2 * HALF), F32)],
        compiler_params=_params(2),
        name="s5_discretize",
    )(rows(lam_re), rows(lam_im), rows(log_dt), cols(lam_re), cols(lam_im), cols(log_dt),
      _block_diag_in(b_re), _block_diag_in(b_im), _block_diag_out(c_re), _block_diag_out(c_im))
    return win, wout, wskip, lam2.reshape(nl, nb, 2 * HALF)


def _slab_pitch(rows):
    assert rows % SUBLANE == 0
    return rows + SUBLANE // 2


def _split_state(st):
    return (tuple(st[:, p * LANE:(p + 1) * LANE] for p in range(N_PAIRS)),
            tuple(st[:, HALF + p * LANE:HALF + (p + 1) * LANE] for p in range(N_PAIRS)))


def _s5_mixer_kernel(x_ref, g2_ref, g3_ref, win_ref, wout_ref, wskip_ref, lam_ref, d_ref, wglu_ref, bglu_ref,
                     init_re_ref, init_im_ref, o_ref, final_re_ref, final_im_ref, state_ref, y_ref, s_ref,
                     *, n_blocks, n_seq, n_t, d_model):
    tile = pl.program_id(1)
    rows = n_seq * n_t
    pitch = _slab_pitch(rows)

    @pl.when(tile == 0)
    def _():
        state_ref[:, :, :HALF] = init_re_ref[...]
        state_ref[:, :, HALF:] = init_im_ref[...]

    lre, lim = _split_state(lam_ref[...])
    xe = x_ref[0, :, :d_model]
    xo = x_ref[0, :, d_model:]
    ue = _rms(xe, g2_ref[...])
    uo = _rms(xo, g2_ref[...])
    ueb = ue.astype(BF16)
    uob = uo.astype(BF16)


    u_pair = [jnp.concatenate([ueb[:, k * LANE:(k + 1) * LANE], uob[:, k * LANE:(k + 1) * LANE]], axis=1)
              for k in range(n_blocks)]
    for k in range(n_blocks):
        v = jnp.dot(u_pair[k], win_ref[k], preferred_element_type=F32)
        for j in range(N_SLABS):
            s_ref[j, pl.ds(k * pitch, rows), :] = v[:, j * LANE:(j + 1) * LANE]

    def seq_body(seq, carry):
        xre, xim = _split_state(state_ref[seq])

        def step(t, st):
            xre, xim = st
            pair = pl.ds(seq * n_t + t, SUBLANE, stride=pitch)
            nre, nim = [], []
            for p in range(N_PAIRS):
                vre = s_ref[p, pair, :]
                vim = s_ref[N_PAIRS + p, pair, :]
                s_ref[p, pair, :] = xre[p]
                s_ref[N_PAIRS + p, pair, :] = xim[p]
                nre.append(lre[p] * xre[p] - lim[p] * xim[p] + vre)
                nim.append(lre[p] * xim[p] + lim[p] * xre[p] + vim)
            return tuple(nre), tuple(nim)

        xre, xim = lax.fori_loop(0, n_t, step, (xre, xim), unroll=min(n_t, 8))
        state_ref[seq] = jnp.concatenate(list(xre) + list(xim), axis=1)
        return carry

    if n_seq == 1:
        seq_body(0, 0)
    else:
        lax.fori_loop(0, n_seq, seq_body, 0)

    for k in range(n_blocks):
        xb = jnp.concatenate([s_ref[j, pl.ds(k * pitch, rows), :] for j in range(N_SLABS)], axis=1)
        yk = (jnp.dot(xb.astype(BF16), wout_ref[k], preferred_element_type=F32)
              + jnp.dot(u_pair[k], wskip_ref[k], preferred_element_type=F32))
        y_ref[:, k * LANE:(k + 1) * LANE] = yk[:, :LANE]
        y_ref[:, d_model + k * LANE:d_model + (k + 1) * LANE] = yk[:, LANE:]
    y = jnp.concatenate([y_ref[:, :d_model] + d_ref[...] * ue, y_ref[:, d_model:] + d_ref[...] * uo], axis=0)
    h = jnp.dot(jax.nn.gelu(y).astype(BF16), wglu_ref[...], preferred_element_type=F32) + bglu_ref[...]
    r = _rms(h[:, :d_model] * _sigmoid(h[:, d_model:]), g3_ref[...])
    o_ref[0] = jnp.concatenate([xe + r[:rows], xo + r[rows:]], axis=1)

    @pl.when(tile == pl.num_programs(1) - 1)
    def _():
        final_re_ref[...] = state_ref[:, :, :HALF]
        final_im_ref[...] = state_ref[:, :, HALF:]


def _s5_layer(x, g2, g3, win, wout, wskip, lam2, d, wglu, bglu, init_re, init_im, layer, *, n_seq, n_t):
    nb_rows, s2, dm2 = x.shape
    dm = dm2 // 2
    n_blocks = win.shape[1]
    assert n_blocks == SUBLANE
    rows = n_seq * n_t
    n_tiles = s2 // rows
    assert n_seq == 1 or n_tiles == 1
    row = pl.BlockSpec((1, rows, dm2), lambda b, i: (b, i, 0))
    vec = _const_spec((1, dm))
    state = pl.BlockSpec((n_seq, SUBLANE, HALF), lambda b, i: (b, 0, 0))
    state_in = pl.BlockSpec(state.block_shape, state.index_map, pipeline_mode=pl.Buffered(1))
    state_shape = jax.ShapeDtypeStruct(init_re.shape, F32)
    slab = pltpu.VMEM((N_SLABS, n_blocks * _slab_pitch(rows), LANE), F32)
    return pl.pallas_call(
        functools.partial(_s5_mixer_kernel, n_blocks=n_blocks, n_seq=n_seq, n_t=n_t, d_model=dm),
        grid=(nb_rows, n_tiles),
        in_specs=[row, vec, vec, _layer_spec(win.shape, layer), _layer_spec(wout.shape, layer),
                  _layer_spec(wskip.shape, layer), _layer_spec(lam2.shape, layer), vec,
                  _layer_spec(wglu.shape, layer), _const_spec((1, 2 * dm)), state_in, state_in],
        out_specs=[row, state, state],
        out_shape=[jax.ShapeDtypeStruct(x.shape, F32), state_shape, state_shape],
        scratch_shapes=[pltpu.VMEM((n_seq, SUBLANE, 2 * HALF), F32), pltpu.VMEM((rows, dm2), F32), slab],
        compiler_params=_params(2),
        name="s5_mixer",
    )(x, g2, g3, win, wout, wskip, lam2, d, wglu, bglu, init_re, init_im)


def _state_rows(s):
    return s.reshape(s.shape[0], -1, HALF)


def _t5_bucket_np(dist):
    n = np.maximum(dist, 0)
    max_exact = NUM_BUCKETS // 2
    nf = np.maximum(n, 1).astype(np.float32)
    large = max_exact + (np.log(nf / np.float32(max_exact)) / np.float32(math.log(MAX_DISTANCE / max_exact))
                         * np.float32(NUM_BUCKETS - max_exact)).astype(np.int32)
    large = np.minimum(large, NUM_BUCKETS - 1)
    return np.where(n < max_exact, n, large).astype(np.int32)


def _bucket_table(n_q, n_k, q_offset, n_q_pad, n_k_pad, first_key=0):
    dist = (np.arange(n_q)[:, None] + q_offset) - np.arange(n_k)[None, :]
    valid = (dist >= 0) & (dist < WINDOW) & (np.arange(n_k)[None, :] >= first_key)
    table = np.full((n_q_pad, n_k_pad), -1, np.int32)
    table[:n_q, :n_k] = np.where(valid, _t5_bucket_np(dist), -1)
    return table


def _bias_kernel(rel_ref, idx_ref, o_ref):
    h = pl.program_id(0)
    idx = idx_ref[...]
    acc = jnp.full(idx.shape, NEG_INF, F32)
    for b in range(NUM_BUCKETS):
        acc = jnp.where(idx == b, rel_ref[b, h] * LOG2_E, acc)
    o_ref[0] = acc


def _bias_planes(rel_bias, table):
    n_heads = rel_bias.shape[1]
    r, c = table.shape
    return pl.pallas_call(
        _bias_kernel,
        grid=(n_heads,),
        in_specs=[pl.BlockSpec(memory_space=pltpu.SMEM), pl.BlockSpec((r, c), lambda h: (0, 0))],
        out_specs=pl.BlockSpec((1, r, c), lambda h: (h, 0, 0)),
        out_shape=jax.ShapeDtypeStruct((n_heads, r, c), F32),
        compiler_params=_params(),
        name="attn_bias",
    )(rel_bias, jnp.asarray(table))


def _bias_tiles_kernel(rel_ref, idx_ref, o_ref, *, n_kv, pairs):
    n_var, n_k, n_q = idx_ref.shape
    for v in range(n_var):
        idx = idx_ref[v]
        for g in range(n_kv):
            for pair in range(pairs):
                for par in range(2):
                    h = (g * pairs + pair) * 2 + par
                    acc = jnp.full(idx.shape, NEG_INF, F32)
                    for b in range(NUM_BUCKETS):
                        acc = jnp.where(idx == b, rel_ref[b, h] * LOG2_E, acc)
                    o_ref[v, g, par * n_k:(par + 1) * n_k, pair * n_q:(pair + 1) * n_q] = acc


def _bias_tiles(rel_bias, tables, n_kv):
    n_heads = rel_bias.shape[1]
    pairs = n_heads // n_kv // 2
    idx = jnp.asarray(np.stack([t.T for t in tables]))
    n_var, n_k, n_q = idx.shape
    return pl.pallas_call(
        functools.partial(_bias_tiles_kernel, n_kv=n_kv, pairs=pairs),
        in_specs=[pl.BlockSpec(memory_space=pltpu.SMEM), pl.BlockSpec(memory_space=pltpu.VMEM)],
        out_specs=pl.BlockSpec(memory_space=pltpu.VMEM),
        out_shape=jax.ShapeDtypeStruct((n_var, n_kv, 2 * n_k, pairs * n_q), F32),
        compiler_params=pltpu.CompilerParams(vmem_limit_bytes=VMEM_LIMIT_BYTES),
        name="attn_bias_tiles",
    )(rel_bias, idx)


def _kv_head_rows(kv, g):
    lane = lax.broadcasted_iota(jnp.int32, kv.shape, 1)
    swapped = pltpu.roll(kv, HEAD_DIM, axis=1)
    even, odd = (kv, swapped) if g == 0 else (swapped, kv)
    return jnp.where(lane < HEAD_DIM, even, 0.0), jnp.where(lane >= HEAD_DIM, odd, 0.0)


def _pair_layout_t(planes, n_kv, n_q):
    h, _, n_k = planes.shape
    pairs = h // n_kv // 2
    p = planes[:, :n_q].reshape(n_kv, pairs, 2, n_q, n_k).transpose(0, 2, 4, 1, 3)
    return p.reshape(n_kv, 2 * n_k, pairs * n_q)


def _sink_cols(sinks, n_kv, n_q):
    h = sinks.shape[0]
    pairs = h // n_kv // 2
    s = sinks.reshape(n_kv, pairs, 2).transpose(0, 2, 1)[..., None]
    return jnp.broadcast_to(s, (n_kv, 2, pairs, n_q)).reshape(n_kv, 2, pairs * n_q)


def _attend_t(chains):
    scores = []
    for q_t, k_even, k_odd, _, _, bias, _ in chains:
        ke = jnp.concatenate([k_even, k_odd], axis=0).astype(BF16)
        scores.append(jnp.dot(ke, q_t, preferred_element_type=F32) + bias)
    weights = []
    for s, chain in zip(scores, chains):
        sink = chain[6]
        n_k = s.shape[0] // 2
        probs, inv = [], []
        for par in range(2):
            sp = s[par * n_k:(par + 1) * n_k]
            sk = sink[par:par + 1] * LOG2_E
            mx = jnp.maximum(jnp.max(sp, axis=0, keepdims=True), sk)
            p = jnp.exp2(sp - mx)
            den = jnp.sum(p, axis=0, keepdims=True) + jnp.exp2(sk - mx)
            probs.append(p.astype(BF16))
            inv.append(1.0 / den)
        weights.append((jnp.concatenate(probs, axis=0), inv))
    outs = []
    for (p_t, inv), chain in zip(weights, chains):
        ve = jnp.concatenate([chain[3], chain[4]], axis=0).astype(BF16)
        o = lax.dot_general(ve, p_t, (((0,), (0,)), ((), ())), preferred_element_type=F32)
        row = lax.broadcasted_iota(jnp.int32, o.shape, 0)
        outs.append(o * jnp.where(row < HEAD_DIM, inv[0], inv[1]))
    return outs


def _prompt_attn_kernel(x_ref, kvc_ref, kvp_ref, bias_ref, sink_ref, g2_ref, g3_ref, wq_ref, bq_ref,
                        wo_ref, bo_ref, o_ref, *, n_kv, n_sub, steps_per_seq, scale):
    blk = WINDOW
    x = x_ref[...]
    u = _rms(x, g2_ref[...]).astype(BF16)
    q_t = lax.dot_general(wq_ref[...], u, (((1,), (1,)), ((), ())), preferred_element_type=F32)
    q_t = ((q_t + jnp.concatenate([bq_ref[...]] * n_sub, axis=1)) * scale).astype(BF16)
    kv = jnp.concatenate([kvp_ref[...], kvc_ref[...]], axis=0)
    pairs = q_t.shape[0] // LANE // n_kv
    first = jnp.where(pl.program_id(0) % steps_per_seq == 0, 1, 0)
    k_rows = [_kv_head_rows(kv[:, :n_kv * HEAD_DIM], g) for g in range(n_kv)]
    v_rows = [_kv_head_rows(kv[:, n_kv * HEAD_DIM:], g) for g in range(n_kv)]
    chains = []
    for s in range(n_sub):
        keys = slice(s * blk, (s + 2) * blk)
        for g in range(n_kv):
            qp = jnp.concatenate([q_t[(g * pairs + i) * LANE:(g * pairs + i + 1) * LANE, s * blk:(s + 1) * blk]
                                  for i in range(pairs)], axis=1)
            bias = bias_ref[first, g] if s == 0 else bias_ref[0, g]
            chains.append((qp, k_rows[g][0][keys], k_rows[g][1][keys], v_rows[g][0][keys],
                           v_rows[g][1][keys], bias, sink_ref[g]))
    outs = _attend_t(chains)
    o_t = jnp.concatenate(
        [jnp.concatenate([outs[s * n_kv + g][:, i * blk:(i + 1) * blk] for s in range(n_sub)], axis=1)
         for g in range(n_kv) for i in range(pairs)], axis=0).astype(BF16)
    a = lax.dot_general(o_t, wo_ref[...], (((0,), (0,)), ((), ())), preferred_element_type=F32) + bo_ref[...]
    o_ref[...] = x + _rms(a, g3_ref[...])


def _prompt_attention(x, kv, bias, sink, g2, g3, wq_t, bq_col, wo, bo, layer, *, blocks_per_seq, n_kv):
    t, d = x.shape
    blk = WINDOW
    n_sub = ATTN_BLOCKS_PER_STEP
    assert n_kv * HEAD_DIM == LANE and blocks_per_seq % n_sub == 0
    row = pl.BlockSpec((n_sub * blk, d), lambda i: (i, 0))
    kvw = kv.shape[1]
    return pl.pallas_call(
        functools.partial(_prompt_attn_kernel, n_kv=n_kv, n_sub=n_sub,
                          steps_per_seq=blocks_per_seq // n_sub, scale=QUERY_SCALE),
        grid=(t // (n_sub * blk),),
        in_specs=[row,
                  pl.BlockSpec((n_sub * blk, kvw), lambda i: (i, 0)),
                  pl.BlockSpec((blk, kvw), lambda i: (jnp.maximum(i * n_sub - 1, 0), 0)),
                  _const_spec(bias.shape),
                  _const_spec(sink.shape), _const_spec((1, d)), _const_spec((1, d)),
                  _layer_spec(wq_t.shape, layer), _const_spec(bq_col.shape),
                  _layer_spec(wo.shape, layer), _const_spec((1, d))],
        out_specs=row,
        out_shape=jax.ShapeDtypeStruct((t, d), F32),
        compiler_params=_params(),
        name="prompt_attention",
    )(x, kv, kv, bias, sink, g2, g3, wq_t, bq_col, wo, bo)


def _sample_attn_kernel(q_ref, kc_ref, vc_ref, kvn_ref, bias_ref, sink_ref, o_ref, *, n_kv, scale):
    half = n_kv * HEAD_DIM
    chains = []
    for b in range(q_ref.shape[0]):
        k = jnp.concatenate([kc_ref[b], kvn_ref[b, :, :half]], axis=0)
        v = jnp.concatenate([vc_ref[b], kvn_ref[b, :, half:]], axis=0)
        for g in range(n_kv):
            chains.append(((q_ref[b, g] * scale).astype(BF16), *_kv_head_rows(k, g), *_kv_head_rows(v, g),
                           bias_ref[g], sink_ref[g]))
    outs = _attend_t(chains)
    for b in range(q_ref.shape[0]):
        for g in range(n_kv):
            o_ref[b, g] = outs[b * n_kv + g]


def _sample_attention(q_t, k_cache, v_cache, kv_new, bias, sink, *, n_kv):
    nb, _, _, n_l = q_t.shape
    n_past = k_cache.shape[1]
    bb = SUBLANE
    return pl.pallas_call(
        functools.partial(_sample_attn_kernel, n_kv=n_kv, scale=QUERY_SCALE),
        grid=(nb // bb,),
        in_specs=[pl.BlockSpec((bb, n_kv, LANE, n_l), lambda i: (i, 0, 0, 0)),
                  pl.BlockSpec((bb, n_past, LANE), lambda i: (i, 0, 0)),
                  pl.BlockSpec((bb, n_past, LANE), lambda i: (i, 0, 0)),
                  pl.BlockSpec((bb,) + kv_new.shape[1:], lambda i: (i, 0, 0)),
                  _const_spec(bias.shape), _const_spec(sink.shape)],
        out_specs=pl.BlockSpec((bb, n_kv, LANE, n_l), lambda i: (i, 0, 0, 0)),
        out_shape=jax.ShapeDtypeStruct(q_t.shape, F32),
        compiler_params=_params(),
        name="sample_attention",
    )(q_t, k_cache, v_cache, kv_new, bias, sink)


def kernel(x_prompt, x_sample, state_ssm_re, state_ssm_im, cache_win_k, cache_win_v, norm_g, ffn1_w_gu, ffn1_w_down, ffn2_w_gu, ffn2_w_down, ssm_lambda_re, ssm_lambda_im, ssm_log_dt, ssm_b_re, ssm_b_im, ssm_c_re, ssm_c_im, ssm_d, ssm_w_glu, ssm_b_glu, kv_norm_g, w_kv, b_kv, attn_w_q, attn_b_q, attn_sinks, attn_w_o, attn_b_o, rel_bias):
    bsz, seq, dm = x_prompt.shape
    dec_b, dec_s, _ = x_sample.shape
    depth = norm_g.shape[0]
    n_a = ssm_lambda_re.shape[0]
    n_groups = ssm_lambda_re.shape[1]
    n_heads = attn_sinks.shape[1]
    n_kv = cache_win_k.shape[2]
    n_past = cache_win_k.shape[1]
    pairs = n_heads // n_kv // 2
    n_k_pad = n_past + SUBLANE
    assert seq % ROW_TILE == 0 and seq % WINDOW == 0 and dec_s % 2 == 0
    assert dec_b * dec_s == ROW_TILE and dec_s <= SUBLANE and n_past % SUBLANE == 0
    assert dec_b % (SUBLANE * SAMPLE_SPLIT) == 0
    assert n_kv == 2 and dm == n_heads * HEAD_DIM and n_groups == SUBLANE * GROUPS_PER_BLOCK

    bf = lambda w: w.astype(BF16)
    row = lambda v: v.reshape(1, -1)
    s5_w = _discretize(ssm_lambda_re, ssm_lambda_im, ssm_log_dt, ssm_b_re, ssm_b_im, ssm_c_re, ssm_c_im)
    ffn_w = (bf(ffn1_w_gu[0])[None], bf(ffn1_w_down[0])[None])
    wglu, wkv, wq, wo = bf(ssm_w_glu), bf(w_kv)[None], bf(attn_w_q), bf(attn_w_o)

    tab_p = _bucket_table(WINDOW, 2 * WINDOW, WINDOW, WINDOW, 2 * WINDOW)
    tab_p0 = _bucket_table(WINDOW, 2 * WINDOW, WINDOW, WINDOW, 2 * WINDOW, first_key=WINDOW)
    tab_s = _bucket_table(dec_s, n_past + dec_s, n_past, SUBLANE, n_k_pad)
    wq_t = wq.transpose(0, 2, 1)
    bias_p = _bias_tiles(rel_bias, [tab_p, tab_p0], n_kv)
    bias_s = _pair_layout_t(_bias_planes(rel_bias, tab_s), n_kv, dec_s)

    xp = x_prompt.reshape(bsz * seq, dm)
    xs = x_sample.reshape(dec_b * dec_s, dm)
    zero_state = jnp.zeros((bsz, SUBLANE, HALF), F32)

    ends_p, ends_s = [], []
    kv_p = kv_s = kv_new = None
    k_cache = cache_win_k.reshape(dec_b, n_past, n_kv * HEAD_DIM)
    v_cache = cache_win_v.reshape(dec_b, n_past, n_kv * HEAD_DIM)
    half = n_kv * HEAD_DIM
    for l in range(depth):
        g = [row(norm_g[l, i]) for i in range(norm_g.shape[1])]
        pair_io = dict(pair_in=0 < l < n_a, pair_out=l < n_a)
        xp, xs, *ffn_w = _ffn(xp, xs, g[0], g[1], *ffn_w, **pair_io, cast_next=(ffn2_w_gu, ffn2_w_down, l))
        if l < n_a:
            common = (g[2], g[3], *s5_w, row(ssm_d[l]), wglu, row(ssm_b_glu[l]))
            xp3, *fin_p = _s5_layer(xp.reshape(bsz, seq // 2, 2 * dm), *common, zero_state, zero_state, l,
                                    n_seq=1, n_t=ROW_TILE // 2)
            xs3, *fin_s = _s5_layer(xs.reshape(SAMPLE_SPLIT, dec_b * dec_s // (2 * SAMPLE_SPLIT), 2 * dm), *common,
                                    _state_rows(state_ssm_re[l]), _state_rows(state_ssm_im[l]), l,
                                    n_seq=dec_b // SAMPLE_SPLIT, n_t=dec_s // 2)
            xp, xs = xp3.reshape(bsz * seq // 2, 2 * dm), xs3.reshape(dec_b * dec_s // 2, 2 * dm)
            ends_p.append([s.reshape(bsz, n_groups, STATE_DIM) for s in fin_p])
            ends_s.append([s.reshape(dec_b, n_groups, STATE_DIM) for s in fin_s])
        else:
            bl = l - n_a
            bq_l, bo_l = row(attn_b_q[bl]), row(attn_b_o[bl])
            bq_col = jnp.broadcast_to(attn_b_q[bl][:, None], (dm, WINDOW))
            xp = _prompt_attention(xp, kv_p, bias_p, _sink_cols(attn_sinks[bl], n_kv, WINDOW), g[2], g[3],
                                   wq_t, bq_col, wo, bo_l, bl, blocks_per_seq=seq // WINDOW, n_kv=n_kv)
            q = _norm_proj(xs, g[2], wq, bq_l, bl)
            q = q.reshape(dec_b, dec_s, n_kv, pairs, LANE).transpose(0, 2, 4, 3, 1)
            q = q.reshape(dec_b, n_kv, LANE, pairs * dec_s)
            o = _sample_attention(q, k_cache, v_cache, kv_new, bias_s, _sink_cols(attn_sinks[bl], n_kv, dec_s), n_kv=n_kv)
            o = o.reshape(dec_b, n_kv, LANE, pairs, dec_s).transpose(0, 4, 1, 3, 2).reshape(dec_b * dec_s, dm)
            xs = _out_proj_residual(xs, o, g[3], wo, bo_l, bl)
        pair_io = dict(pair_in=l < n_a, pair_out=l < n_a - 1)
        cast_next = (ffn1_w_gu, ffn1_w_down, l + 1) if l + 1 < depth else None
        if l == n_a - 1:
            xp, xs, kv_p, kv_s, *ffn_w = _ffn(xp, xs, g[4], g[5], *ffn_w, **pair_io, cast_next=cast_next,
                                              kv=(row(kv_norm_g), wkv, row(b_kv)))
            kv_s = kv_s.reshape(dec_b, dec_s, 2 * half)
            kv_new = jnp.pad(kv_s, ((0, 0), (0, SUBLANE - dec_s), (0, 0)))
        else:
            xp, xs, *ffn_w = _ffn(xp, xs, g[4], g[5], *ffn_w, **pair_io, cast_next=cast_next)

    kv_p3 = kv_p.reshape(bsz, seq, 2 * half)
    new_k_p = kv_p3[:, -WINDOW:, :half].reshape(bsz, WINDOW, n_kv, HEAD_DIM)
    new_v_p = kv_p3[:, -WINDOW:, half:].reshape(bsz, WINDOW, n_kv, HEAD_DIM)
    new_k_s = jnp.concatenate([k_cache, kv_s[:, :, :half]], axis=1)[:, -WINDOW:].reshape(dec_b, WINDOW, n_kv, HEAD_DIM)
    new_v_s = jnp.concatenate([v_cache, kv_s[:, :, half:]], axis=1)[:, -WINDOW:].reshape(dec_b, WINDOW, n_kv, HEAD_DIM)
    return (xp.reshape(bsz, seq, dm), xs.reshape(dec_b, dec_s, dm),
            jnp.stack([e[0] for e in ends_p]), jnp.stack([e[1] for e in ends_p]),
            new_k_p, new_v_p,
            jnp.stack([e[0] for e in ends_s]), jnp.stack([e[1] for e in ends_s]),
            new_k_s, new_v_s)
```

```python
import functools
import math

import numpy as np
import jax
import jax.numpy as jnp
from jax import lax
from jax.experimental import pallas as pl
from jax.experimental.pallas import tpu as pltpu

F32 = jnp.float32
BF16 = jnp.bfloat16

LANE = 128
SUBLANE = 8
VMEM_LIMIT_BYTES = 56 * 1024 * 1024

RMS_EPS = 1e-6
GROUP_SIZE = 16
STATE_DIM = 64
HEAD_DIM = 64
WINDOW = 128
NUM_BUCKETS = 32
MAX_DISTANCE = WINDOW
GROUPS_PER_BLOCK = LANE // GROUP_SIZE
HALF = GROUPS_PER_BLOCK * STATE_DIM
N_SLABS = 2 * HALF // LANE
N_PAIRS = N_SLABS // 2
ROW_TILE = 512
SAMPLE_SPLIT = 2
FFN_CHAINS = 2
ATTN_BLOCKS_PER_STEP = 8
NEG_INF = float("-inf")
LOG2_E = math.log2(math.e)
QUERY_SCALE = LOG2_E / math.sqrt(HEAD_DIM)


def _params(n_axes=1):
    return pltpu.CompilerParams(dimension_semantics=("arbitrary",) * n_axes,
                                vmem_limit_bytes=VMEM_LIMIT_BYTES)


def _const_spec(shape):
    nd = len(shape)
    return pl.BlockSpec(shape, lambda *_: (0,) * nd, pipeline_mode=pl.Buffered(1))


def _layer_spec(shape, layer):
    nd = len(shape)
    return pl.BlockSpec((None,) + tuple(shape[1:]), lambda *_: (layer,) + (0,) * (nd - 1),
                        pipeline_mode=pl.Buffered(1))


def _rms(x, g):
    ms = jnp.mean(x * x, axis=-1, keepdims=True)
    return x * lax.rsqrt(ms + RMS_EPS) * g


def _sigmoid(x):
    return 1.0 / (1.0 + jnp.exp(-x))


def _cast_chunk(rows, n_steps):
    tile = 2 * SUBLANE
    for chunk in range(tile, rows + 1, tile):
        if rows % chunk == 0 and rows // chunk <= n_steps:
            return chunk
    raise ValueError((rows, n_steps))


def _ffn_kernel(*refs, d_ff, pair_in, pair_out, cast_steps, with_kv):
    refs = list(refs)
    x_ref, xr_ref, ga_ref, gb_ref, wgu_ref, wd_ref = refs[:6]
    del refs[:6]
    kvg_ref, wkv_ref, bkv_ref = (refs.pop(0), refs.pop(0), refs.pop(0)) if with_kv else (None,) * 3
    cast_src = (refs.pop(0), refs.pop(0)) if cast_steps else ()
    o_ref, or_ref = refs.pop(0), refs.pop(0)
    kv_ref, kvr_ref = (refs.pop(0), refs.pop(0)) if with_kv else (None, None)
    cast_dst = (refs.pop(0), refs.pop(0)) if cast_steps else ()
    slab_ref = refs
    for src, dst, steps in zip(cast_src, cast_dst, cast_steps or ()):
        @pl.when(pl.program_id(0) < steps)
        def _(src=src, dst=dst):
            dst[...] = src[...].astype(BF16)
    args = (ga_ref, gb_ref, wgu_ref, wd_ref, kvg_ref, wkv_ref, bkv_ref, slab_ref)
    _ffn_tile(x_ref, o_ref, kv_ref, *args, d_ff=d_ff, pair_in=pair_in, pair_out=pair_out)

    @pl.when(pl.program_id(0) == pl.num_programs(0) - 1)
    def _():
        _ffn_tile(xr_ref, or_ref, kvr_ref, *args, d_ff=d_ff, pair_in=pair_in, pair_out=pair_out)


def _ffn_tile(x_ref, o_ref, kv_ref, ga_ref, gb_ref, wgu_ref, wd_ref, kvg_ref, wkv_ref, bkv_ref, slab_ref,
              *, d_ff, pair_in, pair_out):
    d = ga_ref.shape[1]
    if pair_in:
        x = jnp.concatenate([x_ref[:, :d], x_ref[:, d:]], axis=0)
    else:
        x = x_ref[...]
    half = x.shape[0] // 2
    rc = x.shape[0] // FFN_CHAINS
    xc = [x[c * rc:(c + 1) * rc] for c in range(FFN_CHAINS)]
    xn = [_rms(v, ga_ref[...]).astype(BF16) for v in xc]
    gu = [jnp.dot(v, wgu_ref[...], preferred_element_type=F32) for v in xn]
    h = [(v[:, :d_ff] * _sigmoid(v[:, :d_ff]) * v[:, d_ff:]).astype(BF16) for v in gu]
    y = [jnp.dot(v, wd_ref[...], preferred_element_type=F32) for v in h]
    out = jnp.concatenate([v + 0.5 * _rms(w, gb_ref[...]) for v, w in zip(xc, y)], axis=0)
    if pair_in == pair_out:
        o_ref[...] = jnp.concatenate([out[:half], out[half:]], axis=1) if pair_in else out
    else:
        (slab,) = slab_ref
        even = pl.ds(0, half, stride=2)
        odd = pl.ds(1, half, stride=2)
        for j in range(d // LANE):
            lanes = slice(j * LANE, (j + 1) * LANE)
            if pair_out:
                slab[j] = out[:, lanes]
                o_ref[:, lanes] = slab[j, even, :]
                o_ref[:, d + j * LANE:d + (j + 1) * LANE] = slab[j, odd, :]
            else:
                slab[j, even, :] = out[:half, lanes]
                slab[j, odd, :] = out[half:, lanes]
                o_ref[:, lanes] = slab[j]
    if kv_ref is not None:
        assert not pair_out
        kv_ref[...] = jnp.dot(_rms(o_ref[...], kvg_ref[...]).astype(BF16), wkv_ref[...],
                              preferred_element_type=F32) + bkv_ref[...]


def _ffn(x, x_rider, ga, gb, wgu, wd, pair_in=False, pair_out=False, cast_next=None, kv=None):
    d = ga.shape[1]
    t = x.shape[0] * (2 if pair_in else 1)
    d_ff = wd.shape[1]
    tm = ROW_TILE
    n_steps = t // tm
    assert t % tm == 0 and x_rider.shape[0] * (2 if pair_in else 1) == tm

    def tile(paired):
        return (tm // 2, 2 * d) if paired else (tm, d)

    def rows(paired):
        return pl.BlockSpec(tile(paired), lambda i: (i, 0))

    def rider(shape):
        return pl.BlockSpec(shape, lambda i: (0, 0))

    in_specs = [rows(pair_in), _const_spec(tile(pair_in)), _const_spec((1, d)), _const_spec((1, d)),
                _layer_spec(wgu.shape, 0), _layer_spec(wd.shape, 0)]
    out_specs = [rows(pair_out), rider(tile(pair_out))]
    out_shape = [jax.ShapeDtypeStruct((t // 2, 2 * d) if pair_out else (t, d), F32),
                 jax.ShapeDtypeStruct(tile(pair_out), F32)]
    operands = [x, x_rider, ga, gb, wgu, wd]
    if kv is not None:
        kv_g, kv_w, kv_b = kv
        n_kv_out = kv_w.shape[-1]
        in_specs += [_const_spec((1, d)), _layer_spec(kv_w.shape, 0), _const_spec((1, n_kv_out))]
        out_specs += [pl.BlockSpec((tm, n_kv_out), lambda i: (i, 0)), rider((tm, n_kv_out))]
        out_shape += [jax.ShapeDtypeStruct((t, n_kv_out), F32), jax.ShapeDtypeStruct((tm, n_kv_out), F32)]
        operands += [kv_g, kv_w, kv_b]
    cast_steps = None
    if cast_next is not None:
        *next_w, layer = cast_next
        cast_steps = []
        for w in next_w:
            chunk = _cast_chunk(w.shape[1], n_steps)
            last = w.shape[1] // chunk - 1
            cast_steps.append(last + 1)
            in_specs.append(pl.BlockSpec((None, chunk, w.shape[2]),
                                         lambda i, last=last: (layer, jnp.minimum(i, last), 0)))
            out_specs.append(pl.BlockSpec((None, chunk, w.shape[2]),
                                          lambda i, last=last: (0, jnp.minimum(i, last), 0)))
            out_shape.append(jax.ShapeDtypeStruct((1,) + w.shape[1:], BF16))
            operands.append(w)
    return pl.pallas_call(
        functools.partial(_ffn_kernel, d_ff=d_ff, pair_in=pair_in, pair_out=pair_out, cast_steps=cast_steps,
                          with_kv=kv is not None),
        grid=(n_steps,),
        in_specs=in_specs,
        out_specs=out_specs,
        out_shape=out_shape,
        scratch_shapes=[pltpu.VMEM((d // LANE, tm, LANE), F32)] if pair_in != pair_out else [],
        compiler_params=_params(),
        name="ffn",
    )(*operands)


def _proj_kernel(x_ref, g_ref, w_ref, b_ref, o_ref):
    xn = _rms(x_ref[...], g_ref[...]).astype(BF16)
    o_ref[...] = jnp.dot(xn, w_ref[...], preferred_element_type=F32) + b_ref[...]


def _norm_proj(x, g, w, b, layer):
    t, d = x.shape
    n = w.shape[-1]
    tm = min(ROW_TILE, t)
    return pl.pallas_call(
        _proj_kernel,
        grid=(t // tm,),
        in_specs=[pl.BlockSpec((tm, d), lambda i: (i, 0)), _const_spec((1, d)),
                  _layer_spec(w.shape, layer), _const_spec((1, n))],
        out_specs=pl.BlockSpec((tm, n), lambda i: (i, 0)),
        out_shape=jax.ShapeDtypeStruct((t, n), F32),
        compiler_params=_params(),
        name="norm_proj",
    )(x, g, w, b)


def _out_proj_kernel(x_ref, o_ref_in, g_ref, w_ref, b_ref, out_ref):
    a = jnp.dot(o_ref_in[...].astype(BF16), w_ref[...], preferred_element_type=F32) + b_ref[...]
    out_ref[...] = x_ref[...] + _rms(a, g_ref[...])


def _out_proj_residual(x, o, g, w, b, layer):
    t, d = x.shape
    tm = min(ROW_TILE, t)
    row = pl.BlockSpec((tm, d), lambda i: (i, 0))
    return pl.pallas_call(
        _out_proj_kernel,
        grid=(t // tm,),
        in_specs=[row, pl.BlockSpec((tm, o.shape[1]), lambda i: (i, 0)), _const_spec((1, d)),
                  _layer_spec(w.shape, layer), _const_spec((1, d))],
        out_specs=row,
        out_shape=jax.ShapeDtypeStruct((t, d), F32),
        compiler_params=_params(),
        name="out_proj_residual",
    )(x, o, g, w, b)


def _lam_bar(lam_re, lam_im, log_dt):
    dt = jnp.exp(log_dt)
    mag = jnp.exp(lam_re * dt)
    return mag * jnp.cos(lam_im * dt), mag * jnp.sin(lam_im * dt)


def _discretize_kernel(lre_ref, lim_ref, ldt_ref, lre_col_ref, lim_col_ref, ldt_col_ref,
                       bre_ref, bim_ref, cre_ref, cim_ref, win_ref, wout_ref, wskip_ref, lam2_ref):
    a = lre_ref[0, 0]
    b = lim_ref[0, 0]
    lbr, lbi = _lam_bar(a, b, ldt_ref[0, 0])
    den = a * a + b * b
    cr = ((lbr - 1.0) * a + lbi * b) / den
    ci = (lbi * a - (lbr - 1.0) * b) / den
    bre = bre_ref[0, 0]
    bim = bim_ref[0, 0]
    wbr = cr * bre - ci * bim
    wbi = cr * bim + ci * bre
    lwbr = lbr * wbr - lbi * wbi
    lwbi = lbr * wbi + lbi * wbr
    win_ref[0, 0, :LANE, :HALF] = lwbr.astype(BF16)
    win_ref[0, 0, :LANE, HALF:] = lwbi.astype(BF16)
    win_ref[0, 0, LANE:, :HALF] = wbr.astype(BF16)
    win_ref[0, 0, LANE:, HALF:] = wbi.astype(BF16)
    lam2_ref[0, 0, :, :HALF] = lbr * lbr - lbi * lbi
    lam2_ref[0, 0, :, HALF:] = 2.0 * lbr * lbi
    lcr, lci = _lam_bar(lre_col_ref[0, 0], lim_col_ref[0, 0], ldt_col_ref[0, 0])
    cre = cre_ref[0, 0]
    cim = cim_ref[0, 0]
    c1r = cre * lcr - cim * lci
    c1i = cre * lci + cim * lcr
    wout_ref[0, 0, :HALF, :LANE] = c1r.astype(BF16)
    wout_ref[0, 0, HALF:, :LANE] = (-c1i).astype(BF16)
    wout_ref[0, 0, :HALF, LANE:] = (c1r * lcr - c1i * lci).astype(BF16)
    wout_ref[0, 0, HALF:, LANE:] = (-(c1r * lci + c1i * lcr)).astype(BF16)
    def through(br, bi):
        return (jnp.dot(br, cre, precision=lax.Precision.HIGHEST, preferred_element_type=F32)
                - jnp.dot(bi, cim, precision=lax.Precision.HIGHEST, preferred_element_type=F32)).astype(BF16)

    cb = through(wbr, wbi)
    wskip_ref[0, 0, :LANE, :LANE] = cb
    wskip_ref[0, 0, :LANE, LANE:] = through(lwbr, lwbi)
    wskip_ref[0, 0, LANE:, :LANE] = jnp.zeros((LANE, LANE), BF16)
    wskip_ref[0, 0, LANE:, LANE:] = cb


def _block_diag_in(b):
    nl, g, p, h = b.shape
    nb = g // GROUPS_PER_BLOCK
    bt = b.transpose(0, 1, 3, 2).reshape(nl, nb, GROUPS_PER_BLOCK, h, p)
    eye = jnp.eye(GROUPS_PER_BLOCK, dtype=b.dtype)
    out = bt[:, :, :, :, None, :] * eye[None, None, :, None, :, None]
    return out.reshape(nl, nb, GROUPS_PER_BLOCK * h, GROUPS_PER_BLOCK * p)


def _block_diag_out(c):
    nl, g, h, p = c.shape
    nb = g // GROUPS_PER_BLOCK
    ct = c.transpose(0, 1, 3, 2).reshape(nl, nb, GROUPS_PER_BLOCK, p, h)
    eye = jnp.eye(GROUPS_PER_BLOCK, dtype=c.dtype)
    out = ct[:, :, :, :, None, :] * eye[None, None, :, None, :, None]
    return out.reshape(nl, nb, GROUPS_PER_BLOCK * p, GROUPS_PER_BLOCK * h)


def _discretize(lam_re, lam_im, log_dt, b_re, b_im, c_re, c_im):
    nl, g, p = lam_re.shape
    nb = g // GROUPS_PER_BLOCK

    def rows(v):
        return v.reshape(nl, nb, 1, HALF)

    def cols(v):
        return v.reshape(nl, nb, HALF, 1)

    def spec(r, c):
        return pl.BlockSpec((1, 1, r, c), lambda l, k: (l, k, 0, 0))

    win, wout, wskip, lam2 = pl.pallas_call(
        _discretize_kernel,
        grid=(nl, nb),
        in_specs=[spec(1, HALF)] * 3 + [spec(HALF, 1)] * 3 + [spec(LANE, HALF)] * 2 + [spec(HALF, LANE)] * 2,
        out_specs=[spec(2 * LANE, 2 * HALF), spec(2 * HALF, 2 * LANE), spec(2 * LANE, 2 * LANE), spec(1, 2 * HALF)],
        out_shape=[jax.ShapeDtypeStruct((nl, nb, 2 * LANE, 2 * HALF), BF16),
                   jax.ShapeDtypeStruct((nl, nb, 2 * HALF, 2 * LANE), BF16),
                   jax.ShapeDtypeStruct((nl, nb, 2 * LANE, 2 * LANE), BF16),
                   jax.ShapeDtypeStruct((nl, nb, 1, 2 * HALF), F32)],
        compiler_params=_params(2),
        name="s5_discretize",
    )(rows(lam_re), rows(lam_im), rows(log_dt), cols(lam_re), cols(lam_im), cols(log_dt),
      _block_diag_in(b_re), _block_diag_in(b_im), _block_diag_out(c_re), _block_diag_out(c_im))
    return win, wout, wskip, lam2.reshape(nl, nb, 2 * HALF)


def _slab_pitch(rows):
    assert rows % SUBLANE == 0
    return rows + SUBLANE // 2


def _split_state(st):
    return (tuple(st[:, p * LANE:(p + 1) * LANE] for p in range(N_PAIRS)),
            tuple(st[:, HALF + p * LANE:HALF + (p + 1) * LANE] for p in range(N_PAIRS)))


def _s5_mixer_kernel(x_ref, g2_ref, g3_ref, win_ref, wout_ref, wskip_ref, lam_ref, d_ref, wglu_ref, bglu_ref,
                     init_re_ref, init_im_ref, o_ref, final_re_ref, final_im_ref, state_ref, y_ref, s_ref,
                     *, n_blocks, n_seq, n_t, d_model):
    tile = pl.program_id(1)
    rows = n_seq * n_t
    pitch = _slab_pitch(rows)

    @pl.when(tile == 0)
    def _():
        state_ref[:, :, :HALF] = init_re_ref[...]
        state_ref[:, :, HALF:] = init_im_ref[...]

    lre, lim = _split_state(lam_ref[...])
    xe = x_ref[0, :, :d_model]
    xo = x_ref[0, :, d_model:]
    ue = _rms(xe, g2_ref[...])
    uo = _rms(xo, g2_ref[...])
    ueb = ue.astype(BF16)
    uob = uo.astype(BF16)


    u_pair = [jnp.concatenate([ueb[:, k * LANE:(k + 1) * LANE], uob[:, k * LANE:(k + 1) * LANE]], axis=1)
              for k in range(n_blocks)]
    for k in range(n_blocks):
        v = jnp.dot(u_pair[k], win_ref[k], preferred_element_type=F32)
        for j in range(N_SLABS):
            s_ref[j, pl.ds(k * pitch, rows), :] = v[:, j * LANE:(j + 1) * LANE]

    def seq_body(seq, carry):
        xre, xim = _split_state(state_ref[seq])

        def step(t, st):
            xre, xim = st
            pair = pl.ds(seq * n_t + t, SUBLANE, stride=pitch)
            nre, nim = [], []
            for p in range(N_PAIRS):
                vre = s_ref[p, pair, :]
                vim = s_ref[N_PAIRS + p, pair, :]
                s_ref[p, pair, :] = xre[p]
                s_ref[N_PAIRS + p, pair, :] = xim[p]
                nre.append(lre[p] * xre[p] - lim[p] * xim[p] + vre)
                nim.append(lre[p] * xim[p] + lim[p] * xre[p] + vim)
            return tuple(nre), tuple(nim)

        xre, xim = lax.fori_loop(0, n_t, step, (xre, xim), unroll=min(n_t, 8))
        state_ref[seq] = jnp.concatenate(list(xre) + list(xim), axis=1)
        return carry

    if n_seq == 1:
        seq_body(0, 0)
    else:
        lax.fori_loop(0, n_seq, seq_body, 0)

    for k in range(n_blocks):
        xb = jnp.concatenate([s_ref[j, pl.ds(k * pitch, rows), :] for j in range(N_SLABS)], axis=1)
        yk = (jnp.dot(xb.astype(BF16), wout_ref[k], preferred_element_type=F32)
              + jnp.dot(u_pair[k], wskip_ref[k], preferred_element_type=F32))
        y_ref[:, k * LANE:(k + 1) * LANE] = yk[:, :LANE]
        y_ref[:, d_model + k * LANE:d_model + (k + 1) * LANE] = yk[:, LANE:]
    y = jnp.concatenate([y_ref[:, :d_model] + d_ref[...] * ue, y_ref[:, d_model:] + d_ref[...] * uo], axis=0)
    h = jnp.dot(jax.nn.gelu(y).astype(BF16), wglu_ref[...], preferred_element_type=F32) + bglu_ref[...]
    r = _rms(h[:, :d_model] * _sigmoid(h[:, d_model:]), g3_ref[...])
    o_ref[0] = jnp.concatenate([xe + r[:rows], xo + r[rows:]], axis=1)

    @pl.when(tile == pl.num_programs(1) - 1)
    def _():
        final_re_ref[...] = state_ref[:, :, :HALF]
        final_im_ref[...] = state_ref[:, :, HALF:]


def _s5_layer(x, g2, g3, win, wout, wskip, lam2, d, wglu, bglu, init_re, init_im, layer, *, n_seq, n_t):
    nb_rows, s2, dm2 = x.shape
    dm = dm2 // 2
    n_blocks = win.shape[1]
    assert n_blocks == SUBLANE
    rows = n_seq * n_t
    n_tiles = s2 // rows
    assert n_seq == 1 or n_tiles == 1
    row = pl.BlockSpec((1, rows, dm2), lambda b, i: (b, i, 0))
    vec = _const_spec((1, dm))
    state = pl.BlockSpec((n_seq, SUBLANE, HALF), lambda b, i: (b, 0, 0))
    state_in = pl.BlockSpec(state.block_shape, state.index_map, pipeline_mode=pl.Buffered(1))
    state_shape = jax.ShapeDtypeStruct(init_re.shape, F32)
    slab = pltpu.VMEM((N_SLABS, n_blocks * _slab_pitch(rows), LANE), F32)
    return pl.pallas_call(
        functools.partial(_s5_mixer_kernel, n_blocks=n_blocks, n_seq=n_seq, n_t=n_t, d_model=dm),
        grid=(nb_rows, n_tiles),
        in_specs=[row, vec, vec, _layer_spec(win.shape, layer), _layer_spec(wout.shape, layer),
                  _layer_spec(wskip.shape, layer), _layer_spec(lam2.shape, layer), vec,
                  _layer_spec(wglu.shape, layer), _const_spec((1, 2 * dm)), state_in, state_in],
        out_specs=[row, state, state],
        out_shape=[jax.ShapeDtypeStruct(x.shape, F32), state_shape, state_shape],
        scratch_shapes=[pltpu.VMEM((n_seq, SUBLANE, 2 * HALF), F32), pltpu.VMEM((rows, dm2), F32), slab],
        compiler_params=_params(2),
        name="s5_mixer",
    )(x, g2, g3, win, wout, wskip, lam2, d, wglu, bglu, init_re, init_im)


def _state_rows(s):
    return s.reshape(s.shape[0], -1, HALF)


def _t5_bucket_np(dist):
    n = np.maximum(dist, 0)
    max_exact = NUM_BUCKETS // 2
    nf = np.maximum(n, 1).astype(np.float32)
    large = max_exact + (np.log(nf / np.float32(max_exact)) / np.float32(math.log(MAX_DISTANCE / max_exact))
                         * np.float32(NUM_BUCKETS - max_exact)).astype(np.int32)
    large = np.minimum(large, NUM_BUCKETS - 1)
    return np.where(n < max_exact, n, large).astype(np.int32)


def _bucket_table(n_q, n_k, q_offset, n_q_pad, n_k_pad, first_key=0):
    dist = (np.arange(n_q)[:, None] + q_offset) - np.arange(n_k)[None, :]
    valid = (dist >= 0) & (dist < WINDOW) & (np.arange(n_k)[None, :] >= first_key)
    table = np.full((n_q_pad, n_k_pad), -1, np.int32)
    table[:n_q, :n_k] = np.where(valid, _t5_bucket_np(dist), -1)
    return table


def _bias_kernel(rel_ref, idx_ref, o_ref):
    h = pl.program_id(0)
    idx = idx_ref[...]
    acc = jnp.full(idx.shape, NEG_INF, F32)
    for b in range(NUM_BUCKETS):
        acc = jnp.where(idx == b, rel_ref[b, h] * LOG2_E, acc)
    o_ref[0] = acc


def _bias_planes(rel_bias, table):
    n_heads = rel_bias.shape[1]
    r, c = table.shape
    return pl.pallas_call(
        _bias_kernel,
        grid=(n_heads,),
        in_specs=[pl.BlockSpec(memory_space=pltpu.SMEM), pl.BlockSpec((r, c), lambda h: (0, 0))],
        out_specs=pl.BlockSpec((1, r, c), lambda h: (h, 0, 0)),
        out_shape=jax.ShapeDtypeStruct((n_heads, r, c), F32),
        compiler_params=_params(),
        name="attn_bias",
    )(rel_bias, jnp.asarray(table))


def _bias_tiles_kernel(rel_ref, idx_ref, o_ref, *, n_kv, pairs):
    n_var, n_k, n_q = idx_ref.shape
    for v in range(n_var):
        idx = idx_ref[v]
        for g in range(n_kv):
            for pair in range(pairs):
                for par in range(2):
                    h = (g * pairs + pair) * 2 + par
                    acc = jnp.full(idx.shape, NEG_INF, F32)
                    for b in range(NUM_BUCKETS):
                        acc = jnp.where(idx == b, rel_ref[b, h] * LOG2_E, acc)
                    o_ref[v, g, par * n_k:(par + 1) * n_k, pair * n_q:(pair + 1) * n_q] = acc


def _bias_tiles(rel_bias, tables, n_kv):
    n_heads = rel_bias.shape[1]
    pairs = n_heads // n_kv // 2
    idx = jnp.asarray(np.stack([t.T for t in tables]))
    n_var, n_k, n_q = idx.shape
    return pl.pallas_call(
        functools.partial(_bias_tiles_kernel, n_kv=n_kv, pairs=pairs),
        in_specs=[pl.BlockSpec(memory_space=pltpu.SMEM), pl.BlockSpec(memory_space=pltpu.VMEM)],
        out_specs=pl.BlockSpec(memory_space=pltpu.VMEM),
        out_shape=jax.ShapeDtypeStruct((n_var, n_kv, 2 * n_k, pairs * n_q), F32),
        compiler_params=pltpu.CompilerParams(vmem_limit_bytes=VMEM_LIMIT_BYTES),
        name="attn_bias_tiles",
    )(rel_bias, idx)


def _kv_head_rows(kv, g):
    lane = lax.broadcasted_iota(jnp.int32, kv.shape, 1)
    swapped = pltpu.roll(kv, HEAD_DIM, axis=1)
    even, odd = (kv, swapped) if g == 0 else (swapped, kv)
    return jnp.where(lane < HEAD_DIM, even, 0.0), jnp.where(lane >= HEAD_DIM, odd, 0.0)


def _pair_layout_t(planes, n_kv, n_q):
    h, _, n_k = planes.shape
    pairs = h // n_kv // 2
    p = planes[:, :n_q].reshape(n_kv, pairs, 2, n_q, n_k).transpose(0, 2, 4, 1, 3)
    return p.reshape(n_kv, 2 * n_k, pairs * n_q)


def _sink_cols(sinks, n_kv, n_q):
    h = sinks.shape[0]
    pairs = h // n_kv // 2
    s = sinks.reshape(n_kv, pairs, 2).transpose(0, 2, 1)[..., None]
    return jnp.broadcast_to(s, (n_kv, 2, pairs, n_q)).reshape(n_kv, 2, pairs * n_q)


def _attend_t(chains):
    scores = []
    for q_t, k_even, k_odd, _, _, bias, _ in chains:
        ke = jnp.concatenate([k_even, k_odd], axis=0).astype(BF16)
        scores.append(jnp.dot(ke, q_t, preferred_element_type=F32) + bias)
    weights = []
    for s, chain in zip(scores, chains):
        sink = chain[6]
        n_k = s.shape[0] // 2
        probs, inv = [], []
        for par in range(2):
            sp = s[par * n_k:(par + 1) * n_k]
            sk = sink[par:par + 1] * LOG2_E
            mx = jnp.maximum(jnp.max(sp, axis=0, keepdims=True), sk)
            p = jnp.exp2(sp - mx)
            den = jnp.sum(p, axis=0, keepdims=True) + jnp.exp2(sk - mx)
            probs.append(p.astype(BF16))
            inv.append(1.0 / den)
        weights.append((jnp.concatenate(probs, axis=0), inv))
    outs = []
    for (p_t, inv), chain in zip(weights, chains):
        ve = jnp.concatenate([chain[3], chain[4]], axis=0).astype(BF16)
        o = lax.dot_general(ve, p_t, (((0,), (0,)), ((), ())), preferred_element_type=F32)
        row = lax.broadcasted_iota(jnp.int32, o.shape, 0)
        outs.append(o * jnp.where(row < HEAD_DIM, inv[0], inv[1]))
    return outs


def _prompt_attn_kernel(x_ref, kvc_ref, kvp_ref, bias_ref, sink_ref, g2_ref, g3_ref, wq_ref, bq_ref,
                        wo_ref, bo_ref, o_ref, *, n_kv, n_sub, steps_per_seq, scale):
    blk = WINDOW
    x = x_ref[...]
    u = _rms(x, g2_ref[...]).astype(BF16)
    q_t = lax.dot_general(wq_ref[...], u, (((1,), (1,)), ((), ())), preferred_element_type=F32)
    q_t = ((q_t + jnp.concatenate([bq_ref[...]] * n_sub, axis=1)) * scale).astype(BF16)
    kv = jnp.concatenate([kvp_ref[...], kvc_ref[...]], axis=0)
    pairs = q_t.shape[0] // LANE // n_kv
    first = jnp.where(pl.program_id(0) % steps_per_seq == 0, 1, 0)
    k_rows = [_kv_head_rows(kv[:, :n_kv * HEAD_DIM], g) for g in range(n_kv)]
    v_rows = [_kv_head_rows(kv[:, n_kv * HEAD_DIM:], g) for g in range(n_kv)]
    chains = []
    for s in range(n_sub):
        keys = slice(s * blk, (s + 2) * blk)
        for g in range(n_kv):
            qp = jnp.concatenate([q_t[(g * pairs + i) * LANE:(g * pairs + i + 1) * LANE, s * blk:(s + 1) * blk]
                                  for i in range(pairs)], axis=1)
            bias = bias_ref[first, g] if s == 0 else bias_ref[0, g]
            chains.append((qp, k_rows[g][0][keys], k_rows[g][1][keys], v_rows[g][0][keys],
                           v_rows[g][1][keys], bias, sink_ref[g]))
    outs = _attend_t(chains)
    o_t = jnp.concatenate(
        [jnp.concatenate([outs[s * n_kv + g][:, i * blk:(i + 1) * blk] for s in range(n_sub)], axis=1)
         for g in range(n_kv) for i in range(pairs)], axis=0).astype(BF16)
    a = lax.dot_general(o_t, wo_ref[...], (((0,), (0,)), ((), ())), preferred_element_type=F32) + bo_ref[...]
    o_ref[...] = x + _rms(a, g3_ref[...])


def _prompt_attention(x, kv, bias, sink, g2, g3, wq_t, bq_col, wo, bo, layer, *, blocks_per_seq, n_kv):
    t, d = x.shape
    blk = WINDOW
    n_sub = ATTN_BLOCKS_PER_STEP
    assert n_kv * HEAD_DIM == LANE and blocks_per_seq % n_sub == 0
    row = pl.BlockSpec((n_sub * blk, d), lambda i: (i, 0))
    kvw = kv.shape[1]
    return pl.pallas_call(
        functools.partial(_prompt_attn_kernel, n_kv=n_kv, n_sub=n_sub,
                          steps_per_seq=blocks_per_seq // n_sub, scale=QUERY_SCALE),
        grid=(t // (n_sub * blk),),
        in_specs=[row,
                  pl.BlockSpec((n_sub * blk, kvw), lambda i: (i, 0)),
                  pl.BlockSpec((blk, kvw), lambda i: (jnp.maximum(i * n_sub - 1, 0), 0)),
                  _const_spec(bias.shape),
                  _const_spec(sink.shape), _const_spec((1, d)), _const_spec((1, d)),
                  _layer_spec(wq_t.shape, layer), _const_spec(bq_col.shape),
                  _layer_spec(wo.shape, layer), _const_spec((1, d))],
        out_specs=row,
        out_shape=jax.ShapeDtypeStruct((t, d), F32),
        compiler_params=_params(),
        name="prompt_attention",
    )(x, kv, kv, bias, sink, g2, g3, wq_t, bq_col, wo, bo)


def _sample_attn_kernel(q_ref, kc_ref, vc_ref, kvn_ref, bias_ref, sink_ref, o_ref, *, n_kv, scale):
    half = n_kv * HEAD_DIM
    chains = []
    for b in range(q_ref.shape[0]):
        k = jnp.concatenate([kc_ref[b], kvn_ref[b, :, :half]], axis=0)
        v = jnp.concatenate([vc_ref[b], kvn_ref[b, :, half:]], axis=0)
        for g in range(n_kv):
            chains.append(((q_ref[b, g] * scale).astype(BF16), *_kv_head_rows(k, g), *_kv_head_rows(v, g),
                           bias_ref[g], sink_ref[g]))
    outs = _attend_t(chains)
    for b in range(q_ref.shape[0]):
        for g in range(n_kv):
            o_ref[b, g] = outs[b * n_kv + g]


def _sample_attention(q_t, k_cache, v_cache, kv_new, bias, sink, *, n_kv):
    nb, _, _, n_l = q_t.shape
    n_past = k_cache.shape[1]
    bb = SUBLANE
    return pl.pallas_call(
        functools.partial(_sample_attn_kernel, n_kv=n_kv, scale=QUERY_SCALE),
        grid=(nb // bb,),
        in_specs=[pl.BlockSpec((bb, n_kv, LANE, n_l), lambda i: (i, 0, 0, 0)),
                  pl.BlockSpec((bb, n_past, LANE), lambda i: (i, 0, 0)),
                  pl.BlockSpec((bb, n_past, LANE), lambda i: (i, 0, 0)),
                  pl.BlockSpec((bb,) + kv_new.shape[1:], lambda i: (i, 0, 0)),
                  _const_spec(bias.shape), _const_spec(sink.shape)],
        out_specs=pl.BlockSpec((bb, n_kv, LANE, n_l), lambda i: (i, 0, 0, 0)),
        out_shape=jax.ShapeDtypeStruct(q_t.shape, F32),
        compiler_params=_params(),
        name="sample_attention",
    )(q_t, k_cache, v_cache, kv_new, bias, sink)


def kernel(x_prompt, x_sample, state_ssm_re, state_ssm_im, cache_win_k, cache_win_v, norm_g, ffn1_w_gu, ffn1_w_down, ffn2_w_gu, ffn2_w_down, ssm_lambda_re, ssm_lambda_im, ssm_log_dt, ssm_b_re, ssm_b_im, ssm_c_re, ssm_c_im, ssm_d, ssm_w_glu, ssm_b_glu, kv_norm_g, w_kv, b_kv, attn_w_q, attn_b_q, attn_sinks, attn_w_o, attn_b_o, rel_bias):
    bsz, seq, dm = x_prompt.shape
    dec_b, dec_s, _ = x_sample.shape
    depth = norm_g.shape[0]
    n_a = ssm_lambda_re.shape[0]
    n_groups = ssm_lambda_re.shape[1]
    n_heads = attn_sinks.shape[1]
    n_kv = cache_win_k.shape[2]
    n_past = cache_win_k.shape[1]
    pairs = n_heads // n_kv // 2
    n_k_pad = n_past + SUBLANE
    assert seq % ROW_TILE == 0 and seq % WINDOW == 0 and dec_s % 2 == 0
    assert dec_b * dec_s == ROW_TILE and dec_s <= SUBLANE and n_past % SUBLANE == 0
    assert dec_b % (SUBLANE * SAMPLE_SPLIT) == 0
    assert n_kv == 2 and dm == n_heads * HEAD_DIM and n_groups == SUBLANE * GROUPS_PER_BLOCK

    bf = lambda w: w.astype(BF16)
    row = lambda v: v.reshape(1, -1)
    s5_w = _discretize(ssm_lambda_re, ssm_lambda_im, ssm_log_dt, ssm_b_re, ssm_b_im, ssm_c_re, ssm_c_im)
    ffn_w = (bf(ffn1_w_gu[0])[None], bf(ffn1_w_down[0])[None])
    wglu, wkv, wq, wo = bf(ssm_w_glu), bf(w_kv)[None], bf(attn_w_q), bf(attn_w_o)

    tab_p = _bucket_table(WINDOW, 2 * WINDOW, WINDOW, WINDOW, 2 * WINDOW)
    tab_p0 = _bucket_table(WINDOW, 2 * WINDOW, WINDOW, WINDOW, 2 * WINDOW, first_key=WINDOW)
    tab_s = _bucket_table(dec_s, n_past + dec_s, n_past, SUBLANE, n_k_pad)
    wq_t = wq.transpose(0, 2, 1)
    bias_p = _bias_tiles(rel_bias, [tab_p, tab_p0], n_kv)
    bias_s = _pair_layout_t(_bias_planes(rel_bias, tab_s), n_kv, dec_s)

    xp = x_prompt.reshape(bsz * seq, dm)
    xs = x_sample.reshape(dec_b * dec_s, dm)
    zero_state = jnp.zeros((bsz, SUBLANE, HALF), F32)

    ends_p, ends_s = [], []
    kv_p = kv_s = kv_new = None
    k_cache = cache_win_k.reshape(dec_b, n_past, n_kv * HEAD_DIM)
    v_cache = cache_win_v.reshape(dec_b, n_past, n_kv * HEAD_DIM)
    half = n_kv * HEAD_DIM
    for l in range(depth):
        g = [row(norm_g[l, i]) for i in range(norm_g.shape[1])]
        pair_io = dict(pair_in=0 < l < n_a, pair_out=l < n_a)
        xp, xs, *ffn_w = _ffn(xp, xs, g[0], g[1], *ffn_w, **pair_io, cast_next=(ffn2_w_gu, ffn2_w_down, l))
        if l < n_a:
            common = (g[2], g[3], *s5_w, row(ssm_d[l]), wglu, row(ssm_b_glu[l]))
            xp3, *fin_p = _s5_layer(xp.reshape(bsz, seq // 2, 2 * dm), *common, zero_state, zero_state, l,
                                    n_seq=1, n_t=ROW_TILE // 2)
            xs3, *fin_s = _s5_layer(xs.reshape(SAMPLE_SPLIT, dec_b * dec_s // (2 * SAMPLE_SPLIT), 2 * dm), *common,
                                    _state_rows(state_ssm_re[l]), _state_rows(state_ssm_im[l]), l,
                                    n_seq=dec_b // SAMPLE_SPLIT, n_t=dec_s // 2)
            xp, xs = xp3.reshape(bsz * seq // 2, 2 * dm), xs3.reshape(dec_b * dec_s // 2, 2 * dm)
            ends_p.append([s.reshape(bsz, n_groups, STATE_DIM) for s in fin_p])
            ends_s.append([s.reshape(dec_b, n_groups, STATE_DIM) for s in fin_s])
        else:
            bl = l - n_a
            bq_l, bo_l = row(attn_b_q[bl]), row(attn_b_o[bl])
            bq_col = jnp.broadcast_to(attn_b_q[bl][:, None], (dm, WINDOW))
            xp = _prompt_attention(xp, kv_p, bias_p, _sink_cols(attn_sinks[bl], n_kv, WINDOW), g[2], g[3],
                                   wq_t, bq_col, wo, bo_l, bl, blocks_per_seq=seq // WINDOW, n_kv=n_kv)
            q = _norm_proj(xs, g[2], wq, bq_l, bl)
            q = q.reshape(dec_b, dec_s, n_kv, pairs, LANE).transpose(0, 2, 4, 3, 1)
            q = q.reshape(dec_b, n_kv, LANE, pairs * dec_s)
            o = _sample_attention(q, k_cache, v_cache, kv_new, bias_s, _sink_cols(attn_sinks[bl], n_kv, dec_s), n_kv=n_kv)
            o = o.reshape(dec_b, n_kv, LANE, pairs, dec_s).transpose(0, 4, 1, 3, 2).reshape(dec_b * dec_s, dm)
            xs = _out_proj_residual(xs, o, g[3], wo, bo_l, bl)
        pair_io = dict(pair_in=l < n_a, pair_out=l < n_a - 1)
        cast_next = (ffn1_w_gu, ffn1_w_down, l + 1) if l + 1 < depth else None
        if l == n_a - 1:
            xp, xs, kv_p, kv_s, *ffn_w = _ffn(xp, xs, g[4], g[5], *ffn_w, **pair_io, cast_next=cast_next,
                                              kv=(row(kv_norm_g), wkv, row(b_kv)))
            kv_s = kv_s.reshape(dec_b, dec_s, 2 * half)
            kv_new = jnp.pad(kv_s, ((0, 0), (0, SUBLANE - dec_s), (0, 0)))
        else:
            xp, xs, *ffn_w = _ffn(xp, xs, g[4], g[5], *ffn_w, **pair_io, cast_next=cast_next)

    kv_p3 = kv_p.reshape(bsz, seq, 2 * half)
    new_k_p = kv_p3[:, -WINDOW:, :half].reshape(bsz, WINDOW, n_kv, HEAD_DIM)
    new_v_p = kv_p3[:, -WINDOW:, half:].reshape(bsz, WINDOW, n_kv, HEAD_DIM)
    new_k_s = jnp.concatenate([k_cache, kv_s[:, :, :half]], axis=1)[:, -WINDOW:].reshape(dec_b, WINDOW, n_kv, HEAD_DIM)
    new_v_s = jnp.concatenate([v_cache, kv_s[:, :, half:]], axis=1)[:, -WINDOW:].reshape(dec_b, WINDOW, n_kv, HEAD_DIM)
    return (xp.reshape(bsz, seq, dm), xs.reshape(dec_b, dec_s, dm),
            jnp.stack([e[0] for e in ends_p]), jnp.stack([e[1] for e in ends_p]),
            new_k_p, new_v_p,
            jnp.stack([e[0] for e in ends_s]), jnp.stack([e[1] for e in ends_s]),
            new_k_s, new_v_s)
```
